```python
import math
import jax, jax.numpy as jnp
from jax import lax
import numpy as np

D_MODEL = 2048
BATCH = 16
SEQ = 2048
DEPTH = 2

GRID_W = 64
CTX_LEN = 256
N_MIXERS = 2
MIXER_ATTN = 0
MIXER_GMLP = 1
N_HEADS = 32
N_KV_HEADS = 4
Q_PER_KV = N_HEADS // N_KV_HEADS
HEAD_DIM = 64
Q_DIM = N_HEADS * HEAD_DIM
KV_DIM = N_KV_HEADS * HEAD_DIM
QKV_DIM = Q_DIM + 2 * KV_DIM
WINDOW = 128
Q_BLOCK = 128
ROPE_BASE = 10000.0
ROPE_AXIS_DIM = HEAD_DIM // 2
NEG_INF = -1e30
CHUNK = 128
GMLP_WIDTH = 2 * D_MODEL
N_GMLP_GROUPS = 8
GMLP_GROUP_DIM = GMLP_WIDTH // N_GMLP_GROUPS
N_EXPERTS = 64
TOP_K = 8
N_EXPERT_GROUPS = 8
EXPERTS_PER_GROUP = N_EXPERTS // N_EXPERT_GROUPS
TOPK_GROUPS = 4
D_EXPERT = D_MODEL // 4
D_SHARED = D_MODEL // 4
ROUTED_SCALE = 2.5
DN_ALPHA = (2 * DEPTH) ** 0.25
DN_BETA = (8 * DEPTH) ** -0.25
LN_EPS = 1e-5
N_ATTN_LAYERS = (DEPTH + 1) // 2
N_GMLP_LAYERS = DEPTH // 2

kernel_name = "hybrid_swa_gmlp_moe_deepnorm_dit"


def _layernorm(x, g, b):
    xf = x.astype(jnp.float32)
    mu = jnp.mean(xf, axis=-1, keepdims=True)
    var = jnp.mean(jnp.square(xf - mu), axis=-1, keepdims=True)
    return ((xf - mu) * lax.rsqrt(var + LN_EPS)).astype(x.dtype) * g + b


def _axial_rope_tables(rows, dtype):
    row_ids = jnp.repeat(jnp.arange(rows, dtype=jnp.float32), GRID_W)
    col_ids = jnp.tile(jnp.arange(GRID_W, dtype=jnp.float32), rows)
    inv_freq = ROPE_BASE ** (-jnp.arange(0, ROPE_AXIS_DIM, 2, dtype=jnp.float32) / ROPE_AXIS_DIM)
    ang = jnp.stack([row_ids, col_ids], axis=-1)[..., None] * inv_freq
    return jnp.cos(ang).astype(dtype), jnp.sin(ang).astype(dtype)


def _rotate(x, cos, sin):
    x1, x2 = jnp.split(x, 2, axis=-1)
    return jnp.concatenate([x1 * cos - x2 * sin, x2 * cos + x1 * sin], axis=-1)


def _axial_rope(x, cos, sin):
    shape = (x.shape[1],) + (1,) * (x.ndim - 3) + (ROPE_AXIS_DIM // 2,)
    x_row, x_col = jnp.split(x, 2, axis=-1)
    return jnp.concatenate([
        _rotate(x_row, cos[:, 0].reshape(shape), sin[:, 0].reshape(shape)),
        _rotate(x_col, cos[:, 1].reshape(shape), sin[:, 1].reshape(shape))], axis=-1)


def _sink_softmax(sink_kg, parts):
    b, h, g, q, _ = parts[0].shape
    s = jnp.broadcast_to(sink_kg[None, :, :, None, None], (b, h, g, q, 1))
    p = jax.nn.softmax(jnp.concatenate([s] + parts, axis=-1), axis=-1)
    splits = [int(v) for v in np.cumsum([1] + [pt.shape[-1] for pt in parts])[:-1]]
    return jnp.split(p, splits, axis=-1)[1:]


def _windowed_gqa(h_lat, h_ctx, w_qkv, w_o, sink, rope_cos, rope_sin, ctx_queries):
    B, S, _ = h_lat.shape
    C = h_ctx.shape[1]
    scale = HEAD_DIM ** -0.5
    sink_kg = sink.reshape(N_KV_HEADS, Q_PER_KV).astype(jnp.float32)
    qkv = h_lat @ w_qkv
    q = qkv[..., :Q_DIM].reshape(B, S, N_KV_HEADS, Q_PER_KV, HEAD_DIM)
    k = qkv[..., Q_DIM:Q_DIM + KV_DIM].reshape(B, S, N_KV_HEADS, HEAD_DIM)
    v = qkv[..., Q_DIM + KV_DIM:].reshape(B, S, N_KV_HEADS, HEAD_DIM)
    q = _axial_rope(q, rope_cos, rope_sin)
    k = _axial_rope(k, rope_cos, rope_sin)
    if ctx_queries:
        qkv_c = h_ctx @ w_qkv
        q_c = qkv_c[..., :Q_DIM].reshape(B, C, N_KV_HEADS, Q_PER_KV, HEAD_DIM)
        kv_c = qkv_c[..., Q_DIM:]
    else:
        kv_c = h_ctx @ w_qkv[:, Q_DIM:]
    k_c = kv_c[..., :KV_DIM].reshape(B, C, N_KV_HEADS, HEAD_DIM)
    v_c = kv_c[..., KV_DIM:].reshape(B, C, N_KV_HEADS, HEAD_DIM)

    n_blocks = S // Q_BLOCK
    win = Q_BLOCK + 2 * WINDOW
    q_blocks = q.reshape(B, n_blocks, Q_BLOCK, N_KV_HEADS, Q_PER_KV, HEAD_DIM).transpose(1, 0, 2, 3, 4, 5)
    pad = ((0, 0), (WINDOW, WINDOW), (0, 0), (0, 0))
    k_pad = jnp.pad(k, pad)
    v_pad = jnp.pad(v, pad)

    def block(args):
        i, qb = args
        start = i * Q_BLOCK
        kw = lax.dynamic_slice_in_dim(k_pad, start, win, axis=1)
        vw = lax.dynamic_slice_in_dim(v_pad, start, win, axis=1)
        q_pos = start + jnp.arange(Q_BLOCK)
        k_pos = start - WINDOW + jnp.arange(win)
        valid = (jnp.abs(q_pos[:, None] - k_pos[None, :]) <= WINDOW) & (k_pos >= 0)[None, :] & (k_pos < S)[None, :]
        s_win = jnp.einsum('bqhgd,bkhd->bhgqk', qb, kw).astype(jnp.float32) * scale
        s_win = jnp.where(valid, s_win, NEG_INF)
        s_ctx = jnp.einsum('bqhgd,bkhd->bhgqk', qb, k_c).astype(jnp.float32) * scale
        p_ctx, p_win = _sink_softmax(sink_kg, [s_ctx, s_win])
        return (jnp.einsum('bhgqk,bkhd->bqhgd', p_ctx.astype(v_c.dtype), v_c)
                + jnp.einsum('bhgqk,bkhd->bqhgd', p_win.astype(vw.dtype), vw))

    o = lax.map(block, (jnp.arange(n_blocks), q_blocks))
    y = o.transpose(1, 0, 2, 3, 4, 5).reshape(B, S, Q_DIM) @ w_o

    y_c = None
    if ctx_queries:
        s_cc = jnp.einsum('bqhgd,bkhd->bhgqk', q_c, k_c).astype(jnp.float32) * scale
        (p_cc,) = _sink_softmax(sink_kg, [s_cc])
        o_c = jnp.einsum('bhgqk,bkhd->bqhgd', p_cc.astype(v_c.dtype), v_c)
        y_c = o_c.reshape(B, C, Q_DIM) @ w_o
    return y, y_c


def _chunk_gmlp(h, w_in, b_in, v_g, v_b, w_s, b_s, w_o):
    B, L, _ = h.shape
    z = jax.nn.gelu(h @ w_in + b_in, approximate=False)
    u, v = jnp.split(z, 2, axis=-1)
    v = _layernorm(v, v_g, v_b)
    vc = v.reshape(B, L // CHUNK, CHUNK, N_GMLP_GROUPS, GMLP_GROUP_DIM)
    mixed = jnp.einsum('gpq,bcqgd->bcpgd', w_s, vc) + b_s.T[:, :, None]
    return (u * mixed.reshape(B, L, GMLP_WIDTH)) @ w_o


def _moe(h, w_router, router_bias, w1, w3, w2, ws1, ws3, ws2):
    T = h.shape[0]
    scores = jax.nn.sigmoid((h @ w_router).astype(jnp.float32))
    biased = scores + router_bias.astype(jnp.float32)
    grp = biased.reshape(T, N_EXPERT_GROUPS, EXPERTS_PER_GROUP)
    grp_score = lax.top_k(grp, 2)[0].sum(-1)
    _, top_grp = lax.top_k(grp_score, TOPK_GROUPS)
    grp_mask = jax.nn.one_hot(top_grp, N_EXPERT_GROUPS, dtype=jnp.float32).sum(1)
    expert_mask = jnp.repeat(grp_mask, EXPERTS_PER_GROUP, axis=1)
    _, top_idx = lax.top_k(jnp.where(expert_mask > 0, biased, -jnp.inf), TOP_K)
    w = jnp.take_along_axis(scores, top_idx, axis=1)
    w = w / jnp.sum(w, axis=-1, keepdims=True) * ROUTED_SCALE
    gates = jnp.sum(jax.nn.one_hot(top_idx, N_EXPERTS, dtype=jnp.float32) * w[..., None], axis=1).astype(h.dtype)

    def expert(e, acc):
        a = jax.nn.silu(h @ w1[e]) * (h @ w3[e])
        return acc + lax.dynamic_slice_in_dim(gates, e, 1, axis=1) * (a @ w2[e])

    routed = lax.fori_loop(0, N_EXPERTS, expert, jnp.zeros_like(h))
    shared = (jax.nn.silu(h @ ws1) * (h @ ws3)) @ ws2
    return routed + shared


def setup_inputs(seed: int = 0) -> dict:
    key = jax.random.key(seed)
    ks = iter(jax.random.split(key, 32))
    f32 = jnp.float32

    def nrm(shape, scale):
        return jax.random.normal(next(ks), shape, f32) * scale

    D = D_MODEL
    return {
        "x": nrm((BATCH, SEQ, D), 1.0),
        "c": nrm((BATCH, D), 1.0),
        "ctx": nrm((BATCH, CTX_LEN, D), 1.0),
        "c_ctx": nrm((D,), 1.0),
        "w_ada": nrm((DEPTH, D, 6 * D), 0.5 * D ** -0.5),
        "b_ada": nrm((DEPTH, 6 * D), 0.02),
        "ln_mix_g": 1.0 + nrm((DEPTH, D), 0.02),
        "ln_mix_b": nrm((DEPTH, D), 0.02),
        "ln_ffn_g": 1.0 + nrm((DEPTH, D), 0.02),
        "ln_ffn_b": nrm((DEPTH, D), 0.02),
        "attn_w_qkv": nrm((N_ATTN_LAYERS, D, QKV_DIM), D ** -0.5),
        "attn_w_o": nrm((N_ATTN_LAYERS, Q_DIM, D), Q_DIM ** -0.5 * DN_BETA),
        "attn_sink": nrm((N_ATTN_LAYERS, N_HEADS), 0.5),
        "gmlp_w_in": nrm((N_GMLP_LAYERS, D, 2 * GMLP_WIDTH), D ** -0.5),
        "gmlp_b_in": nrm((N_GMLP_LAYERS, 2 * GMLP_WIDTH), 0.02),
        "gmlp_v_g": 1.0 + nrm((N_GMLP_LAYERS, GMLP_WIDTH), 0.02),
        "gmlp_v_b": nrm((N_GMLP_LAYERS, GMLP_WIDTH), 0.02),
        "gmlp_w_s": nrm((N_GMLP_LAYERS, N_GMLP_GROUPS, CHUNK, CHUNK), CHUNK ** -0.5),
        "gmlp_b_s": 1.0 + nrm((N_GMLP_LAYERS, N_GMLP_GROUPS, CHUNK), 0.02),
        "gmlp_w_o": nrm((N_GMLP_LAYERS, GMLP_WIDTH, D), GMLP_WIDTH ** -0.5 * DN_BETA),
        "moe_w_router": nrm((DEPTH, D, N_EXPERTS), D ** -0.5),
        "moe_bias": nrm((DEPTH, N_EXPERTS), 0.01),
        "moe_w1": nrm((DEPTH, N_EXPERTS, D, D_EXPERT), D ** -0.5),
        "moe_w3": nrm((DEPTH, N_EXPERTS, D, D_EXPERT), D ** -0.5),
        "moe_w2": nrm((DEPTH, N_EXPERTS, D_EXPERT, D), D_EXPERT ** -0.5 * DN_BETA),
        "moe_ws1": nrm((DEPTH, D, D_SHARED), D ** -0.5),
        "moe_ws3": nrm((DEPTH, D, D_SHARED), D ** -0.5),
        "moe_ws2": nrm((DEPTH, D_SHARED, D), D_SHARED ** -0.5 * DN_BETA),
    }


def reference(x, c, ctx, c_ctx, w_ada, b_ada, ln_mix_g, ln_mix_b, ln_ffn_g, ln_ffn_b,
              attn_w_qkv, attn_w_o, attn_sink,
              gmlp_w_in, gmlp_b_in, gmlp_v_g, gmlp_v_b, gmlp_w_s, gmlp_b_s, gmlp_w_o,
              moe_w_router, moe_bias, moe_w1, moe_w3, moe_w2, moe_ws1, moe_ws3, moe_ws2):
    B, S, D = x.shape
    C = ctx.shape[1]
    ROWS = S // GRID_W
    rope_cos, rope_sin = _axial_rope_tables(ROWS, x.dtype)
    silu_c = jax.nn.silu(c)
    silu_c_ctx = jax.nn.silu(c_ctx)
    last_ctx_reader = max(i for i in range(DEPTH) if i % N_MIXERS == MIXER_ATTN)
    h_ctx = ctx
    for i in range(DEPTH):
        mixer, j = i % N_MIXERS, i // N_MIXERS
        ctx_live = i <= last_ctx_reader
        update_ctx = i < last_ctx_reader
        sh_m, sc_m, g_m, sh_f, sc_f, g_f = jnp.split((silu_c @ w_ada[i] + b_ada[i])[:, None, :], 6, axis=-1)
        if ctx_live:
            csh_m, csc_m, cg_m, csh_f, csc_f, cg_f = jnp.split(silu_c_ctx @ w_ada[i] + b_ada[i], 6, axis=-1)
            hc_in = h_ctx * (1.0 + csc_m) + csh_m
        h_in = x * (1.0 + sc_m) + sh_m
        if mixer == MIXER_ATTN:
            y, y_c = _windowed_gqa(h_in, hc_in, attn_w_qkv[j], attn_w_o[j], attn_sink[j],
                                   rope_cos, rope_sin, update_ctx)
        else:
            gp = (gmlp_w_in[j], gmlp_b_in[j], gmlp_v_g[j], gmlp_v_b[j], gmlp_w_s[j], gmlp_b_s[j], gmlp_w_o[j])
            y = _chunk_gmlp(h_in, *gp)
            y_c = _chunk_gmlp(hc_in, *gp) if update_ctx else None
        x = _layernorm(DN_ALPHA * x + g_m * y, ln_mix_g[i], ln_mix_b[i])
        if update_ctx:
            h_ctx = _layernorm(DN_ALPHA * h_ctx + cg_m * y_c, ln_mix_g[i], ln_mix_b[i])
        f_in = (x * (1.0 + sc_f) + sh_f).reshape(B * S, D)
        if update_ctx:
            fc_in = (h_ctx * (1.0 + csc_f) + csh_f).reshape(B * C, D)
            f_in = jnp.concatenate([f_in, fc_in], axis=0)
        f = _moe(f_in, moe_w_router[i], moe_bias[i], moe_w1[i], moe_w3[i], moe_w2[i],
                 moe_ws1[i], moe_ws3[i], moe_ws2[i])
        x = _layernorm(DN_ALPHA * x + g_f * f[:B * S].reshape(B, S, D), ln_ffn_g[i], ln_ffn_b[i])
        if update_ctx:
            h_ctx = _layernorm(DN_ALPHA * h_ctx + cg_f * f[B * S:].reshape(B, C, D), ln_ffn_g[i], ln_ffn_b[i])
    return x
```

```python
import functools

import jax
import jax.numpy as jnp
from jax import lax
from jax.experimental import pallas as pl
from jax.experimental.pallas import tpu as pltpu

F32 = jnp.float32
BF16 = jnp.bfloat16
I32 = jnp.int32

GRID_W = 64
N_KV_HEADS = 4
Q_PER_KV = 8
HEAD_DIM = 64
ROPE_HALF = 16
WINDOW = 128
Q_BLOCK = 128
ROPE_BASE = 10000.0
NEG_INF = -1e30
CHUNK = 128
N_GMLP_GROUPS = 8
N_EXPERTS = 64
TOP_K = 8
N_EXPERT_GROUPS = 8
EXPERTS_PER_GROUP = 8
TOPK_GROUPS = 4
ROUTED_SCALE = 2.5
LN_EPS = 1e-5

LANES = 128
VMEM_LIMIT_BYTES = 56 * 1024 * 1024


def _params(*sem):
    return pltpu.CompilerParams(dimension_semantics=sem, vmem_limit_bytes=VMEM_LIMIT_BYTES)


def _silu(a):
    return a * jax.nn.sigmoid(a)


def _layernorm(v, g, b):
    mu = jnp.mean(v, axis=-1, keepdims=True)
    d = v - mu
    var = jnp.mean(d * d, axis=-1, keepdims=True)
    return d * lax.rsqrt(var + LN_EPS) * g + b


def _bdot(a, b):
    return jnp.dot(a, b, preferred_element_type=F32)


def _mod_kernel(c_ref, w_ref, b_ref, o_ref):
    a = _silu(c_ref[...]).astype(BF16)
    o_ref[0] = _bdot(a, w_ref[0].astype(BF16)) + b_ref[0]


def _modulations(cc, w_ada, b_ada):
    depth, d, n6 = w_ada.shape
    r = cc.shape[0]
    tn = max(w for w in range(LANES, min(1024, n6) + 1, LANES) if n6 % w == 0)
    return pl.pallas_call(
        _mod_kernel,
        out_shape=jax.ShapeDtypeStruct((depth, r, n6), F32),
        grid=(depth, n6 // tn),
        in_specs=[pl.BlockSpec((r, d), lambda l, j: (0, 0)),
                  pl.BlockSpec((1, d, tn), lambda l, j: (l, 0, j)),
                  pl.BlockSpec((1, 1, tn), lambda l, j: (l, 0, j))],
        out_specs=pl.BlockSpec((1, r, tn), lambda l, j: (l, 0, j)),
        compiler_params=_params("parallel", "parallel"),
        name="adaln_mod",
    )(cc, w_ada, b_ada.reshape(depth, 1, n6))


def _rope_tables(seq):
    rows = seq // GRID_W
    row_ids = jnp.repeat(jnp.arange(rows, dtype=F32), GRID_W)
    col_ids = jnp.tile(jnp.arange(GRID_W, dtype=F32), rows)
    inv_freq = ROPE_BASE ** (-jnp.arange(0, 2 * ROPE_HALF, 2, dtype=F32) / (2 * ROPE_HALF))
    ar, ac = row_ids[:, None] * inv_freq, col_ids[:, None] * inv_freq
    z = jnp.zeros_like(ar)
    cos = jnp.concatenate([jnp.cos(ar), jnp.cos(ar), jnp.cos(ac), jnp.cos(ac)], axis=-1)
    sin_lo = jnp.concatenate([-jnp.sin(ar), z, -jnp.sin(ac), z], axis=-1)
    sin_hi = jnp.concatenate([z, jnp.sin(ar), z, jnp.sin(ac)], axis=-1)
    rep = LANES // HEAD_DIM
    return tuple(jnp.tile(t, (1, rep)) for t in (cos, sin_lo, sin_hi))


def _qkv_kernel(x_ref, sc_ref, sh_ref, w_ref, cos_ref, slo_ref, shi_ref, q_ref, k_ref, v_ref, *, cw):
    h = (x_ref[...] * (1.0 + sc_ref[0]) + sh_ref[0]).astype(BF16)
    cos, slo, shi = cos_ref[...], slo_ref[...], shi_ref[...]

    def rope(a):
        return (a * cos + pltpu.roll(a, LANES - ROPE_HALF, 1) * slo
                + pltpu.roll(a, ROPE_HALF, 1) * shi)

    qd, kd = q_ref.shape[1], k_ref.shape[1]
    for c0 in range(0, qd, cw):
        acc = _bdot(h, w_ref[:, c0:c0 + cw])
        for b0 in range(0, cw, LANES):
            q_ref[:, c0 + b0:c0 + b0 + LANES] = rope(acc[:, b0:b0 + LANES]).astype(BF16)
    acc = _bdot(h, w_ref[:, qd:qd + kd])
    for b0 in range(0, kd, LANES):
        k_ref[:, b0:b0 + LANES] = rope(acc[:, b0:b0 + LANES]).astype(BF16)
    v_ref[...] = _bdot(h, w_ref[:, qd + kd:qd + 2 * kd]).astype(BF16)


def _qkv_proj(x2, sc, sh, w, tables, seq):
    t, d = x2.shape
    kd = N_KV_HEADS * LANES
    qd = w.shape[1] - 2 * kd
    tm = min(512, seq)
    per_b = seq // tm
    row = lambda i: (i, 0)
    mod = lambda i: (i // per_b, 0, 0)
    tab = lambda i: (i % per_b, 0)
    return pl.pallas_call(
        functools.partial(_qkv_kernel, cw=min(512, qd)),
        out_shape=(jax.ShapeDtypeStruct((t, qd), BF16), jax.ShapeDtypeStruct((t, kd), BF16),
                   jax.ShapeDtypeStruct((t, kd), BF16)),
        grid=(t // tm,),
        in_specs=[pl.BlockSpec((tm, d), row), pl.BlockSpec((1, 1, d), mod), pl.BlockSpec((1, 1, d), mod),
                  pl.BlockSpec(w.shape, lambda i: (0, 0)),
                  pl.BlockSpec((tm, LANES), tab), pl.BlockSpec((tm, LANES), tab), pl.BlockSpec((tm, LANES), tab)],
        out_specs=(pl.BlockSpec((tm, qd), row), pl.BlockSpec((tm, kd), row), pl.BlockSpec((tm, kd), row)),
        compiler_params=_params("parallel"),
        name="qkv_rope",
    )(x2, sc, sh, w, *tables)


def _ctxkv_kernel(x_ref, sc_ref, sh_ref, w_ref, k_ref, v_ref):
    h = (x_ref[...] * (1.0 + sc_ref[...]) + sh_ref[...]).astype(BF16)
    kd = k_ref.shape[1]
    k_ref[...] = _bdot(h, w_ref[:, :kd]).astype(BF16)
    v_ref[...] = _bdot(h, w_ref[:, kd:]).astype(BF16)


def _ctx_kv(c2, sc, sh, w):
    t, d = c2.shape
    kd = w.shape[1] // 2
    tm = min(512, t)
    row = lambda i: (i, 0)
    fix = lambda i: (0, 0)
    return pl.pallas_call(
        _ctxkv_kernel,
        out_shape=(jax.ShapeDtypeStruct((t, kd), BF16), jax.ShapeDtypeStruct((t, kd), BF16)),
        grid=(t // tm,),
        in_specs=[pl.BlockSpec((tm, d), row), pl.BlockSpec((1, d), fix), pl.BlockSpec((1, d), fix),
                  pl.BlockSpec(w.shape, fix)],
        out_specs=(pl.BlockSpec((tm, kd), row), pl.BlockSpec((tm, kd), row)),
        compiler_params=_params("parallel"),
        name="ctx_kv",
    )(c2, sc, sh, w)


def _attn_kernel(sink_ref, q_ref, k_ref, v_ref, kc_ref, vc_ref, o_ref):
    kv, i = pl.program_id(1), pl.program_id(2)
    seq, nctx = k_ref.shape[1], kc_ref.shape[1]
    nwin = Q_BLOCK + 2 * WINDOW
    start = pl.multiple_of(jnp.clip(i * Q_BLOCK - WINDOW, 0, seq - nwin), Q_BLOCK)
    nkey = nctx + nwin
    kall = jnp.concatenate([kc_ref[0], k_ref[0, pl.ds(start, nwin), :]], axis=0)
    vall = jnp.concatenate([vc_ref[0], v_ref[0, pl.ds(start, nwin), :]], axis=0)
    low = lax.broadcasted_iota(I32, (nkey, LANES), 1) < HEAD_DIM
    zero = jnp.zeros((nkey, LANES), BF16)
    npair = Q_PER_KV // 2
    q = q_ref[0]
    qs = jnp.concatenate([q[:, LANES * j:LANES * (j + 1)] for j in range(npair)], axis=0)
    nrow = npair * Q_BLOCK
    rows = lax.broadcasted_iota(I32, (nrow, nkey), 0)
    cols = lax.broadcasted_iota(I32, (nrow, nkey), 1)
    q_pos = i * Q_BLOCK + jnp.bitwise_and(rows, Q_BLOCK - 1)
    k_pos = start + cols - nctx
    valid = (cols < nctx) | (jnp.abs(q_pos - k_pos) <= WINDOW)
    contract_last = (((1,), (1,)), ((), ()))
    out = jnp.zeros((nrow, LANES), F32)
    for par in range(2):
        keep = low if par == 0 else jnp.logical_not(low)
        kh = jnp.where(keep, kall, zero)
        vh = jnp.where(keep, vall, zero)
        s = lax.dot_general(qs, kh, contract_last, preferred_element_type=F32)
        s = jnp.where(valid, s, NEG_INF)
        sink = jnp.concatenate(
            [jnp.full((Q_BLOCK, 1), sink_ref[kv * Q_PER_KV + 2 * j + par], F32) for j in range(npair)], axis=0)
        m = jnp.maximum(jnp.max(s, axis=1, keepdims=True), sink)
        p = jnp.exp(s - m)
        den = jnp.sum(p, axis=1, keepdims=True) + jnp.exp(sink - m)
        out = out + _bdot(p.astype(BF16), vh) / den
    for j in range(npair):
        o_ref[0, :, LANES * j:LANES * (j + 1)] = out[Q_BLOCK * j:Q_BLOCK * (j + 1), :].astype(BF16)


def _attention(q, k2, v2, kc2, vc2, sink, batch, seq, nctx):
    qd = q.shape[1]
    gw = Q_PER_KV * HEAD_DIM
    q3 = q.reshape(batch, seq, qd)
    k3, v3 = k2.reshape(batch, seq, -1), v2.reshape(batch, seq, -1)
    kc3, vc3 = kc2.reshape(batch, nctx, -1), vc2.reshape(batch, nctx, -1)
    full = lambda b, h, i: (b, 0, h)
    blk = lambda b, h, i: (b, i, h)
    out = pl.pallas_call(
        _attn_kernel,
        out_shape=jax.ShapeDtypeStruct((batch, seq, qd), BF16),
        grid=(batch, N_KV_HEADS, seq // Q_BLOCK),
        in_specs=[pl.BlockSpec(memory_space=pltpu.SMEM),
                  pl.BlockSpec((1, Q_BLOCK, gw), blk),
                  pl.BlockSpec((1, seq, LANES), full), pl.BlockSpec((1, seq, LANES), full),
                  pl.BlockSpec((1, nctx, LANES), full), pl.BlockSpec((1, nctx, LANES), full)],
        out_specs=pl.BlockSpec((1, Q_BLOCK, gw), blk),
        compiler_params=_params("parallel", "parallel", "parallel"),
        name="window_gqa",
    )(sink, q3, k3, v3, kc3, vc3)
    return out.reshape(batch * seq, qd)


def _split_bf16(a):
    hi = a.astype(BF16)
    return hi, (a - hi.astype(F32)).astype(BF16)


def _post_mixer(x, y, gm, lng, lnb, scf, shf, wrh_ref, wrl_ref, rb_ref, run_ref,
                x1_ref, f_ref, ids_ref, wts_ref, rank_ref, cnt_ref, alpha):
    tm = x.shape[0]
    x1 = _layernorm(alpha * x + gm * y, lng, lnb)
    x1_ref[...] = x1
    f = x1 * (1.0 + scf) + shf
    f_ref[...] = f

    fh, fl = _split_bf16(f)
    wh, wl = wrh_ref[...], wrl_ref[...]
    nt = (((1,), (1,)), ((), ()))
    dot_nt = lambda a, b: lax.dot_general(a, b, nt, preferred_element_type=F32)
    logits = dot_nt(wh, fh) + dot_nt(wh, fl) + dot_nt(wl, fh)
    scores = jax.nn.sigmoid(logits)
    biased = scores + rb_ref[...]

    eg = EXPERTS_PER_GROUP
    sub = lax.broadcasted_iota(I32, (eg, tm), 0)
    gscore = []
    for g in range(N_EXPERT_GROUPS):
        tg = biased[eg * g:eg * (g + 1), :]
        m1 = jnp.max(tg, axis=0, keepdims=True)
        i1 = jnp.min(jnp.where(tg == m1, sub, eg), axis=0, keepdims=True)
        m2 = jnp.max(jnp.where(sub == i1, -jnp.inf, tg), axis=0, keepdims=True)
        gscore.append(m1 + m2)
    gsel = [jnp.zeros((1, tm), jnp.bool_) for _ in range(N_EXPERT_GROUPS)]
    for _ in range(TOPK_GROUPS):
        best = functools.reduce(jnp.maximum, gscore)
        taken = jnp.zeros((1, tm), jnp.bool_)
        for g in range(N_EXPERT_GROUPS):
            hit = jnp.logical_and(gscore[g] == best, jnp.logical_not(taken))
            taken = jnp.logical_or(taken, hit)
            gsel[g] = jnp.logical_or(gsel[g], hit)
            gscore[g] = jnp.where(hit, -jnp.inf, gscore[g])
    cur = jnp.concatenate(
        [jnp.where(gsel[g], biased[eg * g:eg * (g + 1), :], -jnp.inf) for g in range(N_EXPERT_GROUPS)], axis=0)

    eidx = lax.broadcasted_iota(I32, (N_EXPERTS, tm), 0)
    picks, wts, hots = [], [], []
    for _ in range(TOP_K):
        m = jnp.max(cur, axis=0, keepdims=True)
        idx = jnp.min(jnp.where(cur == m, eidx, N_EXPERTS), axis=0, keepdims=True)
        hot = eidx == idx
        picks.append(idx)
        wts.append(jnp.sum(jnp.where(hot, scores, 0.0), axis=0, keepdims=True))
        hots.append(hot)
        cur = jnp.where(hot, -jnp.inf, cur)
    wsum = functools.reduce(jnp.add, wts)

    assigned = functools.reduce(jnp.add, [h.astype(F32) for h in hots])
    before = (lax.broadcasted_iota(I32, (tm, tm), 0) < lax.broadcasted_iota(I32, (tm, tm), 1)).astype(BF16)
    pos = _bdot(assigned.astype(BF16), before) + run_ref[:, 0:1]
    for k in range(TOP_K):
        ids_ref[k:k + 1, :] = picks[k]
        wts_ref[k:k + 1, :] = wts[k] / wsum * ROUTED_SCALE
        rank_ref[k:k + 1, :] = jnp.sum(jnp.where(hots[k], pos, 0.0), axis=0, keepdims=True).astype(I32)
    run_ref[...] = run_ref[...] + jnp.sum(assigned, axis=1, keepdims=True)
    cnt_ref[...] = run_ref[...]


def _route_out_shapes(t, d):
    return (jax.ShapeDtypeStruct((t, d), F32), jax.ShapeDtypeStruct((t, d), F32),
            jax.ShapeDtypeStruct((TOP_K, t), I32), jax.ShapeDtypeStruct((TOP_K, t), F32),
            jax.ShapeDtypeStruct((TOP_K, t), I32), jax.ShapeDtypeStruct((N_EXPERTS, LANES), F32))


def _route_out_specs(tm, d):
    row = lambda i: (i, 0)
    col = lambda i: (0, i)
    return (pl.BlockSpec((tm, d), row), pl.BlockSpec((tm, d), row),
            pl.BlockSpec((TOP_K, tm), col), pl.BlockSpec((TOP_K, tm), col), pl.BlockSpec((TOP_K, tm), col),
            pl.BlockSpec((N_EXPERTS, LANES), lambda i: (0, 0)))


def _route_in_specs(tm, d, per_b, layer):
    mod = lambda i: (i // per_b, 0, 0)
    lrow = lambda i: (layer, 0, 0)
    fix = lambda i: (0, 0)
    return [pl.BlockSpec((tm, d), lambda i: (i, 0)),
            pl.BlockSpec((1, 1, d), mod), pl.BlockSpec((1, 1, d), mod), pl.BlockSpec((1, 1, d), mod),
            pl.BlockSpec((1, 1, d), lrow), pl.BlockSpec((1, 1, d), lrow),
            pl.BlockSpec((N_EXPERTS, d), fix), pl.BlockSpec((N_EXPERTS, d), fix),
            pl.BlockSpec((N_EXPERTS, 1), fix)]


def _oproj_kernel(o_ref, wo_ref, x_ref, gm_ref, scf_ref, shf_ref, lng_ref, lnb_ref, wrh_ref, wrl_ref, rb_ref,
                  x1_ref, f_ref, ids_ref, wts_ref, rank_ref, cnt_ref, run_ref, *, alpha):
    @pl.when(pl.program_id(0) == 0)
    def _():
        run_ref[...] = jnp.zeros_like(run_ref)

    y = _bdot(o_ref[...], wo_ref[...])
    _post_mixer(x_ref[...], y, gm_ref[0], lng_ref[0], lnb_ref[0], scf_ref[0], shf_ref[0],
                wrh_ref, wrl_ref, rb_ref, run_ref, x1_ref, f_ref, ids_ref, wts_ref, rank_ref, cnt_ref, alpha)


def _attn_out(o, wo, x2, gm, scf, shf, lng, lnb, wrh, wrl, rb, seq, layer, alpha):
    t, d = x2.shape
    tm = min(256, seq)
    return pl.pallas_call(
        functools.partial(_oproj_kernel, alpha=alpha),
        out_shape=_route_out_shapes(t, d),
        grid=(t // tm,),
        in_specs=[pl.BlockSpec((tm, o.shape[1]), lambda i: (i, 0)), pl.BlockSpec(wo.shape, lambda i: (0, 0))]
        + _route_in_specs(tm, d, seq // tm, layer),
        out_specs=_route_out_specs(tm, d),
        scratch_shapes=[pltpu.VMEM((N_EXPERTS, LANES), F32)],
        compiler_params=_params("arbitrary"),
        name="attn_out_route",
    )(o, wo, x2, gm, scf, shf, lng, lnb, wrh, wrl, rb)


def _gmlp_in_kernel(x_ref, sc_ref, sh_ref, w_ref, b_ref, g_ref, beta_ref, o_ref, *, cw, normalize):
    h = (x_ref[...] * (1.0 + sc_ref[0]) + sh_ref[0]).astype(BF16)
    n = o_ref.shape[1]
    inv_sqrt2 = 0.7071067811865476
    parts = []
    for c0 in range(0, n, cw):
        z = _bdot(h, w_ref[:, c0:c0 + cw]) + b_ref[:, c0:c0 + cw]
        z = 0.5 * z * (1.0 + lax.erf(z * inv_sqrt2))
        if normalize:
            parts.append(z)
        else:
            o_ref[:, c0:c0 + cw] = z.astype(BF16)
    if normalize:
        v = jnp.concatenate(parts, axis=1)
        o_ref[...] = _layernorm(v, g_ref[...], beta_ref[...]).astype(BF16)


def _gmlp_in(x2, sc, sh, w, b, g, beta, seq, normalize):
    t, d = x2.shape
    n = w.shape[1]
    tm = min(256, seq)
    per_b = seq // tm
    row = lambda i: (i, 0)
    mod = lambda i: (i // per_b, 0, 0)
    fix = lambda i: (0, 0)
    return pl.pallas_call(
        functools.partial(_gmlp_in_kernel, cw=min(512, n), normalize=normalize),
        out_shape=jax.ShapeDtypeStruct((t, n), BF16),
        grid=(t // tm,),
        in_specs=[pl.BlockSpec((tm, d), row), pl.BlockSpec((1, 1, d), mod), pl.BlockSpec((1, 1, d), mod),
                  pl.BlockSpec(w.shape, fix), pl.BlockSpec((1, n), fix), pl.BlockSpec((1, n), fix),
                  pl.BlockSpec((1, n), fix)],
        out_specs=pl.BlockSpec((tm, n), row),
        compiler_params=_params("parallel"),
        name="gmlp_in_v" if normalize else "gmlp_in_u",
    )(x2, sc, sh, w, b, g, beta)


def _gmlp_out_kernel(u_ref, v_ref, ws_ref, bs_ref, wo_ref, x_ref, gm_ref, scf_ref, shf_ref, lng_ref, lnb_ref,
                     wrh_ref, wrl_ref, rb_ref, x1_ref, f_ref, ids_ref, wts_ref, rank_ref, cnt_ref,
                     run_ref, gated_ref, *, alpha):
    @pl.when(pl.program_id(0) == 0)
    def _():
        run_ref[...] = jnp.zeros_like(run_ref)

    tm, width = u_ref.shape
    gd = width // N_GMLP_GROUPS
    for r0 in range(0, tm, CHUNK):
        for g in range(N_GMLP_GROUPS):
            c0 = g * gd
            mixed = _bdot(ws_ref[g], v_ref[r0:r0 + CHUNK, c0:c0 + gd]) + bs_ref[g]
            gated_ref[r0:r0 + CHUNK, c0:c0 + gd] = (
                u_ref[r0:r0 + CHUNK, c0:c0 + gd].astype(F32) * mixed).astype(BF16)
    y = _bdot(gated_ref[...], wo_ref[...])
    _post_mixer(x_ref[...], y, gm_ref[0], lng_ref[0], lnb_ref[0], scf_ref[0], shf_ref[0],
                wrh_ref, wrl_ref, rb_ref, run_ref, x1_ref, f_ref, ids_ref, wts_ref, rank_ref, cnt_ref, alpha)


def _gmlp_out(u, v, ws, bs, wo, x2, gm, scf, shf, lng, lnb, wrh, wrl, rb, seq, layer, alpha):
    t, d = x2.shape
    width = u.shape[1]
    tm = min(256, seq)
    row = lambda i: (i, 0)
    return pl.pallas_call(
        functools.partial(_gmlp_out_kernel, alpha=alpha),
        out_shape=_route_out_shapes(t, d),
        grid=(t // tm,),
        in_specs=[pl.BlockSpec((tm, width), row), pl.BlockSpec((tm, width), row),
                  pl.BlockSpec(ws.shape, lambda i: (0, 0, 0)), pl.BlockSpec(bs.shape, lambda i: (0, 0, 0)),
                  pl.BlockSpec(wo.shape, lambda i: (0, 0))]
        + _route_in_specs(tm, d, seq // tm, layer),
        out_specs=_route_out_specs(tm, d),
        scratch_shapes=[pltpu.VMEM((N_EXPERTS, LANES), F32), pltpu.VMEM((tm, width), BF16)],
        compiler_params=_params("arbitrary"),
        name="gmlp_out_route",
    )(u, v, ws, bs, wo, x2, gm, scf, shf, lng, lnb, wrh, wrl, rb)


ZERO_ROWS = 64


def _row_copy(src_ref, src_row, dst_ref, dst_row, sem):
    return pltpu.make_async_copy(src_ref.at[pl.ds(src_row, 1), :], dst_ref.at[pl.ds(dst_row, 1), :], sem)


def _dispatch_kernel(ztile_ref, slots_ref, f_ref, ws1_ref, ws3_ref, ws2_ref, sh_ref, xs_ref,
                     zero_ref, sem, *, tme):
    tm = f_ref.shape[0]
    nz = tme // ZERO_ROWS

    def zero_copy(e, c):
        return pltpu.make_async_copy(
            zero_ref, xs_ref.at[pl.ds(pl.multiple_of(ztile_ref[e] + c * ZERO_ROWS, ZERO_ROWS), ZERO_ROWS), :], sem)

    @pl.when(pl.program_id(0) == 0)
    def _():
        zero_ref[...] = jnp.zeros_like(zero_ref)

        def each(fn):
            def body(e, carry):
                @pl.when(ztile_ref[e] >= 0)
                def _():
                    for c in range(nz):
                        fn(zero_copy(e, c))
                return carry
            lax.fori_loop(0, N_EXPERTS, body, 0)

        each(lambda cp: cp.start())
        each(lambda cp: cp.wait())

    def start(t, carry):
        for k in range(TOP_K):
            _row_copy(f_ref, t, xs_ref, slots_ref[k, t], sem).start()
        return carry

    lax.fori_loop(0, tm, start, 0)

    h = f_ref[...].astype(BF16)
    a = _silu(_bdot(h, ws1_ref[...])) * _bdot(h, ws3_ref[...])
    sh_ref[...] = _bdot(a.astype(BF16), ws2_ref[...])

    def wait(t, carry):
        for k in range(TOP_K):
            _row_copy(f_ref, t, xs_ref, slots_ref[k, t], sem).wait()
        return carry

    lax.fori_loop(0, tm, wait, 0)


def _dispatch(ztile, slots, f, ws1, ws3, ws2, nslot, tme):
    t, d = f.shape
    tm = min(256, t)
    fix = lambda i: (0, 0)
    return pl.pallas_call(
        functools.partial(_dispatch_kernel, tme=tme),
        out_shape=(jax.ShapeDtypeStruct((t, d), F32), jax.ShapeDtypeStruct((nslot, d), F32)),
        grid=(t // tm,),
        in_specs=[pl.BlockSpec(memory_space=pltpu.SMEM),
                  pl.BlockSpec((TOP_K, tm), lambda i: (0, i), memory_space=pltpu.SMEM),
                  pl.BlockSpec((tm, d), lambda i: (i, 0)),
                  pl.BlockSpec(ws1.shape, fix), pl.BlockSpec(ws3.shape, fix), pl.BlockSpec(ws2.shape, fix)],
        out_specs=(pl.BlockSpec((tm, d), lambda i: (i, 0)), pl.BlockSpec(memory_space=pl.ANY)),
        scratch_shapes=[pltpu.VMEM((ZERO_ROWS, d), F32), pltpu.SemaphoreType.DMA(())],
        compiler_params=_params("arbitrary"),
        name="moe_dispatch_shared",
    )(ztile, slots, f, ws1, ws3, ws2)


def _experts_kernel(texp_ref, tsrc_ref, xs_ref, w1_ref, w3_ref, w2_ref, y_ref, w1b, w3b, w2b):
    i = pl.program_id(0)
    changed = jnp.logical_or(i == 0, texp_ref[i] != texp_ref[jnp.maximum(i - 1, 0)])

    @pl.when(changed)
    def _():
        w1b[...] = w1_ref[...].astype(BF16)
        w3b[...] = w3_ref[...].astype(BF16)
        w2b[...] = w2_ref[...].astype(BF16)

    h = xs_ref[...].astype(BF16)
    a = _silu(_bdot(h, w1b[...])) * _bdot(h, w3b[...])
    y_ref[...] = _bdot(a.astype(BF16), w2b[...])


def _experts(texp, tsrc, xs, w1, w3, w2, layer, tme):
    nslot, d = xs.shape
    fdim = w1.shape[-1]
    ntile = nslot // tme
    rows = lambda i, te, ts: (ts[i], 0)
    wsel = lambda i, te, ts: (layer, te[i], 0, 0)
    return pl.pallas_call(
        _experts_kernel,
        out_shape=jax.ShapeDtypeStruct((nslot, d), F32),
        grid_spec=pltpu.PrefetchScalarGridSpec(
            num_scalar_prefetch=2,
            grid=(ntile,),
            in_specs=[pl.BlockSpec((tme, d), rows),
                      pl.BlockSpec((None, None, d, fdim), wsel), pl.BlockSpec((None, None, d, fdim), wsel),
                      pl.BlockSpec((None, None, fdim, d), wsel)],
            out_specs=pl.BlockSpec((tme, d), rows),
            scratch_shapes=[pltpu.VMEM((d, fdim), BF16), pltpu.VMEM((d, fdim), BF16), pltpu.VMEM((fdim, d), BF16)]),
        compiler_params=_params("arbitrary"),
        name="moe_experts",
    )(texp, tsrc, xs, w1, w3, w2)


def _combine_kernel(slots_ref, wt_ref, x1_ref, sh_ref, gf_ref, lng_ref, lnb_ref, ys_ref, o_ref, buf, sem, *, alpha):
    tm = x1_ref.shape[0]

    def start(t, carry):
        for k in range(TOP_K):
            _row_copy(ys_ref, slots_ref[k, t], buf.at[k], t, sem).start()
        return carry

    lax.fori_loop(0, tm, start, 0)

    def wait(t, carry):
        for k in range(TOP_K):
            _row_copy(ys_ref, slots_ref[k, t], buf.at[k], t, sem).wait()
        return carry

    lax.fori_loop(0, tm, wait, 0)

    f = sh_ref[...]
    for k in range(TOP_K):
        f = f + wt_ref[:, k:k + 1] * buf[k]
    o_ref[...] = _layernorm(alpha * x1_ref[...] + gf_ref[0] * f, lng_ref[0], lnb_ref[0])


def _combine(slots, wt, x1, sh, gf, lng, lnb, ys, seq, layer, alpha):
    t, d = x1.shape
    tm = min(128, seq)
    per_b = seq // tm
    row = lambda i: (i, 0)
    lrow = lambda i: (layer, 0, 0)
    return pl.pallas_call(
        functools.partial(_combine_kernel, alpha=alpha),
        out_shape=jax.ShapeDtypeStruct((t, d), F32),
        grid=(t // tm,),
        in_specs=[pl.BlockSpec((TOP_K, tm), lambda i: (0, i), memory_space=pltpu.SMEM),
                  pl.BlockSpec((tm, TOP_K), row), pl.BlockSpec((tm, d), row), pl.BlockSpec((tm, d), row),
                  pl.BlockSpec((1, 1, d), lambda i: (i // per_b, 0, 0)),
                  pl.BlockSpec((1, 1, d), lrow), pl.BlockSpec((1, 1, d), lrow),
                  pl.BlockSpec(memory_space=pl.ANY)],
        out_specs=pl.BlockSpec((tm, d), row),
        scratch_shapes=[pltpu.VMEM((TOP_K, tm, d), F32), pltpu.SemaphoreType.DMA(())],
        compiler_params=_params("arbitrary"),
        name="moe_combine_ln",
    )(slots, wt, x1, sh, gf, lng, lnb, ys)


def _moe(f, x1, ids, wts, rank, cnt, gf, lng, lnb, w1, w3, w2, ws1, ws3, ws2, seq, layer, alpha):
    t, d = f.shape
    tme = min(512, t)
    counts = cnt[:, 0].astype(I32)
    tiles_e = (counts + tme - 1) // tme
    padded = tiles_e * tme
    ends = jnp.cumsum(padded)
    base = ends - padded
    slots = jnp.take(base, ids) + rank
    ntile = (t * TOP_K) // tme + N_EXPERTS
    tile_ends = jnp.cumsum(tiles_e)
    nt = tile_ends[-1]
    tsrc = jnp.minimum(jnp.arange(ntile, dtype=I32), nt - 1)
    texp = jnp.minimum(jnp.searchsorted(tile_ends, tsrc, side="right").astype(I32), N_EXPERTS - 1)
    ztile = jnp.where(padded > 0, ends - tme, -1).astype(I32)

    shared, xs = _dispatch(ztile, slots, f, ws1, ws3, ws2, ntile * tme, tme)
    ys = _experts(texp, tsrc, xs, w1, w3, w2, layer, tme)
    return _combine(slots, wts.T, x1, shared, gf, lng, lnb, ys, seq, layer, alpha)


def kernel(x, c, ctx, c_ctx, w_ada, b_ada, ln_mix_g, ln_mix_b, ln_ffn_g, ln_ffn_b, attn_w_qkv, attn_w_o, attn_sink, gmlp_w_in, gmlp_b_in, gmlp_v_g, gmlp_v_b, gmlp_w_s, gmlp_b_s, gmlp_w_o, moe_w_router, moe_bias, moe_w1, moe_w3, moe_w2, moe_ws1, moe_ws3, moe_ws2):
    batch, seq, d = x.shape
    nctx = ctx.shape[1]
    depth = w_ada.shape[0]
    assert depth == 2, "layer 0 is the attention mixer, layer 1 the gMLP mixer"
    alpha = float((2 * depth) ** 0.25)
    t = batch * seq

    pad = (-(batch + 1)) % 8
    cc = jnp.concatenate([c, c_ctx[None, :], jnp.zeros((pad, d), F32)], axis=0)
    mods = _modulations(cc, w_ada, b_ada)

    def mod(layer, j, rows=slice(0, batch)):
        return mods[layer, rows, j * d:(j + 1) * d]

    def per_batch(layer, j):
        return mod(layer, j).reshape(batch, 1, d)

    def router(layer):
        wt = moe_w_router[layer].T
        hi = wt.astype(BF16)
        return hi, (wt - hi.astype(F32)).astype(BF16), moe_bias[layer].reshape(N_EXPERTS, 1)

    def shared_w(layer):
        return moe_ws1[layer].astype(BF16), moe_ws3[layer].astype(BF16), moe_ws2[layer].astype(BF16)

    x2 = x.reshape(t, d)
    ln_mix_g, ln_mix_b, ln_ffn_g, ln_ffn_b = (
        p.reshape(depth, 1, d) for p in (ln_mix_g, ln_mix_b, ln_ffn_g, ln_ffn_b))

    qd = Q_PER_KV * N_KV_HEADS * HEAD_DIM
    kvd = N_KV_HEADS * HEAD_DIM
    wqkv = attn_w_qkv[0]
    dup = lambda w: jnp.concatenate([w.reshape(d, N_KV_HEADS, 1, HEAD_DIM)] * (LANES // HEAD_DIM), axis=2
                                    ).reshape(d, N_KV_HEADS * LANES)
    wk2, wv2 = dup(wqkv[:, qd:qd + kvd]), dup(wqkv[:, qd + kvd:])
    w_all = jnp.concatenate([wqkv[:, :qd] * (HEAD_DIM ** -0.5), wk2, wv2], axis=1).astype(BF16)
    q, k2, v2 = _qkv_proj(x2, per_batch(0, 1), per_batch(0, 0), w_all, _rope_tables(seq), seq)
    ctx_row = slice(batch, batch + 1)
    kc2, vc2 = _ctx_kv(ctx.reshape(batch * nctx, d), mod(0, 1, ctx_row), mod(0, 0, ctx_row),
                       jnp.concatenate([wk2, wv2], axis=1).astype(BF16))
    o = _attention(q, k2, v2, kc2, vc2, attn_sink[0], batch, seq, nctx)
    x1, f, ids, wts, rank, cnt = _attn_out(
        o, attn_w_o[0].astype(BF16), x2, per_batch(0, 2), per_batch(0, 4), per_batch(0, 3),
        ln_mix_g, ln_mix_b, *router(0), seq, 0, alpha)
    x2 = _moe(f, x1, ids, wts, rank, cnt, per_batch(0, 5), ln_ffn_g, ln_ffn_b,
              moe_w1, moe_w3, moe_w2, *shared_w(0), seq, 0, alpha)

    width = gmlp_w_in.shape[2] // 2
    w_in = gmlp_w_in[0].astype(BF16)
    b_in = gmlp_b_in[0].reshape(1, 2 * width)
    vg, vb = gmlp_v_g[0].reshape(1, width), gmlp_v_b[0].reshape(1, width)
    sc, sh = per_batch(1, 1), per_batch(1, 0)
    u = _gmlp_in(x2, sc, sh, w_in[:, :width], b_in[:, :width], vg, vb, seq, False)
    v = _gmlp_in(x2, sc, sh, w_in[:, width:], b_in[:, width:], vg, vb, seq, True)
    x1, f, ids, wts, rank, cnt = _gmlp_out(
        u, v, gmlp_w_s[0].astype(BF16), gmlp_b_s[0][:, :, None], gmlp_w_o[0].astype(BF16), x2,
        per_batch(1, 2), per_batch(1, 4), per_batch(1, 3), ln_mix_g, ln_mix_b, *router(1), seq, 1, alpha)
    x2 = _moe(f, x1, ids, wts, rank, cnt, per_batch(1, 5), ln_ffn_g, ln_ffn_b,
              moe_w1, moe_w3, moe_w2, *shared_w(1), seq, 1, alpha)
    return x2.reshape(batch, seq, d)
```

```python
import functools

import jax
import jax.numpy as jnp
from jax import lax
from jax.experimental import pallas as pl
from jax.experimental.pallas import tpu as pltpu

F32 = jnp.float32
BF16 = jnp.bfloat16
I32 = jnp.int32

GRID_W = 64
N_KV_HEADS = 4
Q_PER_KV = 8
HEAD_DIM = 64
ROPE_HALF = 16
WINDOW = 128
Q_BLOCK = 128
ROPE_BASE = 10000.0
NEG_INF = -1e30
CHUNK = 128
N_GMLP_GROUPS = 8
N_EXPERTS = 64
TOP_K = 8
N_EXPERT_GROUPS = 8
EXPERTS_PER_GROUP = 8
TOPK_GROUPS = 4
ROUTED_SCALE = 2.5
LN_EPS = 1e-5

LANES = 128
VMEM_LIMIT_BYTES = 56 * 1024 * 1024


def _params(*sem):
    return pltpu.CompilerParams(dimension_semantics=sem, vmem_limit_bytes=VMEM_LIMIT_BYTES)


def _silu(a):
    return a * jax.nn.sigmoid(a)


def _layernorm(v, g, b):
    mu = jnp.mean(v, axis=-1, keepdims=True)
    d = v - mu
    var = jnp.mean(d * d, axis=-1, keepdims=True)
    return d * lax.rsqrt(var + LN_EPS) * g + b


def _bdot(a, b):
    return jnp.dot(a, b, preferred_element_type=F32)


U32 = jnp.uint32


def _pack_pairs(a):
    n = a.shape[1] // 2
    lo = lax.bitcast_convert_type(a[:, :n].astype(BF16).astype(F32), U32)
    hi = lax.bitcast_convert_type(a[:, n:].astype(BF16).astype(F32), U32)
    return hi | (lo >> 16)


def _unpack_pairs(p):
    lo = lax.bitcast_convert_type(p << 16, F32)
    hi = lax.bitcast_convert_type(p & jnp.uint32(0xFFFF0000), F32)
    return lo, hi


def _unpack_rows(p):
    lo, hi = _unpack_pairs(p)
    return jnp.concatenate([lo, hi], axis=1).astype(BF16)


def _mod_kernel(c_ref, w_ref, b_ref, o_ref):
    a = _silu(c_ref[...]).astype(BF16)
    o_ref[0] = _bdot(a, w_ref[0].astype(BF16)) + b_ref[0]


def _modulations(cc, w_ada, b_ada):
    depth, d, n6 = w_ada.shape
    r = cc.shape[0]
    tn = max(w for w in range(LANES, min(1024, n6) + 1, LANES) if n6 % w == 0)
    return pl.pallas_call(
        _mod_kernel,
        out_shape=jax.ShapeDtypeStruct((depth, r, n6), F32),
        grid=(depth, n6 // tn),
        in_specs=[pl.BlockSpec((r, d), lambda l, j: (0, 0)),
                  pl.BlockSpec((1, d, tn), lambda l, j: (l, 0, j)),
                  pl.BlockSpec((1, 1, tn), lambda l, j: (l, 0, j))],
        out_specs=pl.BlockSpec((1, r, tn), lambda l, j: (l, 0, j)),
        compiler_params=_params("parallel", "parallel"),
        name="adaln_mod",
    )(cc, w_ada, b_ada.reshape(depth, 1, n6))


def _rope_tables(seq):
    rows = seq // GRID_W
    row_ids = jnp.repeat(jnp.arange(rows, dtype=F32), GRID_W)
    col_ids = jnp.tile(jnp.arange(GRID_W, dtype=F32), rows)
    inv_freq = ROPE_BASE ** (-jnp.arange(0, 2 * ROPE_HALF, 2, dtype=F32) / (2 * ROPE_HALF))
    ar, ac = row_ids[:, None] * inv_freq, col_ids[:, None] * inv_freq
    z = jnp.zeros_like(ar)
    cos = jnp.concatenate([jnp.cos(ar), jnp.cos(ar), jnp.cos(ac), jnp.cos(ac)], axis=-1)
    sin_lo = jnp.concatenate([-jnp.sin(ar), z, -jnp.sin(ac), z], axis=-1)
    sin_hi = jnp.concatenate([z, jnp.sin(ar), z, jnp.sin(ac)], axis=-1)
    rep = LANES // HEAD_DIM
    return tuple(jnp.tile(t, (1, rep)) for t in (cos, sin_lo, sin_hi))


def _qkv_kernel(x_ref, sc_ref, sh_ref, w_ref, cos_ref, slo_ref, shi_ref, q_ref, k_ref, v_ref, *, cw):
    h = (x_ref[...] * (1.0 + sc_ref[0]) + sh_ref[0]).astype(BF16)
    cos, slo, shi = cos_ref[...], slo_ref[...], shi_ref[...]

    def rope(a):
        return (a * cos + pltpu.roll(a, LANES - ROPE_HALF, 1) * slo
                + pltpu.roll(a, ROPE_HALF, 1) * shi)

    qd, kd = q_ref.shape[1], k_ref.shape[1]
    for c0 in range(0, qd, cw):
        acc = _bdot(h, w_ref[:, c0:c0 + cw])
        for b0 in range(0, cw, LANES):
            q_ref[:, c0 + b0:c0 + b0 + LANES] = rope(acc[:, b0:b0 + LANES]).astype(BF16)
    acc = _bdot(h, w_ref[:, qd:qd + kd])
    for b0 in range(0, kd, LANES):
        k_ref[:, b0:b0 + LANES] = rope(acc[:, b0:b0 + LANES]).astype(BF16)
    v_ref[...] = _bdot(h, w_ref[:, qd + kd:qd + 2 * kd]).astype(BF16)


def _qkv_proj(x2, sc, sh, w, tables, seq):
    t, d = x2.shape
    kd = N_KV_HEADS * LANES
    qd = w.shape[1] - 2 * kd
    tm = min(512, seq)
    per_b = seq // tm
    row = lambda i: (i, 0)
    mod = lambda i: (i // per_b, 0, 0)
    tab = lambda i: (i % per_b, 0)
    return pl.pallas_call(
        functools.partial(_qkv_kernel, cw=min(512, qd)),
        out_shape=(jax.ShapeDtypeStruct((t, qd), BF16), jax.ShapeDtypeStruct((t, kd), BF16),
                   jax.ShapeDtypeStruct((t, kd), BF16)),
        grid=(t // tm,),
        in_specs=[pl.BlockSpec((tm, d), row), pl.BlockSpec((1, 1, d), mod), pl.BlockSpec((1, 1, d), mod),
                  pl.BlockSpec(w.shape, lambda i: (0, 0)),
                  pl.BlockSpec((tm, LANES), tab), pl.BlockSpec((tm, LANES), tab), pl.BlockSpec((tm, LANES), tab)],
        out_specs=(pl.BlockSpec((tm, qd), row), pl.BlockSpec((tm, kd), row), pl.BlockSpec((tm, kd), row)),
        compiler_params=_params("parallel"),
        name="qkv_rope",
    )(x2, sc, sh, w, *tables)


def _ctxkv_kernel(x_ref, sc_ref, sh_ref, w_ref, k_ref, v_ref):
    h = (x_ref[...] * (1.0 + sc_ref[...]) + sh_ref[...]).astype(BF16)
    kd = k_ref.shape[1]
    k_ref[...] = _bdot(h, w_ref[:, :kd]).astype(BF16)
    v_ref[...] = _bdot(h, w_ref[:, kd:]).astype(BF16)


def _ctx_kv(c2, sc, sh, w):
    t, d = c2.shape
    kd = w.shape[1] // 2
    tm = min(512, t)
    row = lambda i: (i, 0)
    fix = lambda i: (0, 0)
    return pl.pallas_call(
        _ctxkv_kernel,
        out_shape=(jax.ShapeDtypeStruct((t, kd), BF16), jax.ShapeDtypeStruct((t, kd), BF16)),
        grid=(t // tm,),
        in_specs=[pl.BlockSpec((tm, d), row), pl.BlockSpec((1, d), fix), pl.BlockSpec((1, d), fix),
                  pl.BlockSpec(w.shape, fix)],
        out_specs=(pl.BlockSpec((tm, kd), row), pl.BlockSpec((tm, kd), row)),
        compiler_params=_params("parallel"),
        name="ctx_kv",
    )(c2, sc, sh, w)


def _attn_kernel(sink_ref, q_ref, k_ref, v_ref, kc_ref, vc_ref, o_ref):
    kv, i = pl.program_id(1), pl.program_id(2)
    seq, nctx = k_ref.shape[1], kc_ref.shape[1]
    nwin = Q_BLOCK + 2 * WINDOW
    start = pl.multiple_of(jnp.clip(i * Q_BLOCK - WINDOW, 0, seq - nwin), Q_BLOCK)
    nkey = nctx + nwin
    kall = jnp.concatenate([kc_ref[0], k_ref[0, pl.ds(start, nwin), :]], axis=0)
    vall = jnp.concatenate([vc_ref[0], v_ref[0, pl.ds(start, nwin), :]], axis=0)
    low = lax.broadcasted_iota(I32, (nkey, LANES), 1) < HEAD_DIM
    zero = jnp.zeros((nkey, LANES), BF16)
    npair = Q_PER_KV // 2
    q = q_ref[0]
    qs = jnp.concatenate([q[:, LANES * j:LANES * (j + 1)] for j in range(npair)], axis=0)
    nrow = npair * Q_BLOCK
    rows = lax.broadcasted_iota(I32, (Q_BLOCK, nkey), 0)
    cols = lax.broadcasted_iota(I32, (Q_BLOCK, nkey), 1)
    dist = (i * Q_BLOCK - start + nctx) + rows - cols
    valid = (cols < nctx) | (jnp.abs(dist) <= WINDOW)
    contract_last = (((1,), (1,)), ((), ()))
    out = jnp.zeros((nrow, LANES), F32)
    for par in range(2):
        keep = low if par == 0 else jnp.logical_not(low)
        kh = jnp.where(keep, kall, zero)
        vh = jnp.where(keep, vall, zero)
        s = lax.dot_general(qs, kh, contract_last, preferred_element_type=F32)
        s = jnp.concatenate(
            [jnp.where(valid, s[Q_BLOCK * j:Q_BLOCK * (j + 1), :], NEG_INF) for j in range(npair)], axis=0)
        sink = jnp.concatenate(
            [jnp.full((Q_BLOCK, 1), sink_ref[kv * Q_PER_KV + 2 * j + par], F32) for j in range(npair)], axis=0)
        m = jnp.maximum(jnp.max(s, axis=1, keepdims=True), sink)
        p = jnp.exp(s - m)
        den = jnp.sum(p, axis=1, keepdims=True) + jnp.exp(sink - m)
        out = out + _bdot(p.astype(BF16), vh) / den
    for j in range(npair):
        o_ref[0, :, LANES * j:LANES * (j + 1)] = out[Q_BLOCK * j:Q_BLOCK * (j + 1), :].astype(BF16)


def _attention(q, k2, v2, kc2, vc2, sink, batch, seq, nctx):
    qd = q.shape[1]
    gw = Q_PER_KV * HEAD_DIM
    q3 = q.reshape(batch, seq, qd)
    k3, v3 = k2.reshape(batch, seq, -1), v2.reshape(batch, seq, -1)
    kc3, vc3 = kc2.reshape(batch, nctx, -1), vc2.reshape(batch, nctx, -1)
    full = lambda b, h, i: (b, 0, h)
    blk = lambda b, h, i: (b, i, h)
    out = pl.pallas_call(
        _attn_kernel,
        out_shape=jax.ShapeDtypeStruct((batch, seq, qd), BF16),
        grid=(batch, N_KV_HEADS, seq // Q_BLOCK),
        in_specs=[pl.BlockSpec(memory_space=pltpu.SMEM),
                  pl.BlockSpec((1, Q_BLOCK, gw), blk),
                  pl.BlockSpec((1, seq, LANES), full), pl.BlockSpec((1, seq, LANES), full),
                  pl.BlockSpec((1, nctx, LANES), full), pl.BlockSpec((1, nctx, LANES), full)],
        out_specs=pl.BlockSpec((1, Q_BLOCK, gw), blk),
        compiler_params=_params("parallel", "parallel", "parallel"),
        name="window_gqa",
    )(sink, q3, k3, v3, kc3, vc3)
    return out.reshape(batch * seq, qd)


def _split_bf16(a):
    hi = a.astype(BF16)
    return hi, (a - hi.astype(F32)).astype(BF16)


def _post_mixer(x, y, gm, lng, lnb, scf, shf, wrh_ref, wrl_ref, rb_ref, run_ref,
                x1_ref, f_ref, ids_ref, wts_ref, rank_ref, cnt_ref, alpha):
    tm = x.shape[0]
    x1 = _layernorm(alpha * x + gm * y, lng, lnb)
    x1_ref[...] = x1
    f = x1 * (1.0 + scf) + shf
    f_ref[...] = _pack_pairs(f)

    fh, fl = _split_bf16(f)
    wh, wl = wrh_ref[...], wrl_ref[...]
    nt = (((1,), (1,)), ((), ()))
    dot_nt = lambda a, b: lax.dot_general(a, b, nt, preferred_element_type=F32)
    logits = dot_nt(wh, fh) + dot_nt(wh, fl) + dot_nt(wl, fh)
    scores = jax.nn.sigmoid(logits)
    biased = scores + rb_ref[...]

    eg = EXPERTS_PER_GROUP
    sub = lax.broadcasted_iota(I32, (eg, tm), 0)
    gscore = []
    for g in range(N_EXPERT_GROUPS):
        tg = biased[eg * g:eg * (g + 1), :]
        m1 = jnp.max(tg, axis=0, keepdims=True)
        i1 = jnp.min(jnp.where(tg == m1, sub, eg), axis=0, keepdims=True)
        m2 = jnp.max(jnp.where(sub == i1, -jnp.inf, tg), axis=0, keepdims=True)
        gscore.append(m1 + m2)
    gsel = [jnp.zeros((1, tm), jnp.bool_) for _ in range(N_EXPERT_GROUPS)]
    for _ in range(TOPK_GROUPS):
        best = functools.reduce(jnp.maximum, gscore)
        taken = jnp.zeros((1, tm), jnp.bool_)
        for g in range(N_EXPERT_GROUPS):
            hit = jnp.logical_and(gscore[g] == best, jnp.logical_not(taken))
            taken = jnp.logical_or(taken, hit)
            gsel[g] = jnp.logical_or(gsel[g], hit)
            gscore[g] = jnp.where(hit, -jnp.inf, gscore[g])
    cur = jnp.concatenate(
        [jnp.where(gsel[g], biased[eg * g:eg * (g + 1), :], -jnp.inf) for g in range(N_EXPERT_GROUPS)], axis=0)

    eidx = lax.broadcasted_iota(I32, (N_EXPERTS, tm), 0)
    picks, wts, hots = [], [], []
    for _ in range(TOP_K):
        m = jnp.max(cur, axis=0, keepdims=True)
        idx = jnp.min(jnp.where(cur == m, eidx, N_EXPERTS), axis=0, keepdims=True)
        hot = eidx == idx
        picks.append(idx)
        wts.append(jnp.sum(jnp.where(hot, scores, 0.0), axis=0, keepdims=True))
        hots.append(hot)
        cur = jnp.where(hot, -jnp.inf, cur)
    wsum = functools.reduce(jnp.add, wts)

    assigned = functools.reduce(jnp.add, [h.astype(F32) for h in hots])
    before = (lax.broadcasted_iota(I32, (tm, tm), 0) < lax.broadcasted_iota(I32, (tm, tm), 1)).astype(BF16)
    pos = _bdot(assigned.astype(BF16), before) + run_ref[:, 0:1]
    for k in range(TOP_K):
        ids_ref[k:k + 1, :] = picks[k]
        wts_ref[k:k + 1, :] = wts[k] / wsum * ROUTED_SCALE
        rank_ref[k:k + 1, :] = jnp.sum(jnp.where(hots[k], pos, 0.0), axis=0, keepdims=True).astype(I32)
    run_ref[...] = run_ref[...] + jnp.sum(assigned, axis=1, keepdims=True)
    cnt_ref[...] = run_ref[...]


def _route_out_shapes(t, d):
    return (jax.ShapeDtypeStruct((t, d), F32), jax.ShapeDtypeStruct((t, d // 2), U32),
            jax.ShapeDtypeStruct((TOP_K, t), I32), jax.ShapeDtypeStruct((TOP_K, t), F32),
            jax.ShapeDtypeStruct((TOP_K, t), I32), jax.ShapeDtypeStruct((N_EXPERTS, LANES), F32))


def _route_out_specs(tm, d):
    row = lambda i: (i, 0)
    col = lambda i: (0, i)
    return (pl.BlockSpec((tm, d), row), pl.BlockSpec((tm, d // 2), row),
            pl.BlockSpec((TOP_K, tm), col), pl.BlockSpec((TOP_K, tm), col), pl.BlockSpec((TOP_K, tm), col),
            pl.BlockSpec((N_EXPERTS, LANES), lambda i: (0, 0)))


def _route_in_specs(tm, d, per_b, layer):
    mod = lambda i: (i // per_b, 0, 0)
    lrow = lambda i: (layer, 0, 0)
    fix = lambda i: (0, 0)
    return [pl.BlockSpec((tm, d), lambda i: (i, 0)),
            pl.BlockSpec((1, 1, d), mod), pl.BlockSpec((1, 1, d), mod), pl.BlockSpec((1, 1, d), mod),
            pl.BlockSpec((1, 1, d), lrow), pl.BlockSpec((1, 1, d), lrow),
            pl.BlockSpec((N_EXPERTS, d), fix), pl.BlockSpec((N_EXPERTS, d), fix),
            pl.BlockSpec((N_EXPERTS, 1), fix)]


def _oproj_kernel(o_ref, wo_ref, x_ref, gm_ref, scf_ref, shf_ref, lng_ref, lnb_ref, wrh_ref, wrl_ref, rb_ref,
                  x1_ref, f_ref, ids_ref, wts_ref, rank_ref, cnt_ref, run_ref, *, alpha):
    @pl.when(pl.program_id(0) == 0)
    def _():
        run_ref[...] = jnp.zeros_like(run_ref)

    y = _bdot(o_ref[...], wo_ref[...])
    _post_mixer(x_ref[...], y, gm_ref[0], lng_ref[0], lnb_ref[0], scf_ref[0], shf_ref[0],
                wrh_ref, wrl_ref, rb_ref, run_ref, x1_ref, f_ref, ids_ref, wts_ref, rank_ref, cnt_ref, alpha)


def _attn_out(o, wo, x2, gm, scf, shf, lng, lnb, wrh, wrl, rb, seq, layer, alpha):
    t, d = x2.shape
    tm = min(256, seq)
    return pl.pallas_call(
        functools.partial(_oproj_kernel, alpha=alpha),
        out_shape=_route_out_shapes(t, d),
        grid=(t // tm,),
        in_specs=[pl.BlockSpec((tm, o.shape[1]), lambda i: (i, 0)), pl.BlockSpec(wo.shape, lambda i: (0, 0))]
        + _route_in_specs(tm, d, seq // tm, layer),
        out_specs=_route_out_specs(tm, d),
        scratch_shapes=[pltpu.VMEM((N_EXPERTS, LANES), F32)],
        compiler_params=_params("arbitrary"),
        name="attn_out_route",
    )(o, wo, x2, gm, scf, shf, lng, lnb, wrh, wrl, rb)


def _gmlp_in_kernel(x_ref, sc_ref, sh_ref, w_ref, b_ref, g_ref, beta_ref, o_ref, *, cw, normalize):
    h = (x_ref[...] * (1.0 + sc_ref[0]) + sh_ref[0]).astype(BF16)
    n = o_ref.shape[1]
    inv_sqrt2 = 0.7071067811865476
    parts = []
    for c0 in range(0, n, cw):
        z = _bdot(h, w_ref[:, c0:c0 + cw]) + b_ref[:, c0:c0 + cw]
        z = 0.5 * z * (1.0 + lax.erf(z * inv_sqrt2))
        if normalize:
            parts.append(z)
        else:
            o_ref[:, c0:c0 + cw] = z.astype(BF16)
    if normalize:
        v = jnp.concatenate(parts, axis=1)
        o_ref[...] = _layernorm(v, g_ref[...], beta_ref[...]).astype(BF16)


def _gmlp_in(x2, sc, sh, w, b, g, beta, seq, normalize):
    t, d = x2.shape
    n = w.shape[1]
    tm = min(256, seq)
    per_b = seq // tm
    row = lambda i: (i, 0)
    mod = lambda i: (i // per_b, 0, 0)
    fix = lambda i: (0, 0)
    return pl.pallas_call(
        functools.partial(_gmlp_in_kernel, cw=min(512, n), normalize=normalize),
        out_shape=jax.ShapeDtypeStruct((t, n), BF16),
        grid=(t // tm,),
        in_specs=[pl.BlockSpec((tm, d), row), pl.BlockSpec((1, 1, d), mod), pl.BlockSpec((1, 1, d), mod),
                  pl.BlockSpec(w.shape, fix), pl.BlockSpec((1, n), fix), pl.BlockSpec((1, n), fix),
                  pl.BlockSpec((1, n), fix)],
        out_specs=pl.BlockSpec((tm, n), row),
        compiler_params=_params("parallel"),
        name="gmlp_in_v" if normalize else "gmlp_in_u",
    )(x2, sc, sh, w, b, g, beta)


def _gmlp_out_kernel(u_ref, v_ref, ws_ref, bs_ref, wo_ref, x_ref, gm_ref, scf_ref, shf_ref, lng_ref, lnb_ref,
                     wrh_ref, wrl_ref, rb_ref, x1_ref, f_ref, ids_ref, wts_ref, rank_ref, cnt_ref,
                     run_ref, gated_ref, *, alpha):
    @pl.when(pl.program_id(0) == 0)
    def _():
        run_ref[...] = jnp.zeros_like(run_ref)

    tm, width = u_ref.shape
    gd = width // N_GMLP_GROUPS
    for r0 in range(0, tm, CHUNK):
        for g in range(N_GMLP_GROUPS):
            c0 = g * gd
            mixed = _bdot(ws_ref[g], v_ref[r0:r0 + CHUNK, c0:c0 + gd]) + bs_ref[g]
            gated_ref[r0:r0 + CHUNK, c0:c0 + gd] = (
                u_ref[r0:r0 + CHUNK, c0:c0 + gd].astype(F32) * mixed).astype(BF16)
    y = _bdot(gated_ref[...], wo_ref[...])
    _post_mixer(x_ref[...], y, gm_ref[0], lng_ref[0], lnb_ref[0], scf_ref[0], shf_ref[0],
                wrh_ref, wrl_ref, rb_ref, run_ref, x1_ref, f_ref, ids_ref, wts_ref, rank_ref, cnt_ref, alpha)


def _gmlp_out(u, v, ws, bs, wo, x2, gm, scf, shf, lng, lnb, wrh, wrl, rb, seq, layer, alpha):
    t, d = x2.shape
    width = u.shape[1]
    tm = min(256, seq)
    row = lambda i: (i, 0)
    return pl.pallas_call(
        functools.partial(_gmlp_out_kernel, alpha=alpha),
        out_shape=_route_out_shapes(t, d),
        grid=(t // tm,),
        in_specs=[pl.BlockSpec((tm, width), row), pl.BlockSpec((tm, width), row),
                  pl.BlockSpec(ws.shape, lambda i: (0, 0, 0)), pl.BlockSpec(bs.shape, lambda i: (0, 0, 0)),
                  pl.BlockSpec(wo.shape, lambda i: (0, 0))]
        + _route_in_specs(tm, d, seq // tm, layer),
        out_specs=_route_out_specs(tm, d),
        scratch_shapes=[pltpu.VMEM((N_EXPERTS, LANES), F32), pltpu.VMEM((tm, width), BF16)],
        compiler_params=_params("arbitrary"),
        name="gmlp_out_route",
    )(u, v, ws, bs, wo, x2, gm, scf, shf, lng, lnb, wrh, wrl, rb)


ZERO_ROWS = 64


def _row_copy(src_ref, src_row, dst_ref, dst_row, sem):
    return pltpu.make_async_copy(src_ref.at[pl.ds(src_row, 1), :], dst_ref.at[pl.ds(dst_row, 1), :], sem)


def _dispatch_kernel(ztile_ref, slots_ref, f_ref, ws1_ref, ws3_ref, ws2_ref, sh_ref, xs_ref,
                     zero_ref, sem, *, tme):
    tm = f_ref.shape[0]
    nz = tme // ZERO_ROWS

    def zero_copy(e, c):
        return pltpu.make_async_copy(
            zero_ref, xs_ref.at[pl.ds(pl.multiple_of(ztile_ref[e] + c * ZERO_ROWS, ZERO_ROWS), ZERO_ROWS), :], sem)

    @pl.when(pl.program_id(0) == 0)
    def _():
        zero_ref[...] = jnp.zeros_like(zero_ref)

        def each(fn):
            def body(e, carry):
                @pl.when(ztile_ref[e] >= 0)
                def _():
                    for c in range(nz):
                        fn(zero_copy(e, c))
                return carry
            lax.fori_loop(0, N_EXPERTS, body, 0)

        each(lambda cp: cp.start())
        each(lambda cp: cp.wait())

    def start(t, carry):
        for k in range(TOP_K):
            _row_copy(f_ref, t, xs_ref, slots_ref[k, t], sem).start()
        return carry

    lax.fori_loop(0, tm, start, 0)

    h = _unpack_rows(f_ref[...])
    a = _silu(_bdot(h, ws1_ref[...])) * _bdot(h, ws3_ref[...])
    sh_ref[...] = _bdot(a.astype(BF16), ws2_ref[...]).astype(BF16)

    def wait(t, carry):
        for k in range(TOP_K):
            _row_copy(f_ref, t, xs_ref, slots_ref[k, t], sem).wait()
        return carry

    lax.fori_loop(0, tm, wait, 0)


def _dispatch(ztile, slots, f, ws1, ws3, ws2, nslot, tme):
    t, dp = f.shape
    d = 2 * dp
    tm = min(256, t)
    fix = lambda i: (0, 0)
    return pl.pallas_call(
        functools.partial(_dispatch_kernel, tme=tme),
        out_shape=(jax.ShapeDtypeStruct((t, d), BF16), jax.ShapeDtypeStruct((nslot, dp), U32)),
        grid=(t // tm,),
        in_specs=[pl.BlockSpec(memory_space=pltpu.SMEM),
                  pl.BlockSpec((TOP_K, tm), lambda i: (0, i), memory_space=pltpu.SMEM),
                  pl.BlockSpec((tm, dp), lambda i: (i, 0)),
                  pl.BlockSpec(ws1.shape, fix), pl.BlockSpec(ws3.shape, fix), pl.BlockSpec(ws2.shape, fix)],
        out_specs=(pl.BlockSpec((tm, d), lambda i: (i, 0)), pl.BlockSpec(memory_space=pl.ANY)),
        scratch_shapes=[pltpu.VMEM((ZERO_ROWS, dp), U32), pltpu.SemaphoreType.DMA(())],
        compiler_params=_params("arbitrary"),
        name="moe_dispatch_shared",
    )(ztile, slots, f, ws1, ws3, ws2)


def _experts_kernel(texp_ref, tsrc_ref, xs_ref, w1_ref, w3_ref, w2_ref, y_ref, w1b, w3b, w2b):
    i = pl.program_id(0)
    changed = jnp.logical_or(i == 0, texp_ref[i] != texp_ref[jnp.maximum(i - 1, 0)])

    @pl.when(changed)
    def _():
        w1b[...] = w1_ref[...].astype(BF16)
        w3b[...] = w3_ref[...].astype(BF16)
        w2b[...] = w2_ref[...].astype(BF16)

    h = _unpack_rows(xs_ref[...])
    a = _silu(_bdot(h, w1b[...])) * _bdot(h, w3b[...])
    y_ref[...] = _pack_pairs(_bdot(a.astype(BF16), w2b[...]))


def _experts(texp, tsrc, xs, w1, w3, w2, layer, tme):
    nslot, dp = xs.shape
    d = 2 * dp
    fdim = w1.shape[-1]
    ntile = nslot // tme
    rows = lambda i, te, ts: (ts[i], 0)
    wsel = lambda i, te, ts: (layer, te[i], 0, 0)
    return pl.pallas_call(
        _experts_kernel,
        out_shape=jax.ShapeDtypeStruct((nslot, dp), U32),
        grid_spec=pltpu.PrefetchScalarGridSpec(
            num_scalar_prefetch=2,
            grid=(ntile,),
            in_specs=[pl.BlockSpec((tme, dp), rows),
                      pl.BlockSpec((None, None, d, fdim), wsel), pl.BlockSpec((None, None, d, fdim), wsel),
                      pl.BlockSpec((None, None, fdim, d), wsel)],
            out_specs=pl.BlockSpec((tme, dp), rows),
            scratch_shapes=[pltpu.VMEM((d, fdim), BF16), pltpu.VMEM((d, fdim), BF16), pltpu.VMEM((fdim, d), BF16)]),
        compiler_params=_params("arbitrary"),
        name="moe_experts",
    )(texp, tsrc, xs, w1, w3, w2)


def _combine_kernel(slots_ref, wt_ref, x1_ref, sh_ref, gf_ref, lng_ref, lnb_ref, ys_ref, o_ref, buf, sem, *, alpha):
    tm = x1_ref.shape[0]

    def start(t, carry):
        for k in range(TOP_K):
            _row_copy(ys_ref, slots_ref[k, t], buf.at[k], t, sem).start()
        return carry

    lax.fori_loop(0, tm, start, 0)

    def wait(t, carry):
        for k in range(TOP_K):
            _row_copy(ys_ref, slots_ref[k, t], buf.at[k], t, sem).wait()
        return carry

    lax.fori_loop(0, tm, wait, 0)

    dp = buf.shape[2]
    lo = jnp.zeros((tm, dp), F32)
    hi = jnp.zeros((tm, dp), F32)
    for k in range(TOP_K):
        w = wt_ref[:, k:k + 1]
        rl, rh = _unpack_pairs(buf[k])
        lo = lo + w * rl
        hi = hi + w * rh
    f = jnp.concatenate([lo, hi], axis=1) + sh_ref[...].astype(F32)
    o_ref[...] = _layernorm(alpha * x1_ref[...] + gf_ref[0] * f, lng_ref[0], lnb_ref[0])


def _combine(slots, wt, x1, sh, gf, lng, lnb, ys, seq, layer, alpha):
    t, d = x1.shape
    tm = min(256, seq)
    per_b = seq // tm
    row = lambda i: (i, 0)
    lrow = lambda i: (layer, 0, 0)
    return pl.pallas_call(
        functools.partial(_combine_kernel, alpha=alpha),
        out_shape=jax.ShapeDtypeStruct((t, d), F32),
        grid=(t // tm,),
        in_specs=[pl.BlockSpec((TOP_K, tm), lambda i: (0, i), memory_space=pltpu.SMEM),
                  pl.BlockSpec((tm, TOP_K), row), pl.BlockSpec((tm, d), row), pl.BlockSpec((tm, d), row),
                  pl.BlockSpec((1, 1, d), lambda i: (i // per_b, 0, 0)),
                  pl.BlockSpec((1, 1, d), lrow), pl.BlockSpec((1, 1, d), lrow),
                  pl.BlockSpec(memory_space=pl.ANY)],
        out_specs=pl.BlockSpec((tm, d), row),
        scratch_shapes=[pltpu.VMEM((TOP_K, tm, d // 2), U32), pltpu.SemaphoreType.DMA(())],
        compiler_params=_params("arbitrary"),
        name="moe_combine_ln",
    )(slots, wt, x1, sh, gf, lng, lnb, ys)


def _moe(f, x1, ids, wts, rank, cnt, gf, lng, lnb, w1, w3, w2, ws1, ws3, ws2, seq, layer, alpha):
    t, d = x1.shape
    tme = min(512, t)
    counts = cnt[:, 0].astype(I32)
    tiles_e = (counts + tme - 1) // tme
    padded = tiles_e * tme
    ends = jnp.cumsum(padded)
    base = ends - padded
    expert = jnp.arange(N_EXPERTS, dtype=I32)
    slots = jnp.sum(jnp.where(ids[:, :, None] == expert, base, 0), axis=-1) + rank
    ntile = (t * TOP_K) // tme + N_EXPERTS
    tile_ends = jnp.cumsum(tiles_e)
    tsrc = jnp.minimum(jnp.arange(ntile, dtype=I32), tile_ends[-1] - 1)
    texp = jnp.sum((tsrc[:, None] >= tile_ends[None, :]).astype(I32), axis=1)
    ztile = jnp.where(padded > 0, ends - tme, -1).astype(I32)

    shared, xs = _dispatch(ztile, slots, f, ws1, ws3, ws2, ntile * tme, tme)
    ys = _experts(texp, tsrc, xs, w1, w3, w2, layer, tme)
    return _combine(slots, wts.T, x1, shared, gf, lng, lnb, ys, seq, layer, alpha)


def kernel(x, c, ctx, c_ctx, w_ada, b_ada, ln_mix_g, ln_mix_b, ln_ffn_g, ln_ffn_b, attn_w_qkv, attn_w_o, attn_sink, gmlp_w_in, gmlp_b_in, gmlp_v_g, gmlp_v_b, gmlp_w_s, gmlp_b_s, gmlp_w_o, moe_w_router, moe_bias, moe_w1, moe_w3, moe_w2, moe_ws1, moe_ws3, moe_ws2):
    batch, seq, d = x.shape
    nctx = ctx.shape[1]
    depth = w_ada.shape[0]
    assert depth == 2, "layer 0 is the attention mixer, layer 1 the gMLP mixer"
    alpha = float((2 * depth) ** 0.25)
    t = batch * seq

    pad = (-(batch + 1)) % 8
    cc = jnp.concatenate([c, c_ctx[None, :], jnp.zeros((pad, d), F32)], axis=0)
    mods = _modulations(cc, w_ada, b_ada)

    def mod(layer, j, rows=slice(0, batch)):
        return mods[layer, rows, j * d:(j + 1) * d]

    def per_batch(layer, j):
        return mod(layer, j).reshape(batch, 1, d)

    def router(layer):
        wt = moe_w_router[layer].T
        hi = wt.astype(BF16)
        return hi, (wt - hi.astype(F32)).astype(BF16), moe_bias[layer].reshape(N_EXPERTS, 1)

    def shared_w(layer):
        return moe_ws1[layer].astype(BF16), moe_ws3[layer].astype(BF16), moe_ws2[layer].astype(BF16)

    x2 = x.reshape(t, d)
    ln_mix_g, ln_mix_b, ln_ffn_g, ln_ffn_b = (
        p.reshape(depth, 1, d) for p in (ln_mix_g, ln_mix_b, ln_ffn_g, ln_ffn_b))

    qd = Q_PER_KV * N_KV_HEADS * HEAD_DIM
    kvd = N_KV_HEADS * HEAD_DIM
    wqkv = attn_w_qkv[0]
    dup = lambda w: jnp.concatenate([w.reshape(d, N_KV_HEADS, 1, HEAD_DIM)] * (LANES // HEAD_DIM), axis=2
                                    ).reshape(d, N_KV_HEADS * LANES)
    wk2, wv2 = dup(wqkv[:, qd:qd + kvd]), dup(wqkv[:, qd + kvd:])
    w_all = jnp.concatenate([wqkv[:, :qd] * (HEAD_DIM ** -0.5), wk2, wv2], axis=1).astype(BF16)
    q, k2, v2 = _qkv_proj(x2, per_batch(0, 1), per_batch(0, 0), w_all, _rope_tables(seq), seq)
    ctx_row = slice(batch, batch + 1)
    kc2, vc2 = _ctx_kv(ctx.reshape(batch * nctx, d), mod(0, 1, ctx_row), mod(0, 0, ctx_row),
                       jnp.concatenate([wk2, wv2], axis=1).astype(BF16))
    o = _attention(q, k2, v2, kc2, vc2, attn_sink[0], batch, seq, nctx)
    x1, f, ids, wts, rank, cnt = _attn_out(
        o, attn_w_o[0].astype(BF16), x2, per_batch(0, 2), per_batch(0, 4), per_batch(0, 3),
        ln_mix_g, ln_mix_b, *router(0), seq, 0, alpha)
    x2 = _moe(f, x1, ids, wts, rank, cnt, per_batch(0, 5), ln_ffn_g, ln_ffn_b,
              moe_w1, moe_w3, moe_w2, *shared_w(0), seq, 0, alpha)

    width = gmlp_w_in.shape[2] // 2
    w_in = gmlp_w_in[0].astype(BF16)
    b_in = gmlp_b_in[0].reshape(1, 2 * width)
    vg, vb = gmlp_v_g[0].reshape(1, width), gmlp_v_b[0].reshape(1, width)
    sc, sh = per_batch(1, 1), per_batch(1, 0)
    u = _gmlp_in(x2, sc, sh, w_in[:, :width], b_in[:, :width], vg, vb, seq, False)
    v = _gmlp_in(x2, sc, sh, w_in[:, width:], b_in[:, width:], vg, vb, seq, True)
    x1, f, ids, wts, rank, cnt = _gmlp_out(
        u, v, gmlp_w_s[0].astype(BF16), gmlp_b_s[0][:, :, None], gmlp_w_o[0].astype(BF16), x2,
        per_batch(1, 2), per_batch(1, 4), per_batch(1, 3), ln_mix_g, ln_mix_b, *router(1), seq, 1, alpha)
    x2 = _moe(f, x1, ids, wts, rank, cnt, per_batch(1, 5), ln_ffn_g, ln_ffn_b,
              moe_w1, moe_w3, moe_w2, *shared_w(1), seq, 1, alpha)
    return x2.reshape(batch, seq, d)
```

```python
import functools

import jax
import jax.numpy as jnp
from jax import lax
from jax.experimental import pallas as pl
from jax.experimental.pallas import tpu as pltpu
from jax.experimental.pallas import tpu_sc as plsc

F32 = jnp.float32
BF16 = jnp.bfloat16
I32 = jnp.int32

GRID_W = 64
N_KV_HEADS = 4
Q_PER_KV = 8
HEAD_DIM = 64
ROPE_HALF = 16
WINDOW = 128
Q_BLOCK = 128
ROPE_BASE = 10000.0
NEG_INF = -1e30
CHUNK = 128
N_GMLP_GROUPS = 8
N_EXPERTS = 64
TOP_K = 8
N_EXPERT_GROUPS = 8
EXPERTS_PER_GROUP = 8
TOPK_GROUPS = 4
ROUTED_SCALE = 2.5
LN_EPS = 1e-5

LANES = 128
VMEM_LIMIT_BYTES = 56 * 1024 * 1024


def _params(*sem):
    return pltpu.CompilerParams(dimension_semantics=sem, vmem_limit_bytes=VMEM_LIMIT_BYTES)


def _silu(a):
    return a * jax.nn.sigmoid(a)


def _layernorm(v, g, b):
    mu = jnp.mean(v, axis=-1, keepdims=True)
    d = v - mu
    var = jnp.mean(d * d, axis=-1, keepdims=True)
    return d * lax.rsqrt(var + LN_EPS) * g + b


def _bdot(a, b):
    return jnp.dot(a, b, preferred_element_type=F32)


U32 = jnp.uint32


def _pack_pairs(a):
    n = a.shape[1] // 2
    lo = lax.bitcast_convert_type(a[:, :n].astype(BF16).astype(F32), U32)
    hi = lax.bitcast_convert_type(a[:, n:].astype(BF16).astype(F32), U32)
    return hi | (lo >> 16)


def _unpack_pairs(p):
    lo = lax.bitcast_convert_type(p << 16, F32)
    hi = lax.bitcast_convert_type(p & jnp.uint32(0xFFFF0000), F32)
    return lo, hi


def _store_row_tiles(ref, packed):
    r, n = packed.shape
    sub = n // LANES
    for j in range(sub):
        ref[pl.ds(j, r, stride=sub), :] = packed[:, LANES * j:LANES * (j + 1)]


def _load_row_tiles(ref, r):
    sub = ref.shape[0] // r
    pieces = [_unpack_pairs(ref[pl.ds(j, r, stride=sub), :]) for j in range(sub)]
    return [p[0] for p in pieces], [p[1] for p in pieces]


def _load_rows_bf16(ref, r):
    lo, hi = _load_row_tiles(ref, r)
    return jnp.concatenate(lo + hi, axis=1).astype(BF16)


def _mod_kernel(c_ref, w_ref, b_ref, o_ref):
    a = _silu(c_ref[...]).astype(BF16)
    o_ref[0] = _bdot(a, w_ref[0].astype(BF16)) + b_ref[0]


def _modulations(cc, w_ada, b_ada):
    depth, d, n6 = w_ada.shape
    r = cc.shape[0]
    tn = max(w for w in range(LANES, min(1024, n6) + 1, LANES) if n6 % w == 0)
    return pl.pallas_call(
        _mod_kernel,
        out_shape=jax.ShapeDtypeStruct((depth, r, n6), F32),
        grid=(depth, n6 // tn),
        in_specs=[pl.BlockSpec((r, d), lambda l, j: (0, 0)),
                  pl.BlockSpec((1, d, tn), lambda l, j: (l, 0, j)),
                  pl.BlockSpec((1, 1, tn), lambda l, j: (l, 0, j))],
        out_specs=pl.BlockSpec((1, r, tn), lambda l, j: (l, 0, j)),
        compiler_params=_params("parallel", "parallel"),
        name="adaln_mod",
    )(cc, w_ada, b_ada.reshape(depth, 1, n6))


def _rope_tables(seq):
    rows = seq // GRID_W
    row_ids = jnp.repeat(jnp.arange(rows, dtype=F32), GRID_W)
    col_ids = jnp.tile(jnp.arange(GRID_W, dtype=F32), rows)
    inv_freq = ROPE_BASE ** (-jnp.arange(0, 2 * ROPE_HALF, 2, dtype=F32) / (2 * ROPE_HALF))
    ar, ac = row_ids[:, None] * inv_freq, col_ids[:, None] * inv_freq
    z = jnp.zeros_like(ar)
    cos = jnp.concatenate([jnp.cos(ar), jnp.cos(ar), jnp.cos(ac), jnp.cos(ac)], axis=-1)
    sin_lo = jnp.concatenate([-jnp.sin(ar), z, -jnp.sin(ac), z], axis=-1)
    sin_hi = jnp.concatenate([z, jnp.sin(ar), z, jnp.sin(ac)], axis=-1)
    rep = LANES // HEAD_DIM
    return tuple(jnp.tile(t, (1, rep)) for t in (cos, sin_lo, sin_hi))


def _qkv_kernel(x_ref, sc_ref, sh_ref, w_ref, cos_ref, slo_ref, shi_ref, q_ref, k_ref, v_ref, *, cw):
    h = (x_ref[...] * (1.0 + sc_ref[0]) + sh_ref[0]).astype(BF16)
    cos, slo, shi = cos_ref[...], slo_ref[...], shi_ref[...]

    def rope(a):
        return (a * cos + pltpu.roll(a, LANES - ROPE_HALF, 1) * slo
                + pltpu.roll(a, ROPE_HALF, 1) * shi)

    qd, kd = q_ref.shape[1], k_ref.shape[1]
    for c0 in range(0, qd, cw):
        acc = _bdot(h, w_ref[:, c0:c0 + cw])
        for b0 in range(0, cw, LANES):
            q_ref[:, c0 + b0:c0 + b0 + LANES] = rope(acc[:, b0:b0 + LANES]).astype(BF16)
    acc = _bdot(h, w_ref[:, qd:qd + kd])
    for b0 in range(0, kd, LANES):
        k_ref[:, b0:b0 + LANES] = rope(acc[:, b0:b0 + LANES]).astype(BF16)
    v_ref[...] = _bdot(h, w_ref[:, qd + kd:qd + 2 * kd]).astype(BF16)


def _qkv_proj(x2, sc, sh, w, tables, seq):
    t, d = x2.shape
    kd = N_KV_HEADS * LANES
    qd = w.shape[1] - 2 * kd
    tm = min(512, seq)
    per_b = seq // tm
    row = lambda i: (i, 0)
    mod = lambda i: (i // per_b, 0, 0)
    tab = lambda i: (i % per_b, 0)
    return pl.pallas_call(
        functools.partial(_qkv_kernel, cw=min(512, qd)),
        out_shape=(jax.ShapeDtypeStruct((t, qd), BF16), jax.ShapeDtypeStruct((t, kd), BF16),
                   jax.ShapeDtypeStruct((t, kd), BF16)),
        grid=(t // tm,),
        in_specs=[pl.BlockSpec((tm, d), row), pl.BlockSpec((1, 1, d), mod), pl.BlockSpec((1, 1, d), mod),
                  pl.BlockSpec(w.shape, lambda i: (0, 0)),
                  pl.BlockSpec((tm, LANES), tab), pl.BlockSpec((tm, LANES), tab), pl.BlockSpec((tm, LANES), tab)],
        out_specs=(pl.BlockSpec((tm, qd), row), pl.BlockSpec((tm, kd), row), pl.BlockSpec((tm, kd), row)),
        compiler_params=_params("parallel"),
        name="qkv_rope",
    )(x2, sc, sh, w, *tables)


def _ctxkv_kernel(x_ref, sc_ref, sh_ref, w_ref, k_ref, v_ref):
    h = (x_ref[...] * (1.0 + sc_ref[...]) + sh_ref[...]).astype(BF16)
    kd = k_ref.shape[1]
    k_ref[...] = _bdot(h, w_ref[:, :kd]).astype(BF16)
    v_ref[...] = _bdot(h, w_ref[:, kd:]).astype(BF16)


def _ctx_kv(c2, sc, sh, w):
    t, d = c2.shape
    kd = w.shape[1] // 2
    tm = min(512, t)
    row = lambda i: (i, 0)
    fix = lambda i: (0, 0)
    return pl.pallas_call(
        _ctxkv_kernel,
        out_shape=(jax.ShapeDtypeStruct((t, kd), BF16), jax.ShapeDtypeStruct((t, kd), BF16)),
        grid=(t // tm,),
        in_specs=[pl.BlockSpec((tm, d), row), pl.BlockSpec((1, d), fix), pl.BlockSpec((1, d), fix),
                  pl.BlockSpec(w.shape, fix)],
        out_specs=(pl.BlockSpec((tm, kd), row), pl.BlockSpec((tm, kd), row)),
        compiler_params=_params("parallel"),
        name="ctx_kv",
    )(c2, sc, sh, w)


def _attn_kernel(sink_ref, q_ref, k_ref, v_ref, kc_ref, vc_ref, o_ref):
    kv, i = pl.program_id(1), pl.program_id(2)
    seq, nctx = k_ref.shape[1], kc_ref.shape[1]
    nwin = Q_BLOCK + 2 * WINDOW
    start = pl.multiple_of(jnp.clip(i * Q_BLOCK - WINDOW, 0, seq - nwin), Q_BLOCK)
    nkey = nctx + nwin
    kall = jnp.concatenate([kc_ref[0], k_ref[0, pl.ds(start, nwin), :]], axis=0)
    vall = jnp.concatenate([vc_ref[0], v_ref[0, pl.ds(start, nwin), :]], axis=0)
    low = lax.broadcasted_iota(I32, (nkey, LANES), 1) < HEAD_DIM
    zero = jnp.zeros((nkey, LANES), BF16)
    npair = Q_PER_KV // 2
    q = q_ref[0]
    qs = jnp.concatenate([q[:, LANES * j:LANES * (j + 1)] for j in range(npair)], axis=0)
    nrow = npair * Q_BLOCK
    rows = lax.broadcasted_iota(I32, (Q_BLOCK, nkey), 0)
    cols = lax.broadcasted_iota(I32, (Q_BLOCK, nkey), 1)
    dist = (i * Q_BLOCK - start + nctx) + rows - cols
    valid = (cols < nctx) | (jnp.abs(dist) <= WINDOW)
    contract_last = (((1,), (1,)), ((), ()))
    out = jnp.zeros((nrow, LANES), F32)
    for par in range(2):
        keep = low if par == 0 else jnp.logical_not(low)
        kh = jnp.where(keep, kall, zero)
        vh = jnp.where(keep, vall, zero)
        s = lax.dot_general(qs, kh, contract_last, preferred_element_type=F32)
        s = jnp.concatenate(
            [jnp.where(valid, s[Q_BLOCK * j:Q_BLOCK * (j + 1), :], NEG_INF) for j in range(npair)], axis=0)
        sink = jnp.concatenate(
            [jnp.full((Q_BLOCK, 1), sink_ref[kv * Q_PER_KV + 2 * j + par], F32) for j in range(npair)], axis=0)
        m = jnp.maximum(jnp.max(s, axis=1, keepdims=True), sink)
        p = jnp.exp(s - m)
        den = jnp.sum(p, axis=1, keepdims=True) + jnp.exp(sink - m)
        out = out + _bdot(p.astype(BF16), vh) / den
    for j in range(npair):
        o_ref[0, :, LANES * j:LANES * (j + 1)] = out[Q_BLOCK * j:Q_BLOCK * (j + 1), :].astype(BF16)


def _attention(q, k2, v2, kc2, vc2, sink, batch, seq, nctx):
    qd = q.shape[1]
    gw = Q_PER_KV * HEAD_DIM
    q3 = q.reshape(batch, seq, qd)
    k3, v3 = k2.reshape(batch, seq, -1), v2.reshape(batch, seq, -1)
    kc3, vc3 = kc2.reshape(batch, nctx, -1), vc2.reshape(batch, nctx, -1)
    full = lambda b, h, i: (b, 0, h)
    blk = lambda b, h, i: (b, i, h)
    out = pl.pallas_call(
        _attn_kernel,
        out_shape=jax.ShapeDtypeStruct((batch, seq, qd), BF16),
        grid=(batch, N_KV_HEADS, seq // Q_BLOCK),
        in_specs=[pl.BlockSpec(memory_space=pltpu.SMEM),
                  pl.BlockSpec((1, Q_BLOCK, gw), blk),
                  pl.BlockSpec((1, seq, LANES), full), pl.BlockSpec((1, seq, LANES), full),
                  pl.BlockSpec((1, nctx, LANES), full), pl.BlockSpec((1, nctx, LANES), full)],
        out_specs=pl.BlockSpec((1, Q_BLOCK, gw), blk),
        compiler_params=_params("parallel", "parallel", "parallel"),
        name="window_gqa",
    )(sink, q3, k3, v3, kc3, vc3)
    return out.reshape(batch * seq, qd)


def _split_bf16(a):
    hi = a.astype(BF16)
    return hi, (a - hi.astype(F32)).astype(BF16)


def _post_mixer(x, y, gm, lng, lnb, scf, shf, wrh_ref, wrl_ref, rb_ref, run_ref,
                x1_ref, f_ref, ids_ref, wts_ref, rank_ref, cnt_ref, alpha):
    tm = x.shape[0]
    x1 = _layernorm(alpha * x + gm * y, lng, lnb)
    x1_ref[...] = x1
    f = x1 * (1.0 + scf) + shf
    _store_row_tiles(f_ref, _pack_pairs(f))

    fh, fl = _split_bf16(f)
    wh, wl = wrh_ref[...], wrl_ref[...]
    nt = (((1,), (1,)), ((), ()))
    dot_nt = lambda a, b: lax.dot_general(a, b, nt, preferred_element_type=F32)
    logits = dot_nt(wh, fh) + dot_nt(wh, fl) + dot_nt(wl, fh)
    scores = jax.nn.sigmoid(logits)
    biased = scores + rb_ref[...]

    eg = EXPERTS_PER_GROUP
    sub = lax.broadcasted_iota(I32, (eg, tm), 0)
    gscore = []
    for g in range(N_EXPERT_GROUPS):
        tg = biased[eg * g:eg * (g + 1), :]
        m1 = jnp.max(tg, axis=0, keepdims=True)
        i1 = jnp.min(jnp.where(tg == m1, sub, eg), axis=0, keepdims=True)
        m2 = jnp.max(jnp.where(sub == i1, -jnp.inf, tg), axis=0, keepdims=True)
        gscore.append(m1 + m2)
    gsel = [jnp.zeros((1, tm), jnp.bool_) for _ in range(N_EXPERT_GROUPS)]
    for _ in range(TOPK_GROUPS):
        best = functools.reduce(jnp.maximum, gscore)
        taken = jnp.zeros((1, tm), jnp.bool_)
        for g in range(N_EXPERT_GROUPS):
            hit = jnp.logical_and(gscore[g] == best, jnp.logical_not(taken))
            taken = jnp.logical_or(taken, hit)
            gsel[g] = jnp.logical_or(gsel[g], hit)
            gscore[g] = jnp.where(hit, -jnp.inf, gscore[g])
    cur = jnp.concatenate(
        [jnp.where(gsel[g], biased[eg * g:eg * (g + 1), :], -jnp.inf) for g in range(N_EXPERT_GROUPS)], axis=0)

    eidx = lax.broadcasted_iota(I32, (N_EXPERTS, tm), 0)
    picks, wts, hots = [], [], []
    for _ in range(TOP_K):
        m = jnp.max(cur, axis=0, keepdims=True)
        idx = jnp.min(jnp.where(cur == m, eidx, N_EXPERTS), axis=0, keepdims=True)
        hot = eidx == idx
        picks.append(idx)
        wts.append(jnp.sum(jnp.where(hot, scores, 0.0), axis=0, keepdims=True))
        hots.append(hot)
        cur = jnp.where(hot, -jnp.inf, cur)
    wsum = functools.reduce(jnp.add, wts)

    assigned = functools.reduce(jnp.add, [h.astype(F32) for h in hots])
    before = (lax.broadcasted_iota(I32, (tm, tm), 0) < lax.broadcasted_iota(I32, (tm, tm), 1)).astype(BF16)
    pos = _bdot(assigned.astype(BF16), before) + run_ref[:, 0:1]
    for k in range(TOP_K):
        ids_ref[k:k + 1, :] = picks[k]
        wts_ref[k:k + 1, :] = wts[k] / wsum * ROUTED_SCALE
        rank_ref[k:k + 1, :] = jnp.sum(jnp.where(hots[k], pos, 0.0), axis=0, keepdims=True).astype(I32)
    run_ref[...] = run_ref[...] + jnp.sum(assigned, axis=1, keepdims=True)
    cnt_ref[...] = run_ref[...]


def _route_out_shapes(t, d):
    return (jax.ShapeDtypeStruct((t, d), F32), jax.ShapeDtypeStruct((t * d // (2 * LANES), LANES), U32),
            jax.ShapeDtypeStruct((TOP_K, t), I32), jax.ShapeDtypeStruct((TOP_K, t), F32),
            jax.ShapeDtypeStruct((TOP_K, t), I32), jax.ShapeDtypeStruct((N_EXPERTS, LANES), F32))


def _route_out_specs(tm, d):
    row = lambda i: (i, 0)
    col = lambda i: (0, i)
    return (pl.BlockSpec((tm, d), row), pl.BlockSpec((tm * d // (2 * LANES), LANES), row),
            pl.BlockSpec((TOP_K, tm), col), pl.BlockSpec((TOP_K, tm), col), pl.BlockSpec((TOP_K, tm), col),
            pl.BlockSpec((N_EXPERTS, LANES), lambda i: (0, 0)))


def _route_in_specs(tm, d, per_b, layer):
    mod = lambda i: (i // per_b, 0, 0)
    lrow = lambda i: (layer, 0, 0)
    fix = lambda i: (0, 0)
    return [pl.BlockSpec((tm, d), lambda i: (i, 0)),
            pl.BlockSpec((1, 1, d), mod), pl.BlockSpec((1, 1, d), mod), pl.BlockSpec((1, 1, d), mod),
            pl.BlockSpec((1, 1, d), lrow), pl.BlockSpec((1, 1, d), lrow),
            pl.BlockSpec((N_EXPERTS, d), fix), pl.BlockSpec((N_EXPERTS, d), fix),
            pl.BlockSpec((N_EXPERTS, 1), fix)]


def _oproj_kernel(o_ref, wo_ref, x_ref, gm_ref, scf_ref, shf_ref, lng_ref, lnb_ref, wrh_ref, wrl_ref, rb_ref,
                  x1_ref, f_ref, ids_ref, wts_ref, rank_ref, cnt_ref, run_ref, *, alpha):
    @pl.when(pl.program_id(0) == 0)
    def _():
        run_ref[...] = jnp.zeros_like(run_ref)

    y = _bdot(o_ref[...], wo_ref[...])
    _post_mixer(x_ref[...], y, gm_ref[0], lng_ref[0], lnb_ref[0], scf_ref[0], shf_ref[0],
                wrh_ref, wrl_ref, rb_ref, run_ref, x1_ref, f_ref, ids_ref, wts_ref, rank_ref, cnt_ref, alpha)


def _attn_out(o, wo, x2, gm, scf, shf, lng, lnb, wrh, wrl, rb, seq, layer, alpha):
    t, d = x2.shape
    tm = min(256, seq)
    return pl.pallas_call(
        functools.partial(_oproj_kernel, alpha=alpha),
        out_shape=_route_out_shapes(t, d),
        grid=(t // tm,),
        in_specs=[pl.BlockSpec((tm, o.shape[1]), lambda i: (i, 0)), pl.BlockSpec(wo.shape, lambda i: (0, 0))]
        + _route_in_specs(tm, d, seq // tm, layer),
        out_specs=_route_out_specs(tm, d),
        scratch_shapes=[pltpu.VMEM((N_EXPERTS, LANES), F32)],
        compiler_params=_params("arbitrary"),
        name="attn_out_route",
    )(o, wo, x2, gm, scf, shf, lng, lnb, wrh, wrl, rb)


def _gmlp_in_kernel(x_ref, sc_ref, sh_ref, w_ref, b_ref, g_ref, beta_ref, o_ref, *, cw, normalize):
    h = (x_ref[...] * (1.0 + sc_ref[0]) + sh_ref[0]).astype(BF16)
    n = o_ref.shape[1]
    inv_sqrt2 = 0.7071067811865476
    parts = []
    for c0 in range(0, n, cw):
        z = _bdot(h, w_ref[:, c0:c0 + cw]) + b_ref[:, c0:c0 + cw]
        z = 0.5 * z * (1.0 + lax.erf(z * inv_sqrt2))
        if normalize:
            parts.append(z)
        else:
            o_ref[:, c0:c0 + cw] = z.astype(BF16)
    if normalize:
        v = jnp.concatenate(parts, axis=1)
        o_ref[...] = _layernorm(v, g_ref[...], beta_ref[...]).astype(BF16)


def _gmlp_in(x2, sc, sh, w, b, g, beta, seq, normalize):
    t, d = x2.shape
    n = w.shape[1]
    tm = min(256, seq)
    per_b = seq // tm
    row = lambda i: (i, 0)
    mod = lambda i: (i // per_b, 0, 0)
    fix = lambda i: (0, 0)
    return pl.pallas_call(
        functools.partial(_gmlp_in_kernel, cw=min(512, n), normalize=normalize),
        out_shape=jax.ShapeDtypeStruct((t, n), BF16),
        grid=(t // tm,),
        in_specs=[pl.BlockSpec((tm, d), row), pl.BlockSpec((1, 1, d), mod), pl.BlockSpec((1, 1, d), mod),
                  pl.BlockSpec(w.shape, fix), pl.BlockSpec((1, n), fix), pl.BlockSpec((1, n), fix),
                  pl.BlockSpec((1, n), fix)],
        out_specs=pl.BlockSpec((tm, n), row),
        compiler_params=_params("parallel"),
        name="gmlp_in_v" if normalize else "gmlp_in_u",
    )(x2, sc, sh, w, b, g, beta)


def _gmlp_out_kernel(u_ref, v_ref, ws_ref, bs_ref, wo_ref, x_ref, gm_ref, scf_ref, shf_ref, lng_ref, lnb_ref,
                     wrh_ref, wrl_ref, rb_ref, x1_ref, f_ref, ids_ref, wts_ref, rank_ref, cnt_ref,
                     run_ref, gated_ref, *, alpha):
    @pl.when(pl.program_id(0) == 0)
    def _():
        run_ref[...] = jnp.zeros_like(run_ref)

    tm, width = u_ref.shape
    gd = width // N_GMLP_GROUPS
    for r0 in range(0, tm, CHUNK):
        for g in range(N_GMLP_GROUPS):
            c0 = g * gd
            mixed = _bdot(ws_ref[g], v_ref[r0:r0 + CHUNK, c0:c0 + gd]) + bs_ref[g]
            gated_ref[r0:r0 + CHUNK, c0:c0 + gd] = (
                u_ref[r0:r0 + CHUNK, c0:c0 + gd].astype(F32) * mixed).astype(BF16)
    y = _bdot(gated_ref[...], wo_ref[...])
    _post_mixer(x_ref[...], y, gm_ref[0], lng_ref[0], lnb_ref[0], scf_ref[0], shf_ref[0],
                wrh_ref, wrl_ref, rb_ref, run_ref, x1_ref, f_ref, ids_ref, wts_ref, rank_ref, cnt_ref, alpha)


def _gmlp_out(u, v, ws, bs, wo, x2, gm, scf, shf, lng, lnb, wrh, wrl, rb, seq, layer, alpha):
    t, d = x2.shape
    width = u.shape[1]
    tm = min(256, seq)
    row = lambda i: (i, 0)
    return pl.pallas_call(
        functools.partial(_gmlp_out_kernel, alpha=alpha),
        out_shape=_route_out_shapes(t, d),
        grid=(t // tm,),
        in_specs=[pl.BlockSpec((tm, width), row), pl.BlockSpec((tm, width), row),
                  pl.BlockSpec(ws.shape, lambda i: (0, 0, 0)), pl.BlockSpec(bs.shape, lambda i: (0, 0, 0)),
                  pl.BlockSpec(wo.shape, lambda i: (0, 0))]
        + _route_in_specs(tm, d, seq // tm, layer),
        out_specs=_route_out_specs(tm, d),
        scratch_shapes=[pltpu.VMEM((N_EXPERTS, LANES), F32), pltpu.VMEM((tm, width), BF16)],
        compiler_params=_params("arbitrary"),
        name="gmlp_out_route",
    )(u, v, ws, bs, wo, x2, gm, scf, shf, lng, lnb, wrh, wrl, rb)


SC_CORES, SC_SUBCORES, SC_LANES = 2, 16, 16
MXU_N = 256


def _shared_kernel(f_ref, ws1_ref, ws3_ref, ws2_ref, sh_ref):
    h = _load_rows_bf16(f_ref, sh_ref.shape[0])
    a = _silu(_bdot(h, ws1_ref[...])) * _bdot(h, ws3_ref[...])
    sh_ref[...] = _bdot(a.astype(BF16), ws2_ref[...]).astype(BF16)


def _shared_expert(f, ws1, ws3, ws2, t):
    sub = f.shape[0] // t
    d = ws1.shape[0]
    tm = min(512, t)
    fix = lambda i: (0, 0)
    return pl.pallas_call(
        _shared_kernel,
        out_shape=jax.ShapeDtypeStruct((t, d), BF16),
        grid=(t // tm,),
        in_specs=[pl.BlockSpec((tm * sub, LANES), lambda i: (i, 0)),
                  pl.BlockSpec(ws1.shape, fix), pl.BlockSpec(ws3.shape, fix), pl.BlockSpec(ws2.shape, fix)],
        out_specs=pl.BlockSpec((tm, d), lambda i: (i, 0)),
        compiler_params=_params("parallel"),
        name="moe_shared",
    )(f, ws1, ws3, ws2)


def _inverse_map(slots_flat, nslot):
    n = slots_flat.shape[0]
    workers = SC_CORES * SC_SUBCORES
    per_w = nslot // workers
    chunk = min(8192, n)
    assert nslot % (workers * SC_LANES) == 0 and n % chunk == 0 and chunk % SC_LANES == 0
    mesh = plsc.VectorSubcoreMesh(core_axis_name="c", subcore_axis_name="s")

    @functools.partial(
        pl.kernel, out_type=jax.ShapeDtypeStruct((nslot,), I32), mesh=mesh,
        scratch_types=[pltpu.VMEM((per_w,), I32), pltpu.VMEM((chunk,), I32)],
        compiler_params=pltpu.CompilerParams(needs_layout_passes=False))
    def inverse(slots_hbm, inv_hbm, local, buf):
        lo = (lax.axis_index("s") * SC_CORES + lax.axis_index("c")) * per_w
        unused = jnp.full((SC_LANES,), -1, I32)

        @pl.loop(0, per_w, step=SC_LANES)
        def _(j):
            local[pl.ds(j, SC_LANES)] = unused

        lane = lax.iota(I32, SC_LANES)

        @pl.loop(0, n, step=chunk)
        def _(c0):
            pltpu.sync_copy(slots_hbm.at[pl.ds(c0, chunk)], buf)

            @pl.loop(0, chunk, step=SC_LANES)
            def _(j):
                idx = buf[pl.ds(j, SC_LANES)] - lo
                mine = jnp.logical_and(idx >= 0, idx < per_w)
                plsc.store_scatter(local, [idx], c0 + j + lane, mask=mine)

        pltpu.sync_copy(local, inv_hbm.at[pl.ds(lo, per_w)])

    return inverse(slots_flat)


def _experts_kernel(texp_ref, tok_ref, dst_ref, w1_ref, w3_ref, w2_ref, f_hbm, g_hbm,
                    w1b, w3b, w2b, xa, xb, ya, yb, gsem, ssem, *, tme):
    i = pl.program_id(0)
    last = pl.num_programs(0) - 1
    sub = xa.shape[0] // tme
    d, fdim = w1b.shape
    half = d // 2

    @pl.when(i == 0)
    def _():
        for ref in (xa, xb, ya, yb):
            ref[...] = jnp.zeros_like(ref)

    @pl.when(jnp.logical_or(i == 0, texp_ref[i] != texp_ref[jnp.maximum(i - 1, 0)]))
    def _():
        w1b[...] = w1_ref[...].astype(BF16)
        w3b[...] = w3_ref[...].astype(BF16)
        w2b[...] = w2_ref[...].astype(BF16)

    def all_rows(hbm):
        return hbm.at[pl.ds(0, tme * sub), :]

    def step(par):
        xg, xc = (xa, xb) if par == 0 else (xb, xa)
        ys, yc = (ya, yb) if par == 0 else (yb, ya)

        @pl.when(i > 0)
        def _():
            pltpu.make_async_copy(all_rows(f_hbm), xc, gsem.at[1 - par]).wait()
            pltpu.make_async_copy(yc, all_rows(g_hbm), ssem.at[1 - par]).wait()

        pw1, pw2 = min(MXU_N, fdim), min(MXU_N, half)
        pieces = 2 * (fdim // pw1) + 2 * (half // pw2)
        per_piece = -(-tme // pieces)
        issued = [0]

        def issue_rows():
            for r in range(issued[0], min(issued[0] + per_piece, tme)):
                tok = pl.multiple_of(tok_ref[r] * sub, sub)
                dst = pl.multiple_of(dst_ref[r] * sub, sub)
                pltpu.make_async_copy(f_hbm.at[pl.ds(tok, sub), :], xg.at[pl.ds(r * sub, sub), :],
                                      gsem.at[par]).start(priority=1)
                pltpu.make_async_copy(ys.at[pl.ds(r * sub, sub), :], g_hbm.at[pl.ds(dst, sub), :],
                                      ssem.at[par]).start()
            issued[0] = min(issued[0] + per_piece, tme)

        h = _load_rows_bf16(xc, tme)
        gate, up = [], []
        for c0 in range(0, fdim, pw1):
            gate.append(_bdot(h, w1b[:, c0:c0 + pw1]))
            issue_rows()
        for c0 in range(0, fdim, pw1):
            up.append(_bdot(h, w3b[:, c0:c0 + pw1]))
            issue_rows()
        a = (_silu(jnp.concatenate(gate, axis=1)) * jnp.concatenate(up, axis=1)).astype(BF16)
        for c0 in range(0, half, pw2):
            lo = _bdot(a, w2b[:, c0:c0 + pw2])
            issue_rows()
            hi = _bdot(a, w2b[:, half + c0:half + c0 + pw2])
            issue_rows()
            packed = _pack_pairs(jnp.concatenate([lo, hi], axis=1))
            for b0 in range(0, pw2, LANES):
                yc[pl.ds((c0 + b0) // LANES, tme, stride=sub), :] = packed[:, b0:b0 + LANES]
        assert issued[0] == tme

        @pl.when(i == last)
        def _():
            pltpu.make_async_copy(all_rows(f_hbm), xg, gsem.at[par]).wait()
            pltpu.make_async_copy(ys, all_rows(g_hbm), ssem.at[par]).wait()

    for par in range(2):
        pl.when(jnp.bitwise_and(i, 1) == par)(functools.partial(step, par))


def _experts(texp, tok, dst, w1, w3, w2, f, t, layer, tme):
    sub = f.shape[0] // t
    d, fdim = w1.shape[-2:]
    ntile = texp.shape[0]
    wsel = lambda i, te: (layer, te[i], 0, 0)
    rows = lambda: pl.BlockSpec((tme,), lambda i, te: (i,), memory_space=pltpu.SMEM)
    out = pl.pallas_call(
        functools.partial(_experts_kernel, tme=tme),
        out_shape=jax.ShapeDtypeStruct(((TOP_K + 1) * t * sub, LANES), U32),
        grid_spec=pltpu.PrefetchScalarGridSpec(
            num_scalar_prefetch=1,
            grid=(ntile,),
            in_specs=[rows(), rows(),
                      pl.BlockSpec((None, None, d, fdim), wsel), pl.BlockSpec((None, None, d, fdim), wsel),
                      pl.BlockSpec((None, None, fdim, d), wsel),
                      pl.BlockSpec(memory_space=pl.ANY)],
            out_specs=pl.BlockSpec(memory_space=pl.ANY),
            scratch_shapes=[pltpu.VMEM((d, fdim), BF16), pltpu.VMEM((d, fdim), BF16), pltpu.VMEM((fdim, d), BF16)]
            + [pltpu.VMEM((tme * sub, LANES), U32)] * 4
            + [pltpu.SemaphoreType.DMA((2,)), pltpu.SemaphoreType.DMA((2,))]),
        compiler_params=_params("arbitrary"),
        name="moe_experts",
    )(texp, tok, dst, w1, w3, w2, f)
    return out.reshape(TOP_K + 1, t * sub, LANES)


def _combine_kernel(wt_ref, x1_ref, sh_ref, gf_ref, lng_ref, lnb_ref, g_ref, o_ref, *, alpha):
    tm = x1_ref.shape[0]
    lo = hi = None
    for k in range(TOP_K):
        w = wt_ref[:, k:k + 1]
        rl, rh = _load_row_tiles(g_ref.at[k], tm)
        lo = [w * b for b in rl] if lo is None else [a + w * b for a, b in zip(lo, rl)]
        hi = [w * b for b in rh] if hi is None else [a + w * b for a, b in zip(hi, rh)]
    f = jnp.concatenate(lo + hi, axis=1) + sh_ref[...].astype(F32)
    o_ref[...] = _layernorm(alpha * x1_ref[...] + gf_ref[0] * f, lng_ref[0], lnb_ref[0])


def _combine(wt, x1, sh, gf, lng, lnb, g, seq, layer, alpha):
    t, d = x1.shape
    sub = g.shape[1] // t
    tm = min(256, seq)
    per_b = seq // tm
    row = lambda i: (i, 0)
    lrow = lambda i: (layer, 0, 0)
    return pl.pallas_call(
        functools.partial(_combine_kernel, alpha=alpha),
        out_shape=jax.ShapeDtypeStruct((t, d), F32),
        grid=(t // tm,),
        in_specs=[pl.BlockSpec((tm, TOP_K), row), pl.BlockSpec((tm, d), row), pl.BlockSpec((tm, d), row),
                  pl.BlockSpec((1, 1, d), lambda i: (i // per_b, 0, 0)),
                  pl.BlockSpec((1, 1, d), lrow), pl.BlockSpec((1, 1, d), lrow),
                  pl.BlockSpec((TOP_K, tm * sub, LANES), lambda i: (0, i, 0))],
        out_specs=pl.BlockSpec((tm, d), row),
        compiler_params=_params("parallel"),
        name="moe_combine_ln",
    )(wt, x1, sh, gf, lng, lnb, g)


def _moe(f, x1, ids, wts, rank, cnt, gf, lng, lnb, w1, w3, w2, ws1, ws3, ws2, seq, layer, alpha):
    t, d = x1.shape
    tme = min(512, t)
    counts = cnt[:, 0].astype(I32)
    tiles_e = (counts + tme - 1) // tme
    padded = tiles_e * tme
    base = jnp.cumsum(padded) - padded
    expert = jnp.arange(N_EXPERTS, dtype=I32)
    slots = jnp.sum(jnp.where(ids[:, :, None] == expert, base, 0), axis=-1) + rank
    ntile = (t * TOP_K) // tme + N_EXPERTS + 3
    nslot = ntile * tme
    tile_ends = jnp.cumsum(tiles_e)
    texp = jnp.sum((jnp.arange(ntile, dtype=I32)[:, None] >= tile_ends[None, :]).astype(I32), axis=1)
    texp = jnp.minimum(texp, N_EXPERTS - 1)
    texp_step = jnp.concatenate([texp[:1], texp[:-1]])

    inv = _inverse_map(slots.reshape(-1), nslot)
    spare = TOP_K * t + jnp.arange(nslot, dtype=I32) % tme
    tok = jnp.where(inv >= 0, inv % t, 0)
    dst = jnp.where(inv >= 0, inv, spare)
    dst_step = jnp.concatenate([spare[:2 * tme], dst[:-2 * tme]])

    shared = _shared_expert(f, ws1, ws3, ws2, t)
    g = _experts(texp_step, tok, dst_step, w1, w3, w2, f, t, layer, tme)
    return _combine(wts.T, x1, shared, gf, lng, lnb, g, seq, layer, alpha)


def kernel(x, c, ctx, c_ctx, w_ada, b_ada, ln_mix_g, ln_mix_b, ln_ffn_g, ln_ffn_b, attn_w_qkv, attn_w_o, attn_sink, gmlp_w_in, gmlp_b_in, gmlp_v_g, gmlp_v_b, gmlp_w_s, gmlp_b_s, gmlp_w_o, moe_w_router, moe_bias, moe_w1, moe_w3, moe_w2, moe_ws1, moe_ws3, moe_ws2):
    batch, seq, d = x.shape
    nctx = ctx.shape[1]
    depth = w_ada.shape[0]
    assert depth == 2, "layer 0 is the attention mixer, layer 1 the gMLP mixer"
    alpha = float((2 * depth) ** 0.25)
    t = batch * seq

    pad = (-(batch + 1)) % 8
    cc = jnp.concatenate([c, c_ctx[None, :], jnp.zeros((pad, d), F32)], axis=0)
    mods = _modulations(cc, w_ada, b_ada)

    def mod(layer, j, rows=slice(0, batch)):
        return mods[layer, rows, j * d:(j + 1) * d]

    def per_batch(layer, j):
        return mod(layer, j).reshape(batch, 1, d)

    def router(layer):
        wt = moe_w_router[layer].T
        hi = wt.astype(BF16)
        return hi, (wt - hi.astype(F32)).astype(BF16), moe_bias[layer].reshape(N_EXPERTS, 1)

    def shared_w(layer):
        return moe_ws1[layer].astype(BF16), moe_ws3[layer].astype(BF16), moe_ws2[layer].astype(BF16)

    x2 = x.reshape(t, d)
    ln_mix_g, ln_mix_b, ln_ffn_g, ln_ffn_b = (
        p.reshape(depth, 1, d) for p in (ln_mix_g, ln_mix_b, ln_ffn_g, ln_ffn_b))

    qd = Q_PER_KV * N_KV_HEADS * HEAD_DIM
    kvd = N_KV_HEADS * HEAD_DIM
    wqkv = attn_w_qkv[0]
    dup = lambda w: jnp.concatenate([w.reshape(d, N_KV_HEADS, 1, HEAD_DIM)] * (LANES // HEAD_DIM), axis=2
                                    ).reshape(d, N_KV_HEADS * LANES)
    wk2, wv2 = dup(wqkv[:, qd:qd + kvd]), dup(wqkv[:, qd + kvd:])
    w_all = jnp.concatenate([wqkv[:, :qd] * (HEAD_DIM ** -0.5), wk2, wv2], axis=1).astype(BF16)
    q, k2, v2 = _qkv_proj(x2, per_batch(0, 1), per_batch(0, 0), w_all, _rope_tables(seq), seq)
    ctx_row = slice(batch, batch + 1)
    kc2, vc2 = _ctx_kv(ctx.reshape(batch * nctx, d), mod(0, 1, ctx_row), mod(0, 0, ctx_row),
                       jnp.concatenate([wk2, wv2], axis=1).astype(BF16))
    o = _attention(q, k2, v2, kc2, vc2, attn_sink[0], batch, seq, nctx)
    x1, f, ids, wts, rank, cnt = _attn_out(
        o, attn_w_o[0].astype(BF16), x2, per_batch(0, 2), per_batch(0, 4), per_batch(0, 3),
        ln_mix_g, ln_mix_b, *router(0), seq, 0, alpha)
    x2 = _moe(f, x1, ids, wts, rank, cnt, per_batch(0, 5), ln_ffn_g, ln_ffn_b,
              moe_w1, moe_w3, moe_w2, *shared_w(0), seq, 0, alpha)

    width = gmlp_w_in.shape[2] // 2
    w_in = gmlp_w_in[0].astype(BF16)
    b_in = gmlp_b_in[0].reshape(1, 2 * width)
    vg, vb = gmlp_v_g[0].reshape(1, width), gmlp_v_b[0].reshape(1, width)
    sc, sh = per_batch(1, 1), per_batch(1, 0)
    u = _gmlp_in(x2, sc, sh, w_in[:, :width], b_in[:, :width], vg, vb, seq, False)
    v = _gmlp_in(x2, sc, sh, w_in[:, width:], b_in[:, width:], vg, vb, seq, True)
    x1, f, ids, wts, rank, cnt = _gmlp_out(
        u, v, gmlp_w_s[0].astype(BF16), gmlp_b_s[0][:, :, None], gmlp_w_o[0].astype(BF16), x2,
        per_batch(1, 2), per_batch(1, 4), per_batch(1, 3), ln_mix_g, ln_mix_b, *router(1), seq, 1, alpha)
    x2 = _moe(f, x1, ids, wts, rank, cnt, per_batch(1, 5), ln_ffn_g, ln_ffn_b,
              moe_w1, moe_w3, moe_w2, *shared_w(1), seq, 1, alpha)
    return x2.reshape(batch, seq, d)
```

```python
import functools

import jax
import jax.numpy as jnp
from jax import lax
from jax.experimental import pallas as pl
from jax.experimental.pallas import tpu as pltpu
from jax.experimental.pallas import tpu_sc as plsc

F32 = jnp.float32
BF16 = jnp.bfloat16
I32 = jnp.int32

GRID_W = 64
N_KV_HEADS = 4
Q_PER_KV = 8
HEAD_DIM = 64
ROPE_HALF = 16
WINDOW = 128
Q_BLOCK = 128
ROPE_BASE = 10000.0
NEG_INF = -1e30
CHUNK = 128
N_GMLP_GROUPS = 8
N_EXPERTS = 64
TOP_K = 8
N_EXPERT_GROUPS = 8
EXPERTS_PER_GROUP = 8
TOPK_GROUPS = 4
ROUTED_SCALE = 2.5
LN_EPS = 1e-5

LANES = 128
VMEM_LIMIT_BYTES = 56 * 1024 * 1024


def _params(*sem):
    return pltpu.CompilerParams(dimension_semantics=sem, vmem_limit_bytes=VMEM_LIMIT_BYTES)


def _silu(a):
    return a * jax.nn.sigmoid(a)


def _layernorm(v, g, b):
    mu = jnp.mean(v, axis=-1, keepdims=True)
    d = v - mu
    var = jnp.mean(d * d, axis=-1, keepdims=True)
    return d * lax.rsqrt(var + LN_EPS) * g + b


def _bdot(a, b):
    return jnp.dot(a, b, preferred_element_type=F32)


U32 = jnp.uint32


def _pack_pairs(a):
    n = a.shape[1] // 2
    lo = lax.bitcast_convert_type(a[:, :n].astype(BF16).astype(F32), U32)
    hi = lax.bitcast_convert_type(a[:, n:].astype(BF16).astype(F32), U32)
    return hi | (lo >> 16)


def _unpack_pairs(p):
    lo = lax.bitcast_convert_type(p << 16, F32)
    hi = lax.bitcast_convert_type(p & jnp.uint32(0xFFFF0000), F32)
    return lo, hi


def _store_row_tiles(ref, packed):
    r, n = packed.shape
    sub = n // LANES
    for j in range(sub):
        ref[pl.ds(j, r, stride=sub), :] = packed[:, LANES * j:LANES * (j + 1)]


def _load_row_tiles(ref, r):
    sub = ref.shape[0] // r
    pieces = [_unpack_pairs(ref[pl.ds(j, r, stride=sub), :]) for j in range(sub)]
    return [p[0] for p in pieces], [p[1] for p in pieces]


def _load_rows_bf16(ref, r):
    lo, hi = _load_row_tiles(ref, r)
    return jnp.concatenate(lo + hi, axis=1).astype(BF16)


def _mod_kernel(c_ref, w_ref, b_ref, o_ref):
    a = _silu(c_ref[...]).astype(BF16)
    o_ref[0] = _bdot(a, w_ref[0].astype(BF16)) + b_ref[0]


def _modulations(cc, w_ada, b_ada):
    depth, d, n6 = w_ada.shape
    r = cc.shape[0]
    tn = max(w for w in range(LANES, min(1024, n6) + 1, LANES) if n6 % w == 0)
    return pl.pallas_call(
        _mod_kernel,
        out_shape=jax.ShapeDtypeStruct((depth, r, n6), F32),
        grid=(depth, n6 // tn),
        in_specs=[pl.BlockSpec((r, d), lambda l, j: (0, 0)),
                  pl.BlockSpec((1, d, tn), lambda l, j: (l, 0, j)),
                  pl.BlockSpec((1, 1, tn), lambda l, j: (l, 0, j))],
        out_specs=pl.BlockSpec((1, r, tn), lambda l, j: (l, 0, j)),
        compiler_params=_params("parallel", "parallel"),
        name="adaln_mod",
    )(cc, w_ada, b_ada.reshape(depth, 1, n6))


def _rope_tables(seq):
    rows = seq // GRID_W
    row_ids = jnp.repeat(jnp.arange(rows, dtype=F32), GRID_W)
    col_ids = jnp.tile(jnp.arange(GRID_W, dtype=F32), rows)
    inv_freq = ROPE_BASE ** (-jnp.arange(0, 2 * ROPE_HALF, 2, dtype=F32) / (2 * ROPE_HALF))
    ar, ac = row_ids[:, None] * inv_freq, col_ids[:, None] * inv_freq
    z = jnp.zeros_like(ar)
    cos = jnp.concatenate([jnp.cos(ar), jnp.cos(ar), jnp.cos(ac), jnp.cos(ac)], axis=-1)
    sin_lo = jnp.concatenate([-jnp.sin(ar), z, -jnp.sin(ac), z], axis=-1)
    sin_hi = jnp.concatenate([z, jnp.sin(ar), z, jnp.sin(ac)], axis=-1)
    rep = LANES // HEAD_DIM
    return tuple(jnp.tile(t, (1, rep)) for t in (cos, sin_lo, sin_hi))


def _qkv_kernel(x_ref, sc_ref, sh_ref, w_ref, cos_ref, slo_ref, shi_ref, q_ref, k_ref, v_ref, *, cw):
    h = (x_ref[...] * (1.0 + sc_ref[0]) + sh_ref[0]).astype(BF16)
    cos, slo, shi = cos_ref[...], slo_ref[...], shi_ref[...]

    def rope(a):
        return (a * cos + pltpu.roll(a, LANES - ROPE_HALF, 1) * slo
                + pltpu.roll(a, ROPE_HALF, 1) * shi)

    qd, kd = q_ref.shape[1], k_ref.shape[1]
    for c0 in range(0, qd, cw):
        acc = _bdot(h, w_ref[:, c0:c0 + cw])
        for b0 in range(0, cw, LANES):
            q_ref[:, c0 + b0:c0 + b0 + LANES] = rope(acc[:, b0:b0 + LANES]).astype(BF16)
    acc = _bdot(h, w_ref[:, qd:qd + kd])
    for b0 in range(0, kd, LANES):
        k_ref[:, b0:b0 + LANES] = rope(acc[:, b0:b0 + LANES]).astype(BF16)
    v_ref[...] = _bdot(h, w_ref[:, qd + kd:qd + 2 * kd]).astype(BF16)


def _qkv_proj(x2, sc, sh, w, tables, seq):
    t, d = x2.shape
    kd = N_KV_HEADS * LANES
    qd = w.shape[1] - 2 * kd
    tm = min(512, seq)
    per_b = seq // tm
    row = lambda i: (i, 0)
    mod = lambda i: (i // per_b, 0, 0)
    tab = lambda i: (i % per_b, 0)
    return pl.pallas_call(
        functools.partial(_qkv_kernel, cw=min(512, qd)),
        out_shape=(jax.ShapeDtypeStruct((t, qd), BF16), jax.ShapeDtypeStruct((t, kd), BF16),
                   jax.ShapeDtypeStruct((t, kd), BF16)),
        grid=(t // tm,),
        in_specs=[pl.BlockSpec((tm, d), row), pl.BlockSpec((1, 1, d), mod), pl.BlockSpec((1, 1, d), mod),
                  pl.BlockSpec(w.shape, lambda i: (0, 0)),
                  pl.BlockSpec((tm, LANES), tab), pl.BlockSpec((tm, LANES), tab), pl.BlockSpec((tm, LANES), tab)],
        out_specs=(pl.BlockSpec((tm, qd), row), pl.BlockSpec((tm, kd), row), pl.BlockSpec((tm, kd), row)),
        compiler_params=_params("parallel"),
        name="qkv_rope",
    )(x2, sc, sh, w, *tables)


def _ctxkv_kernel(x_ref, sc_ref, sh_ref, w_ref, k_ref, v_ref):
    h = (x_ref[...] * (1.0 + sc_ref[...]) + sh_ref[...]).astype(BF16)
    kd = k_ref.shape[1]
    k_ref[...] = _bdot(h, w_ref[:, :kd]).astype(BF16)
    v_ref[...] = _bdot(h, w_ref[:, kd:]).astype(BF16)


def _ctx_kv(c2, sc, sh, w):
    t, d = c2.shape
    kd = w.shape[1] // 2
    tm = min(512, t)
    row = lambda i: (i, 0)
    fix = lambda i: (0, 0)
    return pl.pallas_call(
        _ctxkv_kernel,
        out_shape=(jax.ShapeDtypeStruct((t, kd), BF16), jax.ShapeDtypeStruct((t, kd), BF16)),
        grid=(t // tm,),
        in_specs=[pl.BlockSpec((tm, d), row), pl.BlockSpec((1, d), fix), pl.BlockSpec((1, d), fix),
                  pl.BlockSpec(w.shape, fix)],
        out_specs=(pl.BlockSpec((tm, kd), row), pl.BlockSpec((tm, kd), row)),
        compiler_params=_params("parallel"),
        name="ctx_kv",
    )(c2, sc, sh, w)


def _attn_kernel(sink_ref, q_ref, k_ref, v_ref, kc_ref, vc_ref, o_ref):
    kv, i = pl.program_id(1), pl.program_id(2)
    seq, nctx = k_ref.shape[1], kc_ref.shape[1]
    nwin = Q_BLOCK + 2 * WINDOW
    start = pl.multiple_of(jnp.clip(i * Q_BLOCK - WINDOW, 0, seq - nwin), Q_BLOCK)
    nkey = nctx + nwin
    kall = jnp.concatenate([kc_ref[0], k_ref[0, pl.ds(start, nwin), :]], axis=0)
    vall = jnp.concatenate([vc_ref[0], v_ref[0, pl.ds(start, nwin), :]], axis=0)
    low = lax.broadcasted_iota(I32, (nkey, LANES), 1) < HEAD_DIM
    zero = jnp.zeros((nkey, LANES), BF16)
    npair = Q_PER_KV // 2
    q = q_ref[0]
    qs = jnp.concatenate([q[:, LANES * j:LANES * (j + 1)] for j in range(npair)], axis=0)
    nrow = npair * Q_BLOCK
    rows = lax.broadcasted_iota(I32, (Q_BLOCK, nkey), 0)
    cols = lax.broadcasted_iota(I32, (Q_BLOCK, nkey), 1)
    dist = (i * Q_BLOCK - start + nctx) + rows - cols
    valid = (cols < nctx) | (jnp.abs(dist) <= WINDOW)
    contract_last = (((1,), (1,)), ((), ()))
    out = jnp.zeros((nrow, LANES), F32)
    for par in range(2):
        keep = low if par == 0 else jnp.logical_not(low)
        kh = jnp.where(keep, kall, zero)
        vh = jnp.where(keep, vall, zero)
        s = lax.dot_general(qs, kh, contract_last, preferred_element_type=F32)
        s = jnp.concatenate(
            [jnp.where(valid, s[Q_BLOCK * j:Q_BLOCK * (j + 1), :], NEG_INF) for j in range(npair)], axis=0)
        sink = jnp.concatenate(
            [jnp.full((Q_BLOCK, 1), sink_ref[kv * Q_PER_KV + 2 * j + par], F32) for j in range(npair)], axis=0)
        m = jnp.maximum(jnp.max(s, axis=1, keepdims=True), sink)
        p = jnp.exp(s - m)
        den = jnp.sum(p, axis=1, keepdims=True) + jnp.exp(sink - m)
        out = out + _bdot(p.astype(BF16), vh) / den
    for j in range(npair):
        o_ref[0, :, LANES * j:LANES * (j + 1)] = out[Q_BLOCK * j:Q_BLOCK * (j + 1), :].astype(BF16)


def _attention(q, k2, v2, kc2, vc2, sink, batch, seq, nctx):
    qd = q.shape[1]
    gw = Q_PER_KV * HEAD_DIM
    q3 = q.reshape(batch, seq, qd)
    k3, v3 = k2.reshape(batch, seq, -1), v2.reshape(batch, seq, -1)
    kc3, vc3 = kc2.reshape(batch, nctx, -1), vc2.reshape(batch, nctx, -1)
    full = lambda b, h, i: (b, 0, h)
    blk = lambda b, h, i: (b, i, h)
    out = pl.pallas_call(
        _attn_kernel,
        out_shape=jax.ShapeDtypeStruct((batch, seq, qd), BF16),
        grid=(batch, N_KV_HEADS, seq // Q_BLOCK),
        in_specs=[pl.BlockSpec(memory_space=pltpu.SMEM),
                  pl.BlockSpec((1, Q_BLOCK, gw), blk),
                  pl.BlockSpec((1, seq, LANES), full), pl.BlockSpec((1, seq, LANES), full),
                  pl.BlockSpec((1, nctx, LANES), full), pl.BlockSpec((1, nctx, LANES), full)],
        out_specs=pl.BlockSpec((1, Q_BLOCK, gw), blk),
        compiler_params=_params("parallel", "parallel", "parallel"),
        name="window_gqa",
    )(sink, q3, k3, v3, kc3, vc3)
    return out.reshape(batch * seq, qd)


def _split_bf16(a):
    hi = a.astype(BF16)
    return hi, (a - hi.astype(F32)).astype(BF16)


def _post_mixer(x, y, gm, lng, lnb, scf, shf, wrh_ref, wrl_ref, rb_ref, run_ref,
                x1_ref, f_ref, ids_ref, wts_ref, rank_ref, cnt_ref, alpha):
    tm = x.shape[0]
    x1 = _layernorm(alpha * x + gm * y, lng, lnb)
    x1_ref[...] = x1
    f = x1 * (1.0 + scf) + shf
    _store_row_tiles(f_ref, _pack_pairs(f))

    fh, fl = _split_bf16(f)
    wh, wl = wrh_ref[...], wrl_ref[...]
    logits = (_bdot(fh, wh) + _bdot(fl, wh) + _bdot(fh, wl)).T[:N_EXPERTS, :]
    scores = jax.nn.sigmoid(logits)
    biased = scores + rb_ref[...]

    eg = EXPERTS_PER_GROUP
    sub = lax.broadcasted_iota(I32, (eg, tm), 0)
    gscore = []
    for g in range(N_EXPERT_GROUPS):
        tg = biased[eg * g:eg * (g + 1), :]
        m1 = jnp.max(tg, axis=0, keepdims=True)
        i1 = jnp.min(jnp.where(tg == m1, sub, eg), axis=0, keepdims=True)
        m2 = jnp.max(jnp.where(sub == i1, -jnp.inf, tg), axis=0, keepdims=True)
        gscore.append(m1 + m2)
    gsel = [jnp.zeros((1, tm), jnp.bool_) for _ in range(N_EXPERT_GROUPS)]
    for _ in range(TOPK_GROUPS):
        best = functools.reduce(jnp.maximum, gscore)
        taken = jnp.zeros((1, tm), jnp.bool_)
        for g in range(N_EXPERT_GROUPS):
            hit = jnp.logical_and(gscore[g] == best, jnp.logical_not(taken))
            taken = jnp.logical_or(taken, hit)
            gsel[g] = jnp.logical_or(gsel[g], hit)
            gscore[g] = jnp.where(hit, -jnp.inf, gscore[g])
    cur = jnp.concatenate(
        [jnp.where(gsel[g], biased[eg * g:eg * (g + 1), :], -jnp.inf) for g in range(N_EXPERT_GROUPS)], axis=0)

    eidx = lax.broadcasted_iota(I32, (N_EXPERTS, tm), 0)
    picks, wts, hots = [], [], []
    for _ in range(TOP_K):
        m = jnp.max(cur, axis=0, keepdims=True)
        idx = jnp.min(jnp.where(cur == m, eidx, N_EXPERTS), axis=0, keepdims=True)
        hot = eidx == idx
        picks.append(idx)
        wts.append(jnp.sum(jnp.where(hot, scores, 0.0), axis=0, keepdims=True))
        hots.append(hot)
        cur = jnp.where(hot, -jnp.inf, cur)
    wsum = functools.reduce(jnp.add, wts)

    assigned = functools.reduce(jnp.add, [h.astype(F32) for h in hots])
    before = (lax.broadcasted_iota(I32, (tm, tm), 0) < lax.broadcasted_iota(I32, (tm, tm), 1)).astype(BF16)
    pos = _bdot(assigned.astype(BF16), before) + run_ref[:, 0:1]
    for k in range(TOP_K):
        ids_ref[k:k + 1, :] = picks[k]
        wts_ref[k:k + 1, :] = wts[k] / wsum * ROUTED_SCALE
        rank_ref[k:k + 1, :] = jnp.sum(jnp.where(hots[k], pos, 0.0), axis=0, keepdims=True).astype(I32)
    run_ref[...] = run_ref[...] + jnp.sum(assigned, axis=1, keepdims=True)
    cnt_ref[...] = run_ref[...]


def _route_out_shapes(t, d):
    return (jax.ShapeDtypeStruct((t, d), F32), jax.ShapeDtypeStruct((t * d // (2 * LANES), LANES), U32),
            jax.ShapeDtypeStruct((TOP_K, t), I32), jax.ShapeDtypeStruct((TOP_K, t), F32),
            jax.ShapeDtypeStruct((TOP_K, t), I32), jax.ShapeDtypeStruct((N_EXPERTS, LANES), F32))


def _route_out_specs(tm, d):
    row = lambda i: (i, 0)
    col = lambda i: (0, i)
    return (pl.BlockSpec((tm, d), row), pl.BlockSpec((tm * d // (2 * LANES), LANES), row),
            pl.BlockSpec((TOP_K, tm), col), pl.BlockSpec((TOP_K, tm), col), pl.BlockSpec((TOP_K, tm), col),
            pl.BlockSpec((N_EXPERTS, LANES), lambda i: (0, 0)))


def _route_in_specs(tm, d, per_b, layer):
    mod = lambda i: (i // per_b, 0, 0)
    lrow = lambda i: (layer, 0, 0)
    fix = lambda i: (0, 0)
    return [pl.BlockSpec((tm, d), lambda i: (i, 0)),
            pl.BlockSpec((1, 1, d), mod), pl.BlockSpec((1, 1, d), mod), pl.BlockSpec((1, 1, d), mod),
            pl.BlockSpec((1, 1, d), lrow), pl.BlockSpec((1, 1, d), lrow),
            pl.BlockSpec((d, LANES), fix), pl.BlockSpec((d, LANES), fix),
            pl.BlockSpec((N_EXPERTS, 1), fix)]


def _oproj_kernel(o_ref, wo_ref, x_ref, gm_ref, scf_ref, shf_ref, lng_ref, lnb_ref, wrh_ref, wrl_ref, rb_ref,
                  x1_ref, f_ref, ids_ref, wts_ref, rank_ref, cnt_ref, run_ref, *, alpha):
    @pl.when(pl.program_id(0) == 0)
    def _():
        run_ref[...] = jnp.zeros_like(run_ref)

    y = _bdot(o_ref[...], wo_ref[...])
    _post_mixer(x_ref[...], y, gm_ref[0], lng_ref[0], lnb_ref[0], scf_ref[0], shf_ref[0],
                wrh_ref, wrl_ref, rb_ref, run_ref, x1_ref, f_ref, ids_ref, wts_ref, rank_ref, cnt_ref, alpha)


def _attn_out(o, wo, x2, gm, scf, shf, lng, lnb, wrh, wrl, rb, seq, layer, alpha):
    t, d = x2.shape
    tm = min(512, seq)
    return pl.pallas_call(
        functools.partial(_oproj_kernel, alpha=alpha),
        out_shape=_route_out_shapes(t, d),
        grid=(t // tm,),
        in_specs=[pl.BlockSpec((tm, o.shape[1]), lambda i: (i, 0)), pl.BlockSpec(wo.shape, lambda i: (0, 0))]
        + _route_in_specs(tm, d, seq // tm, layer),
        out_specs=_route_out_specs(tm, d),
        scratch_shapes=[pltpu.VMEM((N_EXPERTS, LANES), F32)],
        compiler_params=_params("arbitrary"),
        name="attn_out_route",
    )(o, wo, x2, gm, scf, shf, lng, lnb, wrh, wrl, rb)


def _gmlp_in_kernel(x_ref, sc_ref, sh_ref, w_ref, b_ref, g_ref, beta_ref, o_ref, *, cw, normalize):
    h = (x_ref[...] * (1.0 + sc_ref[0]) + sh_ref[0]).astype(BF16)
    n = o_ref.shape[1]
    inv_sqrt2 = 0.7071067811865476
    parts = []
    for c0 in range(0, n, cw):
        z = _bdot(h, w_ref[:, c0:c0 + cw]) + b_ref[:, c0:c0 + cw]
        z = 0.5 * z * (1.0 + lax.erf(z * inv_sqrt2))
        if normalize:
            parts.append(z)
        else:
            o_ref[:, c0:c0 + cw] = z.astype(BF16)
    if normalize:
        v = jnp.concatenate(parts, axis=1)
        o_ref[...] = _layernorm(v, g_ref[...], beta_ref[...]).astype(BF16)


def _gmlp_in(x2, sc, sh, w, b, g, beta, seq, normalize):
    t, d = x2.shape
    n = w.shape[1]
    tm = min(256, seq)
    per_b = seq // tm
    row = lambda i: (i, 0)
    mod = lambda i: (i // per_b, 0, 0)
    fix = lambda i: (0, 0)
    return pl.pallas_call(
        functools.partial(_gmlp_in_kernel, cw=min(512, n), normalize=normalize),
        out_shape=jax.ShapeDtypeStruct((t, n), BF16),
        grid=(t // tm,),
        in_specs=[pl.BlockSpec((tm, d), row), pl.BlockSpec((1, 1, d), mod), pl.BlockSpec((1, 1, d), mod),
                  pl.BlockSpec(w.shape, fix), pl.BlockSpec((1, n), fix), pl.BlockSpec((1, n), fix),
                  pl.BlockSpec((1, n), fix)],
        out_specs=pl.BlockSpec((tm, n), row),
        compiler_params=_params("parallel"),
        name="gmlp_in_v" if normalize else "gmlp_in_u",
    )(x2, sc, sh, w, b, g, beta)


def _gmlp_out_kernel(u_ref, v_ref, ws_ref, bs_ref, wo_ref, x_ref, gm_ref, scf_ref, shf_ref, lng_ref, lnb_ref,
                     wrh_ref, wrl_ref, rb_ref, x1_ref, f_ref, ids_ref, wts_ref, rank_ref, cnt_ref,
                     run_ref, gated_ref, *, alpha):
    @pl.when(pl.program_id(0) == 0)
    def _():
        run_ref[...] = jnp.zeros_like(run_ref)

    tm, width = u_ref.shape
    gd = width // N_GMLP_GROUPS
    for r0 in range(0, tm, CHUNK):
        for g in range(N_GMLP_GROUPS):
            c0 = g * gd
            mixed = _bdot(ws_ref[g], v_ref[r0:r0 + CHUNK, c0:c0 + gd]) + bs_ref[g]
            gated_ref[r0:r0 + CHUNK, c0:c0 + gd] = (
                u_ref[r0:r0 + CHUNK, c0:c0 + gd].astype(F32) * mixed).astype(BF16)
    y = _bdot(gated_ref[...], wo_ref[...])
    _post_mixer(x_ref[...], y, gm_ref[0], lng_ref[0], lnb_ref[0], scf_ref[0], shf_ref[0],
                wrh_ref, wrl_ref, rb_ref, run_ref, x1_ref, f_ref, ids_ref, wts_ref, rank_ref, cnt_ref, alpha)


def _gmlp_out(u, v, ws, bs, wo, x2, gm, scf, shf, lng, lnb, wrh, wrl, rb, seq, layer, alpha):
    t, d = x2.shape
    width = u.shape[1]
    tm = min(256, seq)
    row = lambda i: (i, 0)
    return pl.pallas_call(
        functools.partial(_gmlp_out_kernel, alpha=alpha),
        out_shape=_route_out_shapes(t, d),
        grid=(t // tm,),
        in_specs=[pl.BlockSpec((tm, width), row), pl.BlockSpec((tm, width), row),
                  pl.BlockSpec(ws.shape, lambda i: (0, 0, 0)), pl.BlockSpec(bs.shape, lambda i: (0, 0, 0)),
                  pl.BlockSpec(wo.shape, lambda i: (0, 0))]
        + _route_in_specs(tm, d, seq // tm, layer),
        out_specs=_route_out_specs(tm, d),
        scratch_shapes=[pltpu.VMEM((N_EXPERTS, LANES), F32), pltpu.VMEM((tm, width), BF16)],
        compiler_params=_params("arbitrary"),
        name="gmlp_out_route",
    )(u, v, ws, bs, wo, x2, gm, scf, shf, lng, lnb, wrh, wrl, rb)


SC_CORES, SC_SUBCORES, SC_LANES = 2, 16, 16
MXU_N = 256


def _shared_kernel(f_ref, ws1_ref, ws3_ref, ws2_ref, sh_ref):
    h = _load_rows_bf16(f_ref, sh_ref.shape[0])
    a = _silu(_bdot(h, ws1_ref[...])) * _bdot(h, ws3_ref[...])
    sh_ref[...] = _bdot(a.astype(BF16), ws2_ref[...]).astype(BF16)


def _shared_expert(f, ws1, ws3, ws2, t):
    sub = f.shape[0] // t
    d = ws1.shape[0]
    tm = min(512, t)
    fix = lambda i: (0, 0)
    return pl.pallas_call(
        _shared_kernel,
        out_shape=jax.ShapeDtypeStruct((t, d), BF16),
        grid=(t // tm,),
        in_specs=[pl.BlockSpec((tm * sub, LANES), lambda i: (i, 0)),
                  pl.BlockSpec(ws1.shape, fix), pl.BlockSpec(ws3.shape, fix), pl.BlockSpec(ws2.shape, fix)],
        out_specs=pl.BlockSpec((tm, d), lambda i: (i, 0)),
        compiler_params=_params("parallel"),
        name="moe_shared",
    )(f, ws1, ws3, ws2)


def _inverse_map(slots_flat, nslot):
    n = slots_flat.shape[0]
    workers = SC_CORES * SC_SUBCORES
    per_w = nslot // workers
    chunk = min(8192, n)
    assert nslot % (workers * SC_LANES) == 0 and n % chunk == 0 and chunk % SC_LANES == 0
    mesh = plsc.VectorSubcoreMesh(core_axis_name="c", subcore_axis_name="s")

    @functools.partial(
        pl.kernel, out_type=jax.ShapeDtypeStruct((nslot,), I32), mesh=mesh,
        scratch_types=[pltpu.VMEM((per_w,), I32), pltpu.VMEM((chunk,), I32)],
        compiler_params=pltpu.CompilerParams(needs_layout_passes=False))
    def inverse(slots_hbm, inv_hbm, local, buf):
        lo = (lax.axis_index("s") * SC_CORES + lax.axis_index("c")) * per_w
        unused = jnp.full((SC_LANES,), -1, I32)

        @pl.loop(0, per_w, step=SC_LANES)
        def _(j):
            local[pl.ds(j, SC_LANES)] = unused

        lane = lax.iota(I32, SC_LANES)

        @pl.loop(0, n, step=chunk)
        def _(c0):
            pltpu.sync_copy(slots_hbm.at[pl.ds(c0, chunk)], buf)

            @pl.loop(0, chunk, step=SC_LANES)
            def _(j):
                idx = buf[pl.ds(j, SC_LANES)] - lo
                mine = jnp.logical_and(idx >= 0, idx < per_w)
                plsc.store_scatter(local, [idx], c0 + j + lane, mask=mine)

        pltpu.sync_copy(local, inv_hbm.at[pl.ds(lo, per_w)])

    return inverse(slots_flat)


def _experts_kernel(texp_ref, tok_ref, dst_ref, w1_ref, w3_ref, w2_ref, f_hbm, g_hbm,
                    w1b, w3b, w2b, xa, xb, ya, yb, gsem, ssem, *, tme):
    i = pl.program_id(0)
    last = pl.num_programs(0) - 1
    sub = xa.shape[0] // tme
    d, fdim = w1b.shape
    half = d // 2

    @pl.when(i == 0)
    def _():
        for ref in (xa, xb, ya, yb):
            ref[...] = jnp.zeros_like(ref)

    @pl.when(jnp.logical_or(i == 0, texp_ref[i] != texp_ref[jnp.maximum(i - 1, 0)]))
    def _():
        w1b[...] = w1_ref[...].astype(BF16)
        w3b[...] = w3_ref[...].astype(BF16)
        w2b[...] = w2_ref[...].astype(BF16)

    def all_rows(hbm):
        return hbm.at[pl.ds(0, tme * sub), :]

    def step(par):
        xg, xc = (xa, xb) if par == 0 else (xb, xa)
        ys, yc = (ya, yb) if par == 0 else (yb, ya)

        @pl.when(i > 0)
        def _():
            pltpu.make_async_copy(all_rows(f_hbm), xc, gsem.at[1 - par]).wait()
            pltpu.make_async_copy(yc, all_rows(g_hbm), ssem.at[1 - par]).wait()

        pw1, pw2 = min(MXU_N, fdim), min(MXU_N, half)
        per_piece = -(-tme // (2 * (fdim // pw1)))
        issued = [0]

        def issue_rows():
            for r in range(issued[0], min(issued[0] + per_piece, tme)):
                tok = pl.multiple_of(tok_ref[r] * sub, sub)
                dst = pl.multiple_of(dst_ref[r] * sub, sub)
                pltpu.make_async_copy(f_hbm.at[pl.ds(tok, sub), :], xg.at[pl.ds(r * sub, sub), :],
                                      gsem.at[par]).start(priority=1)
                pltpu.make_async_copy(ys.at[pl.ds(r * sub, sub), :], g_hbm.at[pl.ds(dst, sub), :],
                                      ssem.at[par]).start()
            issued[0] = min(issued[0] + per_piece, tme)

        h = _load_rows_bf16(xc, tme)
        gate, up = [], []
        for c0 in range(0, fdim, pw1):
            gate.append(_bdot(h, w1b[:, c0:c0 + pw1]))
            issue_rows()
        for c0 in range(0, fdim, pw1):
            up.append(_bdot(h, w3b[:, c0:c0 + pw1]))
            issue_rows()
        a = (_silu(jnp.concatenate(gate, axis=1)) * jnp.concatenate(up, axis=1)).astype(BF16)
        assert issued[0] == tme
        for c0 in range(0, half, pw2):
            lo = _bdot(a, w2b[:, c0:c0 + pw2])
            hi = _bdot(a, w2b[:, half + c0:half + c0 + pw2])
            packed = _pack_pairs(jnp.concatenate([lo, hi], axis=1))
            for b0 in range(0, pw2, LANES):
                yc[pl.ds((c0 + b0) // LANES, tme, stride=sub), :] = packed[:, b0:b0 + LANES]

        @pl.when(i == last)
        def _():
            pltpu.make_async_copy(all_rows(f_hbm), xg, gsem.at[par]).wait()
            pltpu.make_async_copy(ys, all_rows(g_hbm), ssem.at[par]).wait()

    for par in range(2):
        pl.when(jnp.bitwise_and(i, 1) == par)(functools.partial(step, par))


def _experts(texp, tok, dst, w1, w3, w2, f, t, layer, tme):
    sub = f.shape[0] // t
    d, fdim = w1.shape[-2:]
    ntile = texp.shape[0]
    wsel = lambda i, te: (layer, te[i], 0, 0)
    rows = lambda: pl.BlockSpec((tme,), lambda i, te: (i,), memory_space=pltpu.SMEM)
    out = pl.pallas_call(
        functools.partial(_experts_kernel, tme=tme),
        out_shape=jax.ShapeDtypeStruct(((TOP_K + 1) * t * sub, LANES), U32),
        grid_spec=pltpu.PrefetchScalarGridSpec(
            num_scalar_prefetch=1,
            grid=(ntile,),
            in_specs=[rows(), rows(),
                      pl.BlockSpec((None, None, d, fdim), wsel), pl.BlockSpec((None, None, d, fdim), wsel),
                      pl.BlockSpec((None, None, fdim, d), wsel),
                      pl.BlockSpec(memory_space=pl.ANY)],
            out_specs=pl.BlockSpec(memory_space=pl.ANY),
            scratch_shapes=[pltpu.VMEM((d, fdim), BF16), pltpu.VMEM((d, fdim), BF16), pltpu.VMEM((fdim, d), BF16)]
            + [pltpu.VMEM((tme * sub, LANES), U32)] * 4
            + [pltpu.SemaphoreType.DMA((2,)), pltpu.SemaphoreType.DMA((2,))]),
        compiler_params=_params("arbitrary"),
        name="moe_experts",
    )(texp, tok, dst, w1, w3, w2, f)
    return out.reshape(TOP_K + 1, t * sub, LANES)


def _combine_kernel(wt_ref, x1_ref, sh_ref, gf_ref, lng_ref, lnb_ref, g_ref, o_ref, *, alpha):
    tm = x1_ref.shape[0]
    lo = hi = None
    for k in range(TOP_K):
        w = wt_ref[:, k:k + 1]
        rl, rh = _load_row_tiles(g_ref.at[k], tm)
        lo = [w * b for b in rl] if lo is None else [a + w * b for a, b in zip(lo, rl)]
        hi = [w * b for b in rh] if hi is None else [a + w * b for a, b in zip(hi, rh)]
    f = jnp.concatenate(lo + hi, axis=1) + sh_ref[...].astype(F32)
    o_ref[...] = _layernorm(alpha * x1_ref[...] + gf_ref[0] * f, lng_ref[0], lnb_ref[0])


def _combine(wt, x1, sh, gf, lng, lnb, g, seq, layer, alpha):
    t, d = x1.shape
    sub = g.shape[1] // t
    tm = min(256, seq)
    per_b = seq // tm
    row = lambda i: (i, 0)
    lrow = lambda i: (layer, 0, 0)
    return pl.pallas_call(
        functools.partial(_combine_kernel, alpha=alpha),
        out_shape=jax.ShapeDtypeStruct((t, d), F32),
        grid=(t // tm,),
        in_specs=[pl.BlockSpec((tm, TOP_K), row), pl.BlockSpec((tm, d), row), pl.BlockSpec((tm, d), row),
                  pl.BlockSpec((1, 1, d), lambda i: (i // per_b, 0, 0)),
                  pl.BlockSpec((1, 1, d), lrow), pl.BlockSpec((1, 1, d), lrow),
                  pl.BlockSpec((TOP_K, tm * sub, LANES), lambda i: (0, i, 0))],
        out_specs=pl.BlockSpec((tm, d), row),
        compiler_params=_params("parallel"),
        name="moe_combine_ln",
    )(wt, x1, sh, gf, lng, lnb, g)


def _moe(f, x1, ids, wts, rank, cnt, gf, lng, lnb, w1, w3, w2, ws1, ws3, ws2, seq, layer, alpha):
    t, d = x1.shape
    tme = min(512, t)
    counts = cnt[:, 0].astype(I32)
    tiles_e = (counts + tme - 1) // tme
    padded = tiles_e * tme
    base = jnp.cumsum(padded) - padded
    expert = jnp.arange(N_EXPERTS, dtype=I32)
    slots = jnp.sum(jnp.where(ids[:, :, None] == expert, base, 0), axis=-1) + rank
    ntile = (t * TOP_K) // tme + N_EXPERTS + 3
    nslot = ntile * tme
    tile_ends = jnp.cumsum(tiles_e)
    texp = jnp.sum((jnp.arange(ntile, dtype=I32)[:, None] >= tile_ends[None, :]).astype(I32), axis=1)
    texp = jnp.minimum(texp, N_EXPERTS - 1)
    texp_step = jnp.concatenate([texp[:1], texp[:-1]])

    inv = _inverse_map(slots.reshape(-1), nslot)
    spare = TOP_K * t + jnp.arange(nslot, dtype=I32) % tme
    tok = jnp.where(inv >= 0, inv % t, 0)
    dst = jnp.where(inv >= 0, inv, spare)
    dst_step = jnp.concatenate([spare[:2 * tme], dst[:-2 * tme]])

    shared = _shared_expert(f, ws1, ws3, ws2, t)
    g = _experts(texp_step, tok, dst_step, w1, w3, w2, f, t, layer, tme)
    return _combine(wts.T, x1, shared, gf, lng, lnb, g, seq, layer, alpha)


def kernel(x, c, ctx, c_ctx, w_ada, b_ada, ln_mix_g, ln_mix_b, ln_ffn_g, ln_ffn_b, attn_w_qkv, attn_w_o, attn_sink, gmlp_w_in, gmlp_b_in, gmlp_v_g, gmlp_v_b, gmlp_w_s, gmlp_b_s, gmlp_w_o, moe_w_router, moe_bias, moe_w1, moe_w3, moe_w2, moe_ws1, moe_ws3, moe_ws2):
    batch, seq, d = x.shape
    nctx = ctx.shape[1]
    depth = w_ada.shape[0]
    assert depth == 2, "layer 0 is the attention mixer, layer 1 the gMLP mixer"
    alpha = float((2 * depth) ** 0.25)
    t = batch * seq

    pad = (-(batch + 1)) % 8
    cc = jnp.concatenate([c, c_ctx[None, :], jnp.zeros((pad, d), F32)], axis=0)
    mods = _modulations(cc, w_ada, b_ada)

    def mod(layer, j, rows=slice(0, batch)):
        return mods[layer, rows, j * d:(j + 1) * d]

    def per_batch(layer, j):
        return mod(layer, j).reshape(batch, 1, d)

    def router(layer):
        w = jnp.pad(moe_w_router[layer], ((0, 0), (0, LANES - N_EXPERTS)))
        hi = w.astype(BF16)
        return hi, (w - hi.astype(F32)).astype(BF16), moe_bias[layer].reshape(N_EXPERTS, 1)

    def shared_w(layer):
        return moe_ws1[layer].astype(BF16), moe_ws3[layer].astype(BF16), moe_ws2[layer].astype(BF16)

    x2 = x.reshape(t, d)
    ln_mix_g, ln_mix_b, ln_ffn_g, ln_ffn_b = (
        p.reshape(depth, 1, d) for p in (ln_mix_g, ln_mix_b, ln_ffn_g, ln_ffn_b))

    qd = Q_PER_KV * N_KV_HEADS * HEAD_DIM
    kvd = N_KV_HEADS * HEAD_DIM
    wqkv = attn_w_qkv[0]
    dup = lambda w: jnp.concatenate([w.reshape(d, N_KV_HEADS, 1, HEAD_DIM)] * (LANES // HEAD_DIM), axis=2
                                    ).reshape(d, N_KV_HEADS * LANES)
    wk2, wv2 = dup(wqkv[:, qd:qd + kvd]), dup(wqkv[:, qd + kvd:])
    w_all = jnp.concatenate([wqkv[:, :qd] * (HEAD_DIM ** -0.5), wk2, wv2], axis=1).astype(BF16)
    q, k2, v2 = _qkv_proj(x2, per_batch(0, 1), per_batch(0, 0), w_all, _rope_tables(seq), seq)
    ctx_row = slice(batch, batch + 1)
    kc2, vc2 = _ctx_kv(ctx.reshape(batch * nctx, d), mod(0, 1, ctx_row), mod(0, 0, ctx_row),
                       jnp.concatenate([wk2, wv2], axis=1).astype(BF16))
    o = _attention(q, k2, v2, kc2, vc2, attn_sink[0], batch, seq, nctx)
    x1, f, ids, wts, rank, cnt = _attn_out(
        o, attn_w_o[0].astype(BF16), x2, per_batch(0, 2), per_batch(0, 4), per_batch(0, 3),
        ln_mix_g, ln_mix_b, *router(0), seq, 0, alpha)
    x2 = _moe(f, x1, ids, wts, rank, cnt, per_batch(0, 5), ln_ffn_g, ln_ffn_b,
              moe_w1, moe_w3, moe_w2, *shared_w(0), seq, 0, alpha)

    width = gmlp_w_in.shape[2] // 2
    w_in = gmlp_w_in[0].astype(BF16)
    b_in = gmlp_b_in[0].reshape(1, 2 * width)
    vg, vb = gmlp_v_g[0].reshape(1, width), gmlp_v_b[0].reshape(1, width)
    sc, sh = per_batch(1, 1), per_batch(1, 0)
    u = _gmlp_in(x2, sc, sh, w_in[:, :width], b_in[:, :width], vg, vb, seq, False)
    v = _gmlp_in(x2, sc, sh, w_in[:, width:], b_in[:, width:], vg, vb, seq, True)
    x1, f, ids, wts, rank, cnt = _gmlp_out(
        u, v, gmlp_w_s[0].astype(BF16), gmlp_b_s[0][:, :, None], gmlp_w_o[0].astype(BF16), x2,
        per_batch(1, 2), per_batch(1, 4), per_batch(1, 3), ln_mix_g, ln_mix_b, *router(1), seq, 1, alpha)
    x2 = _moe(f, x1, ids, wts, rank, cnt, per_batch(1, 5), ln_ffn_g, ln_ffn_b,
              moe_w1, moe_w3, moe_w2, *shared_w(1), seq, 1, alpha)
    return x2.reshape(batch, seq, d)
```

```python
import functools

import jax
import jax.numpy as jnp
from jax import lax
from jax.experimental import pallas as pl
from jax.experimental.pallas import tpu as pltpu
from jax.experimental.pallas import tpu_sc as plsc

F32 = jnp.float32
BF16 = jnp.bfloat16
I32 = jnp.int32

GRID_W = 64
N_KV_HEADS = 4
Q_PER_KV = 8
HEAD_DIM = 64
ROPE_HALF = 16
WINDOW = 128
Q_BLOCK = 128
ROPE_BASE = 10000.0
NEG_INF = -1e30
CHUNK = 128
N_GMLP_GROUPS = 8
N_EXPERTS = 64
TOP_K = 8
N_EXPERT_GROUPS = 8
EXPERTS_PER_GROUP = 8
TOPK_GROUPS = 4
ROUTED_SCALE = 2.5
LN_EPS = 1e-5

LANES = 128
VMEM_LIMIT_BYTES = 56 * 1024 * 1024


def _params(*sem):
    return pltpu.CompilerParams(dimension_semantics=sem, vmem_limit_bytes=VMEM_LIMIT_BYTES)


def _silu(a):
    return a * jax.nn.sigmoid(a)


def _layernorm(v, g, b):
    mu = jnp.mean(v, axis=-1, keepdims=True)
    d = v - mu
    var = jnp.mean(d * d, axis=-1, keepdims=True)
    return d * lax.rsqrt(var + LN_EPS) * g + b


def _bdot(a, b):
    return jnp.dot(a, b, preferred_element_type=F32)


U32 = jnp.uint32


def _pack_pairs(a):
    n = a.shape[1] // 2
    lo = lax.bitcast_convert_type(a[:, :n].astype(BF16).astype(F32), U32)
    hi = lax.bitcast_convert_type(a[:, n:].astype(BF16).astype(F32), U32)
    return hi | (lo >> 16)


def _unpack_pairs(p):
    lo = lax.bitcast_convert_type(p << 16, F32)
    hi = lax.bitcast_convert_type(p & jnp.uint32(0xFFFF0000), F32)
    return lo, hi


def _store_row_tiles(ref, packed):
    r, n = packed.shape
    sub = n // LANES
    for j in range(sub):
        ref[pl.ds(j, r, stride=sub), :] = packed[:, LANES * j:LANES * (j + 1)]


def _load_row_tiles(ref, r):
    sub = ref.shape[0] // r
    pieces = [_unpack_pairs(ref[pl.ds(j, r, stride=sub), :]) for j in range(sub)]
    return [p[0] for p in pieces], [p[1] for p in pieces]


def _load_rows_bf16(ref, r):
    lo, hi = _load_row_tiles(ref, r)
    return jnp.concatenate(lo + hi, axis=1).astype(BF16)


def _mod_kernel(c_ref, w_ref, b_ref, o_ref):
    a = _silu(c_ref[...]).astype(BF16)
    o_ref[0] = _bdot(a, w_ref[0].astype(BF16)) + b_ref[0]


def _modulations(cc, w_ada, b_ada):
    depth, d, n6 = w_ada.shape
    r = cc.shape[0]
    tn = max(w for w in range(LANES, min(1024, n6) + 1, LANES) if n6 % w == 0)
    return pl.pallas_call(
        _mod_kernel,
        out_shape=jax.ShapeDtypeStruct((depth, r, n6), F32),
        grid=(depth, n6 // tn),
        in_specs=[pl.BlockSpec((r, d), lambda l, j: (0, 0)),
                  pl.BlockSpec((1, d, tn), lambda l, j: (l, 0, j)),
                  pl.BlockSpec((1, 1, tn), lambda l, j: (l, 0, j))],
        out_specs=pl.BlockSpec((1, r, tn), lambda l, j: (l, 0, j)),
        compiler_params=_params("parallel", "parallel"),
        name="adaln_mod",
    )(cc, w_ada, b_ada.reshape(depth, 1, n6))


def _rope_tables(seq):
    rows = seq // GRID_W
    row_ids = jnp.repeat(jnp.arange(rows, dtype=F32), GRID_W)
    col_ids = jnp.tile(jnp.arange(GRID_W, dtype=F32), rows)
    inv_freq = ROPE_BASE ** (-jnp.arange(0, 2 * ROPE_HALF, 2, dtype=F32) / (2 * ROPE_HALF))
    ar, ac = row_ids[:, None] * inv_freq, col_ids[:, None] * inv_freq
    z = jnp.zeros_like(ar)
    cos = jnp.concatenate([jnp.cos(ar), jnp.cos(ar), jnp.cos(ac), jnp.cos(ac)], axis=-1)
    sin_lo = jnp.concatenate([-jnp.sin(ar), z, -jnp.sin(ac), z], axis=-1)
    sin_hi = jnp.concatenate([z, jnp.sin(ar), z, jnp.sin(ac)], axis=-1)
    rep = LANES // HEAD_DIM
    return tuple(jnp.tile(t, (1, rep)) for t in (cos, sin_lo, sin_hi))


def _qkv_kernel(x_ref, sc_ref, sh_ref, w_ref, cos_ref, slo_ref, shi_ref, q_ref, k_ref, v_ref, *, cw):
    h = (x_ref[...] * (1.0 + sc_ref[0]) + sh_ref[0]).astype(BF16)
    cos, slo, shi = cos_ref[...], slo_ref[...], shi_ref[...]

    def rope(a):
        return (a * cos + pltpu.roll(a, LANES - ROPE_HALF, 1) * slo
                + pltpu.roll(a, ROPE_HALF, 1) * shi)

    qd, kd = q_ref.shape[1], k_ref.shape[1]
    for c0 in range(0, qd, cw):
        acc = _bdot(h, w_ref[:, c0:c0 + cw])
        for b0 in range(0, cw, LANES):
            q_ref[:, c0 + b0:c0 + b0 + LANES] = rope(acc[:, b0:b0 + LANES]).astype(BF16)
    acc = _bdot(h, w_ref[:, qd:qd + kd])
    for b0 in range(0, kd, LANES):
        k_ref[:, b0:b0 + LANES] = rope(acc[:, b0:b0 + LANES]).astype(BF16)
    v_ref[...] = _bdot(h, w_ref[:, qd + kd:qd + 2 * kd]).astype(BF16)


def _qkv_proj(x2, sc, sh, w, tables, seq):
    t, d = x2.shape
    kd = N_KV_HEADS * LANES
    qd = w.shape[1] - 2 * kd
    tm = min(512, seq)
    per_b = seq // tm
    row = lambda i: (i, 0)
    mod = lambda i: (i // per_b, 0, 0)
    tab = lambda i: (i % per_b, 0)
    return pl.pallas_call(
        functools.partial(_qkv_kernel, cw=min(512, qd)),
        out_shape=(jax.ShapeDtypeStruct((t, qd), BF16), jax.ShapeDtypeStruct((t, kd), BF16),
                   jax.ShapeDtypeStruct((t, kd), BF16)),
        grid=(t // tm,),
        in_specs=[pl.BlockSpec((tm, d), row), pl.BlockSpec((1, 1, d), mod), pl.BlockSpec((1, 1, d), mod),
                  pl.BlockSpec(w.shape, lambda i: (0, 0)),
                  pl.BlockSpec((tm, LANES), tab), pl.BlockSpec((tm, LANES), tab), pl.BlockSpec((tm, LANES), tab)],
        out_specs=(pl.BlockSpec((tm, qd), row), pl.BlockSpec((tm, kd), row), pl.BlockSpec((tm, kd), row)),
        compiler_params=_params("parallel"),
        name="qkv_rope",
    )(x2, sc, sh, w, *tables)


def _ctxkv_kernel(x_ref, sc_ref, sh_ref, w_ref, k_ref, v_ref):
    h = (x_ref[...] * (1.0 + sc_ref[...]) + sh_ref[...]).astype(BF16)
    kd = k_ref.shape[1]
    k_ref[...] = _bdot(h, w_ref[:, :kd]).astype(BF16)
    v_ref[...] = _bdot(h, w_ref[:, kd:]).astype(BF16)


def _ctx_kv(c2, sc, sh, w):
    t, d = c2.shape
    kd = w.shape[1] // 2
    tm = min(512, t)
    row = lambda i: (i, 0)
    fix = lambda i: (0, 0)
    return pl.pallas_call(
        _ctxkv_kernel,
        out_shape=(jax.ShapeDtypeStruct((t, kd), BF16), jax.ShapeDtypeStruct((t, kd), BF16)),
        grid=(t // tm,),
        in_specs=[pl.BlockSpec((tm, d), row), pl.BlockSpec((1, d), fix), pl.BlockSpec((1, d), fix),
                  pl.BlockSpec(w.shape, fix)],
        out_specs=(pl.BlockSpec((tm, kd), row), pl.BlockSpec((tm, kd), row)),
        compiler_params=_params("parallel"),
        name="ctx_kv",
    )(c2, sc, sh, w)


def _attn_kernel(sink_ref, q_ref, k_ref, v_ref, kc_ref, vc_ref, o_ref):
    kv, i = pl.program_id(1), pl.program_id(2)
    seq, nctx = k_ref.shape[1], kc_ref.shape[1]
    nwin = Q_BLOCK + 2 * WINDOW
    start = pl.multiple_of(jnp.clip(i * Q_BLOCK - WINDOW, 0, seq - nwin), Q_BLOCK)
    nkey = nctx + nwin
    kall = jnp.concatenate([kc_ref[0], k_ref[0, pl.ds(start, nwin), :]], axis=0)
    vall = jnp.concatenate([vc_ref[0], v_ref[0, pl.ds(start, nwin), :]], axis=0)
    low = lax.broadcasted_iota(I32, (nkey, LANES), 1) < HEAD_DIM
    zero = jnp.zeros((nkey, LANES), BF16)
    npair = Q_PER_KV // 2
    q = q_ref[0]
    qs = jnp.concatenate([q[:, LANES * j:LANES * (j + 1)] for j in range(npair)], axis=0)
    nrow = npair * Q_BLOCK
    rows = lax.broadcasted_iota(I32, (Q_BLOCK, nkey), 0)
    cols = lax.broadcasted_iota(I32, (Q_BLOCK, nkey), 1)
    dist = (i * Q_BLOCK - start + nctx) + rows - cols
    valid = (cols < nctx) | (jnp.abs(dist) <= WINDOW)
    contract_last = (((1,), (1,)), ((), ()))
    out = jnp.zeros((nrow, LANES), F32)
    for par in range(2):
        keep = low if par == 0 else jnp.logical_not(low)
        kh = jnp.where(keep, kall, zero)
        vh = jnp.where(keep, vall, zero)
        s = lax.dot_general(qs, kh, contract_last, preferred_element_type=F32)
        s = jnp.concatenate(
            [jnp.where(valid, s[Q_BLOCK * j:Q_BLOCK * (j + 1), :], NEG_INF) for j in range(npair)], axis=0)
        sink = jnp.concatenate(
            [jnp.full((Q_BLOCK, 1), sink_ref[kv * Q_PER_KV + 2 * j + par], F32) for j in range(npair)], axis=0)
        m = jnp.maximum(jnp.max(s, axis=1, keepdims=True), sink)
        p = jnp.exp(s - m)
        den = jnp.sum(p, axis=1, keepdims=True) + jnp.exp(sink - m)
        out = out + _bdot(p.astype(BF16), vh) / den
    for j in range(npair):
        o_ref[0, :, LANES * j:LANES * (j + 1)] = out[Q_BLOCK * j:Q_BLOCK * (j + 1), :].astype(BF16)


def _attention(q, k2, v2, kc2, vc2, sink, batch, seq, nctx):
    qd = q.shape[1]
    gw = Q_PER_KV * HEAD_DIM
    q3 = q.reshape(batch, seq, qd)
    k3, v3 = k2.reshape(batch, seq, -1), v2.reshape(batch, seq, -1)
    kc3, vc3 = kc2.reshape(batch, nctx, -1), vc2.reshape(batch, nctx, -1)
    full = lambda b, h, i: (b, 0, h)
    blk = lambda b, h, i: (b, i, h)
    out = pl.pallas_call(
        _attn_kernel,
        out_shape=jax.ShapeDtypeStruct((batch, seq, qd), BF16),
        grid=(batch, N_KV_HEADS, seq // Q_BLOCK),
        in_specs=[pl.BlockSpec(memory_space=pltpu.SMEM),
                  pl.BlockSpec((1, Q_BLOCK, gw), blk),
                  pl.BlockSpec((1, seq, LANES), full), pl.BlockSpec((1, seq, LANES), full),
                  pl.BlockSpec((1, nctx, LANES), full), pl.BlockSpec((1, nctx, LANES), full)],
        out_specs=pl.BlockSpec((1, Q_BLOCK, gw), blk),
        compiler_params=_params("parallel", "parallel", "parallel"),
        name="window_gqa",
    )(sink, q3, k3, v3, kc3, vc3)
    return out.reshape(batch * seq, qd)


def _split_bf16(a):
    hi = a.astype(BF16)
    return hi, (a - hi.astype(F32)).astype(BF16)


def _post_mixer(x, y, gm, lng, lnb, scf, shf, wr_ref, rb_ref, run_ref,
                x1_ref, f_ref, ids_ref, wts_ref, rank_ref, cnt_ref, alpha):
    tm = x.shape[0]
    x1 = _layernorm(alpha * x + gm * y, lng, lnb)
    x1_ref[...] = x1
    f = x1 * (1.0 + scf) + shf
    _store_row_tiles(f_ref, _pack_pairs(f))

    fh, fl = _split_bf16(f)
    prod = _bdot(fh, wr_ref[...]) + _bdot(fl, wr_ref[...])
    logits = (prod[:, :LANES] + prod[:, LANES:]).T[:N_EXPERTS, :]
    scores = jax.nn.sigmoid(logits)
    biased = scores + rb_ref[...]

    eg = EXPERTS_PER_GROUP
    sub = lax.broadcasted_iota(I32, (eg, tm), 0)
    gscore = []
    for g in range(N_EXPERT_GROUPS):
        tg = biased[eg * g:eg * (g + 1), :]
        m1 = jnp.max(tg, axis=0, keepdims=True)
        i1 = jnp.min(jnp.where(tg == m1, sub, eg), axis=0, keepdims=True)
        m2 = jnp.max(jnp.where(sub == i1, -jnp.inf, tg), axis=0, keepdims=True)
        gscore.append(m1 + m2)
    gsel = [jnp.zeros((1, tm), jnp.bool_) for _ in range(N_EXPERT_GROUPS)]
    for _ in range(TOPK_GROUPS):
        best = functools.reduce(jnp.maximum, gscore)
        taken = jnp.zeros((1, tm), jnp.bool_)
        for g in range(N_EXPERT_GROUPS):
            hit = jnp.logical_and(gscore[g] == best, jnp.logical_not(taken))
            taken = jnp.logical_or(taken, hit)
            gsel[g] = jnp.logical_or(gsel[g], hit)
            gscore[g] = jnp.where(hit, -jnp.inf, gscore[g])
    cur = jnp.concatenate(
        [jnp.where(gsel[g], biased[eg * g:eg * (g + 1), :], -jnp.inf) for g in range(N_EXPERT_GROUPS)], axis=0)

    eidx = lax.broadcasted_iota(I32, (N_EXPERTS, tm), 0)
    picks, wts, hots = [], [], []
    for _ in range(TOP_K):
        m = jnp.max(cur, axis=0, keepdims=True)
        idx = jnp.min(jnp.where(cur == m, eidx, N_EXPERTS), axis=0, keepdims=True)
        hot = eidx == idx
        picks.append(idx)
        wts.append(jnp.sum(jnp.where(hot, scores, 0.0), axis=0, keepdims=True))
        hots.append(hot)
        cur = jnp.where(hot, -jnp.inf, cur)
    wsum = functools.reduce(jnp.add, wts)

    assigned = functools.reduce(jnp.add, [h.astype(F32) for h in hots])
    before = (lax.broadcasted_iota(I32, (tm, tm), 0) < lax.broadcasted_iota(I32, (tm, tm), 1)).astype(BF16)
    pos = _bdot(assigned.astype(BF16), before) + run_ref[:, 0:1]
    for k in range(TOP_K):
        ids_ref[k:k + 1, :] = picks[k]
        wts_ref[k:k + 1, :] = wts[k] / wsum * ROUTED_SCALE
        rank_ref[k:k + 1, :] = jnp.sum(jnp.where(hots[k], pos, 0.0), axis=0, keepdims=True).astype(I32)
    run_ref[...] = run_ref[...] + jnp.sum(assigned, axis=1, keepdims=True)
    cnt_ref[...] = run_ref[...]


def _route_out_shapes(t, d):
    return (jax.ShapeDtypeStruct((t, d), F32), jax.ShapeDtypeStruct((t * d // (2 * LANES), LANES), U32),
            jax.ShapeDtypeStruct((TOP_K, t), I32), jax.ShapeDtypeStruct((TOP_K, t), F32),
            jax.ShapeDtypeStruct((TOP_K, t), I32), jax.ShapeDtypeStruct((N_EXPERTS, LANES), F32))


def _route_out_specs(tm, d):
    row = lambda i: (i, 0)
    col = lambda i: (0, i)
    return (pl.BlockSpec((tm, d), row), pl.BlockSpec((tm * d // (2 * LANES), LANES), row),
            pl.BlockSpec((TOP_K, tm), col), pl.BlockSpec((TOP_K, tm), col), pl.BlockSpec((TOP_K, tm), col),
            pl.BlockSpec((N_EXPERTS, LANES), lambda i: (0, 0)))


def _route_in_specs(tm, d, per_b, layer):
    mod = lambda i: (i // per_b, 0, 0)
    lrow = lambda i: (layer, 0, 0)
    fix = lambda i: (0, 0)
    return [pl.BlockSpec((tm, d), lambda i: (i, 0)),
            pl.BlockSpec((1, 1, d), mod), pl.BlockSpec((1, 1, d), mod), pl.BlockSpec((1, 1, d), mod),
            pl.BlockSpec((1, 1, d), lrow), pl.BlockSpec((1, 1, d), lrow),
            pl.BlockSpec((d, 2 * LANES), fix),
            pl.BlockSpec((N_EXPERTS, 1), fix)]


def _oproj_kernel(o_ref, wo_ref, x_ref, gm_ref, scf_ref, shf_ref, lng_ref, lnb_ref, wr_ref, rb_ref,
                  x1_ref, f_ref, ids_ref, wts_ref, rank_ref, cnt_ref, run_ref, *, alpha):
    @pl.when(pl.program_id(0) == 0)
    def _():
        run_ref[...] = jnp.zeros_like(run_ref)

    y = _bdot(o_ref[...], wo_ref[...])
    _post_mixer(x_ref[...], y, gm_ref[0], lng_ref[0], lnb_ref[0], scf_ref[0], shf_ref[0],
                wr_ref, rb_ref, run_ref, x1_ref, f_ref, ids_ref, wts_ref, rank_ref, cnt_ref, alpha)


def _attn_out(o, wo, x2, gm, scf, shf, lng, lnb, wr, rb, seq, layer, alpha):
    t, d = x2.shape
    tm = min(512, seq)
    return pl.pallas_call(
        functools.partial(_oproj_kernel, alpha=alpha),
        out_shape=_route_out_shapes(t, d),
        grid=(t // tm,),
        in_specs=[pl.BlockSpec((tm, o.shape[1]), lambda i: (i, 0)), pl.BlockSpec(wo.shape, lambda i: (0, 0))]
        + _route_in_specs(tm, d, seq // tm, layer),
        out_specs=_route_out_specs(tm, d),
        scratch_shapes=[pltpu.VMEM((N_EXPERTS, LANES), F32)],
        compiler_params=_params("arbitrary"),
        name="attn_out_route",
    )(o, wo, x2, gm, scf, shf, lng, lnb, wr, rb)


def _gmlp_in_kernel(x_ref, sc_ref, sh_ref, w_ref, b_ref, g_ref, beta_ref, o_ref, *, cw, normalize):
    h = (x_ref[...] * (1.0 + sc_ref[0]) + sh_ref[0]).astype(BF16)
    n = o_ref.shape[1]
    inv_sqrt2 = 0.7071067811865476
    parts = []
    for c0 in range(0, n, cw):
        z = _bdot(h, w_ref[:, c0:c0 + cw]) + b_ref[:, c0:c0 + cw]
        z = 0.5 * z * (1.0 + lax.erf(z * inv_sqrt2))
        if normalize:
            parts.append(z)
        else:
            o_ref[:, c0:c0 + cw] = z.astype(BF16)
    if normalize:
        v = jnp.concatenate(parts, axis=1)
        o_ref[...] = _layernorm(v, g_ref[...], beta_ref[...]).astype(BF16)


def _gmlp_in(x2, sc, sh, w, b, g, beta, seq, normalize):
    t, d = x2.shape
    n = w.shape[1]
    tm = min(256, seq)
    per_b = seq // tm
    row = lambda i: (i, 0)
    mod = lambda i: (i // per_b, 0, 0)
    fix = lambda i: (0, 0)
    return pl.pallas_call(
        functools.partial(_gmlp_in_kernel, cw=min(512, n), normalize=normalize),
        out_shape=jax.ShapeDtypeStruct((t, n), BF16),
        grid=(t // tm,),
        in_specs=[pl.BlockSpec((tm, d), row), pl.BlockSpec((1, 1, d), mod), pl.BlockSpec((1, 1, d), mod),
                  pl.BlockSpec(w.shape, fix), pl.BlockSpec((1, n), fix), pl.BlockSpec((1, n), fix),
                  pl.BlockSpec((1, n), fix)],
        out_specs=pl.BlockSpec((tm, n), row),
        compiler_params=_params("parallel"),
        name="gmlp_in_v" if normalize else "gmlp_in_u",
    )(x2, sc, sh, w, b, g, beta)


def _gmlp_out_kernel(u_ref, v_ref, ws_ref, bs_ref, wo_ref, x_ref, gm_ref, scf_ref, shf_ref, lng_ref, lnb_ref,
                     wr_ref, rb_ref, x1_ref, f_ref, ids_ref, wts_ref, rank_ref, cnt_ref,
                     run_ref, gated_ref, *, alpha):
    @pl.when(pl.program_id(0) == 0)
    def _():
        run_ref[...] = jnp.zeros_like(run_ref)

    tm, width = u_ref.shape
    gd = width // N_GMLP_GROUPS
    for r0 in range(0, tm, CHUNK):
        for g in range(N_GMLP_GROUPS):
            c0 = g * gd
            mixed = _bdot(ws_ref[g], v_ref[r0:r0 + CHUNK, c0:c0 + gd]) + bs_ref[g]
            gated_ref[r0:r0 + CHUNK, c0:c0 + gd] = (
                u_ref[r0:r0 + CHUNK, c0:c0 + gd].astype(F32) * mixed).astype(BF16)
    y = _bdot(gated_ref[...], wo_ref[...])
    _post_mixer(x_ref[...], y, gm_ref[0], lng_ref[0], lnb_ref[0], scf_ref[0], shf_ref[0],
                wr_ref, rb_ref, run_ref, x1_ref, f_ref, ids_ref, wts_ref, rank_ref, cnt_ref, alpha)


def _gmlp_out(u, v, ws, bs, wo, x2, gm, scf, shf, lng, lnb, wr, rb, seq, layer, alpha):
    t, d = x2.shape
    width = u.shape[1]
    tm = min(256, seq)
    row = lambda i: (i, 0)
    return pl.pallas_call(
        functools.partial(_gmlp_out_kernel, alpha=alpha),
        out_shape=_route_out_shapes(t, d),
        grid=(t // tm,),
        in_specs=[pl.BlockSpec((tm, width), row), pl.BlockSpec((tm, width), row),
                  pl.BlockSpec(ws.shape, lambda i: (0, 0, 0)), pl.BlockSpec(bs.shape, lambda i: (0, 0, 0)),
                  pl.BlockSpec(wo.shape, lambda i: (0, 0))]
        + _route_in_specs(tm, d, seq // tm, layer),
        out_specs=_route_out_specs(tm, d),
        scratch_shapes=[pltpu.VMEM((N_EXPERTS, LANES), F32), pltpu.VMEM((tm, width), BF16)],
        compiler_params=_params("arbitrary"),
        name="gmlp_out_route",
    )(u, v, ws, bs, wo, x2, gm, scf, shf, lng, lnb, wr, rb)


SC_CORES, SC_SUBCORES, SC_LANES = 2, 16, 16
MXU_N = 256
MOE_CHUNKS = 2


def _shared_kernel(f_ref, ws1_ref, ws3_ref, ws2_ref, sh_ref):
    h = _load_rows_bf16(f_ref, sh_ref.shape[0])
    a = _silu(_bdot(h, ws1_ref[...])) * _bdot(h, ws3_ref[...])
    sh_ref[...] = _bdot(a.astype(BF16), ws2_ref[...]).astype(BF16)


def _shared_expert(f, ws1, ws3, ws2, t):
    sub = f.shape[0] // t
    d = ws1.shape[0]
    tm = min(512, t)
    fix = lambda i: (0, 0)
    return pl.pallas_call(
        _shared_kernel,
        out_shape=jax.ShapeDtypeStruct((t, d), BF16),
        grid=(t // tm,),
        in_specs=[pl.BlockSpec((tm * sub, LANES), lambda i: (i, 0)),
                  pl.BlockSpec(ws1.shape, fix), pl.BlockSpec(ws3.shape, fix), pl.BlockSpec(ws2.shape, fix)],
        out_specs=pl.BlockSpec((tm, d), lambda i: (i, 0)),
        compiler_params=_params("parallel"),
        name="moe_shared",
    )(f, ws1, ws3, ws2)


def _inverse_map(slots_flat, nslot):
    n = slots_flat.shape[0]
    workers = SC_CORES * SC_SUBCORES
    per_w = nslot // workers
    chunk = min(8192, n)
    assert nslot % (workers * SC_LANES) == 0 and n % chunk == 0 and chunk % SC_LANES == 0
    mesh = plsc.VectorSubcoreMesh(core_axis_name="c", subcore_axis_name="s")

    @functools.partial(
        pl.kernel, out_type=jax.ShapeDtypeStruct((nslot,), I32), mesh=mesh,
        scratch_types=[pltpu.VMEM((per_w,), I32), pltpu.VMEM((chunk,), I32)],
        compiler_params=pltpu.CompilerParams(needs_layout_passes=False))
    def inverse(slots_hbm, inv_hbm, local, buf):
        lo = (lax.axis_index("s") * SC_CORES + lax.axis_index("c")) * per_w
        unused = jnp.full((SC_LANES,), -1, I32)

        @pl.loop(0, per_w, step=SC_LANES)
        def _(j):
            local[pl.ds(j, SC_LANES)] = unused

        lane = lax.iota(I32, SC_LANES)

        @pl.loop(0, n, step=chunk)
        def _(c0):
            pltpu.sync_copy(slots_hbm.at[pl.ds(c0, chunk)], buf)

            @pl.loop(0, chunk, step=SC_LANES)
            def _(j):
                idx = buf[pl.ds(j, SC_LANES)] - lo
                mine = jnp.logical_and(idx >= 0, idx < per_w)
                plsc.store_scatter(local, [idx], c0 + j + lane, mask=mine)

        pltpu.sync_copy(local, inv_hbm.at[pl.ds(lo, per_w)])

    return inverse(slots_flat)


SC_GATHER_ROWS = 32


def _sc_gather(table, idx):
    n = idx.shape[0]
    _, sub, lanes = table.shape
    per_w = n // (SC_CORES * SC_SUBCORES)
    win = SC_GATHER_ROWS
    assert n % (SC_CORES * SC_SUBCORES) == 0 and per_w % (2 * win) == 0
    mesh = plsc.VectorSubcoreMesh(core_axis_name="c", subcore_axis_name="s")
    row_buf = pltpu.VMEM((win, sub, lanes), table.dtype)

    @functools.partial(
        pl.kernel, out_type=jax.ShapeDtypeStruct((n, sub, lanes), table.dtype), mesh=mesh,
        scratch_types=[pltpu.VMEM((per_w,), I32), row_buf, row_buf,
                       pltpu.SemaphoreType.DMA, pltpu.SemaphoreType.DMA])
    def gather(x_hbm, i_hbm, o_hbm, idx_v, buf0, buf1, sem0, sem1):
        base = (lax.axis_index("s") * SC_CORES + lax.axis_index("c")) * per_w
        pltpu.sync_copy(i_hbm.at[pl.ds(base, per_w)], idx_v)

        @pl.loop(0, per_w, step=2 * win)
        def _(j):
            g0 = pltpu.async_copy(x_hbm.at[idx_v.at[pl.ds(j, win)]], buf0, sem0)
            g1 = pltpu.async_copy(x_hbm.at[idx_v.at[pl.ds(j + win, win)]], buf1, sem1)
            g0.wait()
            pltpu.sync_copy(buf0, o_hbm.at[pl.ds(base + j, win)])
            g1.wait()
            pltpu.sync_copy(buf1, o_hbm.at[pl.ds(base + j + win, win)])

    return gather(table, idx)


def _experts_kernel(texp_ref, dst_ref, xs_ref, w1_ref, w3_ref, w2_ref, *rest, tme):
    g_hbm, w1b, w3b, w2b, ya, yb, ssem = rest[-7:]
    i = pl.program_id(0)
    last = pl.num_programs(0) - 1
    sub = ya.shape[0] // tme
    d, fdim = w1b.shape
    half = d // 2

    @pl.when(i == 0)
    def _():
        ya[...] = jnp.zeros_like(ya)
        yb[...] = jnp.zeros_like(yb)

    @pl.when(jnp.logical_or(i == 0, texp_ref[i] != texp_ref[jnp.maximum(i - 1, 0)]))
    def _():
        w1b[...] = w1_ref[...].astype(BF16)
        w3b[...] = w3_ref[...].astype(BF16)
        w2b[...] = w2_ref[...].astype(BF16)

    def all_rows(hbm):
        return hbm.at[pl.ds(0, tme * sub), :]

    def step(par):
        yc, ys = (ya, yb) if par == 0 else (yb, ya)
        pw1, pw2 = min(MXU_N, fdim), min(MXU_N, half)
        per_piece = -(-tme // (2 * (fdim // pw1)))
        issued = [0]

        def issue_rows():
            for r in range(issued[0], min(issued[0] + per_piece, tme)):
                dst = pl.multiple_of(dst_ref[r] * sub, sub)
                pltpu.make_async_copy(ys.at[pl.ds(r * sub, sub), :], g_hbm.at[pl.ds(dst, sub), :],
                                      ssem.at[par]).start()
            issued[0] = min(issued[0] + per_piece, tme)

        h = _load_rows_bf16(xs_ref, tme)
        gate, up = [], []
        for c0 in range(0, fdim, pw1):
            gate.append(_bdot(h, w1b[:, c0:c0 + pw1]))
            issue_rows()
        for c0 in range(0, fdim, pw1):
            up.append(_bdot(h, w3b[:, c0:c0 + pw1]))
            issue_rows()
        assert issued[0] == tme
        a = (_silu(jnp.concatenate(gate, axis=1)) * jnp.concatenate(up, axis=1)).astype(BF16)

        @pl.when(i > 0)
        def _():
            pltpu.make_async_copy(yc, all_rows(g_hbm), ssem.at[1 - par]).wait()

        for c0 in range(0, half, pw2):
            lo = _bdot(a, w2b[:, c0:c0 + pw2])
            hi = _bdot(a, w2b[:, half + c0:half + c0 + pw2])
            packed = _pack_pairs(jnp.concatenate([lo, hi], axis=1))
            for b0 in range(0, pw2, LANES):
                yc[pl.ds((c0 + b0) // LANES, tme, stride=sub), :] = packed[:, b0:b0 + LANES]

        @pl.when(i == last)
        def _():
            pltpu.make_async_copy(ys, all_rows(g_hbm), ssem.at[par]).wait()

    for par in range(2):
        pl.when(jnp.bitwise_and(i, 1) == par)(functools.partial(step, par))


def _experts(texp, dst, xs, w1, w3, w2, g, g_rows, layer, tme):
    d, fdim = w1.shape[-2:]
    sub = d // (2 * LANES)
    ntile = xs.shape[0] // (tme * sub)
    wsel = lambda i, te: (layer, te[i], 0, 0)
    carried = [] if g is None else [g]
    return pl.pallas_call(
        functools.partial(_experts_kernel, tme=tme),
        out_shape=jax.ShapeDtypeStruct((g_rows, LANES), U32),
        grid_spec=pltpu.PrefetchScalarGridSpec(
            num_scalar_prefetch=1,
            grid=(ntile + 1,),
            in_specs=[pl.BlockSpec((tme,), lambda i, te: (i,), memory_space=pltpu.SMEM),
                      pl.BlockSpec((tme * sub, LANES), lambda i, te: (jnp.minimum(i, ntile - 1), 0)),
                      pl.BlockSpec((None, None, d, fdim), wsel), pl.BlockSpec((None, None, d, fdim), wsel),
                      pl.BlockSpec((None, None, fdim, d), wsel)]
            + [pl.BlockSpec(memory_space=pl.ANY)] * len(carried),
            out_specs=pl.BlockSpec(memory_space=pl.ANY),
            scratch_shapes=[pltpu.VMEM((d, fdim), BF16), pltpu.VMEM((d, fdim), BF16), pltpu.VMEM((fdim, d), BF16),
                            pltpu.VMEM((tme * sub, LANES), U32), pltpu.VMEM((tme * sub, LANES), U32),
                            pltpu.SemaphoreType.DMA((2,))]),
        input_output_aliases={6: 0} if carried else {},
        compiler_params=_params("arbitrary"),
        name="moe_experts",
    )(texp, dst, xs, w1, w3, w2, *carried)


def _combine_kernel(wt_ref, x1_ref, sh_ref, gf_ref, lng_ref, lnb_ref, g_ref, o_ref, *, alpha):
    tm = x1_ref.shape[0]
    lo = hi = None
    for k in range(TOP_K):
        w = wt_ref[:, k:k + 1]
        rl, rh = _load_row_tiles(g_ref.at[k], tm)
        lo = [w * b for b in rl] if lo is None else [a + w * b for a, b in zip(lo, rl)]
        hi = [w * b for b in rh] if hi is None else [a + w * b for a, b in zip(hi, rh)]
    f = jnp.concatenate(lo + hi, axis=1) + sh_ref[...].astype(F32)
    o_ref[...] = _layernorm(alpha * x1_ref[...] + gf_ref[0] * f, lng_ref[0], lnb_ref[0])


def _combine(wt, x1, sh, gf, lng, lnb, g, seq, layer, alpha):
    t, d = x1.shape
    sub = g.shape[1] // t
    tm = min(256, seq)
    per_b = seq // tm
    row = lambda i: (i, 0)
    lrow = lambda i: (layer, 0, 0)
    return pl.pallas_call(
        functools.partial(_combine_kernel, alpha=alpha),
        out_shape=jax.ShapeDtypeStruct((t, d), F32),
        grid=(t // tm,),
        in_specs=[pl.BlockSpec((tm, TOP_K), row), pl.BlockSpec((tm, d), row), pl.BlockSpec((tm, d), row),
                  pl.BlockSpec((1, 1, d), lambda i: (i // per_b, 0, 0)),
                  pl.BlockSpec((1, 1, d), lrow), pl.BlockSpec((1, 1, d), lrow),
                  pl.BlockSpec((TOP_K, tm * sub, LANES), lambda i: (0, i, 0))],
        out_specs=pl.BlockSpec((tm, d), row),
        compiler_params=_params("parallel"),
        name="moe_combine_ln",
    )(wt, x1, sh, gf, lng, lnb, g)


def _moe(f, x1, ids, wts, rank, cnt, gf, lng, lnb, w1, w3, w2, ws1, ws3, ws2, seq, layer, alpha):
    t, d = x1.shape
    tme = min(512, t)
    counts = cnt[:, 0].astype(I32)
    tiles_e = (counts + tme - 1) // tme
    padded = tiles_e * tme
    base = jnp.cumsum(padded) - padded
    expert = jnp.arange(N_EXPERTS, dtype=I32)
    slots = jnp.sum(jnp.where(ids[:, :, None] == expert, base, 0), axis=-1) + rank
    sub = f.shape[0] // t
    per_chunk = -(-((t * TOP_K) // tme + N_EXPERTS) // (MOE_CHUNKS * 4)) * 4
    ntile = MOE_CHUNKS * per_chunk
    nslot = ntile * tme
    tile_ends = jnp.cumsum(tiles_e)
    texp = jnp.sum((jnp.arange(ntile, dtype=I32)[:, None] >= tile_ends[None, :]).astype(I32), axis=1)
    texp = jnp.minimum(texp, N_EXPERTS - 1)

    inv = _inverse_map(slots.reshape(-1), nslot)
    spare = TOP_K * t + jnp.arange(nslot, dtype=I32) % tme
    tok = jnp.where(inv >= 0, inv % t, 0)
    dst = jnp.where(inv >= 0, inv, spare)

    shared = _shared_expert(f, ws1, ws3, ws2, t)
    table = f.reshape(t, sub, LANES)
    g = None
    for c in range(MOE_CHUNKS):
        lo, hi = c * per_chunk, (c + 1) * per_chunk
        xs = _sc_gather(table, tok[lo * tme:hi * tme]).reshape(per_chunk * tme * sub, LANES)
        texp_c = jnp.concatenate([texp[lo:hi], texp[hi - 1:hi]])
        dst_c = jnp.concatenate([spare[:tme], dst[lo * tme:hi * tme]])
        g = _experts(texp_c, dst_c, xs, w1, w3, w2, g, (TOP_K + 1) * t * sub, layer, tme)
    return _combine(wts.T, x1, shared, gf, lng, lnb, g.reshape(TOP_K + 1, t * sub, LANES), seq, layer, alpha)


def kernel(x, c, ctx, c_ctx, w_ada, b_ada, ln_mix_g, ln_mix_b, ln_ffn_g, ln_ffn_b, attn_w_qkv, attn_w_o, attn_sink, gmlp_w_in, gmlp_b_in, gmlp_v_g, gmlp_v_b, gmlp_w_s, gmlp_b_s, gmlp_w_o, moe_w_router, moe_bias, moe_w1, moe_w3, moe_w2, moe_ws1, moe_ws3, moe_ws2):
    batch, seq, d = x.shape
    nctx = ctx.shape[1]
    depth = w_ada.shape[0]
    assert depth == 2, "layer 0 is the attention mixer, layer 1 the gMLP mixer"
    alpha = float((2 * depth) ** 0.25)
    t = batch * seq

    pad = (-(batch + 1)) % 8
    cc = jnp.concatenate([c, c_ctx[None, :], jnp.zeros((pad, d), F32)], axis=0)
    mods = _modulations(cc, w_ada, b_ada)

    def mod(layer, j, rows=slice(0, batch)):
        return mods[layer, rows, j * d:(j + 1) * d]

    def per_batch(layer, j):
        return mod(layer, j).reshape(batch, 1, d)

    def router(layer):
        w = jnp.pad(moe_w_router[layer], ((0, 0), (0, LANES - N_EXPERTS)))
        hi = w.astype(BF16)
        lo = (w - hi.astype(F32)).astype(BF16)
        return jnp.concatenate([hi, lo], axis=1), moe_bias[layer].reshape(N_EXPERTS, 1)

    def shared_w(layer):
        return moe_ws1[layer].astype(BF16), moe_ws3[layer].astype(BF16), moe_ws2[layer].astype(BF16)

    x2 = x.reshape(t, d)
    ln_mix_g, ln_mix_b, ln_ffn_g, ln_ffn_b = (
        p.reshape(depth, 1, d) for p in (ln_mix_g, ln_mix_b, ln_ffn_g, ln_ffn_b))

    qd = Q_PER_KV * N_KV_HEADS * HEAD_DIM
    kvd = N_KV_HEADS * HEAD_DIM
    wqkv = attn_w_qkv[0]
    dup = lambda w: jnp.concatenate([w.reshape(d, N_KV_HEADS, 1, HEAD_DIM)] * (LANES // HEAD_DIM), axis=2
                                    ).reshape(d, N_KV_HEADS * LANES)
    wk2, wv2 = dup(wqkv[:, qd:qd + kvd]), dup(wqkv[:, qd + kvd:])
    w_all = jnp.concatenate([wqkv[:, :qd] * (HEAD_DIM ** -0.5), wk2, wv2], axis=1).astype(BF16)
    q, k2, v2 = _qkv_proj(x2, per_batch(0, 1), per_batch(0, 0), w_all, _rope_tables(seq), seq)
    ctx_row = slice(batch, batch + 1)
    kc2, vc2 = _ctx_kv(ctx.reshape(batch * nctx, d), mod(0, 1, ctx_row), mod(0, 0, ctx_row),
                       jnp.concatenate([wk2, wv2], axis=1).astype(BF16))
    o = _attention(q, k2, v2, kc2, vc2, attn_sink[0], batch, seq, nctx)
    x1, f, ids, wts, rank, cnt = _attn_out(
        o, attn_w_o[0].astype(BF16), x2, per_batch(0, 2), per_batch(0, 4), per_batch(0, 3),
        ln_mix_g, ln_mix_b, *router(0), seq, 0, alpha)
    x2 = _moe(f, x1, ids, wts, rank, cnt, per_batch(0, 5), ln_ffn_g, ln_ffn_b,
              moe_w1, moe_w3, moe_w2, *shared_w(0), seq, 0, alpha)

    width = gmlp_w_in.shape[2] // 2
    w_in = gmlp_w_in[0].astype(BF16)
    b_in = gmlp_b_in[0].reshape(1, 2 * width)
    vg, vb = gmlp_v_g[0].reshape(1, width), gmlp_v_b[0].reshape(1, width)
    sc, sh = per_batch(1, 1), per_batch(1, 0)
    u = _gmlp_in(x2, sc, sh, w_in[:, :width], b_in[:, :width], vg, vb, seq, False)
    v = _gmlp_in(x2, sc, sh, w_in[:, width:], b_in[:, width:], vg, vb, seq, True)
    x1, f, ids, wts, rank, cnt = _gmlp_out(
        u, v, gmlp_w_s[0].astype(BF16), gmlp_b_s[0][:, :, None], gmlp_w_o[0].astype(BF16), x2,
        per_batch(1, 2), per_batch(1, 4), per_batch(1, 3), ln_mix_g, ln_mix_b, *router(1), seq, 1, alpha)
    x2 = _moe(f, x1, ids, wts, rank, cnt, per_batch(1, 5), ln_ffn_g, ln_ffn_b,
              moe_w1, moe_w3, moe_w2, *shared_w(1), seq, 1, alpha)
    return x2.reshape(batch, seq, d)
```

```python
import functools

import jax
import jax.numpy as jnp
from jax import lax
from jax.experimental import pallas as pl
from jax.experimental.pallas import tpu as pltpu
from jax.experimental.pallas import tpu_sc as plsc

F32 = jnp.float32
BF16 = jnp.bfloat16
I32 = jnp.int32

GRID_W = 64
N_KV_HEADS = 4
Q_PER_KV = 8
HEAD_DIM = 64
ROPE_HALF = 16
WINDOW = 128
Q_BLOCK = 128
ROPE_BASE = 10000.0
NEG_INF = -1e30
CHUNK = 128
N_GMLP_GROUPS = 8
N_EXPERTS = 64
TOP_K = 8
N_EXPERT_GROUPS = 8
EXPERTS_PER_GROUP = 8
TOPK_GROUPS = 4
ROUTED_SCALE = 2.5
LN_EPS = 1e-5

LANES = 128
VMEM_LIMIT_BYTES = 56 * 1024 * 1024


def _params(*sem):
    return pltpu.CompilerParams(dimension_semantics=sem, vmem_limit_bytes=VMEM_LIMIT_BYTES)


def _silu(a):
    return a * jax.nn.sigmoid(a)


def _layernorm(v, g, b):
    mu = jnp.mean(v, axis=-1, keepdims=True)
    d = v - mu
    var = jnp.mean(d * d, axis=-1, keepdims=True)
    return d * lax.rsqrt(var + LN_EPS) * g + b


def _bdot(a, b):
    return jnp.dot(a, b, preferred_element_type=F32)


U32 = jnp.uint32


def _pack_pairs(a):
    n = a.shape[1] // 2
    lo = lax.bitcast_convert_type(a[:, :n].astype(BF16).astype(F32), U32)
    hi = lax.bitcast_convert_type(a[:, n:].astype(BF16).astype(F32), U32)
    return hi | (lo >> 16)


def _unpack_pairs(p):
    lo = lax.bitcast_convert_type(p << 16, F32)
    hi = lax.bitcast_convert_type(p & jnp.uint32(0xFFFF0000), F32)
    return lo, hi


def _store_row_tiles(ref, packed):
    r, n = packed.shape
    sub = n // LANES
    for j in range(sub):
        ref[pl.ds(j, r, stride=sub), :] = packed[:, LANES * j:LANES * (j + 1)]


def _load_row_tiles(ref, r):
    sub = ref.shape[0] // r
    pieces = [_unpack_pairs(ref[pl.ds(j, r, stride=sub), :]) for j in range(sub)]
    return [p[0] for p in pieces], [p[1] for p in pieces]


def _load_rows_bf16(ref, r):
    lo, hi = _load_row_tiles(ref, r)
    return jnp.concatenate(lo + hi, axis=1).astype(BF16)


def _mod_kernel(c_ref, w_ref, b_ref, o_ref):
    a = _silu(c_ref[...]).astype(BF16)
    o_ref[0] = _bdot(a, w_ref[0].astype(BF16)) + b_ref[0]


def _modulations(cc, w_ada, b_ada):
    depth, d, n6 = w_ada.shape
    r = cc.shape[0]
    tn = max(w for w in range(LANES, min(1024, n6) + 1, LANES) if n6 % w == 0)
    return pl.pallas_call(
        _mod_kernel,
        out_shape=jax.ShapeDtypeStruct((depth, r, n6), F32),
        grid=(depth, n6 // tn),
        in_specs=[pl.BlockSpec((r, d), lambda l, j: (0, 0)),
                  pl.BlockSpec((1, d, tn), lambda l, j: (l, 0, j)),
                  pl.BlockSpec((1, 1, tn), lambda l, j: (l, 0, j))],
        out_specs=pl.BlockSpec((1, r, tn), lambda l, j: (l, 0, j)),
        compiler_params=_params("parallel", "parallel"),
        name="adaln_mod",
    )(cc, w_ada, b_ada.reshape(depth, 1, n6))


def _rope_tables(seq):
    rows = seq // GRID_W
    row_ids = jnp.repeat(jnp.arange(rows, dtype=F32), GRID_W)
    col_ids = jnp.tile(jnp.arange(GRID_W, dtype=F32), rows)
    inv_freq = ROPE_BASE ** (-jnp.arange(0, 2 * ROPE_HALF, 2, dtype=F32) / (2 * ROPE_HALF))
    ar, ac = row_ids[:, None] * inv_freq, col_ids[:, None] * inv_freq
    z = jnp.zeros_like(ar)
    cos = jnp.concatenate([jnp.cos(ar), jnp.cos(ar), jnp.cos(ac), jnp.cos(ac)], axis=-1)
    sin_lo = jnp.concatenate([-jnp.sin(ar), z, -jnp.sin(ac), z], axis=-1)
    sin_hi = jnp.concatenate([z, jnp.sin(ar), z, jnp.sin(ac)], axis=-1)
    rep = LANES // HEAD_DIM
    return tuple(jnp.tile(t, (1, rep)) for t in (cos, sin_lo, sin_hi))


def _qkv_kernel(x_ref, sc_ref, sh_ref, w_ref, cos_ref, slo_ref, shi_ref, q_ref, k_ref, v_ref, *, cw):
    h = (x_ref[...] * (1.0 + sc_ref[0]) + sh_ref[0]).astype(BF16)
    cos, slo, shi = cos_ref[...], slo_ref[...], shi_ref[...]

    def rope(a):
        return (a * cos + pltpu.roll(a, LANES - ROPE_HALF, 1) * slo
                + pltpu.roll(a, ROPE_HALF, 1) * shi)

    qd, kd = q_ref.shape[1], k_ref.shape[1]
    for c0 in range(0, qd, cw):
        acc = _bdot(h, w_ref[:, c0:c0 + cw])
        for b0 in range(0, cw, LANES):
            q_ref[:, c0 + b0:c0 + b0 + LANES] = rope(acc[:, b0:b0 + LANES]).astype(BF16)
    acc = _bdot(h, w_ref[:, qd:qd + kd])
    for b0 in range(0, kd, LANES):
        k_ref[:, b0:b0 + LANES] = rope(acc[:, b0:b0 + LANES]).astype(BF16)
    v_ref[...] = _bdot(h, w_ref[:, qd + kd:qd + 2 * kd]).astype(BF16)


def _qkv_proj(x2, sc, sh, w, tables, seq):
    t, d = x2.shape
    kd = N_KV_HEADS * LANES
    qd = w.shape[1] - 2 * kd
    tm = min(512, seq)
    per_b = seq // tm
    row = lambda i: (i, 0)
    mod = lambda i: (i // per_b, 0, 0)
    tab = lambda i: (i % per_b, 0)
    return pl.pallas_call(
        functools.partial(_qkv_kernel, cw=min(512, qd)),
        out_shape=(jax.ShapeDtypeStruct((t, qd), BF16), jax.ShapeDtypeStruct((t, kd), BF16),
                   jax.ShapeDtypeStruct((t, kd), BF16)),
        grid=(t // tm,),
        in_specs=[pl.BlockSpec((tm, d), row), pl.BlockSpec((1, 1, d), mod), pl.BlockSpec((1, 1, d), mod),
                  pl.BlockSpec(w.shape, lambda i: (0, 0)),
                  pl.BlockSpec((tm, LANES), tab), pl.BlockSpec((tm, LANES), tab), pl.BlockSpec((tm, LANES), tab)],
        out_specs=(pl.BlockSpec((tm, qd), row), pl.BlockSpec((tm, kd), row), pl.BlockSpec((tm, kd), row)),
        compiler_params=_params("parallel"),
        name="qkv_rope",
    )(x2, sc, sh, w, *tables)


def _ctxkv_kernel(x_ref, sc_ref, sh_ref, w_ref, k_ref, v_ref):
    h = (x_ref[...] * (1.0 + sc_ref[...]) + sh_ref[...]).astype(BF16)
    kd = k_ref.shape[1]
    k_ref[...] = _bdot(h, w_ref[:, :kd]).astype(BF16)
    v_ref[...] = _bdot(h, w_ref[:, kd:]).astype(BF16)


def _ctx_kv(c2, sc, sh, w):
    t, d = c2.shape
    kd = w.shape[1] // 2
    tm = min(512, t)
    row = lambda i: (i, 0)
    fix = lambda i: (0, 0)
    return pl.pallas_call(
        _ctxkv_kernel,
        out_shape=(jax.ShapeDtypeStruct((t, kd), BF16), jax.ShapeDtypeStruct((t, kd), BF16)),
        grid=(t // tm,),
        in_specs=[pl.BlockSpec((tm, d), row), pl.BlockSpec((1, d), fix), pl.BlockSpec((1, d), fix),
                  pl.BlockSpec(w.shape, fix)],
        out_specs=(pl.BlockSpec((tm, kd), row), pl.BlockSpec((tm, kd), row)),
        compiler_params=_params("parallel"),
        name="ctx_kv",
    )(c2, sc, sh, w)


def _attn_kernel(sink_ref, q_ref, k_ref, v_ref, kc_ref, vc_ref, o_ref):
    kv, i = pl.program_id(1), pl.program_id(2)
    seq, nctx = k_ref.shape[1], kc_ref.shape[1]
    nwin = Q_BLOCK + 2 * WINDOW
    start = pl.multiple_of(jnp.clip(i * Q_BLOCK - WINDOW, 0, seq - nwin), Q_BLOCK)
    nkey = nctx + nwin
    kall = jnp.concatenate([kc_ref[0], k_ref[0, pl.ds(start, nwin), :]], axis=0)
    vall = jnp.concatenate([vc_ref[0], v_ref[0, pl.ds(start, nwin), :]], axis=0)
    low = lax.broadcasted_iota(I32, (nkey, LANES), 1) < HEAD_DIM
    zero = jnp.zeros((nkey, LANES), BF16)
    npair = Q_PER_KV // 2
    q = q_ref[0]
    qs = jnp.concatenate([q[:, LANES * j:LANES * (j + 1)] for j in range(npair)], axis=0)
    nrow = npair * Q_BLOCK
    rows = lax.broadcasted_iota(I32, (Q_BLOCK, nkey), 0)
    cols = lax.broadcasted_iota(I32, (Q_BLOCK, nkey), 1)
    dist = (i * Q_BLOCK - start + nctx) + rows - cols
    valid = (cols < nctx) | (jnp.abs(dist) <= WINDOW)
    contract_last = (((1,), (1,)), ((), ()))
    out = jnp.zeros((nrow, LANES), F32)
    for par in range(2):
        keep = low if par == 0 else jnp.logical_not(low)
        kh = jnp.where(keep, kall, zero)
        vh = jnp.where(keep, vall, zero)
        s = lax.dot_general(qs, kh, contract_last, preferred_element_type=F32)
        s = jnp.concatenate(
            [jnp.where(valid, s[Q_BLOCK * j:Q_BLOCK * (j + 1), :], NEG_INF) for j in range(npair)], axis=0)
        sink = jnp.concatenate(
            [jnp.full((Q_BLOCK, 1), sink_ref[kv * Q_PER_KV + 2 * j + par], F32) for j in range(npair)], axis=0)
        m = jnp.maximum(jnp.max(s, axis=1, keepdims=True), sink)
        p = jnp.exp(s - m)
        den = jnp.sum(p, axis=1, keepdims=True) + jnp.exp(sink - m)
        out = out + _bdot(p.astype(BF16), vh) / den
    for j in range(npair):
        o_ref[0, :, LANES * j:LANES * (j + 1)] = out[Q_BLOCK * j:Q_BLOCK * (j + 1), :].astype(BF16)


def _attention(q, k2, v2, kc2, vc2, sink, batch, seq, nctx):
    qd = q.shape[1]
    gw = Q_PER_KV * HEAD_DIM
    q3 = q.reshape(batch, seq, qd)
    k3, v3 = k2.reshape(batch, seq, -1), v2.reshape(batch, seq, -1)
    kc3, vc3 = kc2.reshape(batch, nctx, -1), vc2.reshape(batch, nctx, -1)
    full = lambda b, h, i: (b, 0, h)
    blk = lambda b, h, i: (b, i, h)
    out = pl.pallas_call(
        _attn_kernel,
        out_shape=jax.ShapeDtypeStruct((batch, seq, qd), BF16),
        grid=(batch, N_KV_HEADS, seq // Q_BLOCK),
        in_specs=[pl.BlockSpec(memory_space=pltpu.SMEM),
                  pl.BlockSpec((1, Q_BLOCK, gw), blk),
                  pl.BlockSpec((1, seq, LANES), full), pl.BlockSpec((1, seq, LANES), full),
                  pl.BlockSpec((1, nctx, LANES), full), pl.BlockSpec((1, nctx, LANES), full)],
        out_specs=pl.BlockSpec((1, Q_BLOCK, gw), blk),
        compiler_params=_params("parallel", "parallel", "parallel"),
        name="window_gqa",
    )(sink, q3, k3, v3, kc3, vc3)
    return out.reshape(batch * seq, qd)


def _split_bf16(a):
    hi = a.astype(BF16)
    return hi, (a - hi.astype(F32)).astype(BF16)


def _post_mixer(x, y, gm, lng, lnb, scf, shf, wr_ref, rb_ref, run_ref,
                x1_ref, f_ref, ids_ref, wts_ref, rank_ref, cnt_ref, alpha):
    tm = x.shape[0]
    x1 = _layernorm(alpha * x + gm * y, lng, lnb)
    x1_ref[...] = x1
    f = x1 * (1.0 + scf) + shf
    _store_row_tiles(f_ref, _pack_pairs(f))

    fh, fl = _split_bf16(f)
    prod = _bdot(fh, wr_ref[...]) + _bdot(fl, wr_ref[...])
    logits = (prod[:, :LANES] + prod[:, LANES:]).T[:N_EXPERTS, :]
    scores = jax.nn.sigmoid(logits)
    biased = scores + rb_ref[...]

    eg = EXPERTS_PER_GROUP
    sub = lax.broadcasted_iota(I32, (eg, tm), 0)
    gscore = []
    for g in range(N_EXPERT_GROUPS):
        tg = biased[eg * g:eg * (g + 1), :]
        m1 = jnp.max(tg, axis=0, keepdims=True)
        i1 = jnp.min(jnp.where(tg == m1, sub, eg), axis=0, keepdims=True)
        m2 = jnp.max(jnp.where(sub == i1, -jnp.inf, tg), axis=0, keepdims=True)
        gscore.append(m1 + m2)
    gsel = [jnp.zeros((1, tm), jnp.bool_) for _ in range(N_EXPERT_GROUPS)]
    for _ in range(TOPK_GROUPS):
        best = functools.reduce(jnp.maximum, gscore)
        taken = jnp.zeros((1, tm), jnp.bool_)
        for g in range(N_EXPERT_GROUPS):
            hit = jnp.logical_and(gscore[g] == best, jnp.logical_not(taken))
            taken = jnp.logical_or(taken, hit)
            gsel[g] = jnp.logical_or(gsel[g], hit)
            gscore[g] = jnp.where(hit, -jnp.inf, gscore[g])
    cur = jnp.concatenate(
        [jnp.where(gsel[g], biased[eg * g:eg * (g + 1), :], -jnp.inf) for g in range(N_EXPERT_GROUPS)], axis=0)

    eidx = lax.broadcasted_iota(I32, (N_EXPERTS, tm), 0)
    picks, wts, hots = [], [], []
    for _ in range(TOP_K):
        m = jnp.max(cur, axis=0, keepdims=True)
        idx = jnp.min(jnp.where(cur == m, eidx, N_EXPERTS), axis=0, keepdims=True)
        hot = eidx == idx
        picks.append(idx)
        wts.append(jnp.sum(jnp.where(hot, scores, 0.0), axis=0, keepdims=True))
        hots.append(hot)
        cur = jnp.where(hot, -jnp.inf, cur)
    wsum = functools.reduce(jnp.add, wts)

    assigned = functools.reduce(jnp.add, [h.astype(F32) for h in hots])
    before = (lax.broadcasted_iota(I32, (tm, tm), 0) < lax.broadcasted_iota(I32, (tm, tm), 1)).astype(BF16)
    pos = _bdot(assigned.astype(BF16), before) + run_ref[:, 0:1]
    for k in range(TOP_K):
        ids_ref[k:k + 1, :] = picks[k]
        wts_ref[k:k + 1, :] = wts[k] / wsum * ROUTED_SCALE
        rank_ref[k:k + 1, :] = jnp.sum(jnp.where(hots[k], pos, 0.0), axis=0, keepdims=True).astype(I32)
    run_ref[...] = run_ref[...] + jnp.sum(assigned, axis=1, keepdims=True)
    cnt_ref[...] = run_ref[...]


def _route_out_shapes(t, d):
    return (jax.ShapeDtypeStruct((t, d), F32), jax.ShapeDtypeStruct((t * d // (2 * LANES), LANES), U32),
            jax.ShapeDtypeStruct((TOP_K, t), I32), jax.ShapeDtypeStruct((TOP_K, t), F32),
            jax.ShapeDtypeStruct((TOP_K, t), I32), jax.ShapeDtypeStruct((N_EXPERTS, LANES), F32))


def _route_out_specs(tm, d):
    row = lambda i: (i, 0)
    col = lambda i: (0, i)
    return (pl.BlockSpec((tm, d), row), pl.BlockSpec((tm * d // (2 * LANES), LANES), row),
            pl.BlockSpec((TOP_K, tm), col), pl.BlockSpec((TOP_K, tm), col), pl.BlockSpec((TOP_K, tm), col),
            pl.BlockSpec((N_EXPERTS, LANES), lambda i: (0, 0)))


def _route_in_specs(tm, d, per_b, layer):
    mod = lambda i: (i // per_b, 0, 0)
    lrow = lambda i: (layer, 0, 0)
    fix = lambda i: (0, 0)
    return [pl.BlockSpec((tm, d), lambda i: (i, 0)),
            pl.BlockSpec((1, 1, d), mod), pl.BlockSpec((1, 1, d), mod), pl.BlockSpec((1, 1, d), mod),
            pl.BlockSpec((1, 1, d), lrow), pl.BlockSpec((1, 1, d), lrow),
            pl.BlockSpec((d, 2 * LANES), fix),
            pl.BlockSpec((N_EXPERTS, 1), fix)]


def _oproj_kernel(o_ref, wo_ref, x_ref, gm_ref, scf_ref, shf_ref, lng_ref, lnb_ref, wr_ref, rb_ref,
                  x1_ref, f_ref, ids_ref, wts_ref, rank_ref, cnt_ref, run_ref, *, alpha):
    @pl.when(pl.program_id(0) == 0)
    def _():
        run_ref[...] = jnp.zeros_like(run_ref)

    y = _bdot(o_ref[...], wo_ref[...])
    _post_mixer(x_ref[...], y, gm_ref[0], lng_ref[0], lnb_ref[0], scf_ref[0], shf_ref[0],
                wr_ref, rb_ref, run_ref, x1_ref, f_ref, ids_ref, wts_ref, rank_ref, cnt_ref, alpha)


def _attn_out(o, wo, x2, gm, scf, shf, lng, lnb, wr, rb, seq, layer, alpha):
    t, d = x2.shape
    tm = min(512, seq)
    return pl.pallas_call(
        functools.partial(_oproj_kernel, alpha=alpha),
        out_shape=_route_out_shapes(t, d),
        grid=(t // tm,),
        in_specs=[pl.BlockSpec((tm, o.shape[1]), lambda i: (i, 0)), pl.BlockSpec(wo.shape, lambda i: (0, 0))]
        + _route_in_specs(tm, d, seq // tm, layer),
        out_specs=_route_out_specs(tm, d),
        scratch_shapes=[pltpu.VMEM((N_EXPERTS, LANES), F32)],
        compiler_params=_params("arbitrary"),
        name="attn_out_route",
    )(o, wo, x2, gm, scf, shf, lng, lnb, wr, rb)


def _gmlp_in_kernel(x_ref, sc_ref, sh_ref, w_ref, b_ref, g_ref, beta_ref, o_ref, *, cw, normalize):
    h = (x_ref[...] * (1.0 + sc_ref[0]) + sh_ref[0]).astype(BF16)
    n = o_ref.shape[1]
    inv_sqrt2 = 0.7071067811865476
    parts = []
    for c0 in range(0, n, cw):
        z = _bdot(h, w_ref[:, c0:c0 + cw]) + b_ref[:, c0:c0 + cw]
        z = 0.5 * z * (1.0 + lax.erf(z * inv_sqrt2))
        if normalize:
            parts.append(z)
        else:
            o_ref[:, c0:c0 + cw] = z.astype(BF16)
    if normalize:
        v = jnp.concatenate(parts, axis=1)
        o_ref[...] = _layernorm(v, g_ref[...], beta_ref[...]).astype(BF16)


def _gmlp_in(x2, sc, sh, w, b, g, beta, seq, normalize):
    t, d = x2.shape
    n = w.shape[1]
    tm = min(256, seq)
    per_b = seq // tm
    row = lambda i: (i, 0)
    mod = lambda i: (i // per_b, 0, 0)
    fix = lambda i: (0, 0)
    return pl.pallas_call(
        functools.partial(_gmlp_in_kernel, cw=min(512, n), normalize=normalize),
        out_shape=jax.ShapeDtypeStruct((t, n), BF16),
        grid=(t // tm,),
        in_specs=[pl.BlockSpec((tm, d), row), pl.BlockSpec((1, 1, d), mod), pl.BlockSpec((1, 1, d), mod),
                  pl.BlockSpec(w.shape, fix), pl.BlockSpec((1, n), fix), pl.BlockSpec((1, n), fix),
                  pl.BlockSpec((1, n), fix)],
        out_specs=pl.BlockSpec((tm, n), row),
        compiler_params=_params("parallel"),
        name="gmlp_in_v" if normalize else "gmlp_in_u",
    )(x2, sc, sh, w, b, g, beta)


def _gmlp_out_kernel(u_ref, v_ref, ws_ref, bs_ref, wo_ref, x_ref, gm_ref, scf_ref, shf_ref, lng_ref, lnb_ref,
                     wr_ref, rb_ref, x1_ref, f_ref, ids_ref, wts_ref, rank_ref, cnt_ref,
                     run_ref, gated_ref, *, alpha):
    @pl.when(pl.program_id(0) == 0)
    def _():
        run_ref[...] = jnp.zeros_like(run_ref)

    tm, width = u_ref.shape
    gd = width // N_GMLP_GROUPS
    for r0 in range(0, tm, CHUNK):
        for g in range(N_GMLP_GROUPS):
            c0 = g * gd
            mixed = _bdot(ws_ref[g], v_ref[r0:r0 + CHUNK, c0:c0 + gd]) + bs_ref[g]
            gated_ref[r0:r0 + CHUNK, c0:c0 + gd] = (
                u_ref[r0:r0 + CHUNK, c0:c0 + gd].astype(F32) * mixed).astype(BF16)
    y = _bdot(gated_ref[...], wo_ref[...])
    _post_mixer(x_ref[...], y, gm_ref[0], lng_ref[0], lnb_ref[0], scf_ref[0], shf_ref[0],
                wr_ref, rb_ref, run_ref, x1_ref, f_ref, ids_ref, wts_ref, rank_ref, cnt_ref, alpha)


def _gmlp_out(u, v, ws, bs, wo, x2, gm, scf, shf, lng, lnb, wr, rb, seq, layer, alpha):
    t, d = x2.shape
    width = u.shape[1]
    tm = min(256, seq)
    row = lambda i: (i, 0)
    return pl.pallas_call(
        functools.partial(_gmlp_out_kernel, alpha=alpha),
        out_shape=_route_out_shapes(t, d),
        grid=(t // tm,),
        in_specs=[pl.BlockSpec((tm, width), row), pl.BlockSpec((tm, width), row),
                  pl.BlockSpec(ws.shape, lambda i: (0, 0, 0)), pl.BlockSpec(bs.shape, lambda i: (0, 0, 0)),
                  pl.BlockSpec(wo.shape, lambda i: (0, 0))]
        + _route_in_specs(tm, d, seq // tm, layer),
        out_specs=_route_out_specs(tm, d),
        scratch_shapes=[pltpu.VMEM((N_EXPERTS, LANES), F32), pltpu.VMEM((tm, width), BF16)],
        compiler_params=_params("arbitrary"),
        name="gmlp_out_route",
    )(u, v, ws, bs, wo, x2, gm, scf, shf, lng, lnb, wr, rb)


SC_CORES, SC_SUBCORES, SC_LANES = 2, 16, 16
MXU_N = 256


def _shared_kernel(f_ref, ws1_ref, ws3_ref, ws2_ref, sh_ref):
    h = _load_rows_bf16(f_ref, sh_ref.shape[0])
    a = _silu(_bdot(h, ws1_ref[...])) * _bdot(h, ws3_ref[...])
    sh_ref[...] = _bdot(a.astype(BF16), ws2_ref[...]).astype(BF16)


def _shared_expert(f, ws1, ws3, ws2, t):
    sub = f.shape[0] // t
    d = ws1.shape[0]
    tm = min(512, t)
    fix = lambda i: (0, 0)
    return pl.pallas_call(
        _shared_kernel,
        out_shape=jax.ShapeDtypeStruct((t, d), BF16),
        grid=(t // tm,),
        in_specs=[pl.BlockSpec((tm * sub, LANES), lambda i: (i, 0)),
                  pl.BlockSpec(ws1.shape, fix), pl.BlockSpec(ws3.shape, fix), pl.BlockSpec(ws2.shape, fix)],
        out_specs=pl.BlockSpec((tm, d), lambda i: (i, 0)),
        compiler_params=_params("parallel"),
        name="moe_shared",
    )(f, ws1, ws3, ws2)


def _inverse_map(slots_flat, nslot):
    n = slots_flat.shape[0]
    workers = SC_CORES * SC_SUBCORES
    per_w = nslot // workers
    chunk = min(8192, n)
    assert nslot % (workers * SC_LANES) == 0 and n % chunk == 0 and chunk % SC_LANES == 0
    mesh = plsc.VectorSubcoreMesh(core_axis_name="c", subcore_axis_name="s")

    @functools.partial(
        pl.kernel, out_type=jax.ShapeDtypeStruct((nslot,), I32), mesh=mesh,
        scratch_types=[pltpu.VMEM((per_w,), I32), pltpu.VMEM((chunk,), I32)],
        compiler_params=pltpu.CompilerParams(needs_layout_passes=False))
    def inverse(slots_hbm, inv_hbm, local, buf):
        lo = (lax.axis_index("s") * SC_CORES + lax.axis_index("c")) * per_w
        unused = jnp.full((SC_LANES,), -1, I32)

        @pl.loop(0, per_w, step=SC_LANES)
        def _(j):
            local[pl.ds(j, SC_LANES)] = unused

        lane = lax.iota(I32, SC_LANES)

        @pl.loop(0, n, step=chunk)
        def _(c0):
            pltpu.sync_copy(slots_hbm.at[pl.ds(c0, chunk)], buf)

            @pl.loop(0, chunk, step=SC_LANES)
            def _(j):
                idx = buf[pl.ds(j, SC_LANES)] - lo
                mine = jnp.logical_and(idx >= 0, idx < per_w)
                plsc.store_scatter(local, [idx], c0 + j + lane, mask=mine)

        pltpu.sync_copy(local, inv_hbm.at[pl.ds(lo, per_w)])

    return inverse(slots_flat)


def _experts_kernel(texp_ref, tok_ref, w1_ref, w3_ref, w2_ref, f_hbm, y_ref, w1b, w3b, w2b, xa, xb, gsem, *, tme):
    i = pl.program_id(0)
    last = pl.num_programs(0) - 1
    sub = xa.shape[0] // tme
    d, fdim = w1b.shape
    half = d // 2

    @pl.when(i == 0)
    def _():
        xb[...] = jnp.zeros_like(xb)

    @pl.when(jnp.logical_or(i == 0, texp_ref[i] != texp_ref[jnp.maximum(i - 1, 0)]))
    def _():
        w1b[...] = w1_ref[...].astype(BF16)
        w3b[...] = w3_ref[...].astype(BF16)
        w2b[...] = w2_ref[...].astype(BF16)

    def whole_tile(buf, sem):
        return pltpu.make_async_copy(f_hbm.at[pl.ds(0, tme * sub), :], buf, sem)

    def step(par):
        xg, xc = (xa, xb) if par == 0 else (xb, xa)

        @pl.when(i > 0)
        def _():
            whole_tile(xc, gsem.at[1 - par]).wait()

        pw1, pw2 = min(MXU_N, fdim), min(MXU_N, half)
        per_piece = -(-tme // (2 * (fdim // pw1)))
        issued = [0]

        def issue_rows():
            for r in range(issued[0], min(issued[0] + per_piece, tme)):
                tok = pl.multiple_of(tok_ref[r] * sub, sub)
                pltpu.make_async_copy(f_hbm.at[pl.ds(tok, sub), :], xg.at[pl.ds(r * sub, sub), :],
                                      gsem.at[par]).start(priority=1)
            issued[0] = min(issued[0] + per_piece, tme)

        h = _load_rows_bf16(xc, tme)
        gate, up = [], []
        for c0 in range(0, fdim, pw1):
            gate.append(_bdot(h, w1b[:, c0:c0 + pw1]))
            issue_rows()
        for c0 in range(0, fdim, pw1):
            up.append(_bdot(h, w3b[:, c0:c0 + pw1]))
            issue_rows()
        assert issued[0] == tme
        a = (_silu(jnp.concatenate(gate, axis=1)) * jnp.concatenate(up, axis=1)).astype(BF16)
        for c0 in range(0, half, pw2):
            lo = _bdot(a, w2b[:, c0:c0 + pw2])
            hi = _bdot(a, w2b[:, half + c0:half + c0 + pw2])
            packed = _pack_pairs(jnp.concatenate([lo, hi], axis=1))
            for b0 in range(0, pw2, LANES):
                y_ref[pl.ds((c0 + b0) // LANES, tme, stride=sub), :] = packed[:, b0:b0 + LANES]

        @pl.when(i == last)
        def _():
            whole_tile(xg, gsem.at[par]).wait()

    for par in range(2):
        pl.when(jnp.bitwise_and(i, 1) == par)(functools.partial(step, par))


def _experts(texp, tok, w1, w3, w2, f, layer, tme):
    d, fdim = w1.shape[-2:]
    sub = d // (2 * LANES)
    ntile = texp.shape[0] - 1
    wsel = lambda i, te: (layer, te[i], 0, 0)
    return pl.pallas_call(
        functools.partial(_experts_kernel, tme=tme),
        out_shape=jax.ShapeDtypeStruct((ntile * tme * sub, LANES), U32),
        grid_spec=pltpu.PrefetchScalarGridSpec(
            num_scalar_prefetch=1,
            grid=(ntile + 1,),
            in_specs=[pl.BlockSpec((tme,), lambda i, te: (i,), memory_space=pltpu.SMEM),
                      pl.BlockSpec((None, None, d, fdim), wsel), pl.BlockSpec((None, None, d, fdim), wsel),
                      pl.BlockSpec((None, None, fdim, d), wsel),
                      pl.BlockSpec(memory_space=pl.ANY)],
            out_specs=pl.BlockSpec((tme * sub, LANES), lambda i, te: (jnp.maximum(i - 1, 0), 0)),
            scratch_shapes=[pltpu.VMEM((d, fdim), BF16), pltpu.VMEM((d, fdim), BF16), pltpu.VMEM((fdim, d), BF16),
                            pltpu.VMEM((tme * sub, LANES), U32), pltpu.VMEM((tme * sub, LANES), U32),
                            pltpu.SemaphoreType.DMA((2,))]),
        compiler_params=_params("arbitrary"),
        name="moe_experts",
    )(texp, tok, w1, w3, w2, f)


def _combine_kernel(slot_ref, wt_ref, x1_ref, sh_ref, gf_ref, lng_ref, lnb_ref, ys_hbm, o_ref, ga, gb, gsem, *, alpha):
    i = pl.program_id(0)
    last = pl.num_programs(0) - 1
    tm = x1_ref.shape[0]
    sub = ga.shape[1] // tm

    @pl.when(i == 0)
    def _():
        gb[...] = jnp.zeros_like(gb)

    def whole_plane(buf, k, sem):
        return pltpu.make_async_copy(ys_hbm.at[pl.ds(0, tm * sub), :], buf.at[k], sem)

    def step(par):
        gg, gc = (ga, gb) if par == 0 else (gb, ga)

        @pl.when(i > 0)
        def _():
            for k in range(TOP_K):
                whole_plane(gc, k, gsem.at[1 - par]).wait()

        def issue(t0, t1):
            for tt in range(t0, t1):
                for k in range(TOP_K):
                    row = pl.multiple_of(slot_ref[tt * TOP_K + k] * sub, sub)
                    pltpu.make_async_copy(ys_hbm.at[pl.ds(row, sub), :], gg.at[k, pl.ds(tt * sub, sub), :],
                                          gsem.at[par]).start(priority=1)

        lo = hi = None
        for k in range(TOP_K):
            issue(k * tm // TOP_K, (k + 1) * tm // TOP_K)
            w = wt_ref[:, k:k + 1]
            rl, rh = _load_row_tiles(gc.at[k], tm)
            lo = [w * b for b in rl] if lo is None else [a + w * b for a, b in zip(lo, rl)]
            hi = [w * b for b in rh] if hi is None else [a + w * b for a, b in zip(hi, rh)]
        f = jnp.concatenate(lo + hi, axis=1) + sh_ref[...].astype(F32)
        o_ref[...] = _layernorm(alpha * x1_ref[...] + gf_ref[0] * f, lng_ref[0], lnb_ref[0])

        @pl.when(i == last)
        def _():
            for k in range(TOP_K):
                whole_plane(gg, k, gsem.at[par]).wait()

    for par in range(2):
        pl.when(jnp.bitwise_and(i, 1) == par)(functools.partial(step, par))


def _combine(slot_tk, wt, x1, sh, gf, lng, lnb, ys, seq, tm, layer, alpha):
    t, d = x1.shape
    sub = d // (2 * LANES)
    per_b = seq // tm
    nstep = t // tm + 1
    prev = lambda i: jnp.maximum(i - 1, 0)
    row = lambda i: (prev(i), 0)
    lrow = lambda i: (layer, 0, 0)
    return pl.pallas_call(
        functools.partial(_combine_kernel, alpha=alpha),
        out_shape=jax.ShapeDtypeStruct((t, d), F32),
        grid=(nstep,),
        in_specs=[pl.BlockSpec((tm * TOP_K,), lambda i: (i,), memory_space=pltpu.SMEM),
                  pl.BlockSpec((tm, TOP_K), row), pl.BlockSpec((tm, d), row), pl.BlockSpec((tm, d), row),
                  pl.BlockSpec((1, 1, d), lambda i: (prev(i) // per_b, 0, 0)),
                  pl.BlockSpec((1, 1, d), lrow), pl.BlockSpec((1, 1, d), lrow),
                  pl.BlockSpec(memory_space=pl.ANY)],
        out_specs=pl.BlockSpec((tm, d), row),
        scratch_shapes=[pltpu.VMEM((TOP_K, tm * sub, LANES), U32), pltpu.VMEM((TOP_K, tm * sub, LANES), U32),
                        pltpu.SemaphoreType.DMA((2,))],
        compiler_params=_params("arbitrary"),
        name="moe_combine_ln",
    )(slot_tk, wt, x1, sh, gf, lng, lnb, ys)


def _moe(f, x1, ids, wts, rank, cnt, gf, lng, lnb, w1, w3, w2, ws1, ws3, ws2, seq, layer, alpha):
    t, d = x1.shape
    tme = min(512, t)
    counts = cnt[:, 0].astype(I32)
    tiles_e = (counts + tme - 1) // tme
    padded = tiles_e * tme
    base = jnp.cumsum(padded) - padded
    expert = jnp.arange(N_EXPERTS, dtype=I32)
    slots = jnp.sum(jnp.where(ids[:, :, None] == expert, base, 0), axis=-1) + rank
    ntile = (t * TOP_K) // tme + N_EXPERTS
    nslot = ntile * tme
    tile_ends = jnp.cumsum(tiles_e)
    texp = jnp.sum((jnp.arange(ntile, dtype=I32)[:, None] >= tile_ends[None, :]).astype(I32), axis=1)
    texp = jnp.minimum(texp, N_EXPERTS - 1)
    texp_step = jnp.concatenate([texp[:1], texp])

    inv = _inverse_map(slots.reshape(-1), nslot)
    tok = jnp.where(inv >= 0, inv % t, 0)
    tok = jnp.concatenate([tok, jnp.zeros((tme,), I32)])

    shared = _shared_expert(f, ws1, ws3, ws2, t)
    ys = _experts(texp_step, tok, w1, w3, w2, f, layer, tme)
    tmc = min(128, seq)
    slot_tk = jnp.concatenate([slots.T.reshape(-1), jnp.zeros((tmc * TOP_K,), I32)])
    return _combine(slot_tk, wts.T, x1, shared, gf, lng, lnb, ys, seq, tmc, layer, alpha)


def kernel(x, c, ctx, c_ctx, w_ada, b_ada, ln_mix_g, ln_mix_b, ln_ffn_g, ln_ffn_b, attn_w_qkv, attn_w_o, attn_sink, gmlp_w_in, gmlp_b_in, gmlp_v_g, gmlp_v_b, gmlp_w_s, gmlp_b_s, gmlp_w_o, moe_w_router, moe_bias, moe_w1, moe_w3, moe_w2, moe_ws1, moe_ws3, moe_ws2):
    batch, seq, d = x.shape
    nctx = ctx.shape[1]
    depth = w_ada.shape[0]
    assert depth == 2, "layer 0 is the attention mixer, layer 1 the gMLP mixer"
    alpha = float((2 * depth) ** 0.25)
    t = batch * seq

    pad = (-(batch + 1)) % 8
    cc = jnp.concatenate([c, c_ctx[None, :], jnp.zeros((pad, d), F32)], axis=0)
    mods = _modulations(cc, w_ada, b_ada)

    def mod(layer, j, rows=slice(0, batch)):
        return mods[layer, rows, j * d:(j + 1) * d]

    def per_batch(layer, j):
        return mod(layer, j).reshape(batch, 1, d)

    def router(layer):
        w = jnp.pad(moe_w_router[layer], ((0, 0), (0, LANES - N_EXPERTS)))
        hi = w.astype(BF16)
        lo = (w - hi.astype(F32)).astype(BF16)
        return jnp.concatenate([hi, lo], axis=1), moe_bias[layer].reshape(N_EXPERTS, 1)

    def shared_w(layer):
        return moe_ws1[layer].astype(BF16), moe_ws3[layer].astype(BF16), moe_ws2[layer].astype(BF16)

    x2 = x.reshape(t, d)
    ln_mix_g, ln_mix_b, ln_ffn_g, ln_ffn_b = (
        p.reshape(depth, 1, d) for p in (ln_mix_g, ln_mix_b, ln_ffn_g, ln_ffn_b))

    qd = Q_PER_KV * N_KV_HEADS * HEAD_DIM
    kvd = N_KV_HEADS * HEAD_DIM
    wqkv = attn_w_qkv[0]
    dup = lambda w: jnp.concatenate([w.reshape(d, N_KV_HEADS, 1, HEAD_DIM)] * (LANES // HEAD_DIM), axis=2
                                    ).reshape(d, N_KV_HEADS * LANES)
    wk2, wv2 = dup(wqkv[:, qd:qd + kvd]), dup(wqkv[:, qd + kvd:])
    w_all = jnp.concatenate([wqkv[:, :qd] * (HEAD_DIM ** -0.5), wk2, wv2], axis=1).astype(BF16)
    q, k2, v2 = _qkv_proj(x2, per_batch(0, 1), per_batch(0, 0), w_all, _rope_tables(seq), seq)
    ctx_row = slice(batch, batch + 1)
    kc2, vc2 = _ctx_kv(ctx.reshape(batch * nctx, d), mod(0, 1, ctx_row), mod(0, 0, ctx_row),
                       jnp.concatenate([wk2, wv2], axis=1).astype(BF16))
    o = _attention(q, k2, v2, kc2, vc2, attn_sink[0], batch, seq, nctx)
    x1, f, ids, wts, rank, cnt = _attn_out(
        o, attn_w_o[0].astype(BF16), x2, per_batch(0, 2), per_batch(0, 4), per_batch(0, 3),
        ln_mix_g, ln_mix_b, *router(0), seq, 0, alpha)
    x2 = _moe(f, x1, ids, wts, rank, cnt, per_batch(0, 5), ln_ffn_g, ln_ffn_b,
              moe_w1, moe_w3, moe_w2, *shared_w(0), seq, 0, alpha)

    width = gmlp_w_in.shape[2] // 2
    w_in = gmlp_w_in[0].astype(BF16)
    b_in = gmlp_b_in[0].reshape(1, 2 * width)
    vg, vb = gmlp_v_g[0].reshape(1, width), gmlp_v_b[0].reshape(1, width)
    sc, sh = per_batch(1, 1), per_batch(1, 0)
    u = _gmlp_in(x2, sc, sh, w_in[:, :width], b_in[:, :width], vg, vb, seq, False)
    v = _gmlp_in(x2, sc, sh, w_in[:, width:], b_in[:, width:], vg, vb, seq, True)
    x1, f, ids, wts, rank, cnt = _gmlp_out(
        u, v, gmlp_w_s[0].astype(BF16), gmlp_b_s[0][:, :, None], gmlp_w_o[0].astype(BF16), x2,
        per_batch(1, 2), per_batch(1, 4), per_batch(1, 3), ln_mix_g, ln_mix_b, *router(1), seq, 1, alpha)
    x2 = _moe(f, x1, ids, wts, rank, cnt, per_batch(1, 5), ln_ffn_g, ln_ffn_b,
              moe_w1, moe_w3, moe_w2, *shared_w(1), seq, 1, alpha)
    return x2.reshape(batch, seq, d)
```

```python
import functools

import jax
import jax.numpy as jnp
from jax import lax
from jax.experimental import pallas as pl
from jax.experimental.pallas import tpu as pltpu
from jax.experimental.pallas import tpu_sc as plsc

F32 = jnp.float32
BF16 = jnp.bfloat16
I32 = jnp.int32

GRID_W = 64
N_KV_HEADS = 4
Q_PER_KV = 8
HEAD_DIM = 64
ROPE_HALF = 16
WINDOW = 128
Q_BLOCK = 128
ROPE_BASE = 10000.0
NEG_INF = -1e30
CHUNK = 128
N_GMLP_GROUPS = 8
N_EXPERTS = 64
TOP_K = 8
N_EXPERT_GROUPS = 8
EXPERTS_PER_GROUP = 8
TOPK_GROUPS = 4
ROUTED_SCALE = 2.5
LN_EPS = 1e-5

LANES = 128
VMEM_LIMIT_BYTES = 56 * 1024 * 1024


def _params(*sem):
    return pltpu.CompilerParams(dimension_semantics=sem, vmem_limit_bytes=VMEM_LIMIT_BYTES)


def _silu(a):
    return a * jax.nn.sigmoid(a)


def _layernorm(v, g, b):
    mu = jnp.mean(v, axis=-1, keepdims=True)
    d = v - mu
    var = jnp.mean(d * d, axis=-1, keepdims=True)
    return d * lax.rsqrt(var + LN_EPS) * g + b


def _bdot(a, b):
    return jnp.dot(a, b, preferred_element_type=F32)


U32 = jnp.uint32


def _pack_pairs(a):
    n = a.shape[1] // 2
    lo = lax.bitcast_convert_type(a[:, :n].astype(BF16).astype(F32), U32)
    hi = lax.bitcast_convert_type(a[:, n:].astype(BF16).astype(F32), U32)
    return hi | (lo >> 16)


def _unpack_pairs(p):
    lo = lax.bitcast_convert_type(p << 16, F32)
    hi = lax.bitcast_convert_type(p & jnp.uint32(0xFFFF0000), F32)
    return lo, hi


def _store_row_tiles(ref, packed):
    r, n = packed.shape
    sub = n // LANES
    for j in range(sub):
        ref[pl.ds(j, r, stride=sub), :] = packed[:, LANES * j:LANES * (j + 1)]


def _load_row_tiles(ref, r):
    sub = ref.shape[0] // r
    pieces = [_unpack_pairs(ref[pl.ds(j, r, stride=sub), :]) for j in range(sub)]
    return [p[0] for p in pieces], [p[1] for p in pieces]


def _load_rows_bf16(ref, r):
    lo, hi = _load_row_tiles(ref, r)
    return jnp.concatenate(lo + hi, axis=1).astype(BF16)


def _mod_kernel(c_ref, w_ref, b_ref, o_ref):
    a = _silu(c_ref[...]).astype(BF16)
    o_ref[0] = _bdot(a, w_ref[0].astype(BF16)) + b_ref[0]


def _modulations(cc, w_ada, b_ada):
    depth, d, n6 = w_ada.shape
    r = cc.shape[0]
    tn = max(w for w in range(LANES, min(1024, n6) + 1, LANES) if n6 % w == 0)
    return pl.pallas_call(
        _mod_kernel,
        out_shape=jax.ShapeDtypeStruct((depth, r, n6), F32),
        grid=(depth, n6 // tn),
        in_specs=[pl.BlockSpec((r, d), lambda l, j: (0, 0)),
                  pl.BlockSpec((1, d, tn), lambda l, j: (l, 0, j)),
                  pl.BlockSpec((1, 1, tn), lambda l, j: (l, 0, j))],
        out_specs=pl.BlockSpec((1, r, tn), lambda l, j: (l, 0, j)),
        compiler_params=_params("parallel", "parallel"),
        name="adaln_mod",
    )(cc, w_ada, b_ada.reshape(depth, 1, n6))


def _rope_tables(seq):
    rows = seq // GRID_W
    row_ids = jnp.repeat(jnp.arange(rows, dtype=F32), GRID_W)
    col_ids = jnp.tile(jnp.arange(GRID_W, dtype=F32), rows)
    inv_freq = ROPE_BASE ** (-jnp.arange(0, 2 * ROPE_HALF, 2, dtype=F32) / (2 * ROPE_HALF))
    ar, ac = row_ids[:, None] * inv_freq, col_ids[:, None] * inv_freq
    z = jnp.zeros_like(ar)
    cos = jnp.concatenate([jnp.cos(ar), jnp.cos(ar), jnp.cos(ac), jnp.cos(ac)], axis=-1)
    sin_lo = jnp.concatenate([-jnp.sin(ar), z, -jnp.sin(ac), z], axis=-1)
    sin_hi = jnp.concatenate([z, jnp.sin(ar), z, jnp.sin(ac)], axis=-1)
    rep = LANES // HEAD_DIM
    return tuple(jnp.tile(t, (1, rep)) for t in (cos, sin_lo, sin_hi))


def _qkv_kernel(x_ref, sc_ref, sh_ref, w_ref, cos_ref, slo_ref, shi_ref, q_ref, k_ref, v_ref, *, cw):
    h = (x_ref[...] * (1.0 + sc_ref[0]) + sh_ref[0]).astype(BF16)
    cos, slo, shi = cos_ref[...], slo_ref[...], shi_ref[...]

    def rope(a):
        return (a * cos + pltpu.roll(a, LANES - ROPE_HALF, 1) * slo
                + pltpu.roll(a, ROPE_HALF, 1) * shi)

    qd, kd = q_ref.shape[1], k_ref.shape[1]
    for c0 in range(0, qd, cw):
        acc = _bdot(h, w_ref[:, c0:c0 + cw])
        for b0 in range(0, cw, LANES):
            q_ref[:, c0 + b0:c0 + b0 + LANES] = rope(acc[:, b0:b0 + LANES]).astype(BF16)
    acc = _bdot(h, w_ref[:, qd:qd + kd])
    for b0 in range(0, kd, LANES):
        k_ref[:, b0:b0 + LANES] = rope(acc[:, b0:b0 + LANES]).astype(BF16)
    v_ref[...] = _bdot(h, w_ref[:, qd + kd:qd + 2 * kd]).astype(BF16)


def _qkv_proj(x2, sc, sh, w, tables, seq):
    t, d = x2.shape
    kd = N_KV_HEADS * LANES
    qd = w.shape[1] - 2 * kd
    tm = min(512, seq)
    per_b = seq // tm
    row = lambda i: (i, 0)
    mod = lambda i: (i // per_b, 0, 0)
    tab = lambda i: (i % per_b, 0)
    return pl.pallas_call(
        functools.partial(_qkv_kernel, cw=min(512, qd)),
        out_shape=(jax.ShapeDtypeStruct((t, qd), BF16), jax.ShapeDtypeStruct((t, kd), BF16),
                   jax.ShapeDtypeStruct((t, kd), BF16)),
        grid=(t // tm,),
        in_specs=[pl.BlockSpec((tm, d), row), pl.BlockSpec((1, 1, d), mod), pl.BlockSpec((1, 1, d), mod),
                  pl.BlockSpec(w.shape, lambda i: (0, 0)),
                  pl.BlockSpec((tm, LANES), tab), pl.BlockSpec((tm, LANES), tab), pl.BlockSpec((tm, LANES), tab)],
        out_specs=(pl.BlockSpec((tm, qd), row), pl.BlockSpec((tm, kd), row), pl.BlockSpec((tm, kd), row)),
        compiler_params=_params("parallel"),
        name="qkv_rope",
    )(x2, sc, sh, w, *tables)


def _ctxkv_kernel(x_ref, sc_ref, sh_ref, w_ref, k_ref, v_ref):
    h = (x_ref[...] * (1.0 + sc_ref[...]) + sh_ref[...]).astype(BF16)
    kd = k_ref.shape[1]
    k_ref[...] = _bdot(h, w_ref[:, :kd]).astype(BF16)
    v_ref[...] = _bdot(h, w_ref[:, kd:]).astype(BF16)


def _ctx_kv(c2, sc, sh, w):
    t, d = c2.shape
    kd = w.shape[1] // 2
    tm = min(512, t)
    row = lambda i: (i, 0)
    fix = lambda i: (0, 0)
    return pl.pallas_call(
        _ctxkv_kernel,
        out_shape=(jax.ShapeDtypeStruct((t, kd), BF16), jax.ShapeDtypeStruct((t, kd), BF16)),
        grid=(t // tm,),
        in_specs=[pl.BlockSpec((tm, d), row), pl.BlockSpec((1, d), fix), pl.BlockSpec((1, d), fix),
                  pl.BlockSpec(w.shape, fix)],
        out_specs=(pl.BlockSpec((tm, kd), row), pl.BlockSpec((tm, kd), row)),
        compiler_params=_params("parallel"),
        name="ctx_kv",
    )(c2, sc, sh, w)


def _attn_kernel(sink_ref, q_ref, k_ref, v_ref, kc_ref, vc_ref, o_ref):
    for b in range(q_ref.shape[1] // Q_BLOCK):
        _attn_block(sink_ref, q_ref, k_ref, v_ref, kc_ref, vc_ref, o_ref, b)


def _attn_block(sink_ref, q_ref, k_ref, v_ref, kc_ref, vc_ref, o_ref, b):
    kv = pl.program_id(1)
    i = pl.program_id(2) * (q_ref.shape[1] // Q_BLOCK) + b
    seq, nctx = k_ref.shape[1], kc_ref.shape[1]
    nwin = Q_BLOCK + 2 * WINDOW
    start = pl.multiple_of(jnp.clip(i * Q_BLOCK - WINDOW, 0, seq - nwin), Q_BLOCK)
    nkey = nctx + nwin
    kall = jnp.concatenate([kc_ref[0], k_ref[0, pl.ds(start, nwin), :]], axis=0)
    vall = jnp.concatenate([vc_ref[0], v_ref[0, pl.ds(start, nwin), :]], axis=0)
    low = lax.broadcasted_iota(I32, (nkey, LANES), 1) < HEAD_DIM
    zero = jnp.zeros((nkey, LANES), BF16)
    npair = Q_PER_KV // 2
    q = q_ref[0, Q_BLOCK * b:Q_BLOCK * (b + 1), :]
    qs = jnp.concatenate([q[:, LANES * j:LANES * (j + 1)] for j in range(npair)], axis=0)
    nrow = npair * Q_BLOCK
    rows = lax.broadcasted_iota(I32, (Q_BLOCK, nkey), 0)
    cols = lax.broadcasted_iota(I32, (Q_BLOCK, nkey), 1)
    dist = (i * Q_BLOCK - start + nctx) + rows - cols
    valid = (cols < nctx) | (jnp.abs(dist) <= WINDOW)
    contract_last = (((1,), (1,)), ((), ()))
    out = jnp.zeros((nrow, LANES), F32)
    for par in range(2):
        keep = low if par == 0 else jnp.logical_not(low)
        kh = jnp.where(keep, kall, zero)
        vh = jnp.where(keep, vall, zero)
        s = lax.dot_general(qs, kh, contract_last, preferred_element_type=F32)
        s = jnp.concatenate(
            [jnp.where(valid, s[Q_BLOCK * j:Q_BLOCK * (j + 1), :], NEG_INF) for j in range(npair)], axis=0)
        sink = jnp.concatenate(
            [jnp.full((Q_BLOCK, 1), sink_ref[kv * Q_PER_KV + 2 * j + par], F32) for j in range(npair)], axis=0)
        m = jnp.maximum(jnp.max(s, axis=1, keepdims=True), sink)
        p = jnp.exp(s - m)
        den = jnp.sum(p, axis=1, keepdims=True) + jnp.exp(sink - m)
        out = out + _bdot(p.astype(BF16), vh) / den
    for j in range(npair):
        o_ref[0, Q_BLOCK * b:Q_BLOCK * (b + 1), LANES * j:LANES * (j + 1)] = (
            out[Q_BLOCK * j:Q_BLOCK * (j + 1), :].astype(BF16))


def _attention(q, k2, v2, kc2, vc2, sink, batch, seq, nctx):
    qd = q.shape[1]
    gw = Q_PER_KV * HEAD_DIM
    q3 = q.reshape(batch, seq, qd)
    k3, v3 = k2.reshape(batch, seq, -1), v2.reshape(batch, seq, -1)
    kc3, vc3 = kc2.reshape(batch, nctx, -1), vc2.reshape(batch, nctx, -1)
    full = lambda b, h, i: (b, 0, h)
    blk = lambda b, h, i: (b, i, h)
    rows = 2 * Q_BLOCK if seq % (2 * Q_BLOCK) == 0 else Q_BLOCK
    out = pl.pallas_call(
        _attn_kernel,
        out_shape=jax.ShapeDtypeStruct((batch, seq, qd), BF16),
        grid=(batch, N_KV_HEADS, seq // rows),
        in_specs=[pl.BlockSpec(memory_space=pltpu.SMEM),
                  pl.BlockSpec((1, rows, gw), blk),
                  pl.BlockSpec((1, seq, LANES), full), pl.BlockSpec((1, seq, LANES), full),
                  pl.BlockSpec((1, nctx, LANES), full), pl.BlockSpec((1, nctx, LANES), full)],
        out_specs=pl.BlockSpec((1, rows, gw), blk),
        compiler_params=_params("parallel", "parallel", "parallel"),
        name="window_gqa",
    )(sink, q3, k3, v3, kc3, vc3)
    return out.reshape(batch * seq, qd)


def _split_bf16(a):
    hi = a.astype(BF16)
    return hi, (a - hi.astype(F32)).astype(BF16)


def _post_mixer(x, y, gm, lng, lnb, scf, shf, wr_ref, rb_ref, run_ref,
                x1_ref, f_ref, ids_ref, wts_ref, rank_ref, cnt_ref, alpha):
    tm = x.shape[0]
    x1 = _layernorm(alpha * x + gm * y, lng, lnb)
    x1_ref[...] = x1
    f = x1 * (1.0 + scf) + shf
    _store_row_tiles(f_ref, _pack_pairs(f))

    fh, fl = _split_bf16(f)
    prod = _bdot(fh, wr_ref[...]) + _bdot(fl, wr_ref[...])
    logits = (prod[:, :LANES] + prod[:, LANES:]).T[:N_EXPERTS, :]
    scores = jax.nn.sigmoid(logits)
    biased = scores + rb_ref[...]

    eg = EXPERTS_PER_GROUP
    sub = lax.broadcasted_iota(I32, (eg, tm), 0)
    gscore = []
    for g in range(N_EXPERT_GROUPS):
        tg = biased[eg * g:eg * (g + 1), :]
        m1 = jnp.max(tg, axis=0, keepdims=True)
        i1 = jnp.min(jnp.where(tg == m1, sub, eg), axis=0, keepdims=True)
        m2 = jnp.max(jnp.where(sub == i1, -jnp.inf, tg), axis=0, keepdims=True)
        gscore.append(m1 + m2)
    gsel = [jnp.zeros((1, tm), jnp.bool_) for _ in range(N_EXPERT_GROUPS)]
    for _ in range(TOPK_GROUPS):
        best = functools.reduce(jnp.maximum, gscore)
        taken = jnp.zeros((1, tm), jnp.bool_)
        for g in range(N_EXPERT_GROUPS):
            hit = jnp.logical_and(gscore[g] == best, jnp.logical_not(taken))
            taken = jnp.logical_or(taken, hit)
            gsel[g] = jnp.logical_or(gsel[g], hit)
            gscore[g] = jnp.where(hit, -jnp.inf, gscore[g])
    cur = jnp.concatenate(
        [jnp.where(gsel[g], biased[eg * g:eg * (g + 1), :], -jnp.inf) for g in range(N_EXPERT_GROUPS)], axis=0)

    eidx = lax.broadcasted_iota(I32, (N_EXPERTS, tm), 0)
    picks, wts, hots = [], [], []
    for _ in range(TOP_K):
        m = jnp.max(cur, axis=0, keepdims=True)
        idx = jnp.min(jnp.where(cur == m, eidx, N_EXPERTS), axis=0, keepdims=True)
        hot = eidx == idx
        picks.append(idx)
        wts.append(jnp.sum(jnp.where(hot, scores, 0.0), axis=0, keepdims=True))
        hots.append(hot)
        cur = jnp.where(hot, -jnp.inf, cur)
    wsum = functools.reduce(jnp.add, wts)

    assigned = functools.reduce(jnp.add, [h.astype(F32) for h in hots])
    before = (lax.broadcasted_iota(I32, (tm, tm), 0) < lax.broadcasted_iota(I32, (tm, tm), 1)).astype(BF16)
    pos = _bdot(assigned.astype(BF16), before) + run_ref[:, 0:1]
    for k in range(TOP_K):
        ids_ref[k:k + 1, :] = picks[k]
        wts_ref[k:k + 1, :] = wts[k] / wsum * ROUTED_SCALE
        rank_ref[k:k + 1, :] = jnp.sum(jnp.where(hots[k], pos, 0.0), axis=0, keepdims=True).astype(I32)
    run_ref[...] = run_ref[...] + jnp.sum(assigned, axis=1, keepdims=True)
    cnt_ref[...] = run_ref[...]


def _route_out_shapes(t, d):
    return (jax.ShapeDtypeStruct((t, d), F32), jax.ShapeDtypeStruct((t * d // (2 * LANES), LANES), U32),
            jax.ShapeDtypeStruct((TOP_K, t), I32), jax.ShapeDtypeStruct((TOP_K, t), F32),
            jax.ShapeDtypeStruct((TOP_K, t), I32), jax.ShapeDtypeStruct((N_EXPERTS, LANES), F32))


def _route_out_specs(tm, d):
    row = lambda i: (i, 0)
    col = lambda i: (0, i)
    return (pl.BlockSpec((tm, d), row), pl.BlockSpec((tm * d // (2 * LANES), LANES), row),
            pl.BlockSpec((TOP_K, tm), col), pl.BlockSpec((TOP_K, tm), col), pl.BlockSpec((TOP_K, tm), col),
            pl.BlockSpec((N_EXPERTS, LANES), lambda i: (0, 0)))


def _route_in_specs(tm, d, per_b, layer):
    mod = lambda i: (i // per_b, 0, 0)
    lrow = lambda i: (layer, 0, 0)
    fix = lambda i: (0, 0)
    return [pl.BlockSpec((tm, d), lambda i: (i, 0)),
            pl.BlockSpec((1, 1, d), mod), pl.BlockSpec((1, 1, d), mod), pl.BlockSpec((1, 1, d), mod),
            pl.BlockSpec((1, 1, d), lrow), pl.BlockSpec((1, 1, d), lrow),
            pl.BlockSpec((d, 2 * LANES), fix),
            pl.BlockSpec((N_EXPERTS, 1), fix)]


def _oproj_kernel(o_ref, wo_ref, x_ref, gm_ref, scf_ref, shf_ref, lng_ref, lnb_ref, wr_ref, rb_ref,
                  x1_ref, f_ref, ids_ref, wts_ref, rank_ref, cnt_ref, run_ref, *, alpha):
    @pl.when(pl.program_id(0) == 0)
    def _():
        run_ref[...] = jnp.zeros_like(run_ref)

    y = _bdot(o_ref[...], wo_ref[...])
    _post_mixer(x_ref[...], y, gm_ref[0], lng_ref[0], lnb_ref[0], scf_ref[0], shf_ref[0],
                wr_ref, rb_ref, run_ref, x1_ref, f_ref, ids_ref, wts_ref, rank_ref, cnt_ref, alpha)


def _attn_out(o, wo, x2, gm, scf, shf, lng, lnb, wr, rb, seq, layer, alpha):
    t, d = x2.shape
    tm = min(512, seq)
    return pl.pallas_call(
        functools.partial(_oproj_kernel, alpha=alpha),
        out_shape=_route_out_shapes(t, d),
        grid=(t // tm,),
        in_specs=[pl.BlockSpec((tm, o.shape[1]), lambda i: (i, 0)), pl.BlockSpec(wo.shape, lambda i: (0, 0))]
        + _route_in_specs(tm, d, seq // tm, layer),
        out_specs=_route_out_specs(tm, d),
        scratch_shapes=[pltpu.VMEM((N_EXPERTS, LANES), F32)],
        compiler_params=_params("arbitrary"),
        name="attn_out_route",
    )(o, wo, x2, gm, scf, shf, lng, lnb, wr, rb)


def _gmlp_in_kernel(x_ref, sc_ref, sh_ref, w_ref, b_ref, g_ref, beta_ref, o_ref, *, cw, normalize):
    h = (x_ref[...] * (1.0 + sc_ref[0]) + sh_ref[0]).astype(BF16)
    n = o_ref.shape[1]
    inv_sqrt2 = 0.7071067811865476
    parts = []
    for c0 in range(0, n, cw):
        z = _bdot(h, w_ref[:, c0:c0 + cw]) + b_ref[:, c0:c0 + cw]
        z = 0.5 * z * (1.0 + lax.erf(z * inv_sqrt2))
        if normalize:
            parts.append(z)
        else:
            o_ref[:, c0:c0 + cw] = z.astype(BF16)
    if normalize:
        v = jnp.concatenate(parts, axis=1)
        o_ref[...] = _layernorm(v, g_ref[...], beta_ref[...]).astype(BF16)


def _gmlp_in(x2, sc, sh, w, b, g, beta, seq, normalize):
    t, d = x2.shape
    n = w.shape[1]
    tm = min(256, seq)
    per_b = seq // tm
    row = lambda i: (i, 0)
    mod = lambda i: (i // per_b, 0, 0)
    fix = lambda i: (0, 0)
    return pl.pallas_call(
        functools.partial(_gmlp_in_kernel, cw=min(512, n), normalize=normalize),
        out_shape=jax.ShapeDtypeStruct((t, n), BF16),
        grid=(t // tm,),
        in_specs=[pl.BlockSpec((tm, d), row), pl.BlockSpec((1, 1, d), mod), pl.BlockSpec((1, 1, d), mod),
                  pl.BlockSpec(w.shape, fix), pl.BlockSpec((1, n), fix), pl.BlockSpec((1, n), fix),
                  pl.BlockSpec((1, n), fix)],
        out_specs=pl.BlockSpec((tm, n), row),
        compiler_params=_params("parallel"),
        name="gmlp_in_v" if normalize else "gmlp_in_u",
    )(x2, sc, sh, w, b, g, beta)


def _gmlp_out_kernel(u_ref, v_ref, ws_ref, bs_ref, wo_ref, x_ref, gm_ref, scf_ref, shf_ref, lng_ref, lnb_ref,
                     wr_ref, rb_ref, x1_ref, f_ref, ids_ref, wts_ref, rank_ref, cnt_ref,
                     run_ref, gated_ref, *, alpha):
    @pl.when(pl.program_id(0) == 0)
    def _():
        run_ref[...] = jnp.zeros_like(run_ref)

    tm, width = u_ref.shape
    gd = width // N_GMLP_GROUPS
    for r0 in range(0, tm, CHUNK):
        for g in range(N_GMLP_GROUPS):
            c0 = g * gd
            mixed = _bdot(ws_ref[g], v_ref[r0:r0 + CHUNK, c0:c0 + gd]) + bs_ref[g]
            gated_ref[r0:r0 + CHUNK, c0:c0 + gd] = (
                u_ref[r0:r0 + CHUNK, c0:c0 + gd].astype(F32) * mixed).astype(BF16)
    y = _bdot(gated_ref[...], wo_ref[...])
    _post_mixer(x_ref[...], y, gm_ref[0], lng_ref[0], lnb_ref[0], scf_ref[0], shf_ref[0],
                wr_ref, rb_ref, run_ref, x1_ref, f_ref, ids_ref, wts_ref, rank_ref, cnt_ref, alpha)


def _gmlp_out(u, v, ws, bs, wo, x2, gm, scf, shf, lng, lnb, wr, rb, seq, layer, alpha):
    t, d = x2.shape
    width = u.shape[1]
    tm = min(256, seq)
    row = lambda i: (i, 0)
    return pl.pallas_call(
        functools.partial(_gmlp_out_kernel, alpha=alpha),
        out_shape=_route_out_shapes(t, d),
        grid=(t // tm,),
        in_specs=[pl.BlockSpec((tm, width), row), pl.BlockSpec((tm, width), row),
                  pl.BlockSpec(ws.shape, lambda i: (0, 0, 0)), pl.BlockSpec(bs.shape, lambda i: (0, 0, 0)),
                  pl.BlockSpec(wo.shape, lambda i: (0, 0))]
        + _route_in_specs(tm, d, seq // tm, layer),
        out_specs=_route_out_specs(tm, d),
        scratch_shapes=[pltpu.VMEM((N_EXPERTS, LANES), F32), pltpu.VMEM((tm, width), BF16)],
        compiler_params=_params("arbitrary"),
        name="gmlp_out_route",
    )(u, v, ws, bs, wo, x2, gm, scf, shf, lng, lnb, wr, rb)


SC_CORES, SC_SUBCORES, SC_LANES = 2, 16, 16
MXU_N = 256


def _shared_kernel(f_ref, ws1_ref, ws3_ref, ws2_ref, sh_ref):
    h = _load_rows_bf16(f_ref, sh_ref.shape[0])
    a = _silu(_bdot(h, ws1_ref[...])) * _bdot(h, ws3_ref[...])
    sh_ref[...] = _bdot(a.astype(BF16), ws2_ref[...]).astype(BF16)


def _shared_expert(f, ws1, ws3, ws2, t):
    sub = f.shape[0] // t
    d = ws1.shape[0]
    tm = min(512, t)
    fix = lambda i: (0, 0)
    return pl.pallas_call(
        _shared_kernel,
        out_shape=jax.ShapeDtypeStruct((t, d), BF16),
        grid=(t // tm,),
        in_specs=[pl.BlockSpec((tm * sub, LANES), lambda i: (i, 0)),
                  pl.BlockSpec(ws1.shape, fix), pl.BlockSpec(ws3.shape, fix), pl.BlockSpec(ws2.shape, fix)],
        out_specs=pl.BlockSpec((tm, d), lambda i: (i, 0)),
        compiler_params=_params("parallel"),
        name="moe_shared",
    )(f, ws1, ws3, ws2)


def _inverse_map(slots_flat, nslot):
    n = slots_flat.shape[0]
    workers = SC_CORES * SC_SUBCORES
    per_w = nslot // workers
    chunk = min(8192, n)
    assert nslot % (workers * SC_LANES) == 0 and n % chunk == 0 and chunk % SC_LANES == 0
    mesh = plsc.VectorSubcoreMesh(core_axis_name="c", subcore_axis_name="s")

    @functools.partial(
        pl.kernel, out_type=jax.ShapeDtypeStruct((nslot,), I32), mesh=mesh,
        scratch_types=[pltpu.VMEM((per_w,), I32), pltpu.VMEM((chunk,), I32)],
        compiler_params=pltpu.CompilerParams(needs_layout_passes=False))
    def inverse(slots_hbm, inv_hbm, local, buf):
        lo = (lax.axis_index("s") * SC_CORES + lax.axis_index("c")) * per_w
        unused = jnp.full((SC_LANES,), -1, I32)

        @pl.loop(0, per_w, step=SC_LANES)
        def _(j):
            local[pl.ds(j, SC_LANES)] = unused

        lane = lax.iota(I32, SC_LANES)

        @pl.loop(0, n, step=chunk)
        def _(c0):
            pltpu.sync_copy(slots_hbm.at[pl.ds(c0, chunk)], buf)

            @pl.loop(0, chunk, step=SC_LANES)
            def _(j):
                idx = buf[pl.ds(j, SC_LANES)] - lo
                mine = jnp.logical_and(idx >= 0, idx < per_w)
                plsc.store_scatter(local, [idx], c0 + j + lane, mask=mine)

        pltpu.sync_copy(local, inv_hbm.at[pl.ds(lo, per_w)])

    return inverse(slots_flat)


def _experts_kernel(texp_ref, tok_ref, w1_ref, w3_ref, w2_ref, f_hbm, y_ref, w1b, w3b, w2b, xa, xb, gsem, *, tme):
    i = pl.program_id(0)
    last = pl.num_programs(0) - 1
    sub = xa.shape[0] // tme
    d, fdim = w1b.shape
    half = d // 2

    @pl.when(i == 0)
    def _():
        xb[...] = jnp.zeros_like(xb)

    @pl.when(jnp.logical_or(i == 0, texp_ref[i] != texp_ref[jnp.maximum(i - 1, 0)]))
    def _():
        w1b[...] = w1_ref[...].astype(BF16)
        w3b[...] = w3_ref[...].astype(BF16)
        w2b[...] = w2_ref[...].astype(BF16)

    def whole_tile(buf, sem):
        return pltpu.make_async_copy(f_hbm.at[pl.ds(0, tme * sub), :], buf, sem)

    def step(par):
        xg, xc = (xa, xb) if par == 0 else (xb, xa)

        @pl.when(i > 0)
        def _():
            whole_tile(xc, gsem.at[1 - par]).wait()

        pw1, pw2 = min(MXU_N, fdim), min(MXU_N, half)
        per_piece = -(-tme // (2 * (fdim // pw1)))
        issued = [0]

        def issue_rows():
            for r in range(issued[0], min(issued[0] + per_piece, tme)):
                tok = pl.multiple_of(tok_ref[r] * sub, sub)
                pltpu.make_async_copy(f_hbm.at[pl.ds(tok, sub), :], xg.at[pl.ds(r * sub, sub), :],
                                      gsem.at[par]).start()
            issued[0] = min(issued[0] + per_piece, tme)

        h = _load_rows_bf16(xc, tme)
        gate, up = [], []
        for c0 in range(0, fdim, pw1):
            gate.append(_bdot(h, w1b[:, c0:c0 + pw1]))
            issue_rows()
        for c0 in range(0, fdim, pw1):
            up.append(_bdot(h, w3b[:, c0:c0 + pw1]))
            issue_rows()
        assert issued[0] == tme
        a = (_silu(jnp.concatenate(gate, axis=1)) * jnp.concatenate(up, axis=1)).astype(BF16)
        for c0 in range(0, half, pw2):
            lo = _bdot(a, w2b[:, c0:c0 + pw2])
            hi = _bdot(a, w2b[:, half + c0:half + c0 + pw2])
            packed = _pack_pairs(jnp.concatenate([lo, hi], axis=1))
            for b0 in range(0, pw2, LANES):
                y_ref[pl.ds((c0 + b0) // LANES, tme, stride=sub), :] = packed[:, b0:b0 + LANES]

        @pl.when(i == last)
        def _():
            whole_tile(xg, gsem.at[par]).wait()

    for par in range(2):
        pl.when(jnp.bitwise_and(i, 1) == par)(functools.partial(step, par))


def _experts(texp, tok, w1, w3, w2, f, layer, tme):
    d, fdim = w1.shape[-2:]
    sub = d // (2 * LANES)
    ntile = texp.shape[0] - 1
    wsel = lambda i, te: (layer, te[i], 0, 0)
    return pl.pallas_call(
        functools.partial(_experts_kernel, tme=tme),
        out_shape=jax.ShapeDtypeStruct((ntile * tme * sub, LANES), U32),
        grid_spec=pltpu.PrefetchScalarGridSpec(
            num_scalar_prefetch=1,
            grid=(ntile + 1,),
            in_specs=[pl.BlockSpec((tme,), lambda i, te: (i,), memory_space=pltpu.SMEM),
                      pl.BlockSpec((None, None, d, fdim), wsel), pl.BlockSpec((None, None, d, fdim), wsel),
                      pl.BlockSpec((None, None, fdim, d), wsel),
                      pl.BlockSpec(memory_space=pl.ANY)],
            out_specs=pl.BlockSpec((tme * sub, LANES), lambda i, te: (jnp.maximum(i - 1, 0), 0)),
            scratch_shapes=[pltpu.VMEM((d, fdim), BF16), pltpu.VMEM((d, fdim), BF16), pltpu.VMEM((fdim, d), BF16),
                            pltpu.VMEM((tme * sub, LANES), U32), pltpu.VMEM((tme * sub, LANES), U32),
                            pltpu.SemaphoreType.DMA((2,))]),
        compiler_params=_params("arbitrary"),
        name="moe_experts",
    )(texp, tok, w1, w3, w2, f)


def _combine_kernel(slot_ref, wt_ref, x1_ref, sh_ref, gf_ref, lng_ref, lnb_ref, ys_hbm, o_ref, ga, gb, gsem, *, alpha):
    i = pl.program_id(0)
    last = pl.num_programs(0) - 1
    tm = x1_ref.shape[0]
    sub = ga.shape[1] // tm

    @pl.when(i == 0)
    def _():
        gb[...] = jnp.zeros_like(gb)

    def whole_plane(buf, k, sem):
        return pltpu.make_async_copy(ys_hbm.at[pl.ds(0, tm * sub), :], buf.at[k], sem)

    def step(par):
        gg, gc = (ga, gb) if par == 0 else (gb, ga)

        @pl.when(i > 0)
        def _():
            for k in range(TOP_K):
                whole_plane(gc, k, gsem.at[1 - par]).wait()

        def issue(t0, t1):
            for tt in range(t0, t1):
                for k in range(TOP_K):
                    row = pl.multiple_of(slot_ref[tt * TOP_K + k] * sub, sub)
                    pltpu.make_async_copy(ys_hbm.at[pl.ds(row, sub), :], gg.at[k, pl.ds(tt * sub, sub), :],
                                          gsem.at[par]).start(priority=k % 2)

        lo = hi = None
        for k in range(TOP_K):
            issue(k * tm // TOP_K, (k + 1) * tm // TOP_K)
            w = wt_ref[:, k:k + 1]
            rl, rh = _load_row_tiles(gc.at[k], tm)
            lo = [w * b for b in rl] if lo is None else [a + w * b for a, b in zip(lo, rl)]
            hi = [w * b for b in rh] if hi is None else [a + w * b for a, b in zip(hi, rh)]
        f = jnp.concatenate(lo + hi, axis=1) + sh_ref[...].astype(F32)
        o_ref[...] = _layernorm(alpha * x1_ref[...] + gf_ref[0] * f, lng_ref[0], lnb_ref[0])

        @pl.when(i == last)
        def _():
            for k in range(TOP_K):
                whole_plane(gg, k, gsem.at[par]).wait()

    for par in range(2):
        pl.when(jnp.bitwise_and(i, 1) == par)(functools.partial(step, par))


def _combine(slot_tk, wt, x1, sh, gf, lng, lnb, ys, seq, tm, layer, alpha):
    t, d = x1.shape
    sub = d // (2 * LANES)
    per_b = seq // tm
    nstep = t // tm + 1
    prev = lambda i: jnp.maximum(i - 1, 0)
    row = lambda i: (prev(i), 0)
    lrow = lambda i: (layer, 0, 0)
    return pl.pallas_call(
        functools.partial(_combine_kernel, alpha=alpha),
        out_shape=jax.ShapeDtypeStruct((t, d), F32),
        grid=(nstep,),
        in_specs=[pl.BlockSpec((tm * TOP_K,), lambda i: (i,), memory_space=pltpu.SMEM),
                  pl.BlockSpec((tm, TOP_K), row), pl.BlockSpec((tm, d), row), pl.BlockSpec((tm, d), row),
                  pl.BlockSpec((1, 1, d), lambda i: (prev(i) // per_b, 0, 0)),
                  pl.BlockSpec((1, 1, d), lrow), pl.BlockSpec((1, 1, d), lrow),
                  pl.BlockSpec(memory_space=pl.ANY)],
        out_specs=pl.BlockSpec((tm, d), row),
        scratch_shapes=[pltpu.VMEM((TOP_K, tm * sub, LANES), U32), pltpu.VMEM((TOP_K, tm * sub, LANES), U32),
                        pltpu.SemaphoreType.DMA((2,))],
        compiler_params=_params("arbitrary"),
        name="moe_combine_ln",
    )(slot_tk, wt, x1, sh, gf, lng, lnb, ys)


def _moe(f, x1, ids, wts, rank, cnt, gf, lng, lnb, w1, w3, w2, ws1, ws3, ws2, seq, layer, alpha):
    t, d = x1.shape
    tme = min(512, t)
    counts = cnt[:, 0].astype(I32)
    tiles_e = (counts + tme - 1) // tme
    padded = tiles_e * tme
    base = jnp.cumsum(padded) - padded
    expert = jnp.arange(N_EXPERTS, dtype=I32)
    slots = jnp.sum(jnp.where(ids[:, :, None] == expert, base, 0), axis=-1) + rank
    ntile = (t * TOP_K) // tme + N_EXPERTS
    nslot = ntile * tme
    tile_ends = jnp.cumsum(tiles_e)
    texp = jnp.sum((jnp.arange(ntile, dtype=I32)[:, None] >= tile_ends[None, :]).astype(I32), axis=1)
    texp = jnp.minimum(texp, N_EXPERTS - 1)
    texp_step = jnp.concatenate([texp[:1], texp])

    inv = _inverse_map(slots.reshape(-1), nslot)
    tok = jnp.where(inv >= 0, inv % t, 0)
    tok = jnp.concatenate([tok, jnp.zeros((tme,), I32)])

    shared = _shared_expert(f, ws1, ws3, ws2, t)
    ys = _experts(texp_step, tok, w1, w3, w2, f, layer, tme)
    tmc = min(128, seq)
    slot_tk = jnp.concatenate([slots.T.reshape(-1), jnp.zeros((tmc * TOP_K,), I32)])
    return _combine(slot_tk, wts.T, x1, shared, gf, lng, lnb, ys, seq, tmc, layer, alpha)


def kernel(x, c, ctx, c_ctx, w_ada, b_ada, ln_mix_g, ln_mix_b, ln_ffn_g, ln_ffn_b, attn_w_qkv, attn_w_o, attn_sink, gmlp_w_in, gmlp_b_in, gmlp_v_g, gmlp_v_b, gmlp_w_s, gmlp_b_s, gmlp_w_o, moe_w_router, moe_bias, moe_w1, moe_w3, moe_w2, moe_ws1, moe_ws3, moe_ws2):
    batch, seq, d = x.shape
    nctx = ctx.shape[1]
    depth = w_ada.shape[0]
    assert depth == 2, "layer 0 is the attention mixer, layer 1 the gMLP mixer"
    alpha = float((2 * depth) ** 0.25)
    t = batch * seq

    pad = (-(batch + 1)) % 8
    cc = jnp.concatenate([c, c_ctx[None, :], jnp.zeros((pad, d), F32)], axis=0)
    mods = _modulations(cc, w_ada, b_ada)

    def mod(layer, j, rows=slice(0, batch)):
        return mods[layer, rows, j * d:(j + 1) * d]

    def per_batch(layer, j):
        return mod(layer, j).reshape(batch, 1, d)

    def router(layer):
        w = jnp.pad(moe_w_router[layer], ((0, 0), (0, LANES - N_EXPERTS)))
        hi = w.astype(BF16)
        lo = (w - hi.astype(F32)).astype(BF16)
        return jnp.concatenate([hi, lo], axis=1), moe_bias[layer].reshape(N_EXPERTS, 1)

    def shared_w(layer):
        return moe_ws1[layer].astype(BF16), moe_ws3[layer].astype(BF16), moe_ws2[layer].astype(BF16)

    x2 = x.reshape(t, d)
    ln_mix_g, ln_mix_b, ln_ffn_g, ln_ffn_b = (
        p.reshape(depth, 1, d) for p in (ln_mix_g, ln_mix_b, ln_ffn_g, ln_ffn_b))

    qd = Q_PER_KV * N_KV_HEADS * HEAD_DIM
    kvd = N_KV_HEADS * HEAD_DIM
    wqkv = attn_w_qkv[0]
    dup = lambda w: jnp.concatenate([w.reshape(d, N_KV_HEADS, 1, HEAD_DIM)] * (LANES // HEAD_DIM), axis=2
                                    ).reshape(d, N_KV_HEADS * LANES)
    wk2, wv2 = dup(wqkv[:, qd:qd + kvd]), dup(wqkv[:, qd + kvd:])
    w_all = jnp.concatenate([wqkv[:, :qd] * (HEAD_DIM ** -0.5), wk2, wv2], axis=1).astype(BF16)
    q, k2, v2 = _qkv_proj(x2, per_batch(0, 1), per_batch(0, 0), w_all, _rope_tables(seq), seq)
    ctx_row = slice(batch, batch + 1)
    kc2, vc2 = _ctx_kv(ctx.reshape(batch * nctx, d), mod(0, 1, ctx_row), mod(0, 0, ctx_row),
                       jnp.concatenate([wk2, wv2], axis=1).astype(BF16))
    o = _attention(q, k2, v2, kc2, vc2, attn_sink[0], batch, seq, nctx)
    x1, f, ids, wts, rank, cnt = _attn_out(
        o, attn_w_o[0].astype(BF16), x2, per_batch(0, 2), per_batch(0, 4), per_batch(0, 3),
        ln_mix_g, ln_mix_b, *router(0), seq, 0, alpha)
    x2 = _moe(f, x1, ids, wts, rank, cnt, per_batch(0, 5), ln_ffn_g, ln_ffn_b,
              moe_w1, moe_w3, moe_w2, *shared_w(0), seq, 0, alpha)

    width = gmlp_w_in.shape[2] // 2
    w_in = gmlp_w_in[0].astype(BF16)
    b_in = gmlp_b_in[0].reshape(1, 2 * width)
    vg, vb = gmlp_v_g[0].reshape(1, width), gmlp_v_b[0].reshape(1, width)
    sc, sh = per_batch(1, 1), per_batch(1, 0)
    u = _gmlp_in(x2, sc, sh, w_in[:, :width], b_in[:, :width], vg, vb, seq, False)
    v = _gmlp_in(x2, sc, sh, w_in[:, width:], b_in[:, width:], vg, vb, seq, True)
    x1, f, ids, wts, rank, cnt = _gmlp_out(
        u, v, gmlp_w_s[0].astype(BF16), gmlp_b_s[0][:, :, None], gmlp_w_o[0].astype(BF16), x2,
        per_batch(1, 2), per_batch(1, 4), per_batch(1, 3), ln_mix_g, ln_mix_b, *router(1), seq, 1, alpha)
    x2 = _moe(f, x1, ids, wts, rank, cnt, per_batch(1, 5), ln_ffn_g, ln_ffn_b,
              moe_w1, moe_w3, moe_w2, *shared_w(1), seq, 1, alpha)
    return x2.reshape(batch, seq, d)
```

```python
import functools

import jax
import jax.numpy as jnp
from jax import lax
from jax.experimental import pallas as pl
from jax.experimental.pallas import tpu as pltpu
from jax.experimental.pallas import tpu_sc as plsc

F32 = jnp.float32
BF16 = jnp.bfloat16
I32 = jnp.int32

GRID_W = 64
N_KV_HEADS = 4
Q_PER_KV = 8
HEAD_DIM = 64
ROPE_HALF = 16
WINDOW = 128
Q_BLOCK = 128
ROPE_BASE = 10000.0
NEG_INF = -1e30
CHUNK = 128
N_GMLP_GROUPS = 8
N_EXPERTS = 64
TOP_K = 8
N_EXPERT_GROUPS = 8
EXPERTS_PER_GROUP = 8
TOPK_GROUPS = 4
ROUTED_SCALE = 2.5
LN_EPS = 1e-5

LANES = 128
VMEM_LIMIT_BYTES = 56 * 1024 * 1024


def _params(*sem):
    return pltpu.CompilerParams(dimension_semantics=sem, vmem_limit_bytes=VMEM_LIMIT_BYTES)


def _silu(a):
    return a * jax.nn.sigmoid(a)


def _layernorm(v, g, b):
    mu = jnp.mean(v, axis=-1, keepdims=True)
    d = v - mu
    var = jnp.mean(d * d, axis=-1, keepdims=True)
    return d * lax.rsqrt(var + LN_EPS) * g + b


def _bdot(a, b):
    return jnp.dot(a, b, preferred_element_type=F32)


U32 = jnp.uint32


def _pack_pairs(a):
    n = a.shape[1] // 2
    lo = lax.bitcast_convert_type(a[:, :n].astype(BF16).astype(F32), U32)
    hi = lax.bitcast_convert_type(a[:, n:].astype(BF16).astype(F32), U32)
    return hi | (lo >> 16)


def _unpack_pairs(p):
    lo = lax.bitcast_convert_type(p << 16, F32)
    hi = lax.bitcast_convert_type(p & jnp.uint32(0xFFFF0000), F32)
    return lo, hi


def _store_row_tiles(ref, packed):
    r, n = packed.shape
    sub = n // LANES
    for j in range(sub):
        ref[pl.ds(j, r, stride=sub), :] = packed[:, LANES * j:LANES * (j + 1)]


def _load_row_tiles(ref, r):
    sub = ref.shape[0] // r
    pieces = [_unpack_pairs(ref[pl.ds(j, r, stride=sub), :]) for j in range(sub)]
    return [p[0] for p in pieces], [p[1] for p in pieces]


def _load_rows_bf16(ref, r):
    lo, hi = _load_row_tiles(ref, r)
    return jnp.concatenate(lo + hi, axis=1).astype(BF16)


def _mod_kernel(c_ref, w_ref, b_ref, o_ref):
    a = _silu(c_ref[...]).astype(BF16)
    o_ref[0] = _bdot(a, w_ref[0].astype(BF16)) + b_ref[0]


def _modulations(cc, w_ada, b_ada):
    depth, d, n6 = w_ada.shape
    r = cc.shape[0]
    tn = max(w for w in range(LANES, min(1024, n6) + 1, LANES) if n6 % w == 0)
    return pl.pallas_call(
        _mod_kernel,
        out_shape=jax.ShapeDtypeStruct((depth, r, n6), F32),
        grid=(depth, n6 // tn),
        in_specs=[pl.BlockSpec((r, d), lambda l, j: (0, 0)),
                  pl.BlockSpec((1, d, tn), lambda l, j: (l, 0, j)),
                  pl.BlockSpec((1, 1, tn), lambda l, j: (l, 0, j))],
        out_specs=pl.BlockSpec((1, r, tn), lambda l, j: (l, 0, j)),
        compiler_params=_params("parallel", "parallel"),
        name="adaln_mod",
    )(cc, w_ada, b_ada.reshape(depth, 1, n6))


def _rope_tables(seq):
    rows = seq // GRID_W
    row_ids = jnp.repeat(jnp.arange(rows, dtype=F32), GRID_W)
    col_ids = jnp.tile(jnp.arange(GRID_W, dtype=F32), rows)
    inv_freq = ROPE_BASE ** (-jnp.arange(0, 2 * ROPE_HALF, 2, dtype=F32) / (2 * ROPE_HALF))
    ar, ac = row_ids[:, None] * inv_freq, col_ids[:, None] * inv_freq
    z = jnp.zeros_like(ar)
    cos = jnp.concatenate([jnp.cos(ar), jnp.cos(ar), jnp.cos(ac), jnp.cos(ac)], axis=-1)
    sin_lo = jnp.concatenate([-jnp.sin(ar), z, -jnp.sin(ac), z], axis=-1)
    sin_hi = jnp.concatenate([z, jnp.sin(ar), z, jnp.sin(ac)], axis=-1)
    rep = LANES // HEAD_DIM
    return tuple(jnp.tile(t, (1, rep)) for t in (cos, sin_lo, sin_hi))


def _qkv_kernel(x_ref, sc_ref, sh_ref, w_ref, cos_ref, slo_ref, shi_ref, q_ref, k_ref, v_ref, *, cw):
    h = (x_ref[...] * (1.0 + sc_ref[0]) + sh_ref[0]).astype(BF16)
    cos, slo, shi = cos_ref[...], slo_ref[...], shi_ref[...]

    def rope(a):
        return (a * cos + pltpu.roll(a, LANES - ROPE_HALF, 1) * slo
                + pltpu.roll(a, ROPE_HALF, 1) * shi)

    qd, kd = q_ref.shape[1], k_ref.shape[1]
    for c0 in range(0, qd, cw):
        acc = _bdot(h, w_ref[:, c0:c0 + cw])
        for b0 in range(0, cw, LANES):
            q_ref[:, c0 + b0:c0 + b0 + LANES] = rope(acc[:, b0:b0 + LANES]).astype(BF16)
    acc = _bdot(h, w_ref[:, qd:qd + kd])
    for b0 in range(0, kd, LANES):
        k_ref[:, b0:b0 + LANES] = rope(acc[:, b0:b0 + LANES]).astype(BF16)
    v_ref[...] = _bdot(h, w_ref[:, qd + kd:qd + 2 * kd]).astype(BF16)


def _qkv_proj(x2, sc, sh, w, tables, seq):
    t, d = x2.shape
    kd = N_KV_HEADS * LANES
    qd = w.shape[1] - 2 * kd
    tm = min(512, seq)
    per_b = seq // tm
    row = lambda i: (i, 0)
    mod = lambda i: (i // per_b, 0, 0)
    tab = lambda i: (i % per_b, 0)
    return pl.pallas_call(
        functools.partial(_qkv_kernel, cw=min(512, qd)),
        out_shape=(jax.ShapeDtypeStruct((t, qd), BF16), jax.ShapeDtypeStruct((t, kd), BF16),
                   jax.ShapeDtypeStruct((t, kd), BF16)),
        grid=(t // tm,),
        in_specs=[pl.BlockSpec((tm, d), row), pl.BlockSpec((1, 1, d), mod), pl.BlockSpec((1, 1, d), mod),
                  pl.BlockSpec(w.shape, lambda i: (0, 0)),
                  pl.BlockSpec((tm, LANES), tab), pl.BlockSpec((tm, LANES), tab), pl.BlockSpec((tm, LANES), tab)],
        out_specs=(pl.BlockSpec((tm, qd), row), pl.BlockSpec((tm, kd), row), pl.BlockSpec((tm, kd), row)),
        compiler_params=_params("parallel"),
        name="qkv_rope",
    )(x2, sc, sh, w, *tables)


def _ctxkv_kernel(x_ref, sc_ref, sh_ref, w_ref, k_ref, v_ref):
    h = (x_ref[...] * (1.0 + sc_ref[...]) + sh_ref[...]).astype(BF16)
    kd = k_ref.shape[1]
    k_ref[...] = _bdot(h, w_ref[:, :kd]).astype(BF16)
    v_ref[...] = _bdot(h, w_ref[:, kd:]).astype(BF16)


def _ctx_kv(c2, sc, sh, w):
    t, d = c2.shape
    kd = w.shape[1] // 2
    tm = min(512, t)
    row = lambda i: (i, 0)
    fix = lambda i: (0, 0)
    return pl.pallas_call(
        _ctxkv_kernel,
        out_shape=(jax.ShapeDtypeStruct((t, kd), BF16), jax.ShapeDtypeStruct((t, kd), BF16)),
        grid=(t // tm,),
        in_specs=[pl.BlockSpec((tm, d), row), pl.BlockSpec((1, d), fix), pl.BlockSpec((1, d), fix),
                  pl.BlockSpec(w.shape, fix)],
        out_specs=(pl.BlockSpec((tm, kd), row), pl.BlockSpec((tm, kd), row)),
        compiler_params=_params("parallel"),
        name="ctx_kv",
    )(c2, sc, sh, w)


def _attn_kernel(sink_ref, q_ref, k_ref, v_ref, kc_ref, vc_ref, o_ref):
    for b in range(q_ref.shape[1] // Q_BLOCK):
        _attn_block(sink_ref, q_ref, k_ref, v_ref, kc_ref, vc_ref, o_ref, b)


def _attn_block(sink_ref, q_ref, k_ref, v_ref, kc_ref, vc_ref, o_ref, b):
    kv = pl.program_id(1)
    i = pl.program_id(2) * (q_ref.shape[1] // Q_BLOCK) + b
    seq, nctx = k_ref.shape[1], kc_ref.shape[1]
    nwin = Q_BLOCK + 2 * WINDOW
    start = pl.multiple_of(jnp.clip(i * Q_BLOCK - WINDOW, 0, seq - nwin), Q_BLOCK)
    nkey = nctx + nwin
    kall = jnp.concatenate([kc_ref[0], k_ref[0, pl.ds(start, nwin), :]], axis=0)
    vall = jnp.concatenate([vc_ref[0], v_ref[0, pl.ds(start, nwin), :]], axis=0)
    low = lax.broadcasted_iota(I32, (nkey, LANES), 1) < HEAD_DIM
    zero = jnp.zeros((nkey, LANES), BF16)
    npair = Q_PER_KV // 2
    q = q_ref[0, Q_BLOCK * b:Q_BLOCK * (b + 1), :]
    qs = jnp.concatenate([q[:, LANES * j:LANES * (j + 1)] for j in range(npair)], axis=0)
    nrow = npair * Q_BLOCK
    rows = lax.broadcasted_iota(I32, (Q_BLOCK, nkey), 0)
    cols = lax.broadcasted_iota(I32, (Q_BLOCK, nkey), 1)
    dist = (i * Q_BLOCK - start + nctx) + rows - cols
    valid = (cols < nctx) | (jnp.abs(dist) <= WINDOW)
    contract_last = (((1,), (1,)), ((), ()))
    out = jnp.zeros((nrow, LANES), F32)
    for par in range(2):
        keep = low if par == 0 else jnp.logical_not(low)
        kh = jnp.where(keep, kall, zero)
        vh = jnp.where(keep, vall, zero)
        s = lax.dot_general(qs, kh, contract_last, preferred_element_type=F32)
        s = jnp.concatenate(
            [jnp.where(valid, s[Q_BLOCK * j:Q_BLOCK * (j + 1), :], NEG_INF) for j in range(npair)], axis=0)
        sink = jnp.concatenate(
            [jnp.full((Q_BLOCK, 1), sink_ref[kv * Q_PER_KV + 2 * j + par], F32) for j in range(npair)], axis=0)
        m = jnp.maximum(jnp.max(s, axis=1, keepdims=True), sink)
        p = jnp.exp(s - m)
        den = jnp.sum(p, axis=1, keepdims=True) + jnp.exp(sink - m)
        out = out + _bdot(p.astype(BF16), vh) / den
    for j in range(npair):
        o_ref[0, Q_BLOCK * b:Q_BLOCK * (b + 1), LANES * j:LANES * (j + 1)] = (
            out[Q_BLOCK * j:Q_BLOCK * (j + 1), :].astype(BF16))


def _attention(q, k2, v2, kc2, vc2, sink, batch, seq, nctx):
    qd = q.shape[1]
    gw = Q_PER_KV * HEAD_DIM
    q3 = q.reshape(batch, seq, qd)
    k3, v3 = k2.reshape(batch, seq, -1), v2.reshape(batch, seq, -1)
    kc3, vc3 = kc2.reshape(batch, nctx, -1), vc2.reshape(batch, nctx, -1)
    full = lambda b, h, i: (b, 0, h)
    blk = lambda b, h, i: (b, i, h)
    rows = next(r * Q_BLOCK for r in (4, 2, 1) if seq % (r * Q_BLOCK) == 0)
    out = pl.pallas_call(
        _attn_kernel,
        out_shape=jax.ShapeDtypeStruct((batch, seq, qd), BF16),
        grid=(batch, N_KV_HEADS, seq // rows),
        in_specs=[pl.BlockSpec(memory_space=pltpu.SMEM),
                  pl.BlockSpec((1, rows, gw), blk),
                  pl.BlockSpec((1, seq, LANES), full), pl.BlockSpec((1, seq, LANES), full),
                  pl.BlockSpec((1, nctx, LANES), full), pl.BlockSpec((1, nctx, LANES), full)],
        out_specs=pl.BlockSpec((1, rows, gw), blk),
        compiler_params=_params("parallel", "parallel", "parallel"),
        name="window_gqa",
    )(sink, q3, k3, v3, kc3, vc3)
    return out.reshape(batch * seq, qd)


def _split_bf16(a):
    hi = a.astype(BF16)
    return hi, (a - hi.astype(F32)).astype(BF16)


def _post_mixer(x, y, gm, lng, lnb, scf, shf, wr_ref, rb_ref, run_ref,
                x1_ref, f_ref, ids_ref, wts_ref, rank_ref, cnt_ref, alpha):
    tm = x.shape[0]
    x1 = _layernorm(alpha * x + gm * y, lng, lnb)
    x1_ref[...] = x1
    f = x1 * (1.0 + scf) + shf
    _store_row_tiles(f_ref, _pack_pairs(f))

    fh, fl = _split_bf16(f)
    prod = _bdot(fh, wr_ref[...]) + _bdot(fl, wr_ref[...])
    logits = (prod[:, :LANES] + prod[:, LANES:]).T[:N_EXPERTS, :]
    scores = jax.nn.sigmoid(logits)
    biased = scores + rb_ref[...]

    eg = EXPERTS_PER_GROUP
    sub = lax.broadcasted_iota(I32, (eg, tm), 0)
    gscore = []
    for g in range(N_EXPERT_GROUPS):
        tg = biased[eg * g:eg * (g + 1), :]
        m1 = jnp.max(tg, axis=0, keepdims=True)
        i1 = jnp.min(jnp.where(tg == m1, sub, eg), axis=0, keepdims=True)
        m2 = jnp.max(jnp.where(sub == i1, -jnp.inf, tg), axis=0, keepdims=True)
        gscore.append(m1 + m2)
    gsel = [jnp.zeros((1, tm), jnp.bool_) for _ in range(N_EXPERT_GROUPS)]
    for _ in range(TOPK_GROUPS):
        best = functools.reduce(jnp.maximum, gscore)
        taken = jnp.zeros((1, tm), jnp.bool_)
        for g in range(N_EXPERT_GROUPS):
            hit = jnp.logical_and(gscore[g] == best, jnp.logical_not(taken))
            taken = jnp.logical_or(taken, hit)
            gsel[g] = jnp.logical_or(gsel[g], hit)
            gscore[g] = jnp.where(hit, -jnp.inf, gscore[g])
    cur = jnp.concatenate(
        [jnp.where(gsel[g], biased[eg * g:eg * (g + 1), :], -jnp.inf) for g in range(N_EXPERT_GROUPS)], axis=0)

    eidx = lax.broadcasted_iota(I32, (N_EXPERTS, tm), 0)
    picks, wts, hots = [], [], []
    for _ in range(TOP_K):
        m = jnp.max(cur, axis=0, keepdims=True)
        idx = jnp.min(jnp.where(cur == m, eidx, N_EXPERTS), axis=0, keepdims=True)
        hot = eidx == idx
        picks.append(idx)
        wts.append(jnp.sum(jnp.where(hot, scores, 0.0), axis=0, keepdims=True))
        hots.append(hot)
        cur = jnp.where(hot, -jnp.inf, cur)
    wsum = functools.reduce(jnp.add, wts)

    assigned = functools.reduce(jnp.add, [h.astype(F32) for h in hots])
    before = (lax.broadcasted_iota(I32, (tm, tm), 0) < lax.broadcasted_iota(I32, (tm, tm), 1)).astype(BF16)
    pos = _bdot(assigned.astype(BF16), before) + run_ref[:, 0:1]
    for k in range(TOP_K):
        ids_ref[k:k + 1, :] = picks[k]
        wts_ref[k:k + 1, :] = wts[k] / wsum * ROUTED_SCALE
        rank_ref[k:k + 1, :] = jnp.sum(jnp.where(hots[k], pos, 0.0), axis=0, keepdims=True).astype(I32)
    run_ref[...] = run_ref[...] + jnp.sum(assigned, axis=1, keepdims=True)
    cnt_ref[...] = run_ref[...]


def _route_out_shapes(t, d):
    return (jax.ShapeDtypeStruct((t, d), F32), jax.ShapeDtypeStruct((t * d // (2 * LANES), LANES), U32),
            jax.ShapeDtypeStruct((TOP_K, t), I32), jax.ShapeDtypeStruct((TOP_K, t), F32),
            jax.ShapeDtypeStruct((TOP_K, t), I32), jax.ShapeDtypeStruct((N_EXPERTS, LANES), F32))


def _route_out_specs(tm, d):
    row = lambda i: (i, 0)
    col = lambda i: (0, i)
    return (pl.BlockSpec((tm, d), row), pl.BlockSpec((tm * d // (2 * LANES), LANES), row),
            pl.BlockSpec((TOP_K, tm), col), pl.BlockSpec((TOP_K, tm), col), pl.BlockSpec((TOP_K, tm), col),
            pl.BlockSpec((N_EXPERTS, LANES), lambda i: (0, 0)))


def _route_in_specs(tm, d, per_b, layer):
    mod = lambda i: (i // per_b, 0, 0)
    lrow = lambda i: (layer, 0, 0)
    fix = lambda i: (0, 0)
    return [pl.BlockSpec((tm, d), lambda i: (i, 0)),
            pl.BlockSpec((1, 1, d), mod), pl.BlockSpec((1, 1, d), mod), pl.BlockSpec((1, 1, d), mod),
            pl.BlockSpec((1, 1, d), lrow), pl.BlockSpec((1, 1, d), lrow),
            pl.BlockSpec((d, 2 * LANES), fix),
            pl.BlockSpec((N_EXPERTS, 1), fix)]


def _oproj_kernel(o_ref, wo_ref, x_ref, gm_ref, scf_ref, shf_ref, lng_ref, lnb_ref, wr_ref, rb_ref,
                  x1_ref, f_ref, ids_ref, wts_ref, rank_ref, cnt_ref, run_ref, *, alpha):
    @pl.when(pl.program_id(0) == 0)
    def _():
        run_ref[...] = jnp.zeros_like(run_ref)

    y = _bdot(o_ref[...], wo_ref[...])
    _post_mixer(x_ref[...], y, gm_ref[0], lng_ref[0], lnb_ref[0], scf_ref[0], shf_ref[0],
                wr_ref, rb_ref, run_ref, x1_ref, f_ref, ids_ref, wts_ref, rank_ref, cnt_ref, alpha)


def _attn_out(o, wo, x2, gm, scf, shf, lng, lnb, wr, rb, seq, layer, alpha):
    t, d = x2.shape
    tm = min(512, seq)
    return pl.pallas_call(
        functools.partial(_oproj_kernel, alpha=alpha),
        out_shape=_route_out_shapes(t, d),
        grid=(t // tm,),
        in_specs=[pl.BlockSpec((tm, o.shape[1]), lambda i: (i, 0)), pl.BlockSpec(wo.shape, lambda i: (0, 0))]
        + _route_in_specs(tm, d, seq // tm, layer),
        out_specs=_route_out_specs(tm, d),
        scratch_shapes=[pltpu.VMEM((N_EXPERTS, LANES), F32)],
        compiler_params=_params("arbitrary"),
        name="attn_out_route",
    )(o, wo, x2, gm, scf, shf, lng, lnb, wr, rb)


def _gmlp_in_kernel(x_ref, sc_ref, sh_ref, w_ref, b_ref, g_ref, beta_ref, o_ref, *, cw, normalize):
    h = (x_ref[...] * (1.0 + sc_ref[0]) + sh_ref[0]).astype(BF16)
    n = o_ref.shape[1]
    inv_sqrt2 = 0.7071067811865476
    parts = []
    for c0 in range(0, n, cw):
        z = _bdot(h, w_ref[:, c0:c0 + cw]) + b_ref[:, c0:c0 + cw]
        z = 0.5 * z * (1.0 + lax.erf(z * inv_sqrt2))
        if normalize:
            parts.append(z)
        else:
            o_ref[:, c0:c0 + cw] = z.astype(BF16)
    if normalize:
        v = jnp.concatenate(parts, axis=1)
        o_ref[...] = _layernorm(v, g_ref[...], beta_ref[...]).astype(BF16)


def _gmlp_in(x2, sc, sh, w, b, g, beta, seq, normalize):
    t, d = x2.shape
    n = w.shape[1]
    tm = min(256, seq)
    per_b = seq // tm
    row = lambda i: (i, 0)
    mod = lambda i: (i // per_b, 0, 0)
    fix = lambda i: (0, 0)
    return pl.pallas_call(
        functools.partial(_gmlp_in_kernel, cw=min(512, n), normalize=normalize),
        out_shape=jax.ShapeDtypeStruct((t, n), BF16),
        grid=(t // tm,),
        in_specs=[pl.BlockSpec((tm, d), row), pl.BlockSpec((1, 1, d), mod), pl.BlockSpec((1, 1, d), mod),
                  pl.BlockSpec(w.shape, fix), pl.BlockSpec((1, n), fix), pl.BlockSpec((1, n), fix),
                  pl.BlockSpec((1, n), fix)],
        out_specs=pl.BlockSpec((tm, n), row),
        compiler_params=_params("parallel"),
        name="gmlp_in_v" if normalize else "gmlp_in_u",
    )(x2, sc, sh, w, b, g, beta)


def _gmlp_out_kernel(u_ref, v_ref, ws_ref, bs_ref, wo_ref, x_ref, gm_ref, scf_ref, shf_ref, lng_ref, lnb_ref,
                     wr_ref, rb_ref, x1_ref, f_ref, ids_ref, wts_ref, rank_ref, cnt_ref,
                     run_ref, gated_ref, *, alpha):
    @pl.when(pl.program_id(0) == 0)
    def _():
        run_ref[...] = jnp.zeros_like(run_ref)

    tm, width = u_ref.shape
    gd = width // N_GMLP_GROUPS
    for r0 in range(0, tm, CHUNK):
        for g in range(N_GMLP_GROUPS):
            c0 = g * gd
            mixed = _bdot(ws_ref[g], v_ref[r0:r0 + CHUNK, c0:c0 + gd]) + bs_ref[g]
            gated_ref[r0:r0 + CHUNK, c0:c0 + gd] = (
                u_ref[r0:r0 + CHUNK, c0:c0 + gd].astype(F32) * mixed).astype(BF16)
    y = _bdot(gated_ref[...], wo_ref[...])
    _post_mixer(x_ref[...], y, gm_ref[0], lng_ref[0], lnb_ref[0], scf_ref[0], shf_ref[0],
                wr_ref, rb_ref, run_ref, x1_ref, f_ref, ids_ref, wts_ref, rank_ref, cnt_ref, alpha)


def _gmlp_out(u, v, ws, bs, wo, x2, gm, scf, shf, lng, lnb, wr, rb, seq, layer, alpha):
    t, d = x2.shape
    width = u.shape[1]
    tm = min(256, seq)
    row = lambda i: (i, 0)
    return pl.pallas_call(
        functools.partial(_gmlp_out_kernel, alpha=alpha),
        out_shape=_route_out_shapes(t, d),
        grid=(t // tm,),
        in_specs=[pl.BlockSpec((tm, width), row), pl.BlockSpec((tm, width), row),
                  pl.BlockSpec(ws.shape, lambda i: (0, 0, 0)), pl.BlockSpec(bs.shape, lambda i: (0, 0, 0)),
                  pl.BlockSpec(wo.shape, lambda i: (0, 0))]
        + _route_in_specs(tm, d, seq // tm, layer),
        out_specs=_route_out_specs(tm, d),
        scratch_shapes=[pltpu.VMEM((N_EXPERTS, LANES), F32), pltpu.VMEM((tm, width), BF16)],
        compiler_params=_params("arbitrary"),
        name="gmlp_out_route",
    )(u, v, ws, bs, wo, x2, gm, scf, shf, lng, lnb, wr, rb)


SC_CORES, SC_SUBCORES, SC_LANES = 2, 16, 16
MXU_N = 256


def _shared_kernel(f_ref, ws1_ref, ws3_ref, ws2_ref, sh_ref):
    h = _load_rows_bf16(f_ref, sh_ref.shape[0])
    a = _silu(_bdot(h, ws1_ref[...])) * _bdot(h, ws3_ref[...])
    sh_ref[...] = _bdot(a.astype(BF16), ws2_ref[...]).astype(BF16)


def _shared_expert(f, ws1, ws3, ws2, t):
    sub = f.shape[0] // t
    d = ws1.shape[0]
    tm = min(512, t)
    fix = lambda i: (0, 0)
    return pl.pallas_call(
        _shared_kernel,
        out_shape=jax.ShapeDtypeStruct((t, d), BF16),
        grid=(t // tm,),
        in_specs=[pl.BlockSpec((tm * sub, LANES), lambda i: (i, 0)),
                  pl.BlockSpec(ws1.shape, fix), pl.BlockSpec(ws3.shape, fix), pl.BlockSpec(ws2.shape, fix)],
        out_specs=pl.BlockSpec((tm, d), lambda i: (i, 0)),
        compiler_params=_params("parallel"),
        name="moe_shared",
    )(f, ws1, ws3, ws2)


def _inverse_map(slots_flat, nslot):
    n = slots_flat.shape[0]
    workers = SC_CORES * SC_SUBCORES
    per_w = nslot // workers
    chunk = min(8192, n)
    assert nslot % (workers * SC_LANES) == 0 and n % chunk == 0 and chunk % SC_LANES == 0
    mesh = plsc.VectorSubcoreMesh(core_axis_name="c", subcore_axis_name="s")

    @functools.partial(
        pl.kernel, out_type=jax.ShapeDtypeStruct((nslot,), I32), mesh=mesh,
        scratch_types=[pltpu.VMEM((per_w,), I32), pltpu.VMEM((chunk,), I32)],
        compiler_params=pltpu.CompilerParams(needs_layout_passes=False))
    def inverse(slots_hbm, inv_hbm, local, buf):
        lo = (lax.axis_index("s") * SC_CORES + lax.axis_index("c")) * per_w
        unused = jnp.full((SC_LANES,), -1, I32)

        @pl.loop(0, per_w, step=SC_LANES)
        def _(j):
            local[pl.ds(j, SC_LANES)] = unused

        lane = lax.iota(I32, SC_LANES)

        @pl.loop(0, n, step=chunk)
        def _(c0):
            pltpu.sync_copy(slots_hbm.at[pl.ds(c0, chunk)], buf)

            @pl.loop(0, chunk, step=SC_LANES)
            def _(j):
                idx = buf[pl.ds(j, SC_LANES)] - lo
                mine = jnp.logical_and(idx >= 0, idx < per_w)
                plsc.store_scatter(local, [idx], c0 + j + lane, mask=mine)

        pltpu.sync_copy(local, inv_hbm.at[pl.ds(lo, per_w)])

    return inverse(slots_flat)


def _experts_kernel(texp_ref, tok_ref, w1_ref, w3_ref, w2_ref, f_hbm, y_ref, w1b, w3b, w2b, xa, xb, gsem, *, tme):
    i = pl.program_id(0)
    last = pl.num_programs(0) - 1
    sub = xa.shape[0] // tme
    d, fdim = w1b.shape
    half = d // 2

    @pl.when(i == 0)
    def _():
        xb[...] = jnp.zeros_like(xb)

    @pl.when(jnp.logical_or(i == 0, texp_ref[i] != texp_ref[jnp.maximum(i - 1, 0)]))
    def _():
        w1b[...] = w1_ref[...].astype(BF16)
        w3b[...] = w3_ref[...].astype(BF16)
        w2b[...] = w2_ref[...].astype(BF16)

    def whole_tile(buf, sem):
        return pltpu.make_async_copy(f_hbm.at[pl.ds(0, tme * sub), :], buf, sem)

    def step(par):
        xg, xc = (xa, xb) if par == 0 else (xb, xa)

        @pl.when(i > 0)
        def _():
            whole_tile(xc, gsem.at[1 - par]).wait()

        pw1, pw2 = min(MXU_N, fdim), min(MXU_N, half)
        pieces = 2 * (fdim // pw1) + 2 * (half // pw2)
        per_piece = -(-tme // max(pieces - 2, 1))
        issued = [0]

        def issue_rows():
            for r in range(issued[0], min(issued[0] + per_piece, tme)):
                tok = pl.multiple_of(tok_ref[r] * sub, sub)
                pltpu.make_async_copy(f_hbm.at[pl.ds(tok, sub), :], xg.at[pl.ds(r * sub, sub), :],
                                      gsem.at[par]).start(priority=r % 2)
            issued[0] = min(issued[0] + per_piece, tme)

        h = _load_rows_bf16(xc, tme)
        gate, up = [], []
        for c0 in range(0, fdim, pw1):
            gate.append(_bdot(h, w1b[:, c0:c0 + pw1]))
            issue_rows()
        for c0 in range(0, fdim, pw1):
            up.append(_bdot(h, w3b[:, c0:c0 + pw1]))
            issue_rows()
        a = (_silu(jnp.concatenate(gate, axis=1)) * jnp.concatenate(up, axis=1)).astype(BF16)
        for c0 in range(0, half, pw2):
            lo = _bdot(a, w2b[:, c0:c0 + pw2])
            issue_rows()
            hi = _bdot(a, w2b[:, half + c0:half + c0 + pw2])
            issue_rows()
            packed = _pack_pairs(jnp.concatenate([lo, hi], axis=1))
            for b0 in range(0, pw2, LANES):
                y_ref[pl.ds((c0 + b0) // LANES, tme, stride=sub), :] = packed[:, b0:b0 + LANES]
        assert issued[0] == tme

        @pl.when(i == last)
        def _():
            whole_tile(xg, gsem.at[par]).wait()

    for par in range(2):
        pl.when(jnp.bitwise_and(i, 1) == par)(functools.partial(step, par))


def _experts(texp, tok, w1, w3, w2, f, layer, tme):
    d, fdim = w1.shape[-2:]
    sub = d // (2 * LANES)
    ntile = texp.shape[0] - 1
    wsel = lambda i, te: (layer, te[i], 0, 0)
    return pl.pallas_call(
        functools.partial(_experts_kernel, tme=tme),
        out_shape=jax.ShapeDtypeStruct((ntile * tme * sub, LANES), U32),
        grid_spec=pltpu.PrefetchScalarGridSpec(
            num_scalar_prefetch=1,
            grid=(ntile + 1,),
            in_specs=[pl.BlockSpec((tme,), lambda i, te: (i,), memory_space=pltpu.SMEM),
                      pl.BlockSpec((None, None, d, fdim), wsel), pl.BlockSpec((None, None, d, fdim), wsel),
                      pl.BlockSpec((None, None, fdim, d), wsel),
                      pl.BlockSpec(memory_space=pl.ANY)],
            out_specs=pl.BlockSpec((tme * sub, LANES), lambda i, te: (jnp.maximum(i - 1, 0), 0)),
            scratch_shapes=[pltpu.VMEM((d, fdim), BF16), pltpu.VMEM((d, fdim), BF16), pltpu.VMEM((fdim, d), BF16),
                            pltpu.VMEM((tme * sub, LANES), U32), pltpu.VMEM((tme * sub, LANES), U32),
                            pltpu.SemaphoreType.DMA((2,))]),
        compiler_params=_params("arbitrary"),
        name="moe_experts",
    )(texp, tok, w1, w3, w2, f)


def _combine_kernel(slot_ref, wt_ref, x1_ref, sh_ref, gf_ref, lng_ref, lnb_ref, ys_hbm, o_ref, ga, gb, gsem, *, alpha):
    i = pl.program_id(0)
    last = pl.num_programs(0) - 1
    tm = x1_ref.shape[0]
    sub = ga.shape[1] // tm

    @pl.when(i == 0)
    def _():
        gb[...] = jnp.zeros_like(gb)

    def whole_plane(buf, k, sem):
        return pltpu.make_async_copy(ys_hbm.at[pl.ds(0, tm * sub), :], buf.at[k], sem)

    def step(par):
        gg, gc = (ga, gb) if par == 0 else (gb, ga)

        @pl.when(i > 0)
        def _():
            for k in range(TOP_K):
                whole_plane(gc, k, gsem.at[1 - par]).wait()

        def issue(t0, t1):
            for tt in range(t0, t1):
                for k in range(TOP_K):
                    row = pl.multiple_of(slot_ref[tt * TOP_K + k] * sub, sub)
                    pltpu.make_async_copy(ys_hbm.at[pl.ds(row, sub), :], gg.at[k, pl.ds(tt * sub, sub), :],
                                          gsem.at[par]).start(priority=k % 2)

        lo = hi = None
        for k in range(TOP_K):
            issue(k * tm // TOP_K, (k + 1) * tm // TOP_K)
            w = wt_ref[:, k:k + 1]
            rl, rh = _load_row_tiles(gc.at[k], tm)
            lo = [w * b for b in rl] if lo is None else [a + w * b for a, b in zip(lo, rl)]
            hi = [w * b for b in rh] if hi is None else [a + w * b for a, b in zip(hi, rh)]
        f = jnp.concatenate(lo + hi, axis=1) + sh_ref[...].astype(F32)
        o_ref[...] = _layernorm(alpha * x1_ref[...] + gf_ref[0] * f, lng_ref[0], lnb_ref[0])

        @pl.when(i == last)
        def _():
            for k in range(TOP_K):
                whole_plane(gg, k, gsem.at[par]).wait()

    for par in range(2):
        pl.when(jnp.bitwise_and(i, 1) == par)(functools.partial(step, par))


def _combine(slot_tk, wt, x1, sh, gf, lng, lnb, ys, seq, tm, layer, alpha):
    t, d = x1.shape
    sub = d // (2 * LANES)
    per_b = seq // tm
    nstep = t // tm + 1
    prev = lambda i: jnp.maximum(i - 1, 0)
    row = lambda i: (prev(i), 0)
    lrow = lambda i: (layer, 0, 0)
    return pl.pallas_call(
        functools.partial(_combine_kernel, alpha=alpha),
        out_shape=jax.ShapeDtypeStruct((t, d), F32),
        grid=(nstep,),
        in_specs=[pl.BlockSpec((tm * TOP_K,), lambda i: (i,), memory_space=pltpu.SMEM),
                  pl.BlockSpec((tm, TOP_K), row), pl.BlockSpec((tm, d), row), pl.BlockSpec((tm, d), row),
                  pl.BlockSpec((1, 1, d), lambda i: (prev(i) // per_b, 0, 0)),
                  pl.BlockSpec((1, 1, d), lrow), pl.BlockSpec((1, 1, d), lrow),
                  pl.BlockSpec(memory_space=pl.ANY)],
        out_specs=pl.BlockSpec((tm, d), row),
        scratch_shapes=[pltpu.VMEM((TOP_K, tm * sub, LANES), U32), pltpu.VMEM((TOP_K, tm * sub, LANES), U32),
                        pltpu.SemaphoreType.DMA((2,))],
        compiler_params=_params("arbitrary"),
        name="moe_combine_ln",
    )(slot_tk, wt, x1, sh, gf, lng, lnb, ys)


def _moe(f, x1, ids, wts, rank, cnt, gf, lng, lnb, w1, w3, w2, ws1, ws3, ws2, seq, layer, alpha):
    t, d = x1.shape
    tme = min(512, t)
    counts = cnt[:, 0].astype(I32)
    tiles_e = (counts + tme - 1) // tme
    padded = tiles_e * tme
    base = jnp.cumsum(padded) - padded
    expert = jnp.arange(N_EXPERTS, dtype=I32)
    slots = jnp.sum(jnp.where(ids[:, :, None] == expert, base, 0), axis=-1) + rank
    ntile = (t * TOP_K) // tme + N_EXPERTS
    nslot = ntile * tme
    tile_ends = jnp.cumsum(tiles_e)
    texp = jnp.sum((jnp.arange(ntile, dtype=I32)[:, None] >= tile_ends[None, :]).astype(I32), axis=1)
    texp = jnp.minimum(texp, N_EXPERTS - 1)
    texp_step = jnp.concatenate([texp[:1], texp])

    inv = _inverse_map(slots.reshape(-1), nslot)
    tok = jnp.where(inv >= 0, inv % t, 0)
    tok = jnp.concatenate([tok, jnp.zeros((tme,), I32)])

    shared = _shared_expert(f, ws1, ws3, ws2, t)
    ys = _experts(texp_step, tok, w1, w3, w2, f, layer, tme)
    tmc = min(128, seq)
    slot_tk = jnp.concatenate([slots.T.reshape(-1), jnp.zeros((tmc * TOP_K,), I32)])
    return _combine(slot_tk, wts.T, x1, shared, gf, lng, lnb, ys, seq, tmc, layer, alpha)


def kernel(x, c, ctx, c_ctx, w_ada, b_ada, ln_mix_g, ln_mix_b, ln_ffn_g, ln_ffn_b, attn_w_qkv, attn_w_o, attn_sink, gmlp_w_in, gmlp_b_in, gmlp_v_g, gmlp_v_b, gmlp_w_s, gmlp_b_s, gmlp_w_o, moe_w_router, moe_bias, moe_w1, moe_w3, moe_w2, moe_ws1, moe_ws3, moe_ws2):
    batch, seq, d = x.shape
    nctx = ctx.shape[1]
    depth = w_ada.shape[0]
    assert depth == 2, "layer 0 is the attention mixer, layer 1 the gMLP mixer"
    alpha = float((2 * depth) ** 0.25)
    t = batch * seq

    pad = (-(batch + 1)) % 8
    cc = jnp.concatenate([c, c_ctx[None, :], jnp.zeros((pad, d), F32)], axis=0)
    mods = _modulations(cc, w_ada, b_ada)

    def mod(layer, j, rows=slice(0, batch)):
        return mods[layer, rows, j * d:(j + 1) * d]

    def per_batch(layer, j):
        return mod(layer, j).reshape(batch, 1, d)

    def router(layer):
        w = jnp.pad(moe_w_router[layer], ((0, 0), (0, LANES - N_EXPERTS)))
        hi = w.astype(BF16)
        lo = (w - hi.astype(F32)).astype(BF16)
        return jnp.concatenate([hi, lo], axis=1), moe_bias[layer].reshape(N_EXPERTS, 1)

    def shared_w(layer):
        return moe_ws1[layer].astype(BF16), moe_ws3[layer].astype(BF16), moe_ws2[layer].astype(BF16)

    x2 = x.reshape(t, d)
    ln_mix_g, ln_mix_b, ln_ffn_g, ln_ffn_b = (
        p.reshape(depth, 1, d) for p in (ln_mix_g, ln_mix_b, ln_ffn_g, ln_ffn_b))

    qd = Q_PER_KV * N_KV_HEADS * HEAD_DIM
    kvd = N_KV_HEADS * HEAD_DIM
    wqkv = attn_w_qkv[0]
    dup = lambda w: jnp.concatenate([w.reshape(d, N_KV_HEADS, 1, HEAD_DIM)] * (LANES // HEAD_DIM), axis=2
                                    ).reshape(d, N_KV_HEADS * LANES)
    wk2, wv2 = dup(wqkv[:, qd:qd + kvd]), dup(wqkv[:, qd + kvd:])
    w_all = jnp.concatenate([wqkv[:, :qd] * (HEAD_DIM ** -0.5), wk2, wv2], axis=1).astype(BF16)
    q, k2, v2 = _qkv_proj(x2, per_batch(0, 1), per_batch(0, 0), w_all, _rope_tables(seq), seq)
    ctx_row = slice(batch, batch + 1)
    kc2, vc2 = _ctx_kv(ctx.reshape(batch * nctx, d), mod(0, 1, ctx_row), mod(0, 0, ctx_row),
                       jnp.concatenate([wk2, wv2], axis=1).astype(BF16))
    o = _attention(q, k2, v2, kc2, vc2, attn_sink[0], batch, seq, nctx)
    x1, f, ids, wts, rank, cnt = _attn_out(
        o, attn_w_o[0].astype(BF16), x2, per_batch(0, 2), per_batch(0, 4), per_batch(0, 3),
        ln_mix_g, ln_mix_b, *router(0), seq, 0, alpha)
    x2 = _moe(f, x1, ids, wts, rank, cnt, per_batch(0, 5), ln_ffn_g, ln_ffn_b,
              moe_w1, moe_w3, moe_w2, *shared_w(0), seq, 0, alpha)

    width = gmlp_w_in.shape[2] // 2
    w_in = gmlp_w_in[0].astype(BF16)
    b_in = gmlp_b_in[0].reshape(1, 2 * width)
    vg, vb = gmlp_v_g[0].reshape(1, width), gmlp_v_b[0].reshape(1, width)
    sc, sh = per_batch(1, 1), per_batch(1, 0)
    u = _gmlp_in(x2, sc, sh, w_in[:, :width], b_in[:, :width], vg, vb, seq, False)
    v = _gmlp_in(x2, sc, sh, w_in[:, width:], b_in[:, width:], vg, vb, seq, True)
    x1, f, ids, wts, rank, cnt = _gmlp_out(
        u, v, gmlp_w_s[0].astype(BF16), gmlp_b_s[0][:, :, None], gmlp_w_o[0].astype(BF16), x2,
        per_batch(1, 2), per_batch(1, 4), per_batch(1, 3), ln_mix_g, ln_mix_b, *router(1), seq, 1, alpha)
    x2 = _moe(f, x1, ids, wts, rank, cnt, per_batch(1, 5), ln_ffn_g, ln_ffn_b,
              moe_w1, moe_w3, moe_w2, *shared_w(1), seq, 1, alpha)
    return x2.reshape(batch, seq, d)
```

```python
import functools

import jax
import jax.numpy as jnp
from jax import lax
from jax.experimental import pallas as pl
from jax.experimental.pallas import tpu as pltpu
from jax.experimental.pallas import tpu_sc as plsc

F32 = jnp.float32
BF16 = jnp.bfloat16
I32 = jnp.int32

GRID_W = 64
N_KV_HEADS = 4
Q_PER_KV = 8
HEAD_DIM = 64
ROPE_HALF = 16
WINDOW = 128
Q_BLOCK = 128
ROPE_BASE = 10000.0
NEG_INF = -1e30
CHUNK = 128
N_GMLP_GROUPS = 8
N_EXPERTS = 64
TOP_K = 8
N_EXPERT_GROUPS = 8
EXPERTS_PER_GROUP = 8
TOPK_GROUPS = 4
ROUTED_SCALE = 2.5
LN_EPS = 1e-5

LANES = 128
VMEM_LIMIT_BYTES = 56 * 1024 * 1024


def _params(*sem):
    return pltpu.CompilerParams(dimension_semantics=sem, vmem_limit_bytes=VMEM_LIMIT_BYTES)


def _silu(a):
    return a * jax.nn.sigmoid(a)


def _layernorm(v, g, b):
    mu = jnp.mean(v, axis=-1, keepdims=True)
    d = v - mu
    var = jnp.mean(d * d, axis=-1, keepdims=True)
    return d * lax.rsqrt(var + LN_EPS) * g + b


def _bdot(a, b):
    return jnp.dot(a, b, preferred_element_type=F32)


U32 = jnp.uint32


def _pack_pairs(a):
    n = a.shape[1] // 2
    lo = lax.bitcast_convert_type(a[:, :n].astype(BF16).astype(F32), U32)
    hi = lax.bitcast_convert_type(a[:, n:].astype(BF16).astype(F32), U32)
    return hi | (lo >> 16)


def _unpack_pairs(p):
    lo = lax.bitcast_convert_type(p << 16, F32)
    hi = lax.bitcast_convert_type(p & jnp.uint32(0xFFFF0000), F32)
    return lo, hi


def _store_row_tiles(ref, packed):
    r, n = packed.shape
    sub = n // LANES
    for j in range(sub):
        ref[pl.ds(j, r, stride=sub), :] = packed[:, LANES * j:LANES * (j + 1)]


def _load_row_tiles(ref, r):
    sub = ref.shape[0] // r
    pieces = [_unpack_pairs(ref[pl.ds(j, r, stride=sub), :]) for j in range(sub)]
    return [p[0] for p in pieces], [p[1] for p in pieces]


def _load_rows_bf16(ref, r):
    lo, hi = _load_row_tiles(ref, r)
    return jnp.concatenate(lo + hi, axis=1).astype(BF16)


def _mod_kernel(c_ref, w_ref, b_ref, o_ref):
    a = _silu(c_ref[...]).astype(BF16)
    o_ref[0] = _bdot(a, w_ref[0].astype(BF16)) + b_ref[0]


def _modulations(cc, w_ada, b_ada):
    depth, d, n6 = w_ada.shape
    r = cc.shape[0]
    tn = max(w for w in range(LANES, min(1024, n6) + 1, LANES) if n6 % w == 0)
    return pl.pallas_call(
        _mod_kernel,
        out_shape=jax.ShapeDtypeStruct((depth, r, n6), F32),
        grid=(depth, n6 // tn),
        in_specs=[pl.BlockSpec((r, d), lambda l, j: (0, 0)),
                  pl.BlockSpec((1, d, tn), lambda l, j: (l, 0, j)),
                  pl.BlockSpec((1, 1, tn), lambda l, j: (l, 0, j))],
        out_specs=pl.BlockSpec((1, r, tn), lambda l, j: (l, 0, j)),
        compiler_params=_params("parallel", "parallel"),
        name="adaln_mod",
    )(cc, w_ada, b_ada.reshape(depth, 1, n6))


def _rope_tables(seq):
    rows = seq // GRID_W
    row_ids = jnp.repeat(jnp.arange(rows, dtype=F32), GRID_W)
    col_ids = jnp.tile(jnp.arange(GRID_W, dtype=F32), rows)
    inv_freq = ROPE_BASE ** (-jnp.arange(0, 2 * ROPE_HALF, 2, dtype=F32) / (2 * ROPE_HALF))
    ar, ac = row_ids[:, None] * inv_freq, col_ids[:, None] * inv_freq
    z = jnp.zeros_like(ar)
    cos = jnp.concatenate([jnp.cos(ar), jnp.cos(ar), jnp.cos(ac), jnp.cos(ac)], axis=-1)
    sin_lo = jnp.concatenate([-jnp.sin(ar), z, -jnp.sin(ac), z], axis=-1)
    sin_hi = jnp.concatenate([z, jnp.sin(ar), z, jnp.sin(ac)], axis=-1)
    rep = LANES // HEAD_DIM
    return tuple(jnp.tile(t, (1, rep)) for t in (cos, sin_lo, sin_hi))


def _qkv_kernel(x_ref, sc_ref, sh_ref, w_ref, cos_ref, slo_ref, shi_ref, q_ref, k_ref, v_ref, *, cw):
    h = (x_ref[...] * (1.0 + sc_ref[0]) + sh_ref[0]).astype(BF16)
    cos, slo, shi = cos_ref[...], slo_ref[...], shi_ref[...]

    def rope(a):
        return (a * cos + pltpu.roll(a, LANES - ROPE_HALF, 1) * slo
                + pltpu.roll(a, ROPE_HALF, 1) * shi)

    qd, kd = q_ref.shape[1], k_ref.shape[1]
    for c0 in range(0, qd, cw):
        acc = _bdot(h, w_ref[:, c0:c0 + cw])
        for b0 in range(0, cw, LANES):
            q_ref[:, c0 + b0:c0 + b0 + LANES] = rope(acc[:, b0:b0 + LANES]).astype(BF16)
    acc = _bdot(h, w_ref[:, qd:qd + kd])
    for b0 in range(0, kd, LANES):
        k_ref[:, b0:b0 + LANES] = rope(acc[:, b0:b0 + LANES]).astype(BF16)
    v_ref[...] = _bdot(h, w_ref[:, qd + kd:qd + 2 * kd]).astype(BF16)


def _qkv_proj(x2, sc, sh, w, tables, seq):
    t, d = x2.shape
    kd = N_KV_HEADS * LANES
    qd = w.shape[1] - 2 * kd
    tm = min(512, seq)
    per_b = seq // tm
    row = lambda i: (i, 0)
    mod = lambda i: (i // per_b, 0, 0)
    tab = lambda i: (i % per_b, 0)
    return pl.pallas_call(
        functools.partial(_qkv_kernel, cw=min(512, qd)),
        out_shape=(jax.ShapeDtypeStruct((t, qd), BF16), jax.ShapeDtypeStruct((t, kd), BF16),
                   jax.ShapeDtypeStruct((t, kd), BF16)),
        grid=(t // tm,),
        in_specs=[pl.BlockSpec((tm, d), row), pl.BlockSpec((1, 1, d), mod), pl.BlockSpec((1, 1, d), mod),
                  pl.BlockSpec(w.shape, lambda i: (0, 0)),
                  pl.BlockSpec((tm, LANES), tab), pl.BlockSpec((tm, LANES), tab), pl.BlockSpec((tm, LANES), tab)],
        out_specs=(pl.BlockSpec((tm, qd), row), pl.BlockSpec((tm, kd), row), pl.BlockSpec((tm, kd), row)),
        compiler_params=_params("parallel"),
        name="qkv_rope",
    )(x2, sc, sh, w, *tables)


def _ctxkv_kernel(x_ref, sc_ref, sh_ref, w_ref, k_ref, v_ref):
    h = (x_ref[...] * (1.0 + sc_ref[...]) + sh_ref[...]).astype(BF16)
    kd = k_ref.shape[1]
    k_ref[...] = _bdot(h, w_ref[:, :kd]).astype(BF16)
    v_ref[...] = _bdot(h, w_ref[:, kd:]).astype(BF16)


def _ctx_kv(c2, sc, sh, w):
    t, d = c2.shape
    kd = w.shape[1] // 2
    tm = min(512, t)
    row = lambda i: (i, 0)
    fix = lambda i: (0, 0)
    return pl.pallas_call(
        _ctxkv_kernel,
        out_shape=(jax.ShapeDtypeStruct((t, kd), BF16), jax.ShapeDtypeStruct((t, kd), BF16)),
        grid=(t // tm,),
        in_specs=[pl.BlockSpec((tm, d), row), pl.BlockSpec((1, d), fix), pl.BlockSpec((1, d), fix),
                  pl.BlockSpec(w.shape, fix)],
        out_specs=(pl.BlockSpec((tm, kd), row), pl.BlockSpec((tm, kd), row)),
        compiler_params=_params("parallel"),
        name="ctx_kv",
    )(c2, sc, sh, w)


def _attn_kernel(sink_ref, q_ref, k_ref, v_ref, kc_ref, vc_ref, o_ref):
    for b in range(q_ref.shape[1] // Q_BLOCK):
        _attn_block(sink_ref, q_ref, k_ref, v_ref, kc_ref, vc_ref, o_ref, b)


def _attn_block(sink_ref, q_ref, k_ref, v_ref, kc_ref, vc_ref, o_ref, b):
    kv = pl.program_id(1)
    i = pl.program_id(2) * (q_ref.shape[1] // Q_BLOCK) + b
    seq, nctx = k_ref.shape[1], kc_ref.shape[1]
    nwin = Q_BLOCK + 2 * WINDOW
    start = pl.multiple_of(jnp.clip(i * Q_BLOCK - WINDOW, 0, seq - nwin), Q_BLOCK)
    nkey = nctx + nwin
    kall = jnp.concatenate([kc_ref[0], k_ref[0, pl.ds(start, nwin), :]], axis=0)
    vall = jnp.concatenate([vc_ref[0], v_ref[0, pl.ds(start, nwin), :]], axis=0)
    low = lax.broadcasted_iota(I32, (nkey, LANES), 1) < HEAD_DIM
    zero = jnp.zeros((nkey, LANES), BF16)
    npair = Q_PER_KV // 2
    q = q_ref[0, Q_BLOCK * b:Q_BLOCK * (b + 1), :]
    qs = jnp.concatenate([q[:, LANES * j:LANES * (j + 1)] for j in range(npair)], axis=0)
    nrow = npair * Q_BLOCK
    rows = lax.broadcasted_iota(I32, (Q_BLOCK, nkey), 0)
    cols = lax.broadcasted_iota(I32, (Q_BLOCK, nkey), 1)
    dist = (i * Q_BLOCK - start + nctx) + rows - cols
    valid = (cols < nctx) | (jnp.abs(dist) <= WINDOW)
    contract_last = (((1,), (1,)), ((), ()))
    out = jnp.zeros((nrow, LANES), F32)
    for par in range(2):
        keep = low if par == 0 else jnp.logical_not(low)
        kh = jnp.where(keep, kall, zero)
        vh = jnp.where(keep, vall, zero)
        s = lax.dot_general(qs, kh, contract_last, preferred_element_type=F32)
        s = jnp.concatenate(
            [jnp.where(valid, s[Q_BLOCK * j:Q_BLOCK * (j + 1), :], NEG_INF) for j in range(npair)], axis=0)
        sink = jnp.concatenate(
            [jnp.full((Q_BLOCK, 1), sink_ref[kv * Q_PER_KV + 2 * j + par], F32) for j in range(npair)], axis=0)
        m = jnp.maximum(jnp.max(s, axis=1, keepdims=True), sink)
        p = jnp.exp(s - m)
        den = jnp.sum(p, axis=1, keepdims=True) + jnp.exp(sink - m)
        out = out + _bdot(p.astype(BF16), vh) / den
    for j in range(npair):
        o_ref[0, Q_BLOCK * b:Q_BLOCK * (b + 1), LANES * j:LANES * (j + 1)] = (
            out[Q_BLOCK * j:Q_BLOCK * (j + 1), :].astype(BF16))


def _attention(q, k2, v2, kc2, vc2, sink, batch, seq, nctx):
    qd = q.shape[1]
    gw = Q_PER_KV * HEAD_DIM
    q3 = q.reshape(batch, seq, qd)
    k3, v3 = k2.reshape(batch, seq, -1), v2.reshape(batch, seq, -1)
    kc3, vc3 = kc2.reshape(batch, nctx, -1), vc2.reshape(batch, nctx, -1)
    full = lambda b, h, i: (b, 0, h)
    blk = lambda b, h, i: (b, i, h)
    rows = next(r * Q_BLOCK for r in (4, 2, 1) if seq % (r * Q_BLOCK) == 0)
    out = pl.pallas_call(
        _attn_kernel,
        out_shape=jax.ShapeDtypeStruct((batch, seq, qd), BF16),
        grid=(batch, N_KV_HEADS, seq // rows),
        in_specs=[pl.BlockSpec(memory_space=pltpu.SMEM),
                  pl.BlockSpec((1, rows, gw), blk),
                  pl.BlockSpec((1, seq, LANES), full), pl.BlockSpec((1, seq, LANES), full),
                  pl.BlockSpec((1, nctx, LANES), full), pl.BlockSpec((1, nctx, LANES), full)],
        out_specs=pl.BlockSpec((1, rows, gw), blk),
        compiler_params=_params("parallel", "parallel", "parallel"),
        name="window_gqa",
    )(sink, q3, k3, v3, kc3, vc3)
    return out.reshape(batch * seq, qd)


def _split_bf16(a):
    hi = a.astype(BF16)
    return hi, (a - hi.astype(F32)).astype(BF16)


def _post_mixer(x, y, gm, lng, lnb, scf, shf, wr_ref, rb_ref, run_ref,
                x1_ref, f_ref, ids_ref, wts_ref, rank_ref, cnt_ref, alpha):
    tm = x.shape[0]
    x1 = _layernorm(alpha * x + gm * y, lng, lnb)
    x1_ref[...] = x1
    f = x1 * (1.0 + scf) + shf
    _store_row_tiles(f_ref, _pack_pairs(f))

    fh, fl = _split_bf16(f)
    prod = _bdot(fh, wr_ref[...]) + _bdot(fl, wr_ref[...])
    logits = (prod[:, :LANES] + prod[:, LANES:]).T[:N_EXPERTS, :]
    scores = jax.nn.sigmoid(logits)
    biased = scores + rb_ref[...]

    eg = EXPERTS_PER_GROUP
    sub = lax.broadcasted_iota(I32, (eg, tm), 0)
    gscore = []
    for g in range(N_EXPERT_GROUPS):
        tg = biased[eg * g:eg * (g + 1), :]
        m1 = jnp.max(tg, axis=0, keepdims=True)
        i1 = jnp.min(jnp.where(tg == m1, sub, eg), axis=0, keepdims=True)
        m2 = jnp.max(jnp.where(sub == i1, -jnp.inf, tg), axis=0, keepdims=True)
        gscore.append(m1 + m2)
    gsel = [jnp.zeros((1, tm), jnp.bool_) for _ in range(N_EXPERT_GROUPS)]
    for _ in range(TOPK_GROUPS):
        best = functools.reduce(jnp.maximum, gscore)
        taken = jnp.zeros((1, tm), jnp.bool_)
        for g in range(N_EXPERT_GROUPS):
            hit = jnp.logical_and(gscore[g] == best, jnp.logical_not(taken))
            taken = jnp.logical_or(taken, hit)
            gsel[g] = jnp.logical_or(gsel[g], hit)
            gscore[g] = jnp.where(hit, -jnp.inf, gscore[g])
    cur = jnp.concatenate(
        [jnp.where(gsel[g], biased[eg * g:eg * (g + 1), :], -jnp.inf) for g in range(N_EXPERT_GROUPS)], axis=0)

    eidx = lax.broadcasted_iota(I32, (N_EXPERTS, tm), 0)
    picks, wts, hots = [], [], []
    for _ in range(TOP_K):
        m = jnp.max(cur, axis=0, keepdims=True)
        idx = jnp.min(jnp.where(cur == m, eidx, N_EXPERTS), axis=0, keepdims=True)
        hot = eidx == idx
        picks.append(idx)
        wts.append(jnp.sum(jnp.where(hot, scores, 0.0), axis=0, keepdims=True))
        hots.append(hot)
        cur = jnp.where(hot, -jnp.inf, cur)
    wsum = functools.reduce(jnp.add, wts)

    assigned = functools.reduce(jnp.add, [h.astype(F32) for h in hots])
    before = (lax.broadcasted_iota(I32, (tm, tm), 0) < lax.broadcasted_iota(I32, (tm, tm), 1)).astype(BF16)
    pos = _bdot(assigned.astype(BF16), before) + run_ref[:, 0:1]
    for k in range(TOP_K):
        ids_ref[k:k + 1, :] = picks[k]
        wts_ref[k:k + 1, :] = wts[k] / wsum * ROUTED_SCALE
        rank_ref[k:k + 1, :] = jnp.sum(jnp.where(hots[k], pos, 0.0), axis=0, keepdims=True).astype(I32)
    run_ref[...] = run_ref[...] + jnp.sum(assigned, axis=1, keepdims=True)
    cnt_ref[...] = run_ref[...]


def _route_out_shapes(t, d):
    return (jax.ShapeDtypeStruct((t, d), F32), jax.ShapeDtypeStruct((t * d // (2 * LANES), LANES), U32),
            jax.ShapeDtypeStruct((TOP_K, t), I32), jax.ShapeDtypeStruct((TOP_K, t), F32),
            jax.ShapeDtypeStruct((TOP_K, t), I32), jax.ShapeDtypeStruct((N_EXPERTS, LANES), F32))


def _route_out_specs(tm, d):
    row = lambda i: (i, 0)
    col = lambda i: (0, i)
    return (pl.BlockSpec((tm, d), row), pl.BlockSpec((tm * d // (2 * LANES), LANES), row),
            pl.BlockSpec((TOP_K, tm), col), pl.BlockSpec((TOP_K, tm), col), pl.BlockSpec((TOP_K, tm), col),
            pl.BlockSpec((N_EXPERTS, LANES), lambda i: (0, 0)))


def _route_in_specs(tm, d, per_b, layer):
    mod = lambda i: (i // per_b, 0, 0)
    lrow = lambda i: (layer, 0, 0)
    fix = lambda i: (0, 0)
    return [pl.BlockSpec((tm, d), lambda i: (i, 0)),
            pl.BlockSpec((1, 1, d), mod), pl.BlockSpec((1, 1, d), mod), pl.BlockSpec((1, 1, d), mod),
            pl.BlockSpec((1, 1, d), lrow), pl.BlockSpec((1, 1, d), lrow),
            pl.BlockSpec((d, 2 * LANES), fix),
            pl.BlockSpec((N_EXPERTS, 1), fix)]


def _oproj_kernel(o_ref, wo_ref, x_ref, gm_ref, scf_ref, shf_ref, lng_ref, lnb_ref, wr_ref, rb_ref,
                  x1_ref, f_ref, ids_ref, wts_ref, rank_ref, cnt_ref, run_ref, *, alpha):
    @pl.when(pl.program_id(0) == 0)
    def _():
        run_ref[...] = jnp.zeros_like(run_ref)

    y = _bdot(o_ref[...], wo_ref[...])
    _post_mixer(x_ref[...], y, gm_ref[0], lng_ref[0], lnb_ref[0], scf_ref[0], shf_ref[0],
                wr_ref, rb_ref, run_ref, x1_ref, f_ref, ids_ref, wts_ref, rank_ref, cnt_ref, alpha)


def _attn_out(o, wo, x2, gm, scf, shf, lng, lnb, wr, rb, seq, layer, alpha):
    t, d = x2.shape
    tm = min(512, seq)
    return pl.pallas_call(
        functools.partial(_oproj_kernel, alpha=alpha),
        out_shape=_route_out_shapes(t, d),
        grid=(t // tm,),
        in_specs=[pl.BlockSpec((tm, o.shape[1]), lambda i: (i, 0)), pl.BlockSpec(wo.shape, lambda i: (0, 0))]
        + _route_in_specs(tm, d, seq // tm, layer),
        out_specs=_route_out_specs(tm, d),
        scratch_shapes=[pltpu.VMEM((N_EXPERTS, LANES), F32)],
        compiler_params=_params("arbitrary"),
        name="attn_out_route",
    )(o, wo, x2, gm, scf, shf, lng, lnb, wr, rb)


def _gmlp_in_kernel(x_ref, sc_ref, sh_ref, w_ref, b_ref, g_ref, beta_ref, o_ref, *, cw, normalize):
    h = (x_ref[...] * (1.0 + sc_ref[0]) + sh_ref[0]).astype(BF16)
    n = o_ref.shape[1]
    inv_sqrt2 = 0.7071067811865476
    parts = []
    for c0 in range(0, n, cw):
        z = _bdot(h, w_ref[:, c0:c0 + cw]) + b_ref[:, c0:c0 + cw]
        z = 0.5 * z * (1.0 + lax.erf(z * inv_sqrt2))
        if normalize:
            parts.append(z)
        else:
            o_ref[:, c0:c0 + cw] = z.astype(BF16)
    if normalize:
        v = jnp.concatenate(parts, axis=1)
        o_ref[...] = _layernorm(v, g_ref[...], beta_ref[...]).astype(BF16)


def _gmlp_in(x2, sc, sh, w, b, g, beta, seq, normalize):
    t, d = x2.shape
    n = w.shape[1]
    tm = min(256, seq)
    per_b = seq // tm
    row = lambda i: (i, 0)
    mod = lambda i: (i // per_b, 0, 0)
    fix = lambda i: (0, 0)
    return pl.pallas_call(
        functools.partial(_gmlp_in_kernel, cw=min(512, n), normalize=normalize),
        out_shape=jax.ShapeDtypeStruct((t, n), BF16),
        grid=(t // tm,),
        in_specs=[pl.BlockSpec((tm, d), row), pl.BlockSpec((1, 1, d), mod), pl.BlockSpec((1, 1, d), mod),
                  pl.BlockSpec(w.shape, fix), pl.BlockSpec((1, n), fix), pl.BlockSpec((1, n), fix),
                  pl.BlockSpec((1, n), fix)],
        out_specs=pl.BlockSpec((tm, n), row),
        compiler_params=_params("parallel"),
        name="gmlp_in_v" if normalize else "gmlp_in_u",
    )(x2, sc, sh, w, b, g, beta)


def _gmlp_out_kernel(u_ref, v_ref, ws_ref, bs_ref, wo_ref, x_ref, gm_ref, scf_ref, shf_ref, lng_ref, lnb_ref,
                     wr_ref, rb_ref, x1_ref, f_ref, ids_ref, wts_ref, rank_ref, cnt_ref,
                     run_ref, gated_ref, *, alpha):
    @pl.when(pl.program_id(0) == 0)
    def _():
        run_ref[...] = jnp.zeros_like(run_ref)

    tm, width = u_ref.shape
    gd = width // N_GMLP_GROUPS
    for r0 in range(0, tm, CHUNK):
        for g in range(N_GMLP_GROUPS):
            c0 = g * gd
            mixed = _bdot(ws_ref[g], v_ref[r0:r0 + CHUNK, c0:c0 + gd]) + bs_ref[g]
            gated_ref[r0:r0 + CHUNK, c0:c0 + gd] = (
                u_ref[r0:r0 + CHUNK, c0:c0 + gd].astype(F32) * mixed).astype(BF16)
    y = _bdot(gated_ref[...], wo_ref[...])
    _post_mixer(x_ref[...], y, gm_ref[0], lng_ref[0], lnb_ref[0], scf_ref[0], shf_ref[0],
                wr_ref, rb_ref, run_ref, x1_ref, f_ref, ids_ref, wts_ref, rank_ref, cnt_ref, alpha)


def _gmlp_out(u, v, ws, bs, wo, x2, gm, scf, shf, lng, lnb, wr, rb, seq, layer, alpha):
    t, d = x2.shape
    width = u.shape[1]
    tm = min(256, seq)
    row = lambda i: (i, 0)
    return pl.pallas_call(
        functools.partial(_gmlp_out_kernel, alpha=alpha),
        out_shape=_route_out_shapes(t, d),
        grid=(t // tm,),
        in_specs=[pl.BlockSpec((tm, width), row), pl.BlockSpec((tm, width), row),
                  pl.BlockSpec(ws.shape, lambda i: (0, 0, 0)), pl.BlockSpec(bs.shape, lambda i: (0, 0, 0)),
                  pl.BlockSpec(wo.shape, lambda i: (0, 0))]
        + _route_in_specs(tm, d, seq // tm, layer),
        out_specs=_route_out_specs(tm, d),
        scratch_shapes=[pltpu.VMEM((N_EXPERTS, LANES), F32), pltpu.VMEM((tm, width), BF16)],
        compiler_params=_params("arbitrary"),
        name="gmlp_out_route",
    )(u, v, ws, bs, wo, x2, gm, scf, shf, lng, lnb, wr, rb)


SC_CORES, SC_SUBCORES, SC_LANES = 2, 16, 16
MXU_N = 256


def _shared_kernel(f_ref, ws1_ref, ws3_ref, ws2_ref, sh_ref):
    h = _load_rows_bf16(f_ref, sh_ref.shape[0])
    a = _silu(_bdot(h, ws1_ref[...])) * _bdot(h, ws3_ref[...])
    sh_ref[...] = _bdot(a.astype(BF16), ws2_ref[...]).astype(BF16)


def _shared_expert(f, ws1, ws3, ws2, t):
    sub = f.shape[0] // t
    d = ws1.shape[0]
    tm = min(512, t)
    fix = lambda i: (0, 0)
    return pl.pallas_call(
        _shared_kernel,
        out_shape=jax.ShapeDtypeStruct((t, d), BF16),
        grid=(t // tm,),
        in_specs=[pl.BlockSpec((tm * sub, LANES), lambda i: (i, 0)),
                  pl.BlockSpec(ws1.shape, fix), pl.BlockSpec(ws3.shape, fix), pl.BlockSpec(ws2.shape, fix)],
        out_specs=pl.BlockSpec((tm, d), lambda i: (i, 0)),
        compiler_params=_params("parallel"),
        name="moe_shared",
    )(f, ws1, ws3, ws2)


def _inverse_map(slots_flat, nslot):
    n = slots_flat.shape[0]
    workers = SC_CORES * SC_SUBCORES
    per_w = nslot // workers
    chunk = min(8192, n)
    assert nslot % (workers * SC_LANES) == 0 and n % chunk == 0 and chunk % SC_LANES == 0
    mesh = plsc.VectorSubcoreMesh(core_axis_name="c", subcore_axis_name="s")

    @functools.partial(
        pl.kernel, out_type=jax.ShapeDtypeStruct((nslot,), I32), mesh=mesh,
        scratch_types=[pltpu.VMEM((per_w,), I32), pltpu.VMEM((chunk,), I32)],
        compiler_params=pltpu.CompilerParams(needs_layout_passes=False))
    def inverse(slots_hbm, inv_hbm, local, buf):
        lo = (lax.axis_index("s") * SC_CORES + lax.axis_index("c")) * per_w
        unused = jnp.full((SC_LANES,), -1, I32)

        @pl.loop(0, per_w, step=SC_LANES)
        def _(j):
            local[pl.ds(j, SC_LANES)] = unused

        lane = lax.iota(I32, SC_LANES)

        @pl.loop(0, n, step=chunk)
        def _(c0):
            pltpu.sync_copy(slots_hbm.at[pl.ds(c0, chunk)], buf)

            @pl.loop(0, chunk, step=SC_LANES)
            def _(j):
                idx = buf[pl.ds(j, SC_LANES)] - lo
                mine = jnp.logical_and(idx >= 0, idx < per_w)
                plsc.store_scatter(local, [idx], c0 + j + lane, mask=mine)

        pltpu.sync_copy(local, inv_hbm.at[pl.ds(lo, per_w)])

    return inverse(slots_flat)


EXPERT_RING = 3


def _experts_kernel(texp_ref, tok_ref, w1_ref, w3_ref, w2_ref, f_hbm, y_ref, w1b, w3b, w2b, xa, xb, xc_, gsem, *, tme):
    i = pl.program_id(0)
    last = pl.num_programs(0) - 1
    ring = (xa, xb, xc_)
    sub = xa.shape[0] // tme
    d, fdim = w1b.shape
    half = d // 2

    @pl.when(i == 0)
    def _():
        xb[...] = jnp.zeros_like(xb)
        xc_[...] = jnp.zeros_like(xc_)

    @pl.when(jnp.logical_or(i == 0, texp_ref[i] != texp_ref[jnp.maximum(i - 1, 0)]))
    def _():
        w1b[...] = w1_ref[...].astype(BF16)
        w3b[...] = w3_ref[...].astype(BF16)
        w2b[...] = w2_ref[...].astype(BF16)

    def whole_tile(buf, sem):
        return pltpu.make_async_copy(f_hbm.at[pl.ds(0, tme * sub), :], buf, sem)

    def step(par):
        nxt, prv = (par + 1) % EXPERT_RING, (par + 2) % EXPERT_RING
        xg, xc = ring[par], ring[nxt]

        @pl.when(i >= EXPERT_RING - 1)
        def _():
            whole_tile(xc, gsem.at[nxt]).wait()

        pw1, pw2 = min(MXU_N, fdim), min(MXU_N, half)
        pieces = 2 * (fdim // pw1) + 2 * (half // pw2)
        per_piece = -(-tme // max(pieces - 2, 1))
        issued = [0]

        def issue_rows():
            for r in range(issued[0], min(issued[0] + per_piece, tme)):
                tok = pl.multiple_of(tok_ref[r] * sub, sub)
                pltpu.make_async_copy(f_hbm.at[pl.ds(tok, sub), :], xg.at[pl.ds(r * sub, sub), :],
                                      gsem.at[par]).start(priority=r % 2)
            issued[0] = min(issued[0] + per_piece, tme)

        h = _load_rows_bf16(xc, tme)
        gate, up = [], []
        for c0 in range(0, fdim, pw1):
            gate.append(_bdot(h, w1b[:, c0:c0 + pw1]))
            issue_rows()
        for c0 in range(0, fdim, pw1):
            up.append(_bdot(h, w3b[:, c0:c0 + pw1]))
            issue_rows()
        a = (_silu(jnp.concatenate(gate, axis=1)) * jnp.concatenate(up, axis=1)).astype(BF16)
        for c0 in range(0, half, pw2):
            lo = _bdot(a, w2b[:, c0:c0 + pw2])
            issue_rows()
            hi = _bdot(a, w2b[:, half + c0:half + c0 + pw2])
            issue_rows()
            packed = _pack_pairs(jnp.concatenate([lo, hi], axis=1))
            for b0 in range(0, pw2, LANES):
                y_ref[pl.ds((c0 + b0) // LANES, tme, stride=sub), :] = packed[:, b0:b0 + LANES]
        assert issued[0] == tme

        @pl.when(i == last)
        def _():
            whole_tile(ring[prv], gsem.at[prv]).wait()
            whole_tile(xg, gsem.at[par]).wait()

    for par in range(EXPERT_RING):
        pl.when(lax.rem(i, EXPERT_RING) == par)(functools.partial(step, par))


def _experts(texp, tok, w1, w3, w2, f, layer, tme):
    d, fdim = w1.shape[-2:]
    sub = d // (2 * LANES)
    lag = EXPERT_RING - 1
    ntile = texp.shape[0] - lag
    wsel = lambda i, te: (layer, te[i], 0, 0)
    return pl.pallas_call(
        functools.partial(_experts_kernel, tme=tme),
        out_shape=jax.ShapeDtypeStruct((ntile * tme * sub, LANES), U32),
        grid_spec=pltpu.PrefetchScalarGridSpec(
            num_scalar_prefetch=1,
            grid=(ntile + lag,),
            in_specs=[pl.BlockSpec((tme,), lambda i, te: (i,), memory_space=pltpu.SMEM),
                      pl.BlockSpec((None, None, d, fdim), wsel), pl.BlockSpec((None, None, d, fdim), wsel),
                      pl.BlockSpec((None, None, fdim, d), wsel),
                      pl.BlockSpec(memory_space=pl.ANY)],
            out_specs=pl.BlockSpec((tme * sub, LANES), lambda i, te: (jnp.maximum(i - lag, 0), 0)),
            scratch_shapes=[pltpu.VMEM((d, fdim), BF16), pltpu.VMEM((d, fdim), BF16), pltpu.VMEM((fdim, d), BF16)]
            + [pltpu.VMEM((tme * sub, LANES), U32)] * EXPERT_RING
            + [pltpu.SemaphoreType.DMA((EXPERT_RING,))]),
        compiler_params=_params("arbitrary"),
        name="moe_experts",
    )(texp, tok, w1, w3, w2, f)


def _combine_kernel(slot_ref, wt_ref, x1_ref, sh_ref, gf_ref, lng_ref, lnb_ref, ys_hbm, o_ref, ga, gb, gsem, *, alpha):
    i = pl.program_id(0)
    last = pl.num_programs(0) - 1
    tm = x1_ref.shape[0]
    sub = ga.shape[1] // tm

    @pl.when(i == 0)
    def _():
        gb[...] = jnp.zeros_like(gb)

    def whole_plane(buf, k, sem):
        return pltpu.make_async_copy(ys_hbm.at[pl.ds(0, tm * sub), :], buf.at[k], sem)

    def step(par):
        gg, gc = (ga, gb) if par == 0 else (gb, ga)

        @pl.when(i > 0)
        def _():
            for k in range(TOP_K):
                whole_plane(gc, k, gsem.at[1 - par]).wait()

        def issue(t0, t1):
            for tt in range(t0, t1):
                for k in range(TOP_K):
                    row = pl.multiple_of(slot_ref[tt * TOP_K + k] * sub, sub)
                    pltpu.make_async_copy(ys_hbm.at[pl.ds(row, sub), :], gg.at[k, pl.ds(tt * sub, sub), :],
                                          gsem.at[par]).start(priority=k % 2)

        lo = hi = None
        for k in range(TOP_K):
            issue(k * tm // TOP_K, (k + 1) * tm // TOP_K)
            w = wt_ref[:, k:k + 1]
            rl, rh = _load_row_tiles(gc.at[k], tm)
            lo = [w * b for b in rl] if lo is None else [a + w * b for a, b in zip(lo, rl)]
            hi = [w * b for b in rh] if hi is None else [a + w * b for a, b in zip(hi, rh)]
        f = jnp.concatenate(lo + hi, axis=1) + sh_ref[...].astype(F32)
        o_ref[...] = _layernorm(alpha * x1_ref[...] + gf_ref[0] * f, lng_ref[0], lnb_ref[0])

        @pl.when(i == last)
        def _():
            for k in range(TOP_K):
                whole_plane(gg, k, gsem.at[par]).wait()

    for par in range(2):
        pl.when(jnp.bitwise_and(i, 1) == par)(functools.partial(step, par))


def _combine(slot_tk, wt, x1, sh, gf, lng, lnb, ys, seq, tm, layer, alpha):
    t, d = x1.shape
    sub = d // (2 * LANES)
    per_b = seq // tm
    nstep = t // tm + 1
    prev = lambda i: jnp.maximum(i - 1, 0)
    row = lambda i: (prev(i), 0)
    lrow = lambda i: (layer, 0, 0)
    return pl.pallas_call(
        functools.partial(_combine_kernel, alpha=alpha),
        out_shape=jax.ShapeDtypeStruct((t, d), F32),
        grid=(nstep,),
        in_specs=[pl.BlockSpec((tm * TOP_K,), lambda i: (i,), memory_space=pltpu.SMEM),
                  pl.BlockSpec((tm, TOP_K), row), pl.BlockSpec((tm, d), row), pl.BlockSpec((tm, d), row),
                  pl.BlockSpec((1, 1, d), lambda i: (prev(i) // per_b, 0, 0)),
                  pl.BlockSpec((1, 1, d), lrow), pl.BlockSpec((1, 1, d), lrow),
                  pl.BlockSpec(memory_space=pl.ANY)],
        out_specs=pl.BlockSpec((tm, d), row),
        scratch_shapes=[pltpu.VMEM((TOP_K, tm * sub, LANES), U32), pltpu.VMEM((TOP_K, tm * sub, LANES), U32),
                        pltpu.SemaphoreType.DMA((2,))],
        compiler_params=_params("arbitrary"),
        name="moe_combine_ln",
    )(slot_tk, wt, x1, sh, gf, lng, lnb, ys)


def _moe(f, x1, ids, wts, rank, cnt, gf, lng, lnb, w1, w3, w2, ws1, ws3, ws2, seq, layer, alpha):
    t, d = x1.shape
    tme = min(512, t)
    counts = cnt[:, 0].astype(I32)
    tiles_e = (counts + tme - 1) // tme
    padded = tiles_e * tme
    base = jnp.cumsum(padded) - padded
    expert = jnp.arange(N_EXPERTS, dtype=I32)
    slots = jnp.sum(jnp.where(ids[:, :, None] == expert, base, 0), axis=-1) + rank
    ntile = (t * TOP_K) // tme + N_EXPERTS
    nslot = ntile * tme
    tile_ends = jnp.cumsum(tiles_e)
    texp = jnp.sum((jnp.arange(ntile, dtype=I32)[:, None] >= tile_ends[None, :]).astype(I32), axis=1)
    texp = jnp.minimum(texp, N_EXPERTS - 1)
    lag = EXPERT_RING - 1
    texp_step = jnp.concatenate([texp[:1]] * lag + [texp])

    inv = _inverse_map(slots.reshape(-1), nslot)
    tok = jnp.where(inv >= 0, inv % t, 0)
    tok = jnp.concatenate([tok, jnp.zeros((lag * tme,), I32)])

    shared = _shared_expert(f, ws1, ws3, ws2, t)
    ys = _experts(texp_step, tok, w1, w3, w2, f, layer, tme)
    tmc = min(128, seq)
    slot_tk = jnp.concatenate([slots.T.reshape(-1), jnp.zeros((tmc * TOP_K,), I32)])
    return _combine(slot_tk, wts.T, x1, shared, gf, lng, lnb, ys, seq, tmc, layer, alpha)


def kernel(x, c, ctx, c_ctx, w_ada, b_ada, ln_mix_g, ln_mix_b, ln_ffn_g, ln_ffn_b, attn_w_qkv, attn_w_o, attn_sink, gmlp_w_in, gmlp_b_in, gmlp_v_g, gmlp_v_b, gmlp_w_s, gmlp_b_s, gmlp_w_o, moe_w_router, moe_bias, moe_w1, moe_w3, moe_w2, moe_ws1, moe_ws3, moe_ws2):
    batch, seq, d = x.shape
    nctx = ctx.shape[1]
    depth = w_ada.shape[0]
    assert depth == 2, "layer 0 is the attention mixer, layer 1 the gMLP mixer"
    alpha = float((2 * depth) ** 0.25)
    t = batch * seq

    pad = (-(batch + 1)) % 8
    cc = jnp.concatenate([c, c_ctx[None, :], jnp.zeros((pad, d), F32)], axis=0)
    mods = _modulations(cc, w_ada, b_ada)

    def mod(layer, j, rows=slice(0, batch)):
        return mods[layer, rows, j * d:(j + 1) * d]

    def per_batch(layer, j):
        return mod(layer, j).reshape(batch, 1, d)

    def router(layer):
        w = jnp.pad(moe_w_router[layer], ((0, 0), (0, LANES - N_EXPERTS)))
        hi = w.astype(BF16)
        lo = (w - hi.astype(F32)).astype(BF16)
        return jnp.concatenate([hi, lo], axis=1), moe_bias[layer].reshape(N_EXPERTS, 1)

    def shared_w(layer):
        return moe_ws1[layer].astype(BF16), moe_ws3[layer].astype(BF16), moe_ws2[layer].astype(BF16)

    x2 = x.reshape(t, d)
    ln_mix_g, ln_mix_b, ln_ffn_g, ln_ffn_b = (
        p.reshape(depth, 1, d) for p in (ln_mix_g, ln_mix_b, ln_ffn_g, ln_ffn_b))

    qd = Q_PER_KV * N_KV_HEADS * HEAD_DIM
    kvd = N_KV_HEADS * HEAD_DIM
    wqkv = attn_w_qkv[0]
    dup = lambda w: jnp.concatenate([w.reshape(d, N_KV_HEADS, 1, HEAD_DIM)] * (LANES // HEAD_DIM), axis=2
                                    ).reshape(d, N_KV_HEADS * LANES)
    wk2, wv2 = dup(wqkv[:, qd:qd + kvd]), dup(wqkv[:, qd + kvd:])
    w_all = jnp.concatenate([wqkv[:, :qd] * (HEAD_DIM ** -0.5), wk2, wv2], axis=1).astype(BF16)
    q, k2, v2 = _qkv_proj(x2, per_batch(0, 1), per_batch(0, 0), w_all, _rope_tables(seq), seq)
    ctx_row = slice(batch, batch + 1)
    kc2, vc2 = _ctx_kv(ctx.reshape(batch * nctx, d), mod(0, 1, ctx_row), mod(0, 0, ctx_row),
                       jnp.concatenate([wk2, wv2], axis=1).astype(BF16))
    o = _attention(q, k2, v2, kc2, vc2, attn_sink[0], batch, seq, nctx)
    x1, f, ids, wts, rank, cnt = _attn_out(
        o, attn_w_o[0].astype(BF16), x2, per_batch(0, 2), per_batch(0, 4), per_batch(0, 3),
        ln_mix_g, ln_mix_b, *router(0), seq, 0, alpha)
    x2 = _moe(f, x1, ids, wts, rank, cnt, per_batch(0, 5), ln_ffn_g, ln_ffn_b,
              moe_w1, moe_w3, moe_w2, *shared_w(0), seq, 0, alpha)

    width = gmlp_w_in.shape[2] // 2
    w_in = gmlp_w_in[0].astype(BF16)
    b_in = gmlp_b_in[0].reshape(1, 2 * width)
    vg, vb = gmlp_v_g[0].reshape(1, width), gmlp_v_b[0].reshape(1, width)
    sc, sh = per_batch(1, 1), per_batch(1, 0)
    u = _gmlp_in(x2, sc, sh, w_in[:, :width], b_in[:, :width], vg, vb, seq, False)
    v = _gmlp_in(x2, sc, sh, w_in[:, width:], b_in[:, width:], vg, vb, seq, True)
    x1, f, ids, wts, rank, cnt = _gmlp_out(
        u, v, gmlp_w_s[0].astype(BF16), gmlp_b_s[0][:, :, None], gmlp_w_o[0].astype(BF16), x2,
        per_batch(1, 2), per_batch(1, 4), per_batch(1, 3), ln_mix_g, ln_mix_b, *router(1), seq, 1, alpha)
    x2 = _moe(f, x1, ids, wts, rank, cnt, per_batch(1, 5), ln_ffn_g, ln_ffn_b,
              moe_w1, moe_w3, moe_w2, *shared_w(1), seq, 1, alpha)
    return x2.reshape(batch, seq, d)
```

```python
import functools

import jax
import jax.numpy as jnp
from jax import lax
from jax.experimental import pallas as pl
from jax.experimental.pallas import tpu as pltpu
from jax.experimental.pallas import tpu_sc as plsc

F32 = jnp.float32
BF16 = jnp.bfloat16
I32 = jnp.int32

GRID_W = 64
N_KV_HEADS = 4
Q_PER_KV = 8
HEAD_DIM = 64
ROPE_HALF = 16
WINDOW = 128
Q_BLOCK = 128
ROPE_BASE = 10000.0
NEG_INF = -1e30
CHUNK = 128
N_GMLP_GROUPS = 8
N_EXPERTS = 64
TOP_K = 8
N_EXPERT_GROUPS = 8
EXPERTS_PER_GROUP = 8
TOPK_GROUPS = 4
ROUTED_SCALE = 2.5
LN_EPS = 1e-5

LANES = 128
VMEM_LIMIT_BYTES = 56 * 1024 * 1024


def _params(*sem):
    return pltpu.CompilerParams(dimension_semantics=sem, vmem_limit_bytes=VMEM_LIMIT_BYTES)


def _silu(a):
    return a * jax.nn.sigmoid(a)


def _layernorm(v, g, b):
    mu = jnp.mean(v, axis=-1, keepdims=True)
    d = v - mu
    var = jnp.mean(d * d, axis=-1, keepdims=True)
    return d * lax.rsqrt(var + LN_EPS) * g + b


def _bdot(a, b):
    return jnp.dot(a, b, preferred_element_type=F32)


U32 = jnp.uint32


def _pack_pairs(a):
    n = a.shape[1] // 2
    lo = lax.bitcast_convert_type(a[:, :n].astype(BF16).astype(F32), U32)
    hi = lax.bitcast_convert_type(a[:, n:].astype(BF16).astype(F32), U32)
    return hi | (lo >> 16)


def _unpack_pairs(p):
    lo = lax.bitcast_convert_type(p << 16, F32)
    hi = lax.bitcast_convert_type(p & jnp.uint32(0xFFFF0000), F32)
    return lo, hi


def _store_row_tiles(ref, packed):
    r, n = packed.shape
    sub = n // LANES
    for j in range(sub):
        ref[pl.ds(j, r, stride=sub), :] = packed[:, LANES * j:LANES * (j + 1)]


def _load_row_tiles(ref, r):
    sub = ref.shape[0] // r
    pieces = [_unpack_pairs(ref[pl.ds(j, r, stride=sub), :]) for j in range(sub)]
    return [p[0] for p in pieces], [p[1] for p in pieces]


def _load_rows_bf16(ref, r):
    lo, hi = _load_row_tiles(ref, r)
    return jnp.concatenate(lo + hi, axis=1).astype(BF16)


def _mod_kernel(c_ref, w_ref, b_ref, o_ref):
    a = _silu(c_ref[...]).astype(BF16)
    o_ref[0] = _bdot(a, w_ref[0].astype(BF16)) + b_ref[0]


def _modulations(cc, w_ada, b_ada):
    depth, d, n6 = w_ada.shape
    r = cc.shape[0]
    tn = max(w for w in range(LANES, min(1024, n6) + 1, LANES) if n6 % w == 0)
    return pl.pallas_call(
        _mod_kernel,
        out_shape=jax.ShapeDtypeStruct((depth, r, n6), F32),
        grid=(depth, n6 // tn),
        in_specs=[pl.BlockSpec((r, d), lambda l, j: (0, 0)),
                  pl.BlockSpec((1, d, tn), lambda l, j: (l, 0, j)),
                  pl.BlockSpec((1, 1, tn), lambda l, j: (l, 0, j))],
        out_specs=pl.BlockSpec((1, r, tn), lambda l, j: (l, 0, j)),
        compiler_params=_params("parallel", "parallel"),
        name="adaln_mod",
    )(cc, w_ada, b_ada.reshape(depth, 1, n6))


def _rope_tables(seq):
    rows = seq // GRID_W
    row_ids = jnp.repeat(jnp.arange(rows, dtype=F32), GRID_W)
    col_ids = jnp.tile(jnp.arange(GRID_W, dtype=F32), rows)
    inv_freq = ROPE_BASE ** (-jnp.arange(0, 2 * ROPE_HALF, 2, dtype=F32) / (2 * ROPE_HALF))
    ar, ac = row_ids[:, None] * inv_freq, col_ids[:, None] * inv_freq
    z = jnp.zeros_like(ar)
    cos = jnp.concatenate([jnp.cos(ar), jnp.cos(ar), jnp.cos(ac), jnp.cos(ac)], axis=-1)
    sin_lo = jnp.concatenate([-jnp.sin(ar), z, -jnp.sin(ac), z], axis=-1)
    sin_hi = jnp.concatenate([z, jnp.sin(ar), z, jnp.sin(ac)], axis=-1)
    rep = LANES // HEAD_DIM
    return tuple(jnp.tile(t, (1, rep)) for t in (cos, sin_lo, sin_hi))


def _qkv_kernel(x_ref, sc_ref, sh_ref, w_ref, cos_ref, slo_ref, shi_ref, q_ref, k_ref, v_ref, *, cw):
    h = (x_ref[...] * (1.0 + sc_ref[0]) + sh_ref[0]).astype(BF16)
    cos, slo, shi = cos_ref[...], slo_ref[...], shi_ref[...]

    def rope(a):
        return (a * cos + pltpu.roll(a, LANES - ROPE_HALF, 1) * slo
                + pltpu.roll(a, ROPE_HALF, 1) * shi)

    qd, kd = q_ref.shape[1], k_ref.shape[1]
    for c0 in range(0, qd, cw):
        acc = _bdot(h, w_ref[:, c0:c0 + cw])
        for b0 in range(0, cw, LANES):
            q_ref[:, c0 + b0:c0 + b0 + LANES] = rope(acc[:, b0:b0 + LANES]).astype(BF16)
    acc = _bdot(h, w_ref[:, qd:qd + kd])
    for b0 in range(0, kd, LANES):
        k_ref[:, b0:b0 + LANES] = rope(acc[:, b0:b0 + LANES]).astype(BF16)
    v_ref[...] = _bdot(h, w_ref[:, qd + kd:qd + 2 * kd]).astype(BF16)


def _qkv_proj(x2, sc, sh, w, tables, seq):
    t, d = x2.shape
    kd = N_KV_HEADS * LANES
    qd = w.shape[1] - 2 * kd
    tm = min(512, seq)
    per_b = seq // tm
    row = lambda i: (i, 0)
    mod = lambda i: (i // per_b, 0, 0)
    tab = lambda i: (i % per_b, 0)
    return pl.pallas_call(
        functools.partial(_qkv_kernel, cw=min(512, qd)),
        out_shape=(jax.ShapeDtypeStruct((t, qd), BF16), jax.ShapeDtypeStruct((t, kd), BF16),
                   jax.ShapeDtypeStruct((t, kd), BF16)),
        grid=(t // tm,),
        in_specs=[pl.BlockSpec((tm, d), row), pl.BlockSpec((1, 1, d), mod), pl.BlockSpec((1, 1, d), mod),
                  pl.BlockSpec(w.shape, lambda i: (0, 0)),
                  pl.BlockSpec((tm, LANES), tab), pl.BlockSpec((tm, LANES), tab), pl.BlockSpec((tm, LANES), tab)],
        out_specs=(pl.BlockSpec((tm, qd), row), pl.BlockSpec((tm, kd), row), pl.BlockSpec((tm, kd), row)),
        compiler_params=_params("parallel"),
        name="qkv_rope",
    )(x2, sc, sh, w, *tables)


def _ctxkv_kernel(x_ref, sc_ref, sh_ref, w_ref, k_ref, v_ref):
    h = (x_ref[...] * (1.0 + sc_ref[...]) + sh_ref[...]).astype(BF16)
    kd = k_ref.shape[1]
    k_ref[...] = _bdot(h, w_ref[:, :kd]).astype(BF16)
    v_ref[...] = _bdot(h, w_ref[:, kd:]).astype(BF16)


def _ctx_kv(c2, sc, sh, w):
    t, d = c2.shape
    kd = w.shape[1] // 2
    tm = min(512, t)
    row = lambda i: (i, 0)
    fix = lambda i: (0, 0)
    return pl.pallas_call(
        _ctxkv_kernel,
        out_shape=(jax.ShapeDtypeStruct((t, kd), BF16), jax.ShapeDtypeStruct((t, kd), BF16)),
        grid=(t // tm,),
        in_specs=[pl.BlockSpec((tm, d), row), pl.BlockSpec((1, d), fix), pl.BlockSpec((1, d), fix),
                  pl.BlockSpec(w.shape, fix)],
        out_specs=(pl.BlockSpec((tm, kd), row), pl.BlockSpec((tm, kd), row)),
        compiler_params=_params("parallel"),
        name="ctx_kv",
    )(c2, sc, sh, w)


def _attn_kernel(sink_ref, q_ref, k_ref, v_ref, kc_ref, vc_ref, o_ref):
    for b in range(q_ref.shape[1] // Q_BLOCK):
        _attn_block(sink_ref, q_ref, k_ref, v_ref, kc_ref, vc_ref, o_ref, b)


def _attn_block(sink_ref, q_ref, k_ref, v_ref, kc_ref, vc_ref, o_ref, b):
    kv = pl.program_id(1)
    i = pl.program_id(2) * (q_ref.shape[1] // Q_BLOCK) + b
    seq, nctx = k_ref.shape[1], kc_ref.shape[1]
    nwin = Q_BLOCK + 2 * WINDOW
    start = pl.multiple_of(jnp.clip(i * Q_BLOCK - WINDOW, 0, seq - nwin), Q_BLOCK)
    nkey = nctx + nwin
    kall = jnp.concatenate([kc_ref[0], k_ref[0, pl.ds(start, nwin), :]], axis=0)
    vall = jnp.concatenate([vc_ref[0], v_ref[0, pl.ds(start, nwin), :]], axis=0)
    low = lax.broadcasted_iota(I32, (nkey, LANES), 1) < HEAD_DIM
    zero = jnp.zeros((nkey, LANES), BF16)
    npair = Q_PER_KV // 2
    q = q_ref[0, Q_BLOCK * b:Q_BLOCK * (b + 1), :]
    qs = jnp.concatenate([q[:, LANES * j:LANES * (j + 1)] for j in range(npair)], axis=0)
    nrow = npair * Q_BLOCK
    rows = lax.broadcasted_iota(I32, (Q_BLOCK, nkey), 0)
    cols = lax.broadcasted_iota(I32, (Q_BLOCK, nkey), 1)
    dist = (i * Q_BLOCK - start + nctx) + rows - cols
    valid = (cols < nctx) | (jnp.abs(dist) <= WINDOW)
    contract_last = (((1,), (1,)), ((), ()))
    out = jnp.zeros((nrow, LANES), F32)
    for par in range(2):
        keep = low if par == 0 else jnp.logical_not(low)
        kh = jnp.where(keep, kall, zero)
        vh = jnp.where(keep, vall, zero)
        s = lax.dot_general(qs, kh, contract_last, preferred_element_type=F32)
        s = jnp.concatenate(
            [jnp.where(valid, s[Q_BLOCK * j:Q_BLOCK * (j + 1), :], NEG_INF) for j in range(npair)], axis=0)
        sink = jnp.concatenate(
            [jnp.full((Q_BLOCK, 1), sink_ref[kv * Q_PER_KV + 2 * j + par], F32) for j in range(npair)], axis=0)
        m = jnp.maximum(jnp.max(s, axis=1, keepdims=True), sink)
        p = jnp.exp(s - m)
        den = jnp.sum(p, axis=1, keepdims=True) + jnp.exp(sink - m)
        out = out + _bdot(p.astype(BF16), vh) / den
    for j in range(npair):
        o_ref[0, Q_BLOCK * b:Q_BLOCK * (b + 1), LANES * j:LANES * (j + 1)] = (
            out[Q_BLOCK * j:Q_BLOCK * (j + 1), :].astype(BF16))


def _attention(q, k2, v2, kc2, vc2, sink, batch, seq, nctx):
    qd = q.shape[1]
    gw = Q_PER_KV * HEAD_DIM
    q3 = q.reshape(batch, seq, qd)
    k3, v3 = k2.reshape(batch, seq, -1), v2.reshape(batch, seq, -1)
    kc3, vc3 = kc2.reshape(batch, nctx, -1), vc2.reshape(batch, nctx, -1)
    full = lambda b, h, i: (b, 0, h)
    blk = lambda b, h, i: (b, i, h)
    rows = next(r * Q_BLOCK for r in (4, 2, 1) if seq % (r * Q_BLOCK) == 0)
    out = pl.pallas_call(
        _attn_kernel,
        out_shape=jax.ShapeDtypeStruct((batch, seq, qd), BF16),
        grid=(batch, N_KV_HEADS, seq // rows),
        in_specs=[pl.BlockSpec(memory_space=pltpu.SMEM),
                  pl.BlockSpec((1, rows, gw), blk),
                  pl.BlockSpec((1, seq, LANES), full), pl.BlockSpec((1, seq, LANES), full),
                  pl.BlockSpec((1, nctx, LANES), full), pl.BlockSpec((1, nctx, LANES), full)],
        out_specs=pl.BlockSpec((1, rows, gw), blk),
        compiler_params=_params("parallel", "parallel", "parallel"),
        name="window_gqa",
    )(sink, q3, k3, v3, kc3, vc3)
    return out.reshape(batch * seq, qd)


def _split_bf16(a):
    hi = a.astype(BF16)
    return hi, (a - hi.astype(F32)).astype(BF16)


def _post_mixer(x, y, gm, lng, lnb, scf, shf, wr_ref, rb_ref, run_ref,
                x1_ref, f_ref, ids_ref, wts_ref, rank_ref, cnt_ref, alpha):
    tm = x.shape[0]
    x1 = _layernorm(alpha * x + gm * y, lng, lnb)
    x1_ref[...] = x1
    f = x1 * (1.0 + scf) + shf
    _store_row_tiles(f_ref, _pack_pairs(f))

    fh, fl = _split_bf16(f)
    prod = _bdot(fh, wr_ref[...]) + _bdot(fl, wr_ref[...])
    logits = (prod[:, :LANES] + prod[:, LANES:]).T[:N_EXPERTS, :]
    scores = jax.nn.sigmoid(logits)
    biased = scores + rb_ref[...]

    eg = EXPERTS_PER_GROUP
    sub = lax.broadcasted_iota(I32, (eg, tm), 0)
    gscore = []
    for g in range(N_EXPERT_GROUPS):
        tg = biased[eg * g:eg * (g + 1), :]
        m1 = jnp.max(tg, axis=0, keepdims=True)
        i1 = jnp.min(jnp.where(tg == m1, sub, eg), axis=0, keepdims=True)
        m2 = jnp.max(jnp.where(sub == i1, -jnp.inf, tg), axis=0, keepdims=True)
        gscore.append(m1 + m2)
    gsel = [jnp.zeros((1, tm), jnp.bool_) for _ in range(N_EXPERT_GROUPS)]
    for _ in range(TOPK_GROUPS):
        best = functools.reduce(jnp.maximum, gscore)
        taken = jnp.zeros((1, tm), jnp.bool_)
        for g in range(N_EXPERT_GROUPS):
            hit = jnp.logical_and(gscore[g] == best, jnp.logical_not(taken))
            taken = jnp.logical_or(taken, hit)
            gsel[g] = jnp.logical_or(gsel[g], hit)
            gscore[g] = jnp.where(hit, -jnp.inf, gscore[g])
    cur = jnp.concatenate(
        [jnp.where(gsel[g], biased[eg * g:eg * (g + 1), :], -jnp.inf) for g in range(N_EXPERT_GROUPS)], axis=0)

    eidx = lax.broadcasted_iota(I32, (N_EXPERTS, tm), 0)
    picks, wts, hots = [], [], []
    for _ in range(TOP_K):
        m = jnp.max(cur, axis=0, keepdims=True)
        idx = jnp.min(jnp.where(cur == m, eidx, N_EXPERTS), axis=0, keepdims=True)
        hot = eidx == idx
        picks.append(idx)
        wts.append(jnp.sum(jnp.where(hot, scores, 0.0), axis=0, keepdims=True))
        hots.append(hot)
        cur = jnp.where(hot, -jnp.inf, cur)
    wsum = functools.reduce(jnp.add, wts)

    assigned = functools.reduce(jnp.add, [h.astype(F32) for h in hots])
    before = (lax.broadcasted_iota(I32, (tm, tm), 0) < lax.broadcasted_iota(I32, (tm, tm), 1)).astype(BF16)
    pos = _bdot(assigned.astype(BF16), before) + run_ref[:, 0:1]
    for k in range(TOP_K):
        ids_ref[k:k + 1, :] = picks[k]
        wts_ref[k:k + 1, :] = wts[k] / wsum * ROUTED_SCALE
        rank_ref[k:k + 1, :] = jnp.sum(jnp.where(hots[k], pos, 0.0), axis=0, keepdims=True).astype(I32)
    run_ref[...] = run_ref[...] + jnp.sum(assigned, axis=1, keepdims=True)
    cnt_ref[...] = run_ref[...]


def _route_out_shapes(t, d):
    return (jax.ShapeDtypeStruct((t, d), F32), jax.ShapeDtypeStruct((t * d // (2 * LANES), LANES), U32),
            jax.ShapeDtypeStruct((TOP_K, t), I32), jax.ShapeDtypeStruct((TOP_K, t), F32),
            jax.ShapeDtypeStruct((TOP_K, t), I32), jax.ShapeDtypeStruct((N_EXPERTS, LANES), F32))


def _route_out_specs(tm, d):
    row = lambda i: (i, 0)
    col = lambda i: (0, i)
    return (pl.BlockSpec((tm, d), row), pl.BlockSpec((tm * d // (2 * LANES), LANES), row),
            pl.BlockSpec((TOP_K, tm), col), pl.BlockSpec((TOP_K, tm), col), pl.BlockSpec((TOP_K, tm), col),
            pl.BlockSpec((N_EXPERTS, LANES), lambda i: (0, 0)))


def _route_in_specs(tm, d, per_b, layer):
    mod = lambda i: (i // per_b, 0, 0)
    lrow = lambda i: (layer, 0, 0)
    fix = lambda i: (0, 0)
    return [pl.BlockSpec((tm, d), lambda i: (i, 0)),
            pl.BlockSpec((1, 1, d), mod), pl.BlockSpec((1, 1, d), mod), pl.BlockSpec((1, 1, d), mod),
            pl.BlockSpec((1, 1, d), lrow), pl.BlockSpec((1, 1, d), lrow),
            pl.BlockSpec((d, 2 * LANES), fix),
            pl.BlockSpec((N_EXPERTS, 1), fix)]


def _oproj_kernel(o_ref, wo_ref, x_ref, gm_ref, scf_ref, shf_ref, lng_ref, lnb_ref, wr_ref, rb_ref,
                  x1_ref, f_ref, ids_ref, wts_ref, rank_ref, cnt_ref, run_ref, *, alpha):
    @pl.when(pl.program_id(0) == 0)
    def _():
        run_ref[...] = jnp.zeros_like(run_ref)

    y = _bdot(o_ref[...], wo_ref[...])
    _post_mixer(x_ref[...], y, gm_ref[0], lng_ref[0], lnb_ref[0], scf_ref[0], shf_ref[0],
                wr_ref, rb_ref, run_ref, x1_ref, f_ref, ids_ref, wts_ref, rank_ref, cnt_ref, alpha)


def _attn_out(o, wo, x2, gm, scf, shf, lng, lnb, wr, rb, seq, layer, alpha):
    t, d = x2.shape
    tm = min(512, seq)
    return pl.pallas_call(
        functools.partial(_oproj_kernel, alpha=alpha),
        out_shape=_route_out_shapes(t, d),
        grid=(t // tm,),
        in_specs=[pl.BlockSpec((tm, o.shape[1]), lambda i: (i, 0)), pl.BlockSpec(wo.shape, lambda i: (0, 0))]
        + _route_in_specs(tm, d, seq // tm, layer),
        out_specs=_route_out_specs(tm, d),
        scratch_shapes=[pltpu.VMEM((N_EXPERTS, LANES), F32)],
        compiler_params=_params("arbitrary"),
        name="attn_out_route",
    )(o, wo, x2, gm, scf, shf, lng, lnb, wr, rb)


def _gmlp_in_kernel(x_ref, sc_ref, sh_ref, w_ref, b_ref, g_ref, beta_ref, o_ref, *, cw, normalize):
    h = (x_ref[...] * (1.0 + sc_ref[0]) + sh_ref[0]).astype(BF16)
    n = o_ref.shape[1]
    inv_sqrt2 = 0.7071067811865476
    parts = []
    for c0 in range(0, n, cw):
        z = _bdot(h, w_ref[:, c0:c0 + cw]) + b_ref[:, c0:c0 + cw]
        z = 0.5 * z * (1.0 + lax.erf(z * inv_sqrt2))
        if normalize:
            parts.append(z)
        else:
            o_ref[:, c0:c0 + cw] = z.astype(BF16)
    if normalize:
        v = jnp.concatenate(parts, axis=1)
        o_ref[...] = _layernorm(v, g_ref[...], beta_ref[...]).astype(BF16)


def _gmlp_in(x2, sc, sh, w, b, g, beta, seq, normalize):
    t, d = x2.shape
    n = w.shape[1]
    tm = min(256, seq)
    per_b = seq // tm
    row = lambda i: (i, 0)
    mod = lambda i: (i // per_b, 0, 0)
    fix = lambda i: (0, 0)
    return pl.pallas_call(
        functools.partial(_gmlp_in_kernel, cw=min(512, n), normalize=normalize),
        out_shape=jax.ShapeDtypeStruct((t, n), BF16),
        grid=(t // tm,),
        in_specs=[pl.BlockSpec((tm, d), row), pl.BlockSpec((1, 1, d), mod), pl.BlockSpec((1, 1, d), mod),
                  pl.BlockSpec(w.shape, fix), pl.BlockSpec((1, n), fix), pl.BlockSpec((1, n), fix),
                  pl.BlockSpec((1, n), fix)],
        out_specs=pl.BlockSpec((tm, n), row),
        compiler_params=_params("parallel"),
        name="gmlp_in_v" if normalize else "gmlp_in_u",
    )(x2, sc, sh, w, b, g, beta)


def _gmlp_out_kernel(u_ref, v_ref, ws_ref, bs_ref, wo_ref, x_ref, gm_ref, scf_ref, shf_ref, lng_ref, lnb_ref,
                     wr_ref, rb_ref, x1_ref, f_ref, ids_ref, wts_ref, rank_ref, cnt_ref,
                     run_ref, gated_ref, *, alpha):
    @pl.when(pl.program_id(0) == 0)
    def _():
        run_ref[...] = jnp.zeros_like(run_ref)

    tm, width = u_ref.shape
    gd = width // N_GMLP_GROUPS
    for r0 in range(0, tm, CHUNK):
        for g in range(N_GMLP_GROUPS):
            c0 = g * gd
            mixed = _bdot(ws_ref[g], v_ref[r0:r0 + CHUNK, c0:c0 + gd]) + bs_ref[g]
            gated_ref[r0:r0 + CHUNK, c0:c0 + gd] = (
                u_ref[r0:r0 + CHUNK, c0:c0 + gd].astype(F32) * mixed).astype(BF16)
    y = _bdot(gated_ref[...], wo_ref[...])
    _post_mixer(x_ref[...], y, gm_ref[0], lng_ref[0], lnb_ref[0], scf_ref[0], shf_ref[0],
                wr_ref, rb_ref, run_ref, x1_ref, f_ref, ids_ref, wts_ref, rank_ref, cnt_ref, alpha)


def _gmlp_out(u, v, ws, bs, wo, x2, gm, scf, shf, lng, lnb, wr, rb, seq, layer, alpha):
    t, d = x2.shape
    width = u.shape[1]
    tm = min(256, seq)
    row = lambda i: (i, 0)
    return pl.pallas_call(
        functools.partial(_gmlp_out_kernel, alpha=alpha),
        out_shape=_route_out_shapes(t, d),
        grid=(t // tm,),
        in_specs=[pl.BlockSpec((tm, width), row), pl.BlockSpec((tm, width), row),
                  pl.BlockSpec(ws.shape, lambda i: (0, 0, 0)), pl.BlockSpec(bs.shape, lambda i: (0, 0, 0)),
                  pl.BlockSpec(wo.shape, lambda i: (0, 0))]
        + _route_in_specs(tm, d, seq // tm, layer),
        out_specs=_route_out_specs(tm, d),
        scratch_shapes=[pltpu.VMEM((N_EXPERTS, LANES), F32), pltpu.VMEM((tm, width), BF16)],
        compiler_params=_params("arbitrary"),
        name="gmlp_out_route",
    )(u, v, ws, bs, wo, x2, gm, scf, shf, lng, lnb, wr, rb)


SC_CORES, SC_SUBCORES, SC_LANES = 2, 16, 16
MXU_N = 256


def _shared_kernel(f_ref, ws1_ref, ws3_ref, ws2_ref, sh_ref):
    h = _load_rows_bf16(f_ref, sh_ref.shape[0])
    a = _silu(_bdot(h, ws1_ref[...])) * _bdot(h, ws3_ref[...])
    sh_ref[...] = _bdot(a.astype(BF16), ws2_ref[...]).astype(BF16)


def _shared_expert(f, ws1, ws3, ws2, t):
    sub = f.shape[0] // t
    d = ws1.shape[0]
    tm = min(512, t)
    fix = lambda i: (0, 0)
    return pl.pallas_call(
        _shared_kernel,
        out_shape=jax.ShapeDtypeStruct((t, d), BF16),
        grid=(t // tm,),
        in_specs=[pl.BlockSpec((tm * sub, LANES), lambda i: (i, 0)),
                  pl.BlockSpec(ws1.shape, fix), pl.BlockSpec(ws3.shape, fix), pl.BlockSpec(ws2.shape, fix)],
        out_specs=pl.BlockSpec((tm, d), lambda i: (i, 0)),
        compiler_params=_params("parallel"),
        name="moe_shared",
    )(f, ws1, ws3, ws2)


def _inverse_map(slots_flat, nslot):
    n = slots_flat.shape[0]
    workers = SC_CORES * SC_SUBCORES
    per_w = nslot // workers
    chunk = min(8192, n)
    assert nslot % (workers * SC_LANES) == 0 and n % chunk == 0 and chunk % SC_LANES == 0
    mesh = plsc.VectorSubcoreMesh(core_axis_name="c", subcore_axis_name="s")

    @functools.partial(
        pl.kernel, out_type=jax.ShapeDtypeStruct((nslot,), I32), mesh=mesh,
        scratch_types=[pltpu.VMEM((per_w,), I32), pltpu.VMEM((chunk,), I32)],
        compiler_params=pltpu.CompilerParams(needs_layout_passes=False))
    def inverse(slots_hbm, inv_hbm, local, buf):
        lo = (lax.axis_index("s") * SC_CORES + lax.axis_index("c")) * per_w
        unused = jnp.full((SC_LANES,), -1, I32)

        @pl.loop(0, per_w, step=SC_LANES)
        def _(j):
            local[pl.ds(j, SC_LANES)] = unused

        lane = lax.iota(I32, SC_LANES)

        @pl.loop(0, n, step=chunk)
        def _(c0):
            pltpu.sync_copy(slots_hbm.at[pl.ds(c0, chunk)], buf)

            @pl.loop(0, chunk, step=SC_LANES)
            def _(j):
                idx = buf[pl.ds(j, SC_LANES)] - lo
                mine = jnp.logical_and(idx >= 0, idx < per_w)
                plsc.store_scatter(local, [idx], c0 + j + lane, mask=mine)

        pltpu.sync_copy(local, inv_hbm.at[pl.ds(lo, per_w)])

    return inverse(slots_flat)


EXPERT_RING = 3


def _experts_kernel(texp_ref, nstep_ref, tok_ref, w1_ref, w3_ref, w2_ref, f_hbm, y_ref,
                    w1b, w3b, w2b, xa, xb, xc_, gsem, *, tme):
    i = pl.program_id(0)
    last = nstep_ref[0] - 1
    ring = (xa, xb, xc_)
    sub = xa.shape[0] // tme
    d, fdim = w1b.shape
    half = d // 2

    @pl.when(i == 0)
    def _():
        xb[...] = jnp.zeros_like(xb)
        xc_[...] = jnp.zeros_like(xc_)

    @pl.when(jnp.logical_and(i <= last,
                             jnp.logical_or(i == 0, texp_ref[i] != texp_ref[jnp.maximum(i - 1, 0)])))
    def _():
        w1b[...] = w1_ref[...].astype(BF16)
        w3b[...] = w3_ref[...].astype(BF16)
        w2b[...] = w2_ref[...].astype(BF16)

    def whole_tile(buf, sem):
        return pltpu.make_async_copy(f_hbm.at[pl.ds(0, tme * sub), :], buf, sem)

    def step(par):
        nxt, prv = (par + 1) % EXPERT_RING, (par + 2) % EXPERT_RING
        xg, xc = ring[par], ring[nxt]

        @pl.when(i >= EXPERT_RING - 1)
        def _():
            whole_tile(xc, gsem.at[nxt]).wait()

        pw1, pw2 = min(MXU_N, fdim), min(MXU_N, half)
        pieces = 2 * (fdim // pw1) + 2 * (half // pw2)
        per_piece = -(-tme // max(pieces - 2, 1))
        issued = [0]

        def issue_rows():
            for r in range(issued[0], min(issued[0] + per_piece, tme)):
                tok = pl.multiple_of(tok_ref[r] * sub, sub)
                pltpu.make_async_copy(f_hbm.at[pl.ds(tok, sub), :], xg.at[pl.ds(r * sub, sub), :],
                                      gsem.at[par]).start(priority=r % 2)
            issued[0] = min(issued[0] + per_piece, tme)

        h = _load_rows_bf16(xc, tme)
        gate, up = [], []
        for c0 in range(0, fdim, pw1):
            gate.append(_bdot(h, w1b[:, c0:c0 + pw1]))
            issue_rows()
        for c0 in range(0, fdim, pw1):
            up.append(_bdot(h, w3b[:, c0:c0 + pw1]))
            issue_rows()
        a = (_silu(jnp.concatenate(gate, axis=1)) * jnp.concatenate(up, axis=1)).astype(BF16)
        for c0 in range(0, half, pw2):
            lo = _bdot(a, w2b[:, c0:c0 + pw2])
            issue_rows()
            hi = _bdot(a, w2b[:, half + c0:half + c0 + pw2])
            issue_rows()
            packed = _pack_pairs(jnp.concatenate([lo, hi], axis=1))
            for b0 in range(0, pw2, LANES):
                y_ref[pl.ds((c0 + b0) // LANES, tme, stride=sub), :] = packed[:, b0:b0 + LANES]
        assert issued[0] == tme

        @pl.when(i == last)
        def _():
            whole_tile(ring[prv], gsem.at[prv]).wait()
            whole_tile(xg, gsem.at[par]).wait()

    for par in range(EXPERT_RING):
        pl.when(jnp.logical_and(i <= last, lax.rem(i, EXPERT_RING) == par))(functools.partial(step, par))


def _experts(texp, nstep, tok, w1, w3, w2, f, layer, tme):
    d, fdim = w1.shape[-2:]
    sub = d // (2 * LANES)
    lag = EXPERT_RING - 1
    ntile = texp.shape[0] - lag
    wsel = lambda i, te, ns: (layer, te[i], 0, 0)
    return pl.pallas_call(
        functools.partial(_experts_kernel, tme=tme),
        out_shape=jax.ShapeDtypeStruct((ntile * tme * sub, LANES), U32),
        grid_spec=pltpu.PrefetchScalarGridSpec(
            num_scalar_prefetch=2,
            grid=(ntile + lag,),
            in_specs=[pl.BlockSpec((tme,), lambda i, te, ns: (i,), memory_space=pltpu.SMEM),
                      pl.BlockSpec((None, None, d, fdim), wsel), pl.BlockSpec((None, None, d, fdim), wsel),
                      pl.BlockSpec((None, None, fdim, d), wsel),
                      pl.BlockSpec(memory_space=pl.ANY)],
            out_specs=pl.BlockSpec((tme * sub, LANES),
                                   lambda i, te, ns: (jnp.clip(i - lag, 0, ns[0] - lag - 1), 0)),
            scratch_shapes=[pltpu.VMEM((d, fdim), BF16), pltpu.VMEM((d, fdim), BF16), pltpu.VMEM((fdim, d), BF16)]
            + [pltpu.VMEM((tme * sub, LANES), U32)] * EXPERT_RING
            + [pltpu.SemaphoreType.DMA((EXPERT_RING,))]),
        compiler_params=_params("arbitrary"),
        name="moe_experts",
    )(texp, nstep, tok, w1, w3, w2, f)


def _combine_kernel(slot_ref, wt_ref, x1_ref, sh_ref, gf_ref, lng_ref, lnb_ref, ys_hbm, o_ref, ga, gb, gsem, *, alpha):
    i = pl.program_id(0)
    last = pl.num_programs(0) - 1
    tm = x1_ref.shape[0]
    sub = ga.shape[1] // tm

    @pl.when(i == 0)
    def _():
        gb[...] = jnp.zeros_like(gb)

    def whole_plane(buf, k, sem):
        return pltpu.make_async_copy(ys_hbm.at[pl.ds(0, tm * sub), :], buf.at[k], sem)

    def step(par):
        gg, gc = (ga, gb) if par == 0 else (gb, ga)

        @pl.when(i > 0)
        def _():
            for k in range(TOP_K):
                whole_plane(gc, k, gsem.at[1 - par]).wait()

        def issue(t0, t1):
            for tt in range(t0, t1):
                for k in range(TOP_K):
                    row = pl.multiple_of(slot_ref[tt * TOP_K + k] * sub, sub)
                    pltpu.make_async_copy(ys_hbm.at[pl.ds(row, sub), :], gg.at[k, pl.ds(tt * sub, sub), :],
                                          gsem.at[par]).start(priority=k % 2)

        lo = hi = None
        for k in range(TOP_K):
            issue(k * tm // TOP_K, (k + 1) * tm // TOP_K)
            w = wt_ref[:, k:k + 1]
            rl, rh = _load_row_tiles(gc.at[k], tm)
            lo = [w * b for b in rl] if lo is None else [a + w * b for a, b in zip(lo, rl)]
            hi = [w * b for b in rh] if hi is None else [a + w * b for a, b in zip(hi, rh)]
        f = jnp.concatenate(lo + hi, axis=1) + sh_ref[...].astype(F32)
        o_ref[...] = _layernorm(alpha * x1_ref[...] + gf_ref[0] * f, lng_ref[0], lnb_ref[0])

        @pl.when(i == last)
        def _():
            for k in range(TOP_K):
                whole_plane(gg, k, gsem.at[par]).wait()

    for par in range(2):
        pl.when(jnp.bitwise_and(i, 1) == par)(functools.partial(step, par))


def _combine(slot_tk, wt, x1, sh, gf, lng, lnb, ys, seq, tm, layer, alpha):
    t, d = x1.shape
    sub = d // (2 * LANES)
    per_b = seq // tm
    nstep = t // tm + 1
    prev = lambda i: jnp.maximum(i - 1, 0)
    row = lambda i: (prev(i), 0)
    lrow = lambda i: (layer, 0, 0)
    return pl.pallas_call(
        functools.partial(_combine_kernel, alpha=alpha),
        out_shape=jax.ShapeDtypeStruct((t, d), F32),
        grid=(nstep,),
        in_specs=[pl.BlockSpec((tm * TOP_K,), lambda i: (i,), memory_space=pltpu.SMEM),
                  pl.BlockSpec((tm, TOP_K), row), pl.BlockSpec((tm, d), row), pl.BlockSpec((tm, d), row),
                  pl.BlockSpec((1, 1, d), lambda i: (prev(i) // per_b, 0, 0)),
                  pl.BlockSpec((1, 1, d), lrow), pl.BlockSpec((1, 1, d), lrow),
                  pl.BlockSpec(memory_space=pl.ANY)],
        out_specs=pl.BlockSpec((tm, d), row),
        scratch_shapes=[pltpu.VMEM((TOP_K, tm * sub, LANES), U32), pltpu.VMEM((TOP_K, tm * sub, LANES), U32),
                        pltpu.SemaphoreType.DMA((2,))],
        compiler_params=_params("arbitrary"),
        name="moe_combine_ln",
    )(slot_tk, wt, x1, sh, gf, lng, lnb, ys)


def _moe(f, x1, ids, wts, rank, cnt, gf, lng, lnb, w1, w3, w2, ws1, ws3, ws2, seq, layer, alpha):
    t, d = x1.shape
    tme = min(512, t)
    counts = cnt[:, 0].astype(I32)
    tiles_e = (counts + tme - 1) // tme
    padded = tiles_e * tme
    base = jnp.cumsum(padded) - padded
    expert = jnp.arange(N_EXPERTS, dtype=I32)
    slots = jnp.sum(jnp.where(ids[:, :, None] == expert, base, 0), axis=-1) + rank
    ntile = (t * TOP_K) // tme + N_EXPERTS
    nslot = ntile * tme
    tile_ends = jnp.cumsum(tiles_e)
    texp = jnp.sum((jnp.arange(ntile, dtype=I32)[:, None] >= tile_ends[None, :]).astype(I32), axis=1)
    texp = jnp.minimum(texp, N_EXPERTS - 1)
    lag = EXPERT_RING - 1
    texp_step = jnp.concatenate([texp[:1]] * lag + [texp])
    nstep = tile_ends[-1:] + lag
    idle = jnp.arange(ntile + lag, dtype=I32) >= nstep
    texp_step = jnp.where(idle, jnp.take(texp_step, nstep - 1), texp_step)

    inv = _inverse_map(slots.reshape(-1), nslot)
    tok = jnp.where(inv >= 0, inv % t, 0)
    tok = jnp.concatenate([tok, jnp.zeros((lag * tme,), I32)])

    shared = _shared_expert(f, ws1, ws3, ws2, t)
    ys = _experts(texp_step, nstep, tok, w1, w3, w2, f, layer, tme)
    tmc = min(128, seq)
    slot_tk = jnp.concatenate([slots.T.reshape(-1), jnp.zeros((tmc * TOP_K,), I32)])
    return _combine(slot_tk, wts.T, x1, shared, gf, lng, lnb, ys, seq, tmc, layer, alpha)


def kernel(x, c, ctx, c_ctx, w_ada, b_ada, ln_mix_g, ln_mix_b, ln_ffn_g, ln_ffn_b, attn_w_qkv, attn_w_o, attn_sink, gmlp_w_in, gmlp_b_in, gmlp_v_g, gmlp_v_b, gmlp_w_s, gmlp_b_s, gmlp_w_o, moe_w_router, moe_bias, moe_w1, moe_w3, moe_w2, moe_ws1, moe_ws3, moe_ws2):
    batch, seq, d = x.shape
    nctx = ctx.shape[1]
    depth = w_ada.shape[0]
    assert depth == 2, "layer 0 is the attention mixer, layer 1 the gMLP mixer"
    alpha = float((2 * depth) ** 0.25)
    t = batch * seq

    pad = (-(batch + 1)) % 8
    cc = jnp.concatenate([c, c_ctx[None, :], jnp.zeros((pad, d), F32)], axis=0)
    mods = _modulations(cc, w_ada, b_ada)

    def mod(layer, j, rows=slice(0, batch)):
        return mods[layer, rows, j * d:(j + 1) * d]

    def per_batch(layer, j):
        return mod(layer, j).reshape(batch, 1, d)

    def router(layer):
        w = jnp.pad(moe_w_router[layer], ((0, 0), (0, LANES - N_EXPERTS)))
        hi = w.astype(BF16)
        lo = (w - hi.astype(F32)).astype(BF16)
        return jnp.concatenate([hi, lo], axis=1), moe_bias[layer].reshape(N_EXPERTS, 1)

    def shared_w(layer):
        return moe_ws1[layer].astype(BF16), moe_ws3[layer].astype(BF16), moe_ws2[layer].astype(BF16)

    x2 = x.reshape(t, d)
    ln_mix_g, ln_mix_b, ln_ffn_g, ln_ffn_b = (
        p.reshape(depth, 1, d) for p in (ln_mix_g, ln_mix_b, ln_ffn_g, ln_ffn_b))

    qd = Q_PER_KV * N_KV_HEADS * HEAD_DIM
    kvd = N_KV_HEADS * HEAD_DIM
    wqkv = attn_w_qkv[0]
    dup = lambda w: jnp.concatenate([w.reshape(d, N_KV_HEADS, 1, HEAD_DIM)] * (LANES // HEAD_DIM), axis=2
                                    ).reshape(d, N_KV_HEADS * LANES)
    wk2, wv2 = dup(wqkv[:, qd:qd + kvd]), dup(wqkv[:, qd + kvd:])
    w_all = jnp.concatenate([wqkv[:, :qd] * (HEAD_DIM ** -0.5), wk2, wv2], axis=1).astype(BF16)
    q, k2, v2 = _qkv_proj(x2, per_batch(0, 1), per_batch(0, 0), w_all, _rope_tables(seq), seq)
    ctx_row = slice(batch, batch + 1)
    kc2, vc2 = _ctx_kv(ctx.reshape(batch * nctx, d), mod(0, 1, ctx_row), mod(0, 0, ctx_row),
                       jnp.concatenate([wk2, wv2], axis=1).astype(BF16))
    o = _attention(q, k2, v2, kc2, vc2, attn_sink[0], batch, seq, nctx)
    x1, f, ids, wts, rank, cnt = _attn_out(
        o, attn_w_o[0].astype(BF16), x2, per_batch(0, 2), per_batch(0, 4), per_batch(0, 3),
        ln_mix_g, ln_mix_b, *router(0), seq, 0, alpha)
    x2 = _moe(f, x1, ids, wts, rank, cnt, per_batch(0, 5), ln_ffn_g, ln_ffn_b,
              moe_w1, moe_w3, moe_w2, *shared_w(0), seq, 0, alpha)

    width = gmlp_w_in.shape[2] // 2
    w_in = gmlp_w_in[0].astype(BF16)
    b_in = gmlp_b_in[0].reshape(1, 2 * width)
    vg, vb = gmlp_v_g[0].reshape(1, width), gmlp_v_b[0].reshape(1, width)
    sc, sh = per_batch(1, 1), per_batch(1, 0)
    u = _gmlp_in(x2, sc, sh, w_in[:, :width], b_in[:, :width], vg, vb, seq, False)
    v = _gmlp_in(x2, sc, sh, w_in[:, width:], b_in[:, width:], vg, vb, seq, True)
    x1, f, ids, wts, rank, cnt = _gmlp_out(
        u, v, gmlp_w_s[0].astype(BF16), gmlp_b_s[0][:, :, None], gmlp_w_o[0].astype(BF16), x2,
        per_batch(1, 2), per_batch(1, 4), per_batch(1, 3), ln_mix_g, ln_mix_b, *router(1), seq, 1, alpha)
    x2 = _moe(f, x1, ids, wts, rank, cnt, per_batch(1, 5), ln_ffn_g, ln_ffn_b,
              moe_w1, moe_w3, moe_w2, *shared_w(1), seq, 1, alpha)
    return x2.reshape(batch, seq, d)
```

```python
import functools

import jax
import jax.numpy as jnp
from jax import lax
from jax.experimental import pallas as pl
from jax.experimental.pallas import tpu as pltpu
from jax.experimental.pallas import tpu_sc as plsc

F32 = jnp.float32
BF16 = jnp.bfloat16
I32 = jnp.int32

GRID_W = 64
N_KV_HEADS = 4
Q_PER_KV = 8
HEAD_DIM = 64
ROPE_HALF = 16
WINDOW = 128
Q_BLOCK = 128
ROPE_BASE = 10000.0
NEG_INF = -1e30
CHUNK = 128
N_GMLP_GROUPS = 8
N_EXPERTS = 64
TOP_K = 8
N_EXPERT_GROUPS = 8
EXPERTS_PER_GROUP = 8
TOPK_GROUPS = 4
ROUTED_SCALE = 2.5
LN_EPS = 1e-5

LANES = 128
VMEM_LIMIT_BYTES = 56 * 1024 * 1024


def _params(*sem):
    return pltpu.CompilerParams(dimension_semantics=sem, vmem_limit_bytes=VMEM_LIMIT_BYTES)


def _silu(a):
    return a * jax.nn.sigmoid(a)


def _layernorm(v, g, b):
    mu = jnp.mean(v, axis=-1, keepdims=True)
    d = v - mu
    var = jnp.mean(d * d, axis=-1, keepdims=True)
    return d * lax.rsqrt(var + LN_EPS) * g + b


def _bdot(a, b):
    return jnp.dot(a, b, preferred_element_type=F32)


U32 = jnp.uint32


def _pack_pairs(a):
    n = a.shape[1] // 2
    lo = lax.bitcast_convert_type(a[:, :n].astype(BF16).astype(F32), U32)
    hi = lax.bitcast_convert_type(a[:, n:].astype(BF16).astype(F32), U32)
    return hi | (lo >> 16)


def _unpack_pairs(p):
    lo = lax.bitcast_convert_type(p << 16, F32)
    hi = lax.bitcast_convert_type(p & jnp.uint32(0xFFFF0000), F32)
    return lo, hi


def _store_row_tiles(ref, packed):
    r, n = packed.shape
    sub = n // LANES
    for j in range(sub):
        ref[pl.ds(j, r, stride=sub), :] = packed[:, LANES * j:LANES * (j + 1)]


def _load_row_tiles(ref, r):
    sub = ref.shape[0] // r
    pieces = [_unpack_pairs(ref[pl.ds(j, r, stride=sub), :]) for j in range(sub)]
    return [p[0] for p in pieces], [p[1] for p in pieces]


def _load_rows_bf16(ref, r):
    lo, hi = _load_row_tiles(ref, r)
    return jnp.concatenate(lo + hi, axis=1).astype(BF16)


def _mod_kernel(c_ref, w_ref, b_ref, o_ref):
    a = _silu(c_ref[...]).astype(BF16)
    o_ref[0] = _bdot(a, w_ref[0].astype(BF16)) + b_ref[0]


def _modulations(cc, w_ada, b_ada):
    depth, d, n6 = w_ada.shape
    r = cc.shape[0]
    tn = max(w for w in range(LANES, min(1024, n6) + 1, LANES) if n6 % w == 0)
    return pl.pallas_call(
        _mod_kernel,
        out_shape=jax.ShapeDtypeStruct((depth, r, n6), F32),
        grid=(depth, n6 // tn),
        in_specs=[pl.BlockSpec((r, d), lambda l, j: (0, 0)),
                  pl.BlockSpec((1, d, tn), lambda l, j: (l, 0, j)),
                  pl.BlockSpec((1, 1, tn), lambda l, j: (l, 0, j))],
        out_specs=pl.BlockSpec((1, r, tn), lambda l, j: (l, 0, j)),
        compiler_params=_params("parallel", "parallel"),
        name="adaln_mod",
    )(cc, w_ada, b_ada.reshape(depth, 1, n6))


def _rope_tables(seq):
    rows = seq // GRID_W
    row_ids = jnp.repeat(jnp.arange(rows, dtype=F32), GRID_W)
    col_ids = jnp.tile(jnp.arange(GRID_W, dtype=F32), rows)
    inv_freq = ROPE_BASE ** (-jnp.arange(0, 2 * ROPE_HALF, 2, dtype=F32) / (2 * ROPE_HALF))
    ar, ac = row_ids[:, None] * inv_freq, col_ids[:, None] * inv_freq
    z = jnp.zeros_like(ar)
    cos = jnp.concatenate([jnp.cos(ar), jnp.cos(ar), jnp.cos(ac), jnp.cos(ac)], axis=-1)
    sin_lo = jnp.concatenate([-jnp.sin(ar), z, -jnp.sin(ac), z], axis=-1)
    sin_hi = jnp.concatenate([z, jnp.sin(ar), z, jnp.sin(ac)], axis=-1)
    rep = LANES // HEAD_DIM
    return tuple(jnp.tile(t, (1, rep)) for t in (cos, sin_lo, sin_hi))


def _qkv_kernel(x_ref, sc_ref, sh_ref, w_ref, cos_ref, slo_ref, shi_ref, q_ref, k_ref, v_ref, *, cw):
    h = (x_ref[...] * (1.0 + sc_ref[0]) + sh_ref[0]).astype(BF16)
    cos, slo, shi = cos_ref[...], slo_ref[...], shi_ref[...]

    def rope(a):
        return (a * cos + pltpu.roll(a, LANES - ROPE_HALF, 1) * slo
                + pltpu.roll(a, ROPE_HALF, 1) * shi)

    qd, kd = q_ref.shape[1], k_ref.shape[1]
    for c0 in range(0, qd, cw):
        acc = _bdot(h, w_ref[:, c0:c0 + cw])
        for b0 in range(0, cw, LANES):
            q_ref[:, c0 + b0:c0 + b0 + LANES] = rope(acc[:, b0:b0 + LANES]).astype(BF16)
    acc = _bdot(h, w_ref[:, qd:qd + kd])
    for b0 in range(0, kd, LANES):
        k_ref[:, b0:b0 + LANES] = rope(acc[:, b0:b0 + LANES]).astype(BF16)
    v_ref[...] = _bdot(h, w_ref[:, qd + kd:qd + 2 * kd]).astype(BF16)


def _qkv_proj(x2, sc, sh, w, tables, seq):
    t, d = x2.shape
    kd = N_KV_HEADS * LANES
    qd = w.shape[1] - 2 * kd
    tm = min(512, seq)
    per_b = seq // tm
    row = lambda i: (i, 0)
    mod = lambda i: (i // per_b, 0, 0)
    tab = lambda i: (i % per_b, 0)
    return pl.pallas_call(
        functools.partial(_qkv_kernel, cw=min(512, qd)),
        out_shape=(jax.ShapeDtypeStruct((t, qd), BF16), jax.ShapeDtypeStruct((t, kd), BF16),
                   jax.ShapeDtypeStruct((t, kd), BF16)),
        grid=(t // tm,),
        in_specs=[pl.BlockSpec((tm, d), row), pl.BlockSpec((1, 1, d), mod), pl.BlockSpec((1, 1, d), mod),
                  pl.BlockSpec(w.shape, lambda i: (0, 0)),
                  pl.BlockSpec((tm, LANES), tab), pl.BlockSpec((tm, LANES), tab), pl.BlockSpec((tm, LANES), tab)],
        out_specs=(pl.BlockSpec((tm, qd), row), pl.BlockSpec((tm, kd), row), pl.BlockSpec((tm, kd), row)),
        compiler_params=_params("parallel"),
        name="qkv_rope",
    )(x2, sc, sh, w, *tables)


def _ctxkv_kernel(x_ref, sc_ref, sh_ref, w_ref, k_ref, v_ref):
    h = (x_ref[...] * (1.0 + sc_ref[...]) + sh_ref[...]).astype(BF16)
    kd = k_ref.shape[1]
    k_ref[...] = _bdot(h, w_ref[:, :kd]).astype(BF16)
    v_ref[...] = _bdot(h, w_ref[:, kd:]).astype(BF16)


def _ctx_kv(c2, sc, sh, w):
    t, d = c2.shape
    kd = w.shape[1] // 2
    tm = min(512, t)
    row = lambda i: (i, 0)
    fix = lambda i: (0, 0)
    return pl.pallas_call(
        _ctxkv_kernel,
        out_shape=(jax.ShapeDtypeStruct((t, kd), BF16), jax.ShapeDtypeStruct((t, kd), BF16)),
        grid=(t // tm,),
        in_specs=[pl.BlockSpec((tm, d), row), pl.BlockSpec((1, d), fix), pl.BlockSpec((1, d), fix),
                  pl.BlockSpec(w.shape, fix)],
        out_specs=(pl.BlockSpec((tm, kd), row), pl.BlockSpec((tm, kd), row)),
        compiler_params=_params("parallel"),
        name="ctx_kv",
    )(c2, sc, sh, w)


def _attn_kernel(sink_ref, q_ref, k_ref, v_ref, kc_ref, vc_ref, o_ref):
    for b in range(q_ref.shape[1] // Q_BLOCK):
        _attn_block(sink_ref, q_ref, k_ref, v_ref, kc_ref, vc_ref, o_ref, b)


def _attn_block(sink_ref, q_ref, k_ref, v_ref, kc_ref, vc_ref, o_ref, b):
    kv = pl.program_id(1)
    i = pl.program_id(2) * (q_ref.shape[1] // Q_BLOCK) + b
    seq, nctx = k_ref.shape[1], kc_ref.shape[1]
    nwin = Q_BLOCK + 2 * WINDOW
    start = pl.multiple_of(jnp.clip(i * Q_BLOCK - WINDOW, 0, seq - nwin), Q_BLOCK)
    nkey = nctx + nwin
    kall = jnp.concatenate([kc_ref[0], k_ref[0, pl.ds(start, nwin), :]], axis=0)
    vall = jnp.concatenate([vc_ref[0], v_ref[0, pl.ds(start, nwin), :]], axis=0)
    low = lax.broadcasted_iota(I32, (nkey, LANES), 1) < HEAD_DIM
    zero = jnp.zeros((nkey, LANES), BF16)
    npair = Q_PER_KV // 2
    q = q_ref[0, Q_BLOCK * b:Q_BLOCK * (b + 1), :]
    qs = jnp.concatenate([q[:, LANES * j:LANES * (j + 1)] for j in range(npair)], axis=0)
    nrow = npair * Q_BLOCK
    rows = lax.broadcasted_iota(I32, (Q_BLOCK, nkey), 0)
    cols = lax.broadcasted_iota(I32, (Q_BLOCK, nkey), 1)
    dist = (i * Q_BLOCK - start + nctx) + rows - cols
    valid = (cols < nctx) | (jnp.abs(dist) <= WINDOW)
    contract_last = (((1,), (1,)), ((), ()))
    out = jnp.zeros((nrow, LANES), F32)
    for par in range(2):
        keep = low if par == 0 else jnp.logical_not(low)
        kh = jnp.where(keep, kall, zero)
        vh = jnp.where(keep, vall, zero)
        s = lax.dot_general(qs, kh, contract_last, preferred_element_type=F32)
        s = jnp.concatenate(
            [jnp.where(valid, s[Q_BLOCK * j:Q_BLOCK * (j + 1), :], NEG_INF) for j in range(npair)], axis=0)
        sink = jnp.concatenate(
            [jnp.full((Q_BLOCK, 1), sink_ref[kv * Q_PER_KV + 2 * j + par], F32) for j in range(npair)], axis=0)
        m = jnp.maximum(jnp.max(s, axis=1, keepdims=True), sink)
        p = jnp.exp(s - m)
        den = jnp.sum(p, axis=1, keepdims=True) + jnp.exp(sink - m)
        out = out + _bdot(p.astype(BF16), vh) / den
    for j in range(npair):
        o_ref[0, Q_BLOCK * b:Q_BLOCK * (b + 1), LANES * j:LANES * (j + 1)] = (
            out[Q_BLOCK * j:Q_BLOCK * (j + 1), :].astype(BF16))


def _attention(q, k2, v2, kc2, vc2, sink, batch, seq, nctx):
    qd = q.shape[1]
    gw = Q_PER_KV * HEAD_DIM
    q3 = q.reshape(batch, seq, qd)
    k3, v3 = k2.reshape(batch, seq, -1), v2.reshape(batch, seq, -1)
    kc3, vc3 = kc2.reshape(batch, nctx, -1), vc2.reshape(batch, nctx, -1)
    full = lambda b, h, i: (b, 0, h)
    blk = lambda b, h, i: (b, i, h)
    rows = next(r * Q_BLOCK for r in (4, 2, 1) if seq % (r * Q_BLOCK) == 0)
    out = pl.pallas_call(
        _attn_kernel,
        out_shape=jax.ShapeDtypeStruct((batch, seq, qd), BF16),
        grid=(batch, N_KV_HEADS, seq // rows),
        in_specs=[pl.BlockSpec(memory_space=pltpu.SMEM),
                  pl.BlockSpec((1, rows, gw), blk),
                  pl.BlockSpec((1, seq, LANES), full), pl.BlockSpec((1, seq, LANES), full),
                  pl.BlockSpec((1, nctx, LANES), full), pl.BlockSpec((1, nctx, LANES), full)],
        out_specs=pl.BlockSpec((1, rows, gw), blk),
        compiler_params=_params("parallel", "parallel", "parallel"),
        name="window_gqa",
    )(sink, q3, k3, v3, kc3, vc3)
    return out.reshape(batch * seq, qd)


def _split_bf16(a):
    hi = a.astype(BF16)
    return hi, (a - hi.astype(F32)).astype(BF16)


def _post_mixer(x, y, gm, lng, lnb, scf, shf, wr_ref, rb_ref, run_ref,
                x1_ref, f_ref, ids_ref, wts_ref, rank_ref, cnt_ref, alpha):
    tm = x.shape[0]
    x1 = _layernorm(alpha * x + gm * y, lng, lnb)
    x1_ref[...] = x1
    f = x1 * (1.0 + scf) + shf
    _store_row_tiles(f_ref, _pack_pairs(f))

    fh, fl = _split_bf16(f)
    prod = _bdot(fh, wr_ref[...]) + _bdot(fl, wr_ref[...])
    logits = (prod[:, :LANES] + prod[:, LANES:]).T[:N_EXPERTS, :]
    scores = jax.nn.sigmoid(logits)
    biased = scores + rb_ref[...]

    eg = EXPERTS_PER_GROUP
    sub = lax.broadcasted_iota(I32, (eg, tm), 0)
    gscore = []
    for g in range(N_EXPERT_GROUPS):
        tg = biased[eg * g:eg * (g + 1), :]
        m1 = jnp.max(tg, axis=0, keepdims=True)
        i1 = jnp.min(jnp.where(tg == m1, sub, eg), axis=0, keepdims=True)
        m2 = jnp.max(jnp.where(sub == i1, -jnp.inf, tg), axis=0, keepdims=True)
        gscore.append(m1 + m2)
    gsel = [jnp.zeros((1, tm), jnp.bool_) for _ in range(N_EXPERT_GROUPS)]
    for _ in range(TOPK_GROUPS):
        best = functools.reduce(jnp.maximum, gscore)
        taken = jnp.zeros((1, tm), jnp.bool_)
        for g in range(N_EXPERT_GROUPS):
            hit = jnp.logical_and(gscore[g] == best, jnp.logical_not(taken))
            taken = jnp.logical_or(taken, hit)
            gsel[g] = jnp.logical_or(gsel[g], hit)
            gscore[g] = jnp.where(hit, -jnp.inf, gscore[g])
    cur = jnp.concatenate(
        [jnp.where(gsel[g], biased[eg * g:eg * (g + 1), :], -jnp.inf) for g in range(N_EXPERT_GROUPS)], axis=0)

    eidx = lax.broadcasted_iota(I32, (N_EXPERTS, tm), 0)
    picks, wts, hots = [], [], []
    for _ in range(TOP_K):
        m = jnp.max(cur, axis=0, keepdims=True)
        idx = jnp.min(jnp.where(cur == m, eidx, N_EXPERTS), axis=0, keepdims=True)
        hot = eidx == idx
        picks.append(idx)
        wts.append(jnp.sum(jnp.where(hot, scores, 0.0), axis=0, keepdims=True))
        hots.append(hot)
        cur = jnp.where(hot, -jnp.inf, cur)
    wsum = functools.reduce(jnp.add, wts)

    assigned = functools.reduce(jnp.add, [h.astype(F32) for h in hots])
    before = (lax.broadcasted_iota(I32, (tm, tm), 0) < lax.broadcasted_iota(I32, (tm, tm), 1)).astype(BF16)
    pos = _bdot(assigned.astype(BF16), before) + run_ref[:, 0:1]
    for k in range(TOP_K):
        ids_ref[k:k + 1, :] = picks[k]
        wts_ref[k:k + 1, :] = wts[k] / wsum * ROUTED_SCALE
        rank_ref[k:k + 1, :] = jnp.sum(jnp.where(hots[k], pos, 0.0), axis=0, keepdims=True).astype(I32)
    run_ref[...] = run_ref[...] + jnp.sum(assigned, axis=1, keepdims=True)
    cnt_ref[...] = run_ref[...]


def _route_out_shapes(t, d):
    return (jax.ShapeDtypeStruct((t, d), F32), jax.ShapeDtypeStruct((t * d // (2 * LANES), LANES), U32),
            jax.ShapeDtypeStruct((TOP_K, t), I32), jax.ShapeDtypeStruct((TOP_K, t), F32),
            jax.ShapeDtypeStruct((TOP_K, t), I32), jax.ShapeDtypeStruct((N_EXPERTS, LANES), F32))


def _route_out_specs(tm, d):
    row = lambda i: (i, 0)
    col = lambda i: (0, i)
    return (pl.BlockSpec((tm, d), row), pl.BlockSpec((tm * d // (2 * LANES), LANES), row),
            pl.BlockSpec((TOP_K, tm), col), pl.BlockSpec((TOP_K, tm), col), pl.BlockSpec((TOP_K, tm), col),
            pl.BlockSpec((N_EXPERTS, LANES), lambda i: (0, 0)))


def _route_in_specs(tm, d, per_b, layer):
    mod = lambda i: (i // per_b, 0, 0)
    lrow = lambda i: (layer, 0, 0)
    fix = lambda i: (0, 0)
    return [pl.BlockSpec((tm, d), lambda i: (i, 0)),
            pl.BlockSpec((1, 1, d), mod), pl.BlockSpec((1, 1, d), mod), pl.BlockSpec((1, 1, d), mod),
            pl.BlockSpec((1, 1, d), lrow), pl.BlockSpec((1, 1, d), lrow),
            pl.BlockSpec((d, 2 * LANES), fix),
            pl.BlockSpec((N_EXPERTS, 1), fix)]


def _oproj_kernel(o_ref, wo_ref, x_ref, gm_ref, scf_ref, shf_ref, lng_ref, lnb_ref, wr_ref, rb_ref,
                  x1_ref, f_ref, ids_ref, wts_ref, rank_ref, cnt_ref, run_ref, *, alpha):
    @pl.when(pl.program_id(0) == 0)
    def _():
        run_ref[...] = jnp.zeros_like(run_ref)

    y = _bdot(o_ref[...], wo_ref[...])
    _post_mixer(x_ref[...], y, gm_ref[0], lng_ref[0], lnb_ref[0], scf_ref[0], shf_ref[0],
                wr_ref, rb_ref, run_ref, x1_ref, f_ref, ids_ref, wts_ref, rank_ref, cnt_ref, alpha)


def _attn_out(o, wo, x2, gm, scf, shf, lng, lnb, wr, rb, seq, layer, alpha):
    t, d = x2.shape
    tm = min(512, seq)
    return pl.pallas_call(
        functools.partial(_oproj_kernel, alpha=alpha),
        out_shape=_route_out_shapes(t, d),
        grid=(t // tm,),
        in_specs=[pl.BlockSpec((tm, o.shape[1]), lambda i: (i, 0)), pl.BlockSpec(wo.shape, lambda i: (0, 0))]
        + _route_in_specs(tm, d, seq // tm, layer),
        out_specs=_route_out_specs(tm, d),
        scratch_shapes=[pltpu.VMEM((N_EXPERTS, LANES), F32)],
        compiler_params=_params("arbitrary"),
        name="attn_out_route",
    )(o, wo, x2, gm, scf, shf, lng, lnb, wr, rb)


def _gmlp_in_kernel(x_ref, sc_ref, sh_ref, w_ref, b_ref, g_ref, beta_ref, o_ref, *, cw, normalize):
    h = (x_ref[...] * (1.0 + sc_ref[0]) + sh_ref[0]).astype(BF16)
    n = o_ref.shape[1]
    inv_sqrt2 = 0.7071067811865476
    parts = []
    for c0 in range(0, n, cw):
        z = _bdot(h, w_ref[:, c0:c0 + cw]) + b_ref[:, c0:c0 + cw]
        z = 0.5 * z * (1.0 + lax.erf(z * inv_sqrt2))
        if normalize:
            parts.append(z)
        else:
            o_ref[:, c0:c0 + cw] = z.astype(BF16)
    if normalize:
        v = jnp.concatenate(parts, axis=1)
        o_ref[...] = _layernorm(v, g_ref[...], beta_ref[...]).astype(BF16)


def _gmlp_in(x2, sc, sh, w, b, g, beta, seq, normalize):
    t, d = x2.shape
    n = w.shape[1]
    tm = min(256, seq)
    per_b = seq // tm
    row = lambda i: (i, 0)
    mod = lambda i: (i // per_b, 0, 0)
    fix = lambda i: (0, 0)
    return pl.pallas_call(
        functools.partial(_gmlp_in_kernel, cw=min(512, n), normalize=normalize),
        out_shape=jax.ShapeDtypeStruct((t, n), BF16),
        grid=(t // tm,),
        in_specs=[pl.BlockSpec((tm, d), row), pl.BlockSpec((1, 1, d), mod), pl.BlockSpec((1, 1, d), mod),
                  pl.BlockSpec(w.shape, fix), pl.BlockSpec((1, n), fix), pl.BlockSpec((1, n), fix),
                  pl.BlockSpec((1, n), fix)],
        out_specs=pl.BlockSpec((tm, n), row),
        compiler_params=_params("parallel"),
        name="gmlp_in_v" if normalize else "gmlp_in_u",
    )(x2, sc, sh, w, b, g, beta)


def _gmlp_out_kernel(u_ref, v_ref, ws_ref, bs_ref, wo_ref, x_ref, gm_ref, scf_ref, shf_ref, lng_ref, lnb_ref,
                     wr_ref, rb_ref, x1_ref, f_ref, ids_ref, wts_ref, rank_ref, cnt_ref,
                     run_ref, gated_ref, *, alpha):
    @pl.when(pl.program_id(0) == 0)
    def _():
        run_ref[...] = jnp.zeros_like(run_ref)

    tm, width = u_ref.shape
    gd = width // N_GMLP_GROUPS
    for r0 in range(0, tm, CHUNK):
        for g in range(N_GMLP_GROUPS):
            c0 = g * gd
            mixed = _bdot(ws_ref[g], v_ref[r0:r0 + CHUNK, c0:c0 + gd]) + bs_ref[g]
            gated_ref[r0:r0 + CHUNK, c0:c0 + gd] = (
                u_ref[r0:r0 + CHUNK, c0:c0 + gd].astype(F32) * mixed).astype(BF16)
    y = _bdot(gated_ref[...], wo_ref[...])
    _post_mixer(x_ref[...], y, gm_ref[0], lng_ref[0], lnb_ref[0], scf_ref[0], shf_ref[0],
                wr_ref, rb_ref, run_ref, x1_ref, f_ref, ids_ref, wts_ref, rank_ref, cnt_ref, alpha)


def _gmlp_out(u, v, ws, bs, wo, x2, gm, scf, shf, lng, lnb, wr, rb, seq, layer, alpha):
    t, d = x2.shape
    width = u.shape[1]
    tm = min(256, seq)
    row = lambda i: (i, 0)
    return pl.pallas_call(
        functools.partial(_gmlp_out_kernel, alpha=alpha),
        out_shape=_route_out_shapes(t, d),
        grid=(t // tm,),
        in_specs=[pl.BlockSpec((tm, width), row), pl.BlockSpec((tm, width), row),
                  pl.BlockSpec(ws.shape, lambda i: (0, 0, 0)), pl.BlockSpec(bs.shape, lambda i: (0, 0, 0)),
                  pl.BlockSpec(wo.shape, lambda i: (0, 0))]
        + _route_in_specs(tm, d, seq // tm, layer),
        out_specs=_route_out_specs(tm, d),
        scratch_shapes=[pltpu.VMEM((N_EXPERTS, LANES), F32), pltpu.VMEM((tm, width), BF16)],
        compiler_params=_params("arbitrary"),
        name="gmlp_out_route",
    )(u, v, ws, bs, wo, x2, gm, scf, shf, lng, lnb, wr, rb)


SC_CORES, SC_SUBCORES, SC_LANES = 2, 16, 16
MXU_N = 256


def _shared_kernel(f_ref, ws1_ref, ws3_ref, ws2_ref, sh_ref):
    h = _load_rows_bf16(f_ref, sh_ref.shape[0])
    a = _silu(_bdot(h, ws1_ref[...])) * _bdot(h, ws3_ref[...])
    sh_ref[...] = _bdot(a.astype(BF16), ws2_ref[...]).astype(BF16)


def _shared_expert(f, ws1, ws3, ws2, t):
    sub = f.shape[0] // t
    d = ws1.shape[0]
    tm = min(512, t)
    fix = lambda i: (0, 0)
    return pl.pallas_call(
        _shared_kernel,
        out_shape=jax.ShapeDtypeStruct((t, d), BF16),
        grid=(t // tm,),
        in_specs=[pl.BlockSpec((tm * sub, LANES), lambda i: (i, 0)),
                  pl.BlockSpec(ws1.shape, fix), pl.BlockSpec(ws3.shape, fix), pl.BlockSpec(ws2.shape, fix)],
        out_specs=pl.BlockSpec((tm, d), lambda i: (i, 0)),
        compiler_params=_params("parallel"),
        name="moe_shared",
    )(f, ws1, ws3, ws2)


def _inverse_map(slots_flat, nslot):
    n = slots_flat.shape[0]
    workers = SC_CORES * SC_SUBCORES
    per_w = nslot // workers
    chunk = min(8192, n)
    assert nslot % (workers * SC_LANES) == 0 and n % chunk == 0 and chunk % SC_LANES == 0
    mesh = plsc.VectorSubcoreMesh(core_axis_name="c", subcore_axis_name="s")

    @functools.partial(
        pl.kernel, out_type=jax.ShapeDtypeStruct((nslot,), I32), mesh=mesh,
        scratch_types=[pltpu.VMEM((per_w,), I32), pltpu.VMEM((chunk,), I32)],
        compiler_params=pltpu.CompilerParams(needs_layout_passes=False))
    def inverse(slots_hbm, inv_hbm, local, buf):
        lo = (lax.axis_index("s") * SC_CORES + lax.axis_index("c")) * per_w
        unused = jnp.full((SC_LANES,), -1, I32)

        @pl.loop(0, per_w, step=SC_LANES)
        def _(j):
            local[pl.ds(j, SC_LANES)] = unused

        lane = lax.iota(I32, SC_LANES)

        @pl.loop(0, n, step=chunk)
        def _(c0):
            pltpu.sync_copy(slots_hbm.at[pl.ds(c0, chunk)], buf)

            @pl.loop(0, chunk, step=SC_LANES)
            def _(j):
                idx = buf[pl.ds(j, SC_LANES)] - lo
                mine = jnp.logical_and(idx >= 0, idx < per_w)
                plsc.store_scatter(local, [idx], c0 + j + lane, mask=mine)

        pltpu.sync_copy(local, inv_hbm.at[pl.ds(lo, per_w)])

    return inverse(slots_flat)


EXPERT_RING = 3


def _experts_kernel(texp_ref, nstep_ref, tok_ref, w1_ref, w3_ref, w2_ref, f_hbm, y_ref,
                    w1b, w3b, w2b, xa, xb, xc_, gsem, *, tme):
    i = pl.program_id(0)
    last = nstep_ref[0] - 1
    ring = (xa, xb, xc_)
    sub = xa.shape[0] // tme
    d, fdim = w1b.shape
    half = d // 2

    @pl.when(i == 0)
    def _():
        xb[...] = jnp.zeros_like(xb)
        xc_[...] = jnp.zeros_like(xc_)

    @pl.when(jnp.logical_and(i <= last,
                             jnp.logical_or(i == 0, texp_ref[i] != texp_ref[jnp.maximum(i - 1, 0)])))
    def _():
        w1b[...] = w1_ref[...].astype(BF16)
        w3b[...] = w3_ref[...].astype(BF16)
        w2b[...] = w2_ref[...].astype(BF16)

    def whole_tile(buf, sem):
        return pltpu.make_async_copy(f_hbm.at[pl.ds(0, tme * sub), :], buf, sem)

    def step(par):
        nxt, prv = (par + 1) % EXPERT_RING, (par + 2) % EXPERT_RING
        xg, xc = ring[par], ring[nxt]

        @pl.when(i >= EXPERT_RING - 1)
        def _():
            whole_tile(xc, gsem.at[nxt]).wait()

        pw1, pw2 = min(MXU_N, fdim), min(MXU_N, half)
        pieces = 2 * (fdim // pw1) + 2 * (half // pw2)
        per_piece = -(-tme // pieces)
        issued = [0]

        def issue_rows():
            for r in range(issued[0], min(issued[0] + per_piece, tme)):
                tok = pl.multiple_of(tok_ref[r] * sub, sub)
                pltpu.make_async_copy(f_hbm.at[pl.ds(tok, sub), :], xg.at[pl.ds(r * sub, sub), :],
                                      gsem.at[par]).start(priority=r % 2)
            issued[0] = min(issued[0] + per_piece, tme)

        h = _load_rows_bf16(xc, tme)
        gate, up = [], []
        for c0 in range(0, fdim, pw1):
            gate.append(_bdot(h, w1b[:, c0:c0 + pw1]))
            issue_rows()
        for c0 in range(0, fdim, pw1):
            up.append(_bdot(h, w3b[:, c0:c0 + pw1]))
            issue_rows()
        a = (_silu(jnp.concatenate(gate, axis=1)) * jnp.concatenate(up, axis=1)).astype(BF16)
        for c0 in range(0, half, pw2):
            lo = _bdot(a, w2b[:, c0:c0 + pw2])
            issue_rows()
            hi = _bdot(a, w2b[:, half + c0:half + c0 + pw2])
            issue_rows()
            packed = _pack_pairs(jnp.concatenate([lo, hi], axis=1))
            for b0 in range(0, pw2, LANES):
                y_ref[pl.ds((c0 + b0) // LANES, tme, stride=sub), :] = packed[:, b0:b0 + LANES]
        assert issued[0] == tme

        @pl.when(i == last)
        def _():
            whole_tile(ring[prv], gsem.at[prv]).wait()
            whole_tile(xg, gsem.at[par]).wait()

    for par in range(EXPERT_RING):
        pl.when(jnp.logical_and(i <= last, lax.rem(i, EXPERT_RING) == par))(functools.partial(step, par))


def _experts(texp, nstep, tok, w1, w3, w2, f, layer, tme):
    d, fdim = w1.shape[-2:]
    sub = d // (2 * LANES)
    lag = EXPERT_RING - 1
    ntile = texp.shape[0] - lag
    wsel = lambda i, te, ns: (layer, te[i], 0, 0)
    return pl.pallas_call(
        functools.partial(_experts_kernel, tme=tme),
        out_shape=jax.ShapeDtypeStruct((ntile * tme * sub, LANES), U32),
        grid_spec=pltpu.PrefetchScalarGridSpec(
            num_scalar_prefetch=2,
            grid=(ntile + lag,),
            in_specs=[pl.BlockSpec((tme,), lambda i, te, ns: (i,), memory_space=pltpu.SMEM),
                      pl.BlockSpec((None, None, d, fdim), wsel), pl.BlockSpec((None, None, d, fdim), wsel),
                      pl.BlockSpec((None, None, fdim, d), wsel),
                      pl.BlockSpec(memory_space=pl.ANY)],
            out_specs=pl.BlockSpec((tme * sub, LANES),
                                   lambda i, te, ns: (jnp.clip(i - lag, 0, ns[0] - lag - 1), 0)),
            scratch_shapes=[pltpu.VMEM((d, fdim), BF16), pltpu.VMEM((d, fdim), BF16), pltpu.VMEM((fdim, d), BF16)]
            + [pltpu.VMEM((tme * sub, LANES), U32)] * EXPERT_RING
            + [pltpu.SemaphoreType.DMA((EXPERT_RING,))]),
        compiler_params=_params("arbitrary"),
        name="moe_experts",
    )(texp, nstep, tok, w1, w3, w2, f)


def _combine_kernel(slot_ref, wt_ref, x1_ref, sh_ref, gf_ref, lng_ref, lnb_ref, ys_hbm, o_ref, ga, gb, gsem, *, alpha):
    i = pl.program_id(0)
    last = pl.num_programs(0) - 1
    tm = x1_ref.shape[0]
    sub = ga.shape[1] // tm

    @pl.when(i == 0)
    def _():
        gb[...] = jnp.zeros_like(gb)

    def whole_plane(buf, k, sem):
        return pltpu.make_async_copy(ys_hbm.at[pl.ds(0, tm * sub), :], buf.at[k], sem)

    def step(par):
        gg, gc = (ga, gb) if par == 0 else (gb, ga)

        @pl.when(i > 0)
        def _():
            for k in range(TOP_K):
                whole_plane(gc, k, gsem.at[1 - par]).wait()

        def issue(t0, t1):
            for tt in range(t0, t1):
                for k in range(TOP_K):
                    row = pl.multiple_of(slot_ref[tt * TOP_K + k] * sub, sub)
                    pltpu.make_async_copy(ys_hbm.at[pl.ds(row, sub), :], gg.at[k, pl.ds(tt * sub, sub), :],
                                          gsem.at[par]).start(priority=k % 2)

        lo = hi = None
        for k in range(TOP_K):
            issue(k * tm // TOP_K, (k + 1) * tm // TOP_K)
            w = wt_ref[:, k:k + 1]
            rl, rh = _load_row_tiles(gc.at[k], tm)
            lo = [w * b for b in rl] if lo is None else [a + w * b for a, b in zip(lo, rl)]
            hi = [w * b for b in rh] if hi is None else [a + w * b for a, b in zip(hi, rh)]
        f = jnp.concatenate(lo + hi, axis=1) + sh_ref[...].astype(F32)
        o_ref[...] = _layernorm(alpha * x1_ref[...] + gf_ref[0] * f, lng_ref[0], lnb_ref[0])

        @pl.when(i == last)
        def _():
            for k in range(TOP_K):
                whole_plane(gg, k, gsem.at[par]).wait()

    for par in range(2):
        pl.when(jnp.bitwise_and(i, 1) == par)(functools.partial(step, par))


def _combine(slot_tk, wt, x1, sh, gf, lng, lnb, ys, seq, tm, layer, alpha):
    t, d = x1.shape
    sub = d // (2 * LANES)
    per_b = seq // tm
    nstep = t // tm + 1
    prev = lambda i: jnp.maximum(i - 1, 0)
    row = lambda i: (prev(i), 0)
    lrow = lambda i: (layer, 0, 0)
    return pl.pallas_call(
        functools.partial(_combine_kernel, alpha=alpha),
        out_shape=jax.ShapeDtypeStruct((t, d), F32),
        grid=(nstep,),
        in_specs=[pl.BlockSpec((tm * TOP_K,), lambda i: (i,), memory_space=pltpu.SMEM),
                  pl.BlockSpec((tm, TOP_K), row), pl.BlockSpec((tm, d), row), pl.BlockSpec((tm, d), row),
                  pl.BlockSpec((1, 1, d), lambda i: (prev(i) // per_b, 0, 0)),
                  pl.BlockSpec((1, 1, d), lrow), pl.BlockSpec((1, 1, d), lrow),
                  pl.BlockSpec(memory_space=pl.ANY)],
        out_specs=pl.BlockSpec((tm, d), row),
        scratch_shapes=[pltpu.VMEM((TOP_K, tm * sub, LANES), U32), pltpu.VMEM((TOP_K, tm * sub, LANES), U32),
                        pltpu.SemaphoreType.DMA((2,))],
        compiler_params=_params("arbitrary"),
        name="moe_combine_ln",
    )(slot_tk, wt, x1, sh, gf, lng, lnb, ys)


def _moe(f, x1, ids, wts, rank, cnt, gf, lng, lnb, w1, w3, w2, ws1, ws3, ws2, seq, layer, alpha):
    t, d = x1.shape
    tme = min(512, t)
    counts = cnt[:, 0].astype(I32)
    tiles_e = (counts + tme - 1) // tme
    padded = tiles_e * tme
    base = jnp.cumsum(padded) - padded
    expert = jnp.arange(N_EXPERTS, dtype=I32)
    slots = jnp.sum(jnp.where(ids[:, :, None] == expert, base, 0), axis=-1) + rank
    ntile = (t * TOP_K) // tme + N_EXPERTS
    nslot = ntile * tme
    tile_ends = jnp.cumsum(tiles_e)
    texp = jnp.sum((jnp.arange(ntile, dtype=I32)[:, None] >= tile_ends[None, :]).astype(I32), axis=1)
    texp = jnp.minimum(texp, N_EXPERTS - 1)
    lag = EXPERT_RING - 1
    texp_step = jnp.concatenate([texp[:1]] * lag + [texp])
    nstep = tile_ends[-1:] + lag
    idle = jnp.arange(ntile + lag, dtype=I32) >= nstep
    texp_step = jnp.where(idle, jnp.take(texp_step, nstep - 1), texp_step)

    inv = _inverse_map(slots.reshape(-1), nslot)
    filler = jnp.arange(nslot + lag * tme, dtype=I32) % t
    tok = jnp.concatenate([jnp.where(inv >= 0, inv % t, filler[:nslot]), filler[nslot:]])

    shared = _shared_expert(f, ws1, ws3, ws2, t)
    ys = _experts(texp_step, nstep, tok, w1, w3, w2, f, layer, tme)
    tmc = min(128, seq)
    slot_tk = jnp.concatenate([slots.T.reshape(-1), jnp.zeros((tmc * TOP_K,), I32)])
    return _combine(slot_tk, wts.T, x1, shared, gf, lng, lnb, ys, seq, tmc, layer, alpha)


def kernel(x, c, ctx, c_ctx, w_ada, b_ada, ln_mix_g, ln_mix_b, ln_ffn_g, ln_ffn_b, attn_w_qkv, attn_w_o, attn_sink, gmlp_w_in, gmlp_b_in, gmlp_v_g, gmlp_v_b, gmlp_w_s, gmlp_b_s, gmlp_w_o, moe_w_router, moe_bias, moe_w1, moe_w3, moe_w2, moe_ws1, moe_ws3, moe_ws2):
    batch, seq, d = x.shape
    nctx = ctx.shape[1]
    depth = w_ada.shape[0]
    assert depth == 2, "layer 0 is the attention mixer, layer 1 the gMLP mixer"
    alpha = float((2 * depth) ** 0.25)
    t = batch * seq

    pad = (-(batch + 1)) % 8
    cc = jnp.concatenate([c, c_ctx[None, :], jnp.zeros((pad, d), F32)], axis=0)
    mods = _modulations(cc, w_ada, b_ada)

    def mod(layer, j, rows=slice(0, batch)):
        return mods[layer, rows, j * d:(j + 1) * d]

    def per_batch(layer, j):
        return mod(layer, j).reshape(batch, 1, d)

    def router(layer):
        w = jnp.pad(moe_w_router[layer], ((0, 0), (0, LANES - N_EXPERTS)))
        hi = w.astype(BF16)
        lo = (w - hi.astype(F32)).astype(BF16)
        return jnp.concatenate([hi, lo], axis=1), moe_bias[layer].reshape(N_EXPERTS, 1)

    def shared_w(layer):
        return moe_ws1[layer].astype(BF16), moe_ws3[layer].astype(BF16), moe_ws2[layer].astype(BF16)

    x2 = x.reshape(t, d)
    ln_mix_g, ln_mix_b, ln_ffn_g, ln_ffn_b = (
        p.reshape(depth, 1, d) for p in (ln_mix_g, ln_mix_b, ln_ffn_g, ln_ffn_b))

    qd = Q_PER_KV * N_KV_HEADS * HEAD_DIM
    kvd = N_KV_HEADS * HEAD_DIM
    wqkv = attn_w_qkv[0]
    dup = lambda w: jnp.concatenate([w.reshape(d, N_KV_HEADS, 1, HEAD_DIM)] * (LANES // HEAD_DIM), axis=2
                                    ).reshape(d, N_KV_HEADS * LANES)
    wk2, wv2 = dup(wqkv[:, qd:qd + kvd]), dup(wqkv[:, qd + kvd:])
    w_all = jnp.concatenate([wqkv[:, :qd] * (HEAD_DIM ** -0.5), wk2, wv2], axis=1).astype(BF16)
    q, k2, v2 = _qkv_proj(x2, per_batch(0, 1), per_batch(0, 0), w_all, _rope_tables(seq), seq)
    ctx_row = slice(batch, batch + 1)
    kc2, vc2 = _ctx_kv(ctx.reshape(batch * nctx, d), mod(0, 1, ctx_row), mod(0, 0, ctx_row),
                       jnp.concatenate([wk2, wv2], axis=1).astype(BF16))
    o = _attention(q, k2, v2, kc2, vc2, attn_sink[0], batch, seq, nctx)
    x1, f, ids, wts, rank, cnt = _attn_out(
        o, attn_w_o[0].astype(BF16), x2, per_batch(0, 2), per_batch(0, 4), per_batch(0, 3),
        ln_mix_g, ln_mix_b, *router(0), seq, 0, alpha)
    x2 = _moe(f, x1, ids, wts, rank, cnt, per_batch(0, 5), ln_ffn_g, ln_ffn_b,
              moe_w1, moe_w3, moe_w2, *shared_w(0), seq, 0, alpha)

    width = gmlp_w_in.shape[2] // 2
    w_in = gmlp_w_in[0].astype(BF16)
    b_in = gmlp_b_in[0].reshape(1, 2 * width)
    vg, vb = gmlp_v_g[0].reshape(1, width), gmlp_v_b[0].reshape(1, width)
    sc, sh = per_batch(1, 1), per_batch(1, 0)
    u = _gmlp_in(x2, sc, sh, w_in[:, :width], b_in[:, :width], vg, vb, seq, False)
    v = _gmlp_in(x2, sc, sh, w_in[:, width:], b_in[:, width:], vg, vb, seq, True)
    x1, f, ids, wts, rank, cnt = _gmlp_out(
        u, v, gmlp_w_s[0].astype(BF16), gmlp_b_s[0][:, :, None], gmlp_w_o[0].astype(BF16), x2,
        per_batch(1, 2), per_batch(1, 4), per_batch(1, 3), ln_mix_g, ln_mix_b, *router(1), seq, 1, alpha)
    x2 = _moe(f, x1, ids, wts, rank, cnt, per_batch(1, 5), ln_ffn_g, ln_ffn_b,
              moe_w1, moe_w3, moe_w2, *shared_w(1), seq, 1, alpha)
    return x2.reshape(batch, seq, d)
```

```python
import functools

import jax
import jax.numpy as jnp
from jax import lax
from jax.experimental import pallas as pl
from jax.experimental.pallas import tpu as pltpu
from jax.experimental.pallas import tpu_sc as plsc

F32 = jnp.float32
BF16 = jnp.bfloat16
I32 = jnp.int32

GRID_W = 64
N_KV_HEADS = 4
Q_PER_KV = 8
HEAD_DIM = 64
ROPE_HALF = 16
WINDOW = 128
Q_BLOCK = 128
ROPE_BASE = 10000.0
NEG_INF = -1e30
CHUNK = 128
N_GMLP_GROUPS = 8
N_EXPERTS = 64
TOP_K = 8
N_EXPERT_GROUPS = 8
EXPERTS_PER_GROUP = 8
TOPK_GROUPS = 4
ROUTED_SCALE = 2.5
LN_EPS = 1e-5

LANES = 128
VMEM_LIMIT_BYTES = 56 * 1024 * 1024


def _params(*sem):
    return pltpu.CompilerParams(dimension_semantics=sem, vmem_limit_bytes=VMEM_LIMIT_BYTES)


def _silu(a):
    return a * jax.nn.sigmoid(a)


def _layernorm(v, g, b):
    mu = jnp.mean(v, axis=-1, keepdims=True)
    d = v - mu
    var = jnp.mean(d * d, axis=-1, keepdims=True)
    return d * lax.rsqrt(var + LN_EPS) * g + b


def _bdot(a, b):
    return jnp.dot(a, b, preferred_element_type=F32)


U32 = jnp.uint32


def _pack_pairs(a):
    n = a.shape[1] // 2
    lo = lax.bitcast_convert_type(a[:, :n].astype(BF16).astype(F32), U32)
    hi = lax.bitcast_convert_type(a[:, n:].astype(BF16).astype(F32), U32)
    return hi | (lo >> 16)


def _unpack_pairs(p):
    lo = lax.bitcast_convert_type(p << 16, F32)
    hi = lax.bitcast_convert_type(p & jnp.uint32(0xFFFF0000), F32)
    return lo, hi


def _store_row_tiles(ref, packed):
    r, n = packed.shape
    sub = n // LANES
    for j in range(sub):
        ref[pl.ds(j, r, stride=sub), :] = packed[:, LANES * j:LANES * (j + 1)]


def _load_row_tiles(ref, r):
    sub = ref.shape[0] // r
    pieces = [_unpack_pairs(ref[pl.ds(j, r, stride=sub), :]) for j in range(sub)]
    return [p[0] for p in pieces], [p[1] for p in pieces]


def _load_rows_bf16(ref, r):
    lo, hi = _load_row_tiles(ref, r)
    return jnp.concatenate(lo + hi, axis=1).astype(BF16)


def _mod_kernel(c_ref, w_ref, b_ref, o_ref):
    a = _silu(c_ref[...]).astype(BF16)
    o_ref[0] = _bdot(a, w_ref[0].astype(BF16)) + b_ref[0]


def _modulations(cc, w_ada, b_ada):
    depth, d, n6 = w_ada.shape
    r = cc.shape[0]
    tn = max(w for w in range(LANES, min(1024, n6) + 1, LANES) if n6 % w == 0)
    return pl.pallas_call(
        _mod_kernel,
        out_shape=jax.ShapeDtypeStruct((depth, r, n6), F32),
        grid=(depth, n6 // tn),
        in_specs=[pl.BlockSpec((r, d), lambda l, j: (0, 0)),
                  pl.BlockSpec((1, d, tn), lambda l, j: (l, 0, j)),
                  pl.BlockSpec((1, 1, tn), lambda l, j: (l, 0, j))],
        out_specs=pl.BlockSpec((1, r, tn), lambda l, j: (l, 0, j)),
        compiler_params=_params("parallel", "parallel"),
        name="adaln_mod",
    )(cc, w_ada, b_ada.reshape(depth, 1, n6))


def _rope_tables(seq):
    rows = seq // GRID_W
    row_ids = jnp.repeat(jnp.arange(rows, dtype=F32), GRID_W)
    col_ids = jnp.tile(jnp.arange(GRID_W, dtype=F32), rows)
    inv_freq = ROPE_BASE ** (-jnp.arange(0, 2 * ROPE_HALF, 2, dtype=F32) / (2 * ROPE_HALF))
    ar, ac = row_ids[:, None] * inv_freq, col_ids[:, None] * inv_freq
    z = jnp.zeros_like(ar)
    cos = jnp.concatenate([jnp.cos(ar), jnp.cos(ar), jnp.cos(ac), jnp.cos(ac)], axis=-1)
    sin_lo = jnp.concatenate([-jnp.sin(ar), z, -jnp.sin(ac), z], axis=-1)
    sin_hi = jnp.concatenate([z, jnp.sin(ar), z, jnp.sin(ac)], axis=-1)
    rep = LANES // HEAD_DIM
    return tuple(jnp.tile(t, (1, rep)) for t in (cos, sin_lo, sin_hi))


def _qkv_kernel(x_ref, sc_ref, sh_ref, w_ref, cos_ref, slo_ref, shi_ref, q_ref, k_ref, v_ref, *, cw):
    h = (x_ref[...] * (1.0 + sc_ref[0]) + sh_ref[0]).astype(BF16)
    cos, slo, shi = cos_ref[...], slo_ref[...], shi_ref[...]

    def rope(a):
        return (a * cos + pltpu.roll(a, LANES - ROPE_HALF, 1) * slo
                + pltpu.roll(a, ROPE_HALF, 1) * shi)

    qd, kd = q_ref.shape[1], k_ref.shape[1]
    for c0 in range(0, qd, cw):
        acc = _bdot(h, w_ref[:, c0:c0 + cw])
        for b0 in range(0, cw, LANES):
            q_ref[:, c0 + b0:c0 + b0 + LANES] = rope(acc[:, b0:b0 + LANES]).astype(BF16)
    acc = _bdot(h, w_ref[:, qd:qd + kd])
    for b0 in range(0, kd, LANES):
        k_ref[:, b0:b0 + LANES] = rope(acc[:, b0:b0 + LANES]).astype(BF16)
    v_ref[...] = _bdot(h, w_ref[:, qd + kd:qd + 2 * kd]).astype(BF16)


def _qkv_proj(x2, sc, sh, w, tables, seq):
    t, d = x2.shape
    kd = N_KV_HEADS * LANES
    qd = w.shape[1] - 2 * kd
    tm = min(512, seq)
    per_b = seq // tm
    row = lambda i: (i, 0)
    mod = lambda i: (i // per_b, 0, 0)
    tab = lambda i: (i % per_b, 0)
    return pl.pallas_call(
        functools.partial(_qkv_kernel, cw=min(512, qd)),
        out_shape=(jax.ShapeDtypeStruct((t, qd), BF16), jax.ShapeDtypeStruct((t, kd), BF16),
                   jax.ShapeDtypeStruct((t, kd), BF16)),
        grid=(t // tm,),
        in_specs=[pl.BlockSpec((tm, d), row), pl.BlockSpec((1, 1, d), mod), pl.BlockSpec((1, 1, d), mod),
                  pl.BlockSpec(w.shape, lambda i: (0, 0)),
                  pl.BlockSpec((tm, LANES), tab), pl.BlockSpec((tm, LANES), tab), pl.BlockSpec((tm, LANES), tab)],
        out_specs=(pl.BlockSpec((tm, qd), row), pl.BlockSpec((tm, kd), row), pl.BlockSpec((tm, kd), row)),
        compiler_params=_params("parallel"),
        name="qkv_rope",
    )(x2, sc, sh, w, *tables)


def _ctxkv_kernel(x_ref, sc_ref, sh_ref, w_ref, k_ref, v_ref):
    h = (x_ref[...] * (1.0 + sc_ref[...]) + sh_ref[...]).astype(BF16)
    kd = k_ref.shape[1]
    k_ref[...] = _bdot(h, w_ref[:, :kd]).astype(BF16)
    v_ref[...] = _bdot(h, w_ref[:, kd:]).astype(BF16)


def _ctx_kv(c2, sc, sh, w):
    t, d = c2.shape
    kd = w.shape[1] // 2
    tm = min(512, t)
    row = lambda i: (i, 0)
    fix = lambda i: (0, 0)
    return pl.pallas_call(
        _ctxkv_kernel,
        out_shape=(jax.ShapeDtypeStruct((t, kd), BF16), jax.ShapeDtypeStruct((t, kd), BF16)),
        grid=(t // tm,),
        in_specs=[pl.BlockSpec((tm, d), row), pl.BlockSpec((1, d), fix), pl.BlockSpec((1, d), fix),
                  pl.BlockSpec(w.shape, fix)],
        out_specs=(pl.BlockSpec((tm, kd), row), pl.BlockSpec((tm, kd), row)),
        compiler_params=_params("parallel"),
        name="ctx_kv",
    )(c2, sc, sh, w)


def _attn_kernel(sink_ref, q_ref, k_ref, v_ref, kc_ref, vc_ref, o_ref):
    for b in range(q_ref.shape[1] // Q_BLOCK):
        _attn_block(sink_ref, q_ref, k_ref, v_ref, kc_ref, vc_ref, o_ref, b)


def _attn_block(sink_ref, q_ref, k_ref, v_ref, kc_ref, vc_ref, o_ref, b):
    kv = pl.program_id(1)
    i = pl.program_id(2) * (q_ref.shape[1] // Q_BLOCK) + b
    seq, nctx = k_ref.shape[1], kc_ref.shape[1]
    nwin = Q_BLOCK + 2 * WINDOW
    start = pl.multiple_of(jnp.clip(i * Q_BLOCK - WINDOW, 0, seq - nwin), Q_BLOCK)
    nkey = nctx + nwin
    kall = jnp.concatenate([kc_ref[0], k_ref[0, pl.ds(start, nwin), :]], axis=0)
    vall = jnp.concatenate([vc_ref[0], v_ref[0, pl.ds(start, nwin), :]], axis=0)
    low = lax.broadcasted_iota(I32, (nkey, LANES), 1) < HEAD_DIM
    zero = jnp.zeros((nkey, LANES), BF16)
    npair = Q_PER_KV // 2
    q = q_ref[0, Q_BLOCK * b:Q_BLOCK * (b + 1), :]
    qs = jnp.concatenate([q[:, LANES * j:LANES * (j + 1)] for j in range(npair)], axis=0)
    nrow = npair * Q_BLOCK
    rows = lax.broadcasted_iota(I32, (Q_BLOCK, nkey), 0)
    cols = lax.broadcasted_iota(I32, (Q_BLOCK, nkey), 1)
    dist = (i * Q_BLOCK - start + nctx) + rows - cols
    valid = (cols < nctx) | (jnp.abs(dist) <= WINDOW)
    contract_last = (((1,), (1,)), ((), ()))
    out = jnp.zeros((nrow, LANES), F32)
    for par in range(2):
        keep = low if par == 0 else jnp.logical_not(low)
        kh = jnp.where(keep, kall, zero)
        vh = jnp.where(keep, vall, zero)
        s = lax.dot_general(qs, kh, contract_last, preferred_element_type=F32)
        s = jnp.concatenate(
            [jnp.where(valid, s[Q_BLOCK * j:Q_BLOCK * (j + 1), :], NEG_INF) for j in range(npair)], axis=0)
        sink = jnp.concatenate(
            [jnp.full((Q_BLOCK, 1), sink_ref[kv * Q_PER_KV + 2 * j + par], F32) for j in range(npair)], axis=0)
        m = jnp.maximum(jnp.max(s, axis=1, keepdims=True), sink)
        p = jnp.exp(s - m)
        den = jnp.sum(p, axis=1, keepdims=True) + jnp.exp(sink - m)
        out = out + _bdot(p.astype(BF16), vh) / den
    for j in range(npair):
        o_ref[0, Q_BLOCK * b:Q_BLOCK * (b + 1), LANES * j:LANES * (j + 1)] = (
            out[Q_BLOCK * j:Q_BLOCK * (j + 1), :].astype(BF16))


def _attention(q, k2, v2, kc2, vc2, sink, batch, seq, nctx):
    qd = q.shape[1]
    gw = Q_PER_KV * HEAD_DIM
    q3 = q.reshape(batch, seq, qd)
    k3, v3 = k2.reshape(batch, seq, -1), v2.reshape(batch, seq, -1)
    kc3, vc3 = kc2.reshape(batch, nctx, -1), vc2.reshape(batch, nctx, -1)
    full = lambda b, h, i: (b, 0, h)
    blk = lambda b, h, i: (b, i, h)
    rows = next(r * Q_BLOCK for r in (8, 4, 2, 1) if seq % (r * Q_BLOCK) == 0)
    out = pl.pallas_call(
        _attn_kernel,
        out_shape=jax.ShapeDtypeStruct((batch, seq, qd), BF16),
        grid=(batch, N_KV_HEADS, seq // rows),
        in_specs=[pl.BlockSpec(memory_space=pltpu.SMEM),
                  pl.BlockSpec((1, rows, gw), blk),
                  pl.BlockSpec((1, seq, LANES), full), pl.BlockSpec((1, seq, LANES), full),
                  pl.BlockSpec((1, nctx, LANES), full), pl.BlockSpec((1, nctx, LANES), full)],
        out_specs=pl.BlockSpec((1, rows, gw), blk),
        compiler_params=_params("parallel", "parallel", "parallel"),
        name="window_gqa",
    )(sink, q3, k3, v3, kc3, vc3)
    return out.reshape(batch * seq, qd)


def _split_bf16(a):
    hi = a.astype(BF16)
    return hi, (a - hi.astype(F32)).astype(BF16)


def _post_mixer(x, y, gm, lng, lnb, scf, shf, wr_ref, rb_ref, run_ref,
                x1_ref, f_ref, ids_ref, wts_ref, rank_ref, cnt_ref, alpha):
    tm = x.shape[0]
    x1 = _layernorm(alpha * x + gm * y, lng, lnb)
    x1_ref[...] = x1
    f = x1 * (1.0 + scf) + shf
    _store_row_tiles(f_ref, _pack_pairs(f))

    fh, fl = _split_bf16(f)
    prod = _bdot(fh, wr_ref[...]) + _bdot(fl, wr_ref[...])
    logits = (prod[:, :LANES] + prod[:, LANES:]).T[:N_EXPERTS, :]
    scores = jax.nn.sigmoid(logits)
    biased = scores + rb_ref[...]

    eg = EXPERTS_PER_GROUP
    sub = lax.broadcasted_iota(I32, (eg, tm), 0)
    gscore = []
    for g in range(N_EXPERT_GROUPS):
        tg = biased[eg * g:eg * (g + 1), :]
        m1 = jnp.max(tg, axis=0, keepdims=True)
        i1 = jnp.min(jnp.where(tg == m1, sub, eg), axis=0, keepdims=True)
        m2 = jnp.max(jnp.where(sub == i1, -jnp.inf, tg), axis=0, keepdims=True)
        gscore.append(m1 + m2)
    gsel = [jnp.zeros((1, tm), jnp.bool_) for _ in range(N_EXPERT_GROUPS)]
    for _ in range(TOPK_GROUPS):
        best = functools.reduce(jnp.maximum, gscore)
        taken = jnp.zeros((1, tm), jnp.bool_)
        for g in range(N_EXPERT_GROUPS):
            hit = jnp.logical_and(gscore[g] == best, jnp.logical_not(taken))
            taken = jnp.logical_or(taken, hit)
            gsel[g] = jnp.logical_or(gsel[g], hit)
            gscore[g] = jnp.where(hit, -jnp.inf, gscore[g])
    cur = jnp.concatenate(
        [jnp.where(gsel[g], biased[eg * g:eg * (g + 1), :], -jnp.inf) for g in range(N_EXPERT_GROUPS)], axis=0)

    eidx = lax.broadcasted_iota(I32, (N_EXPERTS, tm), 0)
    picks, wts, hots = [], [], []
    for _ in range(TOP_K):
        m = jnp.max(cur, axis=0, keepdims=True)
        idx = jnp.min(jnp.where(cur == m, eidx, N_EXPERTS), axis=0, keepdims=True)
        hot = eidx == idx
        picks.append(idx)
        wts.append(jnp.sum(jnp.where(hot, scores, 0.0), axis=0, keepdims=True))
        hots.append(hot)
        cur = jnp.where(hot, -jnp.inf, cur)
    wsum = functools.reduce(jnp.add, wts)

    assigned = functools.reduce(jnp.add, [h.astype(F32) for h in hots])
    before = (lax.broadcasted_iota(I32, (tm, tm), 0) < lax.broadcasted_iota(I32, (tm, tm), 1)).astype(BF16)
    pos = _bdot(assigned.astype(BF16), before) + run_ref[:, 0:1]
    for k in range(TOP_K):
        ids_ref[k:k + 1, :] = picks[k]
        wts_ref[k:k + 1, :] = wts[k] / wsum * ROUTED_SCALE
        rank_ref[k:k + 1, :] = jnp.sum(jnp.where(hots[k], pos, 0.0), axis=0, keepdims=True).astype(I32)
    run_ref[...] = run_ref[...] + jnp.sum(assigned, axis=1, keepdims=True)
    cnt_ref[...] = run_ref[...]


def _route_out_shapes(t, d):
    return (jax.ShapeDtypeStruct((t, d), F32), jax.ShapeDtypeStruct((t * d // (2 * LANES), LANES), U32),
            jax.ShapeDtypeStruct((TOP_K, t), I32), jax.ShapeDtypeStruct((TOP_K, t), F32),
            jax.ShapeDtypeStruct((TOP_K, t), I32), jax.ShapeDtypeStruct((N_EXPERTS, LANES), F32))


def _route_out_specs(tm, d):
    row = lambda i: (i, 0)
    col = lambda i: (0, i)
    return (pl.BlockSpec((tm, d), row), pl.BlockSpec((tm * d // (2 * LANES), LANES), row),
            pl.BlockSpec((TOP_K, tm), col), pl.BlockSpec((TOP_K, tm), col), pl.BlockSpec((TOP_K, tm), col),
            pl.BlockSpec((N_EXPERTS, LANES), lambda i: (0, 0)))


def _route_in_specs(tm, d, per_b, layer):
    mod = lambda i: (i // per_b, 0, 0)
    lrow = lambda i: (layer, 0, 0)
    fix = lambda i: (0, 0)
    return [pl.BlockSpec((tm, d), lambda i: (i, 0)),
            pl.BlockSpec((1, 1, d), mod), pl.BlockSpec((1, 1, d), mod), pl.BlockSpec((1, 1, d), mod),
            pl.BlockSpec((1, 1, d), lrow), pl.BlockSpec((1, 1, d), lrow),
            pl.BlockSpec((d, 2 * LANES), fix),
            pl.BlockSpec((N_EXPERTS, 1), fix)]


def _oproj_kernel(o_ref, wo_ref, x_ref, gm_ref, scf_ref, shf_ref, lng_ref, lnb_ref, wr_ref, rb_ref,
                  x1_ref, f_ref, ids_ref, wts_ref, rank_ref, cnt_ref, run_ref, *, alpha):
    @pl.when(pl.program_id(0) == 0)
    def _():
        run_ref[...] = jnp.zeros_like(run_ref)

    y = _bdot(o_ref[...], wo_ref[...])
    _post_mixer(x_ref[...], y, gm_ref[0], lng_ref[0], lnb_ref[0], scf_ref[0], shf_ref[0],
                wr_ref, rb_ref, run_ref, x1_ref, f_ref, ids_ref, wts_ref, rank_ref, cnt_ref, alpha)


def _attn_out(o, wo, x2, gm, scf, shf, lng, lnb, wr, rb, seq, layer, alpha):
    t, d = x2.shape
    tm = min(512, seq)
    return pl.pallas_call(
        functools.partial(_oproj_kernel, alpha=alpha),
        out_shape=_route_out_shapes(t, d),
        grid=(t // tm,),
        in_specs=[pl.BlockSpec((tm, o.shape[1]), lambda i: (i, 0)), pl.BlockSpec(wo.shape, lambda i: (0, 0))]
        + _route_in_specs(tm, d, seq // tm, layer),
        out_specs=_route_out_specs(tm, d),
        scratch_shapes=[pltpu.VMEM((N_EXPERTS, LANES), F32)],
        compiler_params=_params("arbitrary"),
        name="attn_out_route",
    )(o, wo, x2, gm, scf, shf, lng, lnb, wr, rb)


def _gmlp_in_kernel(x_ref, sc_ref, sh_ref, w_ref, b_ref, g_ref, beta_ref, o_ref, *, cw, normalize):
    h = (x_ref[...] * (1.0 + sc_ref[0]) + sh_ref[0]).astype(BF16)
    n = o_ref.shape[1]
    inv_sqrt2 = 0.7071067811865476
    parts = []
    for c0 in range(0, n, cw):
        z = _bdot(h, w_ref[:, c0:c0 + cw]) + b_ref[:, c0:c0 + cw]
        z = 0.5 * z * (1.0 + lax.erf(z * inv_sqrt2))
        if normalize:
            parts.append(z)
        else:
            o_ref[:, c0:c0 + cw] = z.astype(BF16)
    if normalize:
        v = jnp.concatenate(parts, axis=1)
        o_ref[...] = _layernorm(v, g_ref[...], beta_ref[...]).astype(BF16)


def _gmlp_in(x2, sc, sh, w, b, g, beta, seq, normalize):
    t, d = x2.shape
    n = w.shape[1]
    tm = min(256, seq)
    per_b = seq // tm
    row = lambda i: (i, 0)
    mod = lambda i: (i // per_b, 0, 0)
    fix = lambda i: (0, 0)
    return pl.pallas_call(
        functools.partial(_gmlp_in_kernel, cw=min(512, n), normalize=normalize),
        out_shape=jax.ShapeDtypeStruct((t, n), BF16),
        grid=(t // tm,),
        in_specs=[pl.BlockSpec((tm, d), row), pl.BlockSpec((1, 1, d), mod), pl.BlockSpec((1, 1, d), mod),
                  pl.BlockSpec(w.shape, fix), pl.BlockSpec((1, n), fix), pl.BlockSpec((1, n), fix),
                  pl.BlockSpec((1, n), fix)],
        out_specs=pl.BlockSpec((tm, n), row),
        compiler_params=_params("parallel"),
        name="gmlp_in_v" if normalize else "gmlp_in_u",
    )(x2, sc, sh, w, b, g, beta)


def _gmlp_out_kernel(u_ref, v_ref, ws_ref, bs_ref, wo_ref, x_ref, gm_ref, scf_ref, shf_ref, lng_ref, lnb_ref,
                     wr_ref, rb_ref, x1_ref, f_ref, ids_ref, wts_ref, rank_ref, cnt_ref,
                     run_ref, gated_ref, *, alpha):
    @pl.when(pl.program_id(0) == 0)
    def _():
        run_ref[...] = jnp.zeros_like(run_ref)

    tm, width = u_ref.shape
    gd = width // N_GMLP_GROUPS
    for r0 in range(0, tm, CHUNK):
        for g in range(N_GMLP_GROUPS):
            c0 = g * gd
            mixed = _bdot(ws_ref[g], v_ref[r0:r0 + CHUNK, c0:c0 + gd]) + bs_ref[g]
            gated_ref[r0:r0 + CHUNK, c0:c0 + gd] = (
                u_ref[r0:r0 + CHUNK, c0:c0 + gd].astype(F32) * mixed).astype(BF16)
    y = _bdot(gated_ref[...], wo_ref[...])
    _post_mixer(x_ref[...], y, gm_ref[0], lng_ref[0], lnb_ref[0], scf_ref[0], shf_ref[0],
                wr_ref, rb_ref, run_ref, x1_ref, f_ref, ids_ref, wts_ref, rank_ref, cnt_ref, alpha)


def _gmlp_out(u, v, ws, bs, wo, x2, gm, scf, shf, lng, lnb, wr, rb, seq, layer, alpha):
    t, d = x2.shape
    width = u.shape[1]
    tm = min(256, seq)
    row = lambda i: (i, 0)
    return pl.pallas_call(
        functools.partial(_gmlp_out_kernel, alpha=alpha),
        out_shape=_route_out_shapes(t, d),
        grid=(t // tm,),
        in_specs=[pl.BlockSpec((tm, width), row), pl.BlockSpec((tm, width), row),
                  pl.BlockSpec(ws.shape, lambda i: (0, 0, 0)), pl.BlockSpec(bs.shape, lambda i: (0, 0, 0)),
                  pl.BlockSpec(wo.shape, lambda i: (0, 0))]
        + _route_in_specs(tm, d, seq // tm, layer),
        out_specs=_route_out_specs(tm, d),
        scratch_shapes=[pltpu.VMEM((N_EXPERTS, LANES), F32), pltpu.VMEM((tm, width), BF16)],
        compiler_params=_params("arbitrary"),
        name="gmlp_out_route",
    )(u, v, ws, bs, wo, x2, gm, scf, shf, lng, lnb, wr, rb)


SC_CORES, SC_SUBCORES, SC_LANES = 2, 16, 16
MXU_N = 256


def _shared_kernel(f_ref, ws1_ref, ws3_ref, ws2_ref, sh_ref):
    h = _load_rows_bf16(f_ref, sh_ref.shape[0])
    a = _silu(_bdot(h, ws1_ref[...])) * _bdot(h, ws3_ref[...])
    sh_ref[...] = _bdot(a.astype(BF16), ws2_ref[...]).astype(BF16)


def _shared_expert(f, ws1, ws3, ws2, t):
    sub = f.shape[0] // t
    d = ws1.shape[0]
    tm = min(512, t)
    fix = lambda i: (0, 0)
    return pl.pallas_call(
        _shared_kernel,
        out_shape=jax.ShapeDtypeStruct((t, d), BF16),
        grid=(t // tm,),
        in_specs=[pl.BlockSpec((tm * sub, LANES), lambda i: (i, 0)),
                  pl.BlockSpec(ws1.shape, fix), pl.BlockSpec(ws3.shape, fix), pl.BlockSpec(ws2.shape, fix)],
        out_specs=pl.BlockSpec((tm, d), lambda i: (i, 0)),
        compiler_params=_params("parallel"),
        name="moe_shared",
    )(f, ws1, ws3, ws2)


def _inverse_map(slots_flat, nslot):
    n = slots_flat.shape[0]
    workers = SC_CORES * SC_SUBCORES
    per_w = nslot // workers
    chunk = min(8192, n)
    assert nslot % (workers * SC_LANES) == 0 and n % chunk == 0 and chunk % SC_LANES == 0
    mesh = plsc.VectorSubcoreMesh(core_axis_name="c", subcore_axis_name="s")

    @functools.partial(
        pl.kernel, out_type=jax.ShapeDtypeStruct((nslot,), I32), mesh=mesh,
        scratch_types=[pltpu.VMEM((per_w,), I32), pltpu.VMEM((chunk,), I32)],
        compiler_params=pltpu.CompilerParams(needs_layout_passes=False))
    def inverse(slots_hbm, inv_hbm, local, buf):
        lo = (lax.axis_index("s") * SC_CORES + lax.axis_index("c")) * per_w
        unused = jnp.full((SC_LANES,), -1, I32)

        @pl.loop(0, per_w, step=SC_LANES)
        def _(j):
            local[pl.ds(j, SC_LANES)] = unused

        lane = lax.iota(I32, SC_LANES)

        @pl.loop(0, n, step=chunk)
        def _(c0):
            pltpu.sync_copy(slots_hbm.at[pl.ds(c0, chunk)], buf)

            @pl.loop(0, chunk, step=SC_LANES)
            def _(j):
                idx = buf[pl.ds(j, SC_LANES)] - lo
                mine = jnp.logical_and(idx >= 0, idx < per_w)
                plsc.store_scatter(local, [idx], c0 + j + lane, mask=mine)

        pltpu.sync_copy(local, inv_hbm.at[pl.ds(lo, per_w)])

    return inverse(slots_flat)


EXPERT_RING = 3


def _experts_kernel(texp_ref, nstep_ref, tok_ref, w1_ref, w3_ref, w2_ref, f_hbm, y_ref,
                    w1b, w3b, w2b, xa, xb, xc_, gsem, *, tme):
    i = pl.program_id(0)
    last = nstep_ref[0] - 1
    ring = (xa, xb, xc_)
    sub = xa.shape[0] // tme
    d, fdim = w1b.shape
    half = d // 2

    @pl.when(i == 0)
    def _():
        xb[...] = jnp.zeros_like(xb)
        xc_[...] = jnp.zeros_like(xc_)

    @pl.when(jnp.logical_and(i <= last,
                             jnp.logical_or(i == 0, texp_ref[i] != texp_ref[jnp.maximum(i - 1, 0)])))
    def _():
        w1b[...] = w1_ref[...].astype(BF16)
        w3b[...] = w3_ref[...].astype(BF16)
        w2b[...] = w2_ref[...].astype(BF16)

    def whole_tile(buf, sem):
        return pltpu.make_async_copy(f_hbm.at[pl.ds(0, tme * sub), :], buf, sem)

    def step(par):
        nxt, prv = (par + 1) % EXPERT_RING, (par + 2) % EXPERT_RING
        xg, xc = ring[par], ring[nxt]

        @pl.when(i >= EXPERT_RING - 1)
        def _():
            whole_tile(xc, gsem.at[nxt]).wait()

        pw1, pw2 = min(MXU_N, fdim), min(MXU_N, half)
        pieces = 2 * (fdim // pw1) + 2 * (half // pw2)
        per_piece = -(-tme // pieces)
        issued = [0]

        def issue_rows():
            for r in range(issued[0], min(issued[0] + per_piece, tme)):
                tok = pl.multiple_of(tok_ref[r] * sub, sub)
                pltpu.make_async_copy(f_hbm.at[pl.ds(tok, sub), :], xg.at[pl.ds(r * sub, sub), :],
                                      gsem.at[par]).start(priority=r % 2)
            issued[0] = min(issued[0] + per_piece, tme)

        h = _load_rows_bf16(xc, tme)
        gate, up = [], []
        for c0 in range(0, fdim, pw1):
            gate.append(_bdot(h, w1b[:, c0:c0 + pw1]))
            issue_rows()
        for c0 in range(0, fdim, pw1):
            up.append(_bdot(h, w3b[:, c0:c0 + pw1]))
            issue_rows()
        a = (_silu(jnp.concatenate(gate, axis=1)) * jnp.concatenate(up, axis=1)).astype(BF16)
        for c0 in range(0, half, pw2):
            lo = _bdot(a, w2b[:, c0:c0 + pw2])
            issue_rows()
            hi = _bdot(a, w2b[:, half + c0:half + c0 + pw2])
            issue_rows()
            packed = _pack_pairs(jnp.concatenate([lo, hi], axis=1))
            for b0 in range(0, pw2, LANES):
                y_ref[pl.ds((c0 + b0) // LANES, tme, stride=sub), :] = packed[:, b0:b0 + LANES]
        assert issued[0] == tme

        @pl.when(i == last)
        def _():
            whole_tile(ring[prv], gsem.at[prv]).wait()
            whole_tile(xg, gsem.at[par]).wait()

    for par in range(EXPERT_RING):
        pl.when(jnp.logical_and(i <= last, lax.rem(i, EXPERT_RING) == par))(functools.partial(step, par))


def _experts(texp, nstep, tok, w1, w3, w2, f, layer, tme):
    d, fdim = w1.shape[-2:]
    sub = d // (2 * LANES)
    lag = EXPERT_RING - 1
    ntile = texp.shape[0] - lag
    wsel = lambda i, te, ns: (layer, te[i], 0, 0)
    return pl.pallas_call(
        functools.partial(_experts_kernel, tme=tme),
        out_shape=jax.ShapeDtypeStruct((ntile * tme * sub, LANES), U32),
        grid_spec=pltpu.PrefetchScalarGridSpec(
            num_scalar_prefetch=2,
            grid=(ntile + lag,),
            in_specs=[pl.BlockSpec((tme,), lambda i, te, ns: (i,), memory_space=pltpu.SMEM),
                      pl.BlockSpec((None, None, d, fdim), wsel), pl.BlockSpec((None, None, d, fdim), wsel),
                      pl.BlockSpec((None, None, fdim, d), wsel),
                      pl.BlockSpec(memory_space=pl.ANY)],
            out_specs=pl.BlockSpec((tme * sub, LANES),
                                   lambda i, te, ns: (jnp.clip(i - lag, 0, ns[0] - lag - 1), 0)),
            scratch_shapes=[pltpu.VMEM((d, fdim), BF16), pltpu.VMEM((d, fdim), BF16), pltpu.VMEM((fdim, d), BF16)]
            + [pltpu.VMEM((tme * sub, LANES), U32)] * EXPERT_RING
            + [pltpu.SemaphoreType.DMA((EXPERT_RING,))]),
        compiler_params=_params("arbitrary"),
        name="moe_experts",
    )(texp, nstep, tok, w1, w3, w2, f)


def _combine_kernel(slot_ref, wt_ref, x1_ref, sh_ref, gf_ref, lng_ref, lnb_ref, ys_hbm, o_ref, ga, gb, gsem, *, alpha):
    i = pl.program_id(0)
    last = pl.num_programs(0) - 1
    tm = x1_ref.shape[0]
    sub = ga.shape[1] // tm

    @pl.when(i == 0)
    def _():
        gb[...] = jnp.zeros_like(gb)

    def whole_plane(buf, k, sem):
        return pltpu.make_async_copy(ys_hbm.at[pl.ds(0, tm * sub), :], buf.at[k], sem)

    def step(par):
        gg, gc = (ga, gb) if par == 0 else (gb, ga)

        @pl.when(i > 0)
        def _():
            for k in range(TOP_K):
                whole_plane(gc, k, gsem.at[1 - par]).wait()

        def issue(t0, t1):
            for tt in range(t0, t1):
                for k in range(TOP_K):
                    row = pl.multiple_of(slot_ref[tt * TOP_K + k] * sub, sub)
                    pltpu.make_async_copy(ys_hbm.at[pl.ds(row, sub), :], gg.at[k, pl.ds(tt * sub, sub), :],
                                          gsem.at[par]).start(priority=k % 2)

        lo = hi = None
        for k in range(TOP_K):
            issue(k * tm // TOP_K, (k + 1) * tm // TOP_K)
            w = wt_ref[:, k:k + 1]
            rl, rh = _load_row_tiles(gc.at[k], tm)
            lo = [w * b for b in rl] if lo is None else [a + w * b for a, b in zip(lo, rl)]
            hi = [w * b for b in rh] if hi is None else [a + w * b for a, b in zip(hi, rh)]
        f = jnp.concatenate(lo + hi, axis=1) + sh_ref[...].astype(F32)
        o_ref[...] = _layernorm(alpha * x1_ref[...] + gf_ref[0] * f, lng_ref[0], lnb_ref[0])

        @pl.when(i == last)
        def _():
            for k in range(TOP_K):
                whole_plane(gg, k, gsem.at[par]).wait()

    for par in range(2):
        pl.when(jnp.bitwise_and(i, 1) == par)(functools.partial(step, par))


def _combine(slot_tk, wt, x1, sh, gf, lng, lnb, ys, seq, tm, layer, alpha):
    t, d = x1.shape
    sub = d // (2 * LANES)
    per_b = seq // tm
    nstep = t // tm + 1
    prev = lambda i: jnp.maximum(i - 1, 0)
    row = lambda i: (prev(i), 0)
    lrow = lambda i: (layer, 0, 0)
    return pl.pallas_call(
        functools.partial(_combine_kernel, alpha=alpha),
        out_shape=jax.ShapeDtypeStruct((t, d), F32),
        grid=(nstep,),
        in_specs=[pl.BlockSpec((tm * TOP_K,), lambda i: (i,), memory_space=pltpu.SMEM),
                  pl.BlockSpec((tm, TOP_K), row), pl.BlockSpec((tm, d), row), pl.BlockSpec((tm, d), row),
                  pl.BlockSpec((1, 1, d), lambda i: (prev(i) // per_b, 0, 0)),
                  pl.BlockSpec((1, 1, d), lrow), pl.BlockSpec((1, 1, d), lrow),
                  pl.BlockSpec(memory_space=pl.ANY)],
        out_specs=pl.BlockSpec((tm, d), row),
        scratch_shapes=[pltpu.VMEM((TOP_K, tm * sub, LANES), U32), pltpu.VMEM((TOP_K, tm * sub, LANES), U32),
                        pltpu.SemaphoreType.DMA((2,))],
        compiler_params=_params("arbitrary"),
        name="moe_combine_ln",
    )(slot_tk, wt, x1, sh, gf, lng, lnb, ys)


def _moe(f, x1, ids, wts, rank, cnt, gf, lng, lnb, w1, w3, w2, ws1, ws3, ws2, seq, layer, alpha):
    t, d = x1.shape
    tme = min(512, t)
    counts = cnt[:, 0].astype(I32)
    tiles_e = (counts + tme - 1) // tme
    padded = tiles_e * tme
    base = jnp.cumsum(padded) - padded
    expert = jnp.arange(N_EXPERTS, dtype=I32)
    slots = jnp.sum(jnp.where(ids[:, :, None] == expert, base, 0), axis=-1) + rank
    ntile = (t * TOP_K) // tme + N_EXPERTS
    nslot = ntile * tme
    tile_ends = jnp.cumsum(tiles_e)
    texp = jnp.sum((jnp.arange(ntile, dtype=I32)[:, None] >= tile_ends[None, :]).astype(I32), axis=1)
    texp = jnp.minimum(texp, N_EXPERTS - 1)
    lag = EXPERT_RING - 1
    texp_step = jnp.concatenate([texp[:1]] * lag + [texp])
    nstep = tile_ends[-1:] + lag
    idle = jnp.arange(ntile + lag, dtype=I32) >= nstep
    texp_step = jnp.where(idle, jnp.take(texp_step, nstep - 1), texp_step)

    inv = _inverse_map(slots.reshape(-1), nslot)
    filler = jnp.arange(nslot + lag * tme, dtype=I32) % t
    tok = jnp.concatenate([jnp.where(inv >= 0, inv % t, filler[:nslot]), filler[nslot:]])

    shared = _shared_expert(f, ws1, ws3, ws2, t)
    ys = _experts(texp_step, nstep, tok, w1, w3, w2, f, layer, tme)
    tmc = min(256, seq)
    slot_tk = jnp.concatenate([slots.T.reshape(-1), jnp.zeros((tmc * TOP_K,), I32)])
    return _combine(slot_tk, wts.T, x1, shared, gf, lng, lnb, ys, seq, tmc, layer, alpha)


def kernel(x, c, ctx, c_ctx, w_ada, b_ada, ln_mix_g, ln_mix_b, ln_ffn_g, ln_ffn_b, attn_w_qkv, attn_w_o, attn_sink, gmlp_w_in, gmlp_b_in, gmlp_v_g, gmlp_v_b, gmlp_w_s, gmlp_b_s, gmlp_w_o, moe_w_router, moe_bias, moe_w1, moe_w3, moe_w2, moe_ws1, moe_ws3, moe_ws2):
    batch, seq, d = x.shape
    nctx = ctx.shape[1]
    depth = w_ada.shape[0]
    assert depth == 2, "layer 0 is the attention mixer, layer 1 the gMLP mixer"
    alpha = float((2 * depth) ** 0.25)
    t = batch * seq

    pad = (-(batch + 1)) % 8
    cc = jnp.concatenate([c, c_ctx[None, :], jnp.zeros((pad, d), F32)], axis=0)
    mods = _modulations(cc, w_ada, b_ada)

    def mod(layer, j, rows=slice(0, batch)):
        return mods[layer, rows, j * d:(j + 1) * d]

    def per_batch(layer, j):
        return mod(layer, j).reshape(batch, 1, d)

    def router(layer):
        w = jnp.pad(moe_w_router[layer], ((0, 0), (0, LANES - N_EXPERTS)))
        hi = w.astype(BF16)
        lo = (w - hi.astype(F32)).astype(BF16)
        return jnp.concatenate([hi, lo], axis=1), moe_bias[layer].reshape(N_EXPERTS, 1)

    def shared_w(layer):
        return moe_ws1[layer].astype(BF16), moe_ws3[layer].astype(BF16), moe_ws2[layer].astype(BF16)

    x2 = x.reshape(t, d)
    ln_mix_g, ln_mix_b, ln_ffn_g, ln_ffn_b = (
        p.reshape(depth, 1, d) for p in (ln_mix_g, ln_mix_b, ln_ffn_g, ln_ffn_b))

    qd = Q_PER_KV * N_KV_HEADS * HEAD_DIM
    kvd = N_KV_HEADS * HEAD_DIM
    wqkv = attn_w_qkv[0]
    dup = lambda w: jnp.concatenate([w.reshape(d, N_KV_HEADS, 1, HEAD_DIM)] * (LANES // HEAD_DIM), axis=2
                                    ).reshape(d, N_KV_HEADS * LANES)
    wk2, wv2 = dup(wqkv[:, qd:qd + kvd]), dup(wqkv[:, qd + kvd:])
    w_all = jnp.concatenate([wqkv[:, :qd] * (HEAD_DIM ** -0.5), wk2, wv2], axis=1).astype(BF16)
    q, k2, v2 = _qkv_proj(x2, per_batch(0, 1), per_batch(0, 0), w_all, _rope_tables(seq), seq)
    ctx_row = slice(batch, batch + 1)
    kc2, vc2 = _ctx_kv(ctx.reshape(batch * nctx, d), mod(0, 1, ctx_row), mod(0, 0, ctx_row),
                       jnp.concatenate([wk2, wv2], axis=1).astype(BF16))
    o = _attention(q, k2, v2, kc2, vc2, attn_sink[0], batch, seq, nctx)
    x1, f, ids, wts, rank, cnt = _attn_out(
        o, attn_w_o[0].astype(BF16), x2, per_batch(0, 2), per_batch(0, 4), per_batch(0, 3),
        ln_mix_g, ln_mix_b, *router(0), seq, 0, alpha)
    x2 = _moe(f, x1, ids, wts, rank, cnt, per_batch(0, 5), ln_ffn_g, ln_ffn_b,
              moe_w1, moe_w3, moe_w2, *shared_w(0), seq, 0, alpha)

    width = gmlp_w_in.shape[2] // 2
    w_in = gmlp_w_in[0].astype(BF16)
    b_in = gmlp_b_in[0].reshape(1, 2 * width)
    vg, vb = gmlp_v_g[0].reshape(1, width), gmlp_v_b[0].reshape(1, width)
    sc, sh = per_batch(1, 1), per_batch(1, 0)
    u = _gmlp_in(x2, sc, sh, w_in[:, :width], b_in[:, :width], vg, vb, seq, False)
    v = _gmlp_in(x2, sc, sh, w_in[:, width:], b_in[:, width:], vg, vb, seq, True)
    x1, f, ids, wts, rank, cnt = _gmlp_out(
        u, v, gmlp_w_s[0].astype(BF16), gmlp_b_s[0][:, :, None], gmlp_w_o[0].astype(BF16), x2,
        per_batch(1, 2), per_batch(1, 4), per_batch(1, 3), ln_mix_g, ln_mix_b, *router(1), seq, 1, alpha)
    x2 = _moe(f, x1, ids, wts, rank, cnt, per_batch(1, 5), ln_ffn_g, ln_ffn_b,
              moe_w1, moe_w3, moe_w2, *shared_w(1), seq, 1, alpha)
    return x2.reshape(batch, seq, d)
```

```python
import functools

import jax
import jax.numpy as jnp
from jax import lax
from jax.experimental import pallas as pl
from jax.experimental.pallas import tpu as pltpu
from jax.experimental.pallas import tpu_sc as plsc

F32 = jnp.float32
BF16 = jnp.bfloat16
I32 = jnp.int32

GRID_W = 64
N_KV_HEADS = 4
Q_PER_KV = 8
HEAD_DIM = 64
ROPE_HALF = 16
WINDOW = 128
Q_BLOCK = 128
ROPE_BASE = 10000.0
NEG_INF = -1e30
CHUNK = 128
N_GMLP_GROUPS = 8
N_EXPERTS = 64
TOP_K = 8
N_EXPERT_GROUPS = 8
EXPERTS_PER_GROUP = 8
TOPK_GROUPS = 4
ROUTED_SCALE = 2.5
LN_EPS = 1e-5

LANES = 128
VMEM_LIMIT_BYTES = 56 * 1024 * 1024


def _params(*sem):
    return pltpu.CompilerParams(dimension_semantics=sem, vmem_limit_bytes=VMEM_LIMIT_BYTES)


def _silu(a):
    return a * jax.nn.sigmoid(a)


def _layernorm(v, g, b):
    mu = jnp.mean(v, axis=-1, keepdims=True)
    d = v - mu
    var = jnp.mean(d * d, axis=-1, keepdims=True)
    return d * lax.rsqrt(var + LN_EPS) * g + b


def _bdot(a, b):
    return jnp.dot(a, b, preferred_element_type=F32)


U32 = jnp.uint32


def _pack_pairs(a):
    n = a.shape[1] // 2
    lo = lax.bitcast_convert_type(a[:, :n].astype(BF16).astype(F32), U32)
    hi = lax.bitcast_convert_type(a[:, n:].astype(BF16).astype(F32), U32)
    return hi | (lo >> 16)


def _unpack_pairs(p):
    lo = lax.bitcast_convert_type(p << 16, F32)
    hi = lax.bitcast_convert_type(p & jnp.uint32(0xFFFF0000), F32)
    return lo, hi


def _store_row_tiles(ref, packed):
    r, n = packed.shape
    sub = n // LANES
    for j in range(sub):
        ref[pl.ds(j, r, stride=sub), :] = packed[:, LANES * j:LANES * (j + 1)]


def _load_row_tiles(ref, r):
    sub = ref.shape[0] // r
    pieces = [_unpack_pairs(ref[pl.ds(j, r, stride=sub), :]) for j in range(sub)]
    return [p[0] for p in pieces], [p[1] for p in pieces]


def _load_rows_bf16(ref, r):
    lo, hi = _load_row_tiles(ref, r)
    return jnp.concatenate(lo + hi, axis=1).astype(BF16)


def _mod_kernel(c_ref, w_ref, b_ref, o_ref):
    a = _silu(c_ref[...]).astype(BF16)
    o_ref[0] = _bdot(a, w_ref[0].astype(BF16)) + b_ref[0]


def _modulations(cc, w_ada, b_ada):
    depth, d, n6 = w_ada.shape
    r = cc.shape[0]
    tn = max(w for w in range(LANES, min(1024, n6) + 1, LANES) if n6 % w == 0)
    return pl.pallas_call(
        _mod_kernel,
        out_shape=jax.ShapeDtypeStruct((depth, r, n6), F32),
        grid=(depth, n6 // tn),
        in_specs=[pl.BlockSpec((r, d), lambda l, j: (0, 0)),
                  pl.BlockSpec((1, d, tn), lambda l, j: (l, 0, j)),
                  pl.BlockSpec((1, 1, tn), lambda l, j: (l, 0, j))],
        out_specs=pl.BlockSpec((1, r, tn), lambda l, j: (l, 0, j)),
        compiler_params=_params("parallel", "parallel"),
        name="adaln_mod",
    )(cc, w_ada, b_ada.reshape(depth, 1, n6))


def _rope_tables(seq):
    rows = seq // GRID_W
    row_ids = jnp.repeat(jnp.arange(rows, dtype=F32), GRID_W)
    col_ids = jnp.tile(jnp.arange(GRID_W, dtype=F32), rows)
    inv_freq = ROPE_BASE ** (-jnp.arange(0, 2 * ROPE_HALF, 2, dtype=F32) / (2 * ROPE_HALF))
    ar, ac = row_ids[:, None] * inv_freq, col_ids[:, None] * inv_freq
    z = jnp.zeros_like(ar)
    cos = jnp.concatenate([jnp.cos(ar), jnp.cos(ar), jnp.cos(ac), jnp.cos(ac)], axis=-1)
    sin_lo = jnp.concatenate([-jnp.sin(ar), z, -jnp.sin(ac), z], axis=-1)
    sin_hi = jnp.concatenate([z, jnp.sin(ar), z, jnp.sin(ac)], axis=-1)
    rep = LANES // HEAD_DIM
    return tuple(jnp.tile(t, (1, rep)) for t in (cos, sin_lo, sin_hi))


def _qkv_kernel(x_ref, sc_ref, sh_ref, w_ref, cos_ref, slo_ref, shi_ref, q_ref, k_ref, v_ref, *, cw):
    h = (x_ref[...] * (1.0 + sc_ref[0]) + sh_ref[0]).astype(BF16)
    cos, slo, shi = cos_ref[...], slo_ref[...], shi_ref[...]

    def rope(a):
        return (a * cos + pltpu.roll(a, LANES - ROPE_HALF, 1) * slo
                + pltpu.roll(a, ROPE_HALF, 1) * shi)

    qd, kd = q_ref.shape[1], k_ref.shape[1]
    for c0 in range(0, qd, cw):
        acc = _bdot(h, w_ref[:, c0:c0 + cw])
        for b0 in range(0, cw, LANES):
            q_ref[:, c0 + b0:c0 + b0 + LANES] = rope(acc[:, b0:b0 + LANES]).astype(BF16)
    acc = _bdot(h, w_ref[:, qd:qd + kd])
    for b0 in range(0, kd, LANES):
        k_ref[:, b0:b0 + LANES] = rope(acc[:, b0:b0 + LANES]).astype(BF16)
    v_ref[...] = _bdot(h, w_ref[:, qd + kd:qd + 2 * kd]).astype(BF16)


def _qkv_proj(x2, sc, sh, w, tables, seq):
    t, d = x2.shape
    kd = N_KV_HEADS * LANES
    qd = w.shape[1] - 2 * kd
    tm = min(512, seq)
    per_b = seq // tm
    row = lambda i: (i, 0)
    mod = lambda i: (i // per_b, 0, 0)
    tab = lambda i: (i % per_b, 0)
    return pl.pallas_call(
        functools.partial(_qkv_kernel, cw=min(512, qd)),
        out_shape=(jax.ShapeDtypeStruct((t, qd), BF16), jax.ShapeDtypeStruct((t, kd), BF16),
                   jax.ShapeDtypeStruct((t, kd), BF16)),
        grid=(t // tm,),
        in_specs=[pl.BlockSpec((tm, d), row), pl.BlockSpec((1, 1, d), mod), pl.BlockSpec((1, 1, d), mod),
                  pl.BlockSpec(w.shape, lambda i: (0, 0)),
                  pl.BlockSpec((tm, LANES), tab), pl.BlockSpec((tm, LANES), tab), pl.BlockSpec((tm, LANES), tab)],
        out_specs=(pl.BlockSpec((tm, qd), row), pl.BlockSpec((tm, kd), row), pl.BlockSpec((tm, kd), row)),
        compiler_params=_params("parallel"),
        name="qkv_rope",
    )(x2, sc, sh, w, *tables)


def _ctxkv_kernel(x_ref, sc_ref, sh_ref, w_ref, k_ref, v_ref):
    h = (x_ref[...] * (1.0 + sc_ref[...]) + sh_ref[...]).astype(BF16)
    kd = k_ref.shape[1]
    k_ref[...] = _bdot(h, w_ref[:, :kd]).astype(BF16)
    v_ref[...] = _bdot(h, w_ref[:, kd:]).astype(BF16)


def _ctx_kv(c2, sc, sh, w):
    t, d = c2.shape
    kd = w.shape[1] // 2
    tm = min(512, t)
    row = lambda i: (i, 0)
    fix = lambda i: (0, 0)
    return pl.pallas_call(
        _ctxkv_kernel,
        out_shape=(jax.ShapeDtypeStruct((t, kd), BF16), jax.ShapeDtypeStruct((t, kd), BF16)),
        grid=(t // tm,),
        in_specs=[pl.BlockSpec((tm, d), row), pl.BlockSpec((1, d), fix), pl.BlockSpec((1, d), fix),
                  pl.BlockSpec(w.shape, fix)],
        out_specs=(pl.BlockSpec((tm, kd), row), pl.BlockSpec((tm, kd), row)),
        compiler_params=_params("parallel"),
        name="ctx_kv",
    )(c2, sc, sh, w)


def _attn_kernel(sink_ref, q_ref, k_ref, v_ref, kc_ref, vc_ref, o_ref):
    for b in range(q_ref.shape[1] // Q_BLOCK):
        _attn_block(sink_ref, q_ref, k_ref, v_ref, kc_ref, vc_ref, o_ref, b)


def _attn_block(sink_ref, q_ref, k_ref, v_ref, kc_ref, vc_ref, o_ref, b):
    kv = pl.program_id(1)
    i = pl.program_id(2) * (q_ref.shape[1] // Q_BLOCK) + b
    seq, nctx = k_ref.shape[1], kc_ref.shape[1]
    nwin = Q_BLOCK + 2 * WINDOW
    start = pl.multiple_of(jnp.clip(i * Q_BLOCK - WINDOW, 0, seq - nwin), Q_BLOCK)
    nkey = nctx + nwin
    kall = jnp.concatenate([kc_ref[0], k_ref[0, pl.ds(start, nwin), :]], axis=0)
    vall = jnp.concatenate([vc_ref[0], v_ref[0, pl.ds(start, nwin), :]], axis=0)
    low = lax.broadcasted_iota(I32, (nkey, LANES), 1) < HEAD_DIM
    zero = jnp.zeros((nkey, LANES), BF16)
    npair = Q_PER_KV // 2
    q = q_ref[0, Q_BLOCK * b:Q_BLOCK * (b + 1), :]
    qs = jnp.concatenate([q[:, LANES * j:LANES * (j + 1)] for j in range(npair)], axis=0)
    nrow = npair * Q_BLOCK
    rows = lax.broadcasted_iota(I32, (Q_BLOCK, nkey), 0)
    cols = lax.broadcasted_iota(I32, (Q_BLOCK, nkey), 1)
    dist = (i * Q_BLOCK - start + nctx) + rows - cols
    valid = (cols < nctx) | (jnp.abs(dist) <= WINDOW)
    contract_last = (((1,), (1,)), ((), ()))
    out = jnp.zeros((nrow, LANES), F32)
    for par in range(2):
        keep = low if par == 0 else jnp.logical_not(low)
        kh = jnp.where(keep, kall, zero)
        vh = jnp.where(keep, vall, zero)
        s = lax.dot_general(qs, kh, contract_last, preferred_element_type=F32)
        s = jnp.concatenate(
            [jnp.where(valid, s[Q_BLOCK * j:Q_BLOCK * (j + 1), :], NEG_INF) for j in range(npair)], axis=0)
        sink = jnp.concatenate(
            [jnp.full((Q_BLOCK, 1), sink_ref[kv * Q_PER_KV + 2 * j + par], F32) for j in range(npair)], axis=0)
        m = jnp.maximum(jnp.max(s, axis=1, keepdims=True), sink)
        p = jnp.exp(s - m)
        den = jnp.sum(p, axis=1, keepdims=True) + jnp.exp(sink - m)
        out = out + _bdot(p.astype(BF16), vh) / den
    for j in range(npair):
        o_ref[0, Q_BLOCK * b:Q_BLOCK * (b + 1), LANES * j:LANES * (j + 1)] = (
            out[Q_BLOCK * j:Q_BLOCK * (j + 1), :].astype(BF16))


def _attention(q, k2, v2, kc2, vc2, sink, batch, seq, nctx):
    qd = q.shape[1]
    gw = Q_PER_KV * HEAD_DIM
    q3 = q.reshape(batch, seq, qd)
    k3, v3 = k2.reshape(batch, seq, -1), v2.reshape(batch, seq, -1)
    kc3, vc3 = kc2.reshape(batch, nctx, -1), vc2.reshape(batch, nctx, -1)
    full = lambda b, h, i: (b, 0, h)
    blk = lambda b, h, i: (b, i, h)
    rows = next(r * Q_BLOCK for r in (8, 4, 2, 1) if seq % (r * Q_BLOCK) == 0)
    out = pl.pallas_call(
        _attn_kernel,
        out_shape=jax.ShapeDtypeStruct((batch, seq, qd), BF16),
        grid=(batch, N_KV_HEADS, seq // rows),
        in_specs=[pl.BlockSpec(memory_space=pltpu.SMEM),
                  pl.BlockSpec((1, rows, gw), blk),
                  pl.BlockSpec((1, seq, LANES), full), pl.BlockSpec((1, seq, LANES), full),
                  pl.BlockSpec((1, nctx, LANES), full), pl.BlockSpec((1, nctx, LANES), full)],
        out_specs=pl.BlockSpec((1, rows, gw), blk),
        compiler_params=_params("parallel", "parallel", "parallel"),
        name="window_gqa",
    )(sink, q3, k3, v3, kc3, vc3)
    return out.reshape(batch * seq, qd)


def _split_bf16(a):
    hi = a.astype(BF16)
    return hi, (a - hi.astype(F32)).astype(BF16)


def _post_mixer(x, y, gm, lng, lnb, scf, shf, wr_ref, rb_ref, run_ref,
                x1_ref, f_ref, ids_ref, wts_ref, rank_ref, cnt_ref, alpha):
    tm = x.shape[0]
    x1 = _layernorm(alpha * x + gm * y, lng, lnb)
    x1_ref[...] = x1
    f = x1 * (1.0 + scf) + shf
    _store_row_tiles(f_ref, _pack_pairs(f))

    fh, fl = _split_bf16(f)
    prod = _bdot(fh, wr_ref[...]) + _bdot(fl, wr_ref[...])
    logits = (prod[:, :LANES] + prod[:, LANES:]).T[:N_EXPERTS, :]
    scores = jax.nn.sigmoid(logits)
    biased = scores + rb_ref[...]

    eg = EXPERTS_PER_GROUP
    sub = lax.broadcasted_iota(I32, (eg, tm), 0)
    gscore = []
    for g in range(N_EXPERT_GROUPS):
        tg = biased[eg * g:eg * (g + 1), :]
        m1 = jnp.max(tg, axis=0, keepdims=True)
        i1 = jnp.min(jnp.where(tg == m1, sub, eg), axis=0, keepdims=True)
        m2 = jnp.max(jnp.where(sub == i1, -jnp.inf, tg), axis=0, keepdims=True)
        gscore.append(m1 + m2)
    gsel = [jnp.zeros((1, tm), jnp.bool_) for _ in range(N_EXPERT_GROUPS)]
    for _ in range(TOPK_GROUPS):
        best = functools.reduce(jnp.maximum, gscore)
        taken = jnp.zeros((1, tm), jnp.bool_)
        for g in range(N_EXPERT_GROUPS):
            hit = jnp.logical_and(gscore[g] == best, jnp.logical_not(taken))
            taken = jnp.logical_or(taken, hit)
            gsel[g] = jnp.logical_or(gsel[g], hit)
            gscore[g] = jnp.where(hit, -jnp.inf, gscore[g])
    cur = jnp.concatenate(
        [jnp.where(gsel[g], biased[eg * g:eg * (g + 1), :], -jnp.inf) for g in range(N_EXPERT_GROUPS)], axis=0)

    eidx = lax.broadcasted_iota(I32, (N_EXPERTS, tm), 0)
    picks, wts, hots = [], [], []
    for _ in range(TOP_K):
        m = jnp.max(cur, axis=0, keepdims=True)
        idx = jnp.min(jnp.where(cur == m, eidx, N_EXPERTS), axis=0, keepdims=True)
        hot = eidx == idx
        picks.append(idx)
        wts.append(jnp.sum(jnp.where(hot, scores, 0.0), axis=0, keepdims=True))
        hots.append(hot)
        cur = jnp.where(hot, -jnp.inf, cur)
    wsum = functools.reduce(jnp.add, wts)

    assigned = functools.reduce(jnp.add, [h.astype(F32) for h in hots])
    before = (lax.broadcasted_iota(I32, (tm, tm), 0) < lax.broadcasted_iota(I32, (tm, tm), 1)).astype(BF16)
    pos = _bdot(assigned.astype(BF16), before) + run_ref[:, 0:1]
    for k in range(TOP_K):
        ids_ref[k:k + 1, :] = picks[k]
        wts_ref[k:k + 1, :] = wts[k] / wsum * ROUTED_SCALE
        rank_ref[k:k + 1, :] = jnp.sum(jnp.where(hots[k], pos, 0.0), axis=0, keepdims=True).astype(I32)
    run_ref[...] = run_ref[...] + jnp.sum(assigned, axis=1, keepdims=True)
    cnt_ref[...] = run_ref[...]


def _route_out_shapes(t, d):
    return (jax.ShapeDtypeStruct((t, d), F32), jax.ShapeDtypeStruct((t * d // (2 * LANES), LANES), U32),
            jax.ShapeDtypeStruct((TOP_K, t), I32), jax.ShapeDtypeStruct((TOP_K, t), F32),
            jax.ShapeDtypeStruct((TOP_K, t), I32), jax.ShapeDtypeStruct((N_EXPERTS, LANES), F32))


def _route_out_specs(tm, d):
    row = lambda i: (i, 0)
    col = lambda i: (0, i)
    return (pl.BlockSpec((tm, d), row), pl.BlockSpec((tm * d // (2 * LANES), LANES), row),
            pl.BlockSpec((TOP_K, tm), col), pl.BlockSpec((TOP_K, tm), col), pl.BlockSpec((TOP_K, tm), col),
            pl.BlockSpec((N_EXPERTS, LANES), lambda i: (0, 0)))


def _route_in_specs(tm, d, per_b, layer):
    mod = lambda i: (i // per_b, 0, 0)
    lrow = lambda i: (layer, 0, 0)
    fix = lambda i: (0, 0)
    return [pl.BlockSpec((tm, d), lambda i: (i, 0)),
            pl.BlockSpec((1, 1, d), mod), pl.BlockSpec((1, 1, d), mod), pl.BlockSpec((1, 1, d), mod),
            pl.BlockSpec((1, 1, d), lrow), pl.BlockSpec((1, 1, d), lrow),
            pl.BlockSpec((d, 2 * LANES), fix),
            pl.BlockSpec((N_EXPERTS, 1), fix)]


def _oproj_kernel(o_ref, wo_ref, x_ref, gm_ref, scf_ref, shf_ref, lng_ref, lnb_ref, wr_ref, rb_ref,
                  x1_ref, f_ref, ids_ref, wts_ref, rank_ref, cnt_ref, run_ref, *, alpha):
    @pl.when(pl.program_id(0) == 0)
    def _():
        run_ref[...] = jnp.zeros_like(run_ref)

    y = _bdot(o_ref[...], wo_ref[...])
    _post_mixer(x_ref[...], y, gm_ref[0], lng_ref[0], lnb_ref[0], scf_ref[0], shf_ref[0],
                wr_ref, rb_ref, run_ref, x1_ref, f_ref, ids_ref, wts_ref, rank_ref, cnt_ref, alpha)


def _attn_out(o, wo, x2, gm, scf, shf, lng, lnb, wr, rb, seq, layer, alpha):
    t, d = x2.shape
    tm = min(512, seq)
    return pl.pallas_call(
        functools.partial(_oproj_kernel, alpha=alpha),
        out_shape=_route_out_shapes(t, d),
        grid=(t // tm,),
        in_specs=[pl.BlockSpec((tm, o.shape[1]), lambda i: (i, 0)), pl.BlockSpec(wo.shape, lambda i: (0, 0))]
        + _route_in_specs(tm, d, seq // tm, layer),
        out_specs=_route_out_specs(tm, d),
        scratch_shapes=[pltpu.VMEM((N_EXPERTS, LANES), F32)],
        compiler_params=_params("arbitrary"),
        name="attn_out_route",
    )(o, wo, x2, gm, scf, shf, lng, lnb, wr, rb)


def _gmlp_in_kernel(x_ref, sc_ref, sh_ref, w_ref, b_ref, g_ref, beta_ref, o_ref, *, cw, normalize):
    h = (x_ref[...] * (1.0 + sc_ref[0]) + sh_ref[0]).astype(BF16)
    n = o_ref.shape[1]
    inv_sqrt2 = 0.7071067811865476
    parts = []
    for c0 in range(0, n, cw):
        z = _bdot(h, w_ref[:, c0:c0 + cw]) + b_ref[:, c0:c0 + cw]
        z = 0.5 * z * (1.0 + lax.erf(z * inv_sqrt2))
        if normalize:
            parts.append(z)
        else:
            o_ref[:, c0:c0 + cw] = z.astype(BF16)
    if normalize:
        v = jnp.concatenate(parts, axis=1)
        o_ref[...] = _layernorm(v, g_ref[...], beta_ref[...]).astype(BF16)


def _gmlp_in(x2, sc, sh, w, b, g, beta, seq, normalize):
    t, d = x2.shape
    n = w.shape[1]
    tm = min(256 if normalize else 512, seq)
    per_b = seq // tm
    row = lambda i: (i, 0)
    mod = lambda i: (i // per_b, 0, 0)
    fix = lambda i: (0, 0)
    return pl.pallas_call(
        functools.partial(_gmlp_in_kernel, cw=min(512, n), normalize=normalize),
        out_shape=jax.ShapeDtypeStruct((t, n), BF16),
        grid=(t // tm,),
        in_specs=[pl.BlockSpec((tm, d), row), pl.BlockSpec((1, 1, d), mod), pl.BlockSpec((1, 1, d), mod),
                  pl.BlockSpec(w.shape, fix), pl.BlockSpec((1, n), fix), pl.BlockSpec((1, n), fix),
                  pl.BlockSpec((1, n), fix)],
        out_specs=pl.BlockSpec((tm, n), row),
        compiler_params=_params("parallel"),
        name="gmlp_in_v" if normalize else "gmlp_in_u",
    )(x2, sc, sh, w, b, g, beta)


def _gmlp_out_kernel(u_ref, v_ref, ws_ref, bs_ref, wo_ref, x_ref, gm_ref, scf_ref, shf_ref, lng_ref, lnb_ref,
                     wr_ref, rb_ref, x1_ref, f_ref, ids_ref, wts_ref, rank_ref, cnt_ref,
                     run_ref, gated_ref, *, alpha):
    @pl.when(pl.program_id(0) == 0)
    def _():
        run_ref[...] = jnp.zeros_like(run_ref)

    tm, width = u_ref.shape
    gd = width // N_GMLP_GROUPS
    for r0 in range(0, tm, CHUNK):
        for g in range(N_GMLP_GROUPS):
            c0 = g * gd
            mixed = _bdot(ws_ref[g], v_ref[r0:r0 + CHUNK, c0:c0 + gd]) + bs_ref[g]
            gated_ref[r0:r0 + CHUNK, c0:c0 + gd] = (
                u_ref[r0:r0 + CHUNK, c0:c0 + gd].astype(F32) * mixed).astype(BF16)
    y = _bdot(gated_ref[...], wo_ref[...])
    _post_mixer(x_ref[...], y, gm_ref[0], lng_ref[0], lnb_ref[0], scf_ref[0], shf_ref[0],
                wr_ref, rb_ref, run_ref, x1_ref, f_ref, ids_ref, wts_ref, rank_ref, cnt_ref, alpha)


def _gmlp_out(u, v, ws, bs, wo, x2, gm, scf, shf, lng, lnb, wr, rb, seq, layer, alpha):
    t, d = x2.shape
    width = u.shape[1]
    tm = min(256, seq)
    row = lambda i: (i, 0)
    return pl.pallas_call(
        functools.partial(_gmlp_out_kernel, alpha=alpha),
        out_shape=_route_out_shapes(t, d),
        grid=(t // tm,),
        in_specs=[pl.BlockSpec((tm, width), row), pl.BlockSpec((tm, width), row),
                  pl.BlockSpec(ws.shape, lambda i: (0, 0, 0)), pl.BlockSpec(bs.shape, lambda i: (0, 0, 0)),
                  pl.BlockSpec(wo.shape, lambda i: (0, 0))]
        + _route_in_specs(tm, d, seq // tm, layer),
        out_specs=_route_out_specs(tm, d),
        scratch_shapes=[pltpu.VMEM((N_EXPERTS, LANES), F32), pltpu.VMEM((tm, width), BF16)],
        compiler_params=_params("arbitrary"),
        name="gmlp_out_route",
    )(u, v, ws, bs, wo, x2, gm, scf, shf, lng, lnb, wr, rb)


SC_CORES, SC_SUBCORES, SC_LANES = 2, 16, 16
MXU_N = 256


def _shared_kernel(f_ref, ws1_ref, ws3_ref, ws2_ref, sh_ref):
    h = _load_rows_bf16(f_ref, sh_ref.shape[0])
    a = _silu(_bdot(h, ws1_ref[...])) * _bdot(h, ws3_ref[...])
    sh_ref[...] = _bdot(a.astype(BF16), ws2_ref[...]).astype(BF16)


def _shared_expert(f, ws1, ws3, ws2, t):
    sub = f.shape[0] // t
    d = ws1.shape[0]
    tm = min(512, t)
    fix = lambda i: (0, 0)
    return pl.pallas_call(
        _shared_kernel,
        out_shape=jax.ShapeDtypeStruct((t, d), BF16),
        grid=(t // tm,),
        in_specs=[pl.BlockSpec((tm * sub, LANES), lambda i: (i, 0)),
                  pl.BlockSpec(ws1.shape, fix), pl.BlockSpec(ws3.shape, fix), pl.BlockSpec(ws2.shape, fix)],
        out_specs=pl.BlockSpec((tm, d), lambda i: (i, 0)),
        compiler_params=_params("parallel"),
        name="moe_shared",
    )(f, ws1, ws3, ws2)


def _inverse_map(slots_flat, nslot):
    n = slots_flat.shape[0]
    workers = SC_CORES * SC_SUBCORES
    per_w = nslot // workers
    chunk = min(8192, n)
    assert nslot % (workers * SC_LANES) == 0 and n % chunk == 0 and chunk % SC_LANES == 0
    mesh = plsc.VectorSubcoreMesh(core_axis_name="c", subcore_axis_name="s")

    @functools.partial(
        pl.kernel, out_type=jax.ShapeDtypeStruct((nslot,), I32), mesh=mesh,
        scratch_types=[pltpu.VMEM((per_w,), I32), pltpu.VMEM((chunk,), I32)],
        compiler_params=pltpu.CompilerParams(needs_layout_passes=False))
    def inverse(slots_hbm, inv_hbm, local, buf):
        lo = (lax.axis_index("s") * SC_CORES + lax.axis_index("c")) * per_w
        unused = jnp.full((SC_LANES,), -1, I32)

        @pl.loop(0, per_w, step=SC_LANES)
        def _(j):
            local[pl.ds(j, SC_LANES)] = unused

        lane = lax.iota(I32, SC_LANES)

        @pl.loop(0, n, step=chunk)
        def _(c0):
            pltpu.sync_copy(slots_hbm.at[pl.ds(c0, chunk)], buf)

            @pl.loop(0, chunk, step=SC_LANES)
            def _(j):
                idx = buf[pl.ds(j, SC_LANES)] - lo
                mine = jnp.logical_and(idx >= 0, idx < per_w)
                plsc.store_scatter(local, [idx], c0 + j + lane, mask=mine)

        pltpu.sync_copy(local, inv_hbm.at[pl.ds(lo, per_w)])

    return inverse(slots_flat)


EXPERT_RING = 3


def _experts_kernel(texp_ref, nstep_ref, tok_ref, w1_ref, w3_ref, w2_ref, f_hbm, y_ref,
                    w1b, w3b, w2b, xa, xb, xc_, gsem, *, tme):
    i = pl.program_id(0)
    last = nstep_ref[0] - 1
    ring = (xa, xb, xc_)
    sub = xa.shape[0] // tme
    d, fdim = w1b.shape
    half = d // 2

    @pl.when(i == 0)
    def _():
        xb[...] = jnp.zeros_like(xb)
        xc_[...] = jnp.zeros_like(xc_)

    @pl.when(jnp.logical_and(i <= last,
                             jnp.logical_or(i == 0, texp_ref[i] != texp_ref[jnp.maximum(i - 1, 0)])))
    def _():
        w1b[...] = w1_ref[...].astype(BF16)
        w3b[...] = w3_ref[...].astype(BF16)
        w2b[...] = w2_ref[...].astype(BF16)

    def whole_tile(buf, sem):
        return pltpu.make_async_copy(f_hbm.at[pl.ds(0, tme * sub), :], buf, sem)

    def step(par):
        nxt, prv = (par + 1) % EXPERT_RING, (par + 2) % EXPERT_RING
        xg, xc = ring[par], ring[nxt]

        @pl.when(i >= EXPERT_RING - 1)
        def _():
            whole_tile(xc, gsem.at[nxt]).wait()

        pw1, pw2 = min(MXU_N, fdim), min(MXU_N, half)
        pieces = 2 * (fdim // pw1) + 2 * (half // pw2)
        per_piece = -(-tme // pieces)
        issued = [0]

        def issue_rows():
            for r in range(issued[0], min(issued[0] + per_piece, tme)):
                tok = pl.multiple_of(tok_ref[r] * sub, sub)
                pltpu.make_async_copy(f_hbm.at[pl.ds(tok, sub), :], xg.at[pl.ds(r * sub, sub), :],
                                      gsem.at[par]).start(priority=r % 2)
            issued[0] = min(issued[0] + per_piece, tme)

        h = _load_rows_bf16(xc, tme)
        gate, up = [], []
        for c0 in range(0, fdim, pw1):
            gate.append(_bdot(h, w1b[:, c0:c0 + pw1]))
            issue_rows()
        for c0 in range(0, fdim, pw1):
            up.append(_bdot(h, w3b[:, c0:c0 + pw1]))
            issue_rows()
        a = (_silu(jnp.concatenate(gate, axis=1)) * jnp.concatenate(up, axis=1)).astype(BF16)
        for c0 in range(0, half, pw2):
            lo = _bdot(a, w2b[:, c0:c0 + pw2])
            issue_rows()
            hi = _bdot(a, w2b[:, half + c0:half + c0 + pw2])
            issue_rows()
            packed = _pack_pairs(jnp.concatenate([lo, hi], axis=1))
            for b0 in range(0, pw2, LANES):
                y_ref[pl.ds((c0 + b0) // LANES, tme, stride=sub), :] = packed[:, b0:b0 + LANES]
        assert issued[0] == tme

        @pl.when(i == last)
        def _():
            whole_tile(ring[prv], gsem.at[prv]).wait()
            whole_tile(xg, gsem.at[par]).wait()

    for par in range(EXPERT_RING):
        pl.when(jnp.logical_and(i <= last, lax.rem(i, EXPERT_RING) == par))(functools.partial(step, par))


def _experts(texp, nstep, tok, w1, w3, w2, f, layer, tme):
    d, fdim = w1.shape[-2:]
    sub = d // (2 * LANES)
    lag = EXPERT_RING - 1
    ntile = texp.shape[0] - lag
    wsel = lambda i, te, ns: (layer, te[i], 0, 0)
    return pl.pallas_call(
        functools.partial(_experts_kernel, tme=tme),
        out_shape=jax.ShapeDtypeStruct((ntile * tme * sub, LANES), U32),
        grid_spec=pltpu.PrefetchScalarGridSpec(
            num_scalar_prefetch=2,
            grid=(ntile + lag,),
            in_specs=[pl.BlockSpec((tme,), lambda i, te, ns: (i,), memory_space=pltpu.SMEM),
                      pl.BlockSpec((None, None, d, fdim), wsel), pl.BlockSpec((None, None, d, fdim), wsel),
                      pl.BlockSpec((None, None, fdim, d), wsel),
                      pl.BlockSpec(memory_space=pl.ANY)],
            out_specs=pl.BlockSpec((tme * sub, LANES),
                                   lambda i, te, ns: (jnp.clip(i - lag, 0, ns[0] - lag - 1), 0)),
            scratch_shapes=[pltpu.VMEM((d, fdim), BF16), pltpu.VMEM((d, fdim), BF16), pltpu.VMEM((fdim, d), BF16)]
            + [pltpu.VMEM((tme * sub, LANES), U32)] * EXPERT_RING
            + [pltpu.SemaphoreType.DMA((EXPERT_RING,))]),
        compiler_params=_params("arbitrary"),
        name="moe_experts",
    )(texp, nstep, tok, w1, w3, w2, f)


def _combine_kernel(slot_ref, wt_ref, x1_ref, sh_ref, gf_ref, lng_ref, lnb_ref, ys_hbm, o_ref, ga, gb, gsem, *, alpha):
    i = pl.program_id(0)
    last = pl.num_programs(0) - 1
    tm = x1_ref.shape[0]
    sub = ga.shape[1] // tm

    @pl.when(i == 0)
    def _():
        gb[...] = jnp.zeros_like(gb)

    def whole_plane(buf, k, sem):
        return pltpu.make_async_copy(ys_hbm.at[pl.ds(0, tm * sub), :], buf.at[k], sem)

    def step(par):
        gg, gc = (ga, gb) if par == 0 else (gb, ga)

        @pl.when(i > 0)
        def _():
            for k in range(TOP_K):
                whole_plane(gc, k, gsem.at[1 - par]).wait()

        def issue(t0, t1):
            for tt in range(t0, t1):
                for k in range(TOP_K):
                    row = pl.multiple_of(slot_ref[tt * TOP_K + k] * sub, sub)
                    pltpu.make_async_copy(ys_hbm.at[pl.ds(row, sub), :], gg.at[k, pl.ds(tt * sub, sub), :],
                                          gsem.at[par]).start(priority=k % 2)

        lo = hi = None
        for k in range(TOP_K):
            issue(k * tm // TOP_K, (k + 1) * tm // TOP_K)
            w = wt_ref[:, k:k + 1]
            rl, rh = _load_row_tiles(gc.at[k], tm)
            lo = [w * b for b in rl] if lo is None else [a + w * b for a, b in zip(lo, rl)]
            hi = [w * b for b in rh] if hi is None else [a + w * b for a, b in zip(hi, rh)]
        f = jnp.concatenate(lo + hi, axis=1) + sh_ref[...].astype(F32)
        o_ref[...] = _layernorm(alpha * x1_ref[...] + gf_ref[0] * f, lng_ref[0], lnb_ref[0])

        @pl.when(i == last)
        def _():
            for k in range(TOP_K):
                whole_plane(gg, k, gsem.at[par]).wait()

    for par in range(2):
        pl.when(jnp.bitwise_and(i, 1) == par)(functools.partial(step, par))


def _combine(slot_tk, wt, x1, sh, gf, lng, lnb, ys, seq, tm, layer, alpha):
    t, d = x1.shape
    sub = d // (2 * LANES)
    per_b = seq // tm
    nstep = t // tm + 1
    prev = lambda i: jnp.maximum(i - 1, 0)
    row = lambda i: (prev(i), 0)
    lrow = lambda i: (layer, 0, 0)
    return pl.pallas_call(
        functools.partial(_combine_kernel, alpha=alpha),
        out_shape=jax.ShapeDtypeStruct((t, d), F32),
        grid=(nstep,),
        in_specs=[pl.BlockSpec((tm * TOP_K,), lambda i: (i,), memory_space=pltpu.SMEM),
                  pl.BlockSpec((tm, TOP_K), row), pl.BlockSpec((tm, d), row), pl.BlockSpec((tm, d), row),
                  pl.BlockSpec((1, 1, d), lambda i: (prev(i) // per_b, 0, 0)),
                  pl.BlockSpec((1, 1, d), lrow), pl.BlockSpec((1, 1, d), lrow),
                  pl.BlockSpec(memory_space=pl.ANY)],
        out_specs=pl.BlockSpec((tm, d), row),
        scratch_shapes=[pltpu.VMEM((TOP_K, tm * sub, LANES), U32), pltpu.VMEM((TOP_K, tm * sub, LANES), U32),
                        pltpu.SemaphoreType.DMA((2,))],
        compiler_params=_params("arbitrary"),
        name="moe_combine_ln",
    )(slot_tk, wt, x1, sh, gf, lng, lnb, ys)


def _moe(f, x1, ids, wts, rank, cnt, gf, lng, lnb, w1, w3, w2, ws1, ws3, ws2, seq, layer, alpha):
    t, d = x1.shape
    tme = min(512, t)
    counts = cnt[:, 0].astype(I32)
    tiles_e = (counts + tme - 1) // tme
    padded = tiles_e * tme
    base = jnp.cumsum(padded) - padded
    expert = jnp.arange(N_EXPERTS, dtype=I32)
    slots = jnp.sum(jnp.where(ids[:, :, None] == expert, base, 0), axis=-1) + rank
    ntile = (t * TOP_K) // tme + N_EXPERTS
    nslot = ntile * tme
    tile_ends = jnp.cumsum(tiles_e)
    texp = jnp.sum((jnp.arange(ntile, dtype=I32)[:, None] >= tile_ends[None, :]).astype(I32), axis=1)
    texp = jnp.minimum(texp, N_EXPERTS - 1)
    lag = EXPERT_RING - 1
    texp_step = jnp.concatenate([texp[:1]] * lag + [texp])
    nstep = tile_ends[-1:] + lag
    idle = jnp.arange(ntile + lag, dtype=I32) >= nstep
    texp_step = jnp.where(idle, jnp.take(texp_step, nstep - 1), texp_step)

    inv = _inverse_map(slots.reshape(-1), nslot)
    filler = jnp.arange(nslot + lag * tme, dtype=I32) % t
    tok = jnp.concatenate([jnp.where(inv >= 0, inv % t, filler[:nslot]), filler[nslot:]])

    shared = _shared_expert(f, ws1, ws3, ws2, t)
    ys = _experts(texp_step, nstep, tok, w1, w3, w2, f, layer, tme)
    tmc = min(256, seq)
    slot_tk = jnp.concatenate([slots.T.reshape(-1), jnp.zeros((tmc * TOP_K,), I32)])
    return _combine(slot_tk, wts.T, x1, shared, gf, lng, lnb, ys, seq, tmc, layer, alpha)


def kernel(x, c, ctx, c_ctx, w_ada, b_ada, ln_mix_g, ln_mix_b, ln_ffn_g, ln_ffn_b, attn_w_qkv, attn_w_o, attn_sink, gmlp_w_in, gmlp_b_in, gmlp_v_g, gmlp_v_b, gmlp_w_s, gmlp_b_s, gmlp_w_o, moe_w_router, moe_bias, moe_w1, moe_w3, moe_w2, moe_ws1, moe_ws3, moe_ws2):
    batch, seq, d = x.shape
    nctx = ctx.shape[1]
    depth = w_ada.shape[0]
    assert depth == 2, "layer 0 is the attention mixer, layer 1 the gMLP mixer"
    alpha = float((2 * depth) ** 0.25)
    t = batch * seq

    pad = (-(batch + 1)) % 8
    cc = jnp.concatenate([c, c_ctx[None, :], jnp.zeros((pad, d), F32)], axis=0)
    mods = _modulations(cc, w_ada, b_ada)

    def mod(layer, j, rows=slice(0, batch)):
        return mods[layer, rows, j * d:(j + 1) * d]

    def per_batch(layer, j):
        return mod(layer, j).reshape(batch, 1, d)

    def router(layer):
        w = jnp.pad(moe_w_router[layer], ((0, 0), (0, LANES - N_EXPERTS)))
        hi = w.astype(BF16)
        lo = (w - hi.astype(F32)).astype(BF16)
        return jnp.concatenate([hi, lo], axis=1), moe_bias[layer].reshape(N_EXPERTS, 1)

    def shared_w(layer):
        return moe_ws1[layer].astype(BF16), moe_ws3[layer].astype(BF16), moe_ws2[layer].astype(BF16)

    x2 = x.reshape(t, d)
    ln_mix_g, ln_mix_b, ln_ffn_g, ln_ffn_b = (
        p.reshape(depth, 1, d) for p in (ln_mix_g, ln_mix_b, ln_ffn_g, ln_ffn_b))

    qd = Q_PER_KV * N_KV_HEADS * HEAD_DIM
    kvd = N_KV_HEADS * HEAD_DIM
    wqkv = attn_w_qkv[0]
    dup = lambda w: jnp.concatenate([w.reshape(d, N_KV_HEADS, 1, HEAD_DIM)] * (LANES // HEAD_DIM), axis=2
                                    ).reshape(d, N_KV_HEADS * LANES)
    wk2, wv2 = dup(wqkv[:, qd:qd + kvd]), dup(wqkv[:, qd + kvd:])
    w_all = jnp.concatenate([wqkv[:, :qd] * (HEAD_DIM ** -0.5), wk2, wv2], axis=1).astype(BF16)
    q, k2, v2 = _qkv_proj(x2, per_batch(0, 1), per_batch(0, 0), w_all, _rope_tables(seq), seq)
    ctx_row = slice(batch, batch + 1)
    kc2, vc2 = _ctx_kv(ctx.reshape(batch * nctx, d), mod(0, 1, ctx_row), mod(0, 0, ctx_row),
                       jnp.concatenate([wk2, wv2], axis=1).astype(BF16))
    o = _attention(q, k2, v2, kc2, vc2, attn_sink[0], batch, seq, nctx)
    x1, f, ids, wts, rank, cnt = _attn_out(
        o, attn_w_o[0].astype(BF16), x2, per_batch(0, 2), per_batch(0, 4), per_batch(0, 3),
        ln_mix_g, ln_mix_b, *router(0), seq, 0, alpha)
    x2 = _moe(f, x1, ids, wts, rank, cnt, per_batch(0, 5), ln_ffn_g, ln_ffn_b,
              moe_w1, moe_w3, moe_w2, *shared_w(0), seq, 0, alpha)

    width = gmlp_w_in.shape[2] // 2
    w_in = gmlp_w_in[0].astype(BF16)
    b_in = gmlp_b_in[0].reshape(1, 2 * width)
    vg, vb = gmlp_v_g[0].reshape(1, width), gmlp_v_b[0].reshape(1, width)
    sc, sh = per_batch(1, 1), per_batch(1, 0)
    u = _gmlp_in(x2, sc, sh, w_in[:, :width], b_in[:, :width], vg, vb, seq, False)
    v = _gmlp_in(x2, sc, sh, w_in[:, width:], b_in[:, width:], vg, vb, seq, True)
    x1, f, ids, wts, rank, cnt = _gmlp_out(
        u, v, gmlp_w_s[0].astype(BF16), gmlp_b_s[0][:, :, None], gmlp_w_o[0].astype(BF16), x2,
        per_batch(1, 2), per_batch(1, 4), per_batch(1, 3), ln_mix_g, ln_mix_b, *router(1), seq, 1, alpha)
    x2 = _moe(f, x1, ids, wts, rank, cnt, per_batch(1, 5), ln_ffn_g, ln_ffn_b,
              moe_w1, moe_w3, moe_w2, *shared_w(1), seq, 1, alpha)
    return x2.reshape(batch, seq, d)
```

```python
import functools

import jax
import jax.numpy as jnp
from jax import lax
from jax.experimental import pallas as pl
from jax.experimental.pallas import tpu as pltpu
from jax.experimental.pallas import tpu_sc as plsc

F32 = jnp.float32
BF16 = jnp.bfloat16
I32 = jnp.int32

GRID_W = 64
N_KV_HEADS = 4
Q_PER_KV = 8
HEAD_DIM = 64
ROPE_HALF = 16
WINDOW = 128
Q_BLOCK = 128
ROPE_BASE = 10000.0
NEG_INF = -1e30
CHUNK = 128
N_GMLP_GROUPS = 8
N_EXPERTS = 64
TOP_K = 8
N_EXPERT_GROUPS = 8
EXPERTS_PER_GROUP = 8
TOPK_GROUPS = 4
ROUTED_SCALE = 2.5
LN_EPS = 1e-5

LANES = 128
VMEM_LIMIT_BYTES = 56 * 1024 * 1024


def _params(*sem):
    return pltpu.CompilerParams(dimension_semantics=sem, vmem_limit_bytes=VMEM_LIMIT_BYTES)


def _silu(a):
    return a * jax.nn.sigmoid(a)


def _layernorm(v, g, b):
    mu = jnp.mean(v, axis=-1, keepdims=True)
    d = v - mu
    var = jnp.mean(d * d, axis=-1, keepdims=True)
    return d * lax.rsqrt(var + LN_EPS) * g + b


def _bdot(a, b):
    return jnp.dot(a, b, preferred_element_type=F32)


U32 = jnp.uint32


def _pack_pairs(a):
    n = a.shape[1] // 2
    lo = lax.bitcast_convert_type(a[:, :n].astype(BF16).astype(F32), U32)
    hi = lax.bitcast_convert_type(a[:, n:].astype(BF16).astype(F32), U32)
    return hi | (lo >> 16)


def _unpack_pairs(p):
    lo = lax.bitcast_convert_type(p << 16, F32)
    hi = lax.bitcast_convert_type(p & jnp.uint32(0xFFFF0000), F32)
    return lo, hi


def _store_row_tiles(ref, packed):
    r, n = packed.shape
    sub = n // LANES
    for j in range(sub):
        ref[pl.ds(j, r, stride=sub), :] = packed[:, LANES * j:LANES * (j + 1)]


def _load_row_tiles(ref, r):
    sub = ref.shape[0] // r
    pieces = [_unpack_pairs(ref[pl.ds(j, r, stride=sub), :]) for j in range(sub)]
    return [p[0] for p in pieces], [p[1] for p in pieces]


def _load_rows_bf16(ref, r):
    lo, hi = _load_row_tiles(ref, r)
    return jnp.concatenate(lo + hi, axis=1).astype(BF16)


def _mod_kernel(c_ref, w_ref, b_ref, o_ref):
    a = _silu(c_ref[...]).astype(BF16)
    o_ref[0] = _bdot(a, w_ref[0].astype(BF16)) + b_ref[0]


def _modulations(cc, w_ada, b_ada):
    depth, d, n6 = w_ada.shape
    r = cc.shape[0]
    tn = max(w for w in range(LANES, min(1024, n6) + 1, LANES) if n6 % w == 0)
    return pl.pallas_call(
        _mod_kernel,
        out_shape=jax.ShapeDtypeStruct((depth, r, n6), F32),
        grid=(depth, n6 // tn),
        in_specs=[pl.BlockSpec((r, d), lambda l, j: (0, 0)),
                  pl.BlockSpec((1, d, tn), lambda l, j: (l, 0, j)),
                  pl.BlockSpec((1, 1, tn), lambda l, j: (l, 0, j))],
        out_specs=pl.BlockSpec((1, r, tn), lambda l, j: (l, 0, j)),
        compiler_params=_params("parallel", "parallel"),
        name="adaln_mod",
    )(cc, w_ada, b_ada.reshape(depth, 1, n6))


def _rope_tables(seq):
    rows = seq // GRID_W
    row_ids = jnp.repeat(jnp.arange(rows, dtype=F32), GRID_W)
    col_ids = jnp.tile(jnp.arange(GRID_W, dtype=F32), rows)
    inv_freq = ROPE_BASE ** (-jnp.arange(0, 2 * ROPE_HALF, 2, dtype=F32) / (2 * ROPE_HALF))
    ar, ac = row_ids[:, None] * inv_freq, col_ids[:, None] * inv_freq
    z = jnp.zeros_like(ar)
    cos = jnp.concatenate([jnp.cos(ar), jnp.cos(ar), jnp.cos(ac), jnp.cos(ac)], axis=-1)
    sin_lo = jnp.concatenate([-jnp.sin(ar), z, -jnp.sin(ac), z], axis=-1)
    sin_hi = jnp.concatenate([z, jnp.sin(ar), z, jnp.sin(ac)], axis=-1)
    rep = LANES // HEAD_DIM
    return tuple(jnp.tile(t, (1, rep)) for t in (cos, sin_lo, sin_hi))


def _qkv_kernel(x_ref, sc_ref, sh_ref, w_ref, cos_ref, slo_ref, shi_ref, q_ref, k_ref, v_ref, *, cw):
    h = (x_ref[...] * (1.0 + sc_ref[0]) + sh_ref[0]).astype(BF16)
    cos, slo, shi = cos_ref[...], slo_ref[...], shi_ref[...]

    def rope(a):
        return (a * cos + pltpu.roll(a, LANES - ROPE_HALF, 1) * slo
                + pltpu.roll(a, ROPE_HALF, 1) * shi)

    qd, kd = q_ref.shape[1], k_ref.shape[1]
    for c0 in range(0, qd, cw):
        acc = _bdot(h, w_ref[:, c0:c0 + cw])
        for b0 in range(0, cw, LANES):
            q_ref[:, c0 + b0:c0 + b0 + LANES] = rope(acc[:, b0:b0 + LANES]).astype(BF16)
    acc = _bdot(h, w_ref[:, qd:qd + kd])
    for b0 in range(0, kd, LANES):
        k_ref[:, b0:b0 + LANES] = rope(acc[:, b0:b0 + LANES]).astype(BF16)
    v_ref[...] = _bdot(h, w_ref[:, qd + kd:qd + 2 * kd]).astype(BF16)


def _qkv_proj(x2, sc, sh, w, tables, seq):
    t, d = x2.shape
    kd = N_KV_HEADS * LANES
    qd = w.shape[1] - 2 * kd
    tm = min(512, seq)
    per_b = seq // tm
    row = lambda i: (i, 0)
    mod = lambda i: (i // per_b, 0, 0)
    tab = lambda i: (i % per_b, 0)
    return pl.pallas_call(
        functools.partial(_qkv_kernel, cw=min(512, qd)),
        out_shape=(jax.ShapeDtypeStruct((t, qd), BF16), jax.ShapeDtypeStruct((t, kd), BF16),
                   jax.ShapeDtypeStruct((t, kd), BF16)),
        grid=(t // tm,),
        in_specs=[pl.BlockSpec((tm, d), row), pl.BlockSpec((1, 1, d), mod), pl.BlockSpec((1, 1, d), mod),
                  pl.BlockSpec(w.shape, lambda i: (0, 0)),
                  pl.BlockSpec((tm, LANES), tab), pl.BlockSpec((tm, LANES), tab), pl.BlockSpec((tm, LANES), tab)],
        out_specs=(pl.BlockSpec((tm, qd), row), pl.BlockSpec((tm, kd), row), pl.BlockSpec((tm, kd), row)),
        compiler_params=_params("parallel"),
        name="qkv_rope",
    )(x2, sc, sh, w, *tables)


def _ctxkv_kernel(x_ref, sc_ref, sh_ref, w_ref, k_ref, v_ref):
    h = (x_ref[...] * (1.0 + sc_ref[...]) + sh_ref[...]).astype(BF16)
    kd = k_ref.shape[1]
    k_ref[...] = _bdot(h, w_ref[:, :kd]).astype(BF16)
    v_ref[...] = _bdot(h, w_ref[:, kd:]).astype(BF16)


def _ctx_kv(c2, sc, sh, w):
    t, d = c2.shape
    kd = w.shape[1] // 2
    tm = min(512, t)
    row = lambda i: (i, 0)
    fix = lambda i: (0, 0)
    return pl.pallas_call(
        _ctxkv_kernel,
        out_shape=(jax.ShapeDtypeStruct((t, kd), BF16), jax.ShapeDtypeStruct((t, kd), BF16)),
        grid=(t // tm,),
        in_specs=[pl.BlockSpec((tm, d), row), pl.BlockSpec((1, d), fix), pl.BlockSpec((1, d), fix),
                  pl.BlockSpec(w.shape, fix)],
        out_specs=(pl.BlockSpec((tm, kd), row), pl.BlockSpec((tm, kd), row)),
        compiler_params=_params("parallel"),
        name="ctx_kv",
    )(c2, sc, sh, w)


def _attn_kernel(sink_ref, q_ref, k_ref, v_ref, kc_ref, vc_ref, o_ref):
    for b in range(q_ref.shape[1] // Q_BLOCK):
        _attn_block(sink_ref, q_ref, k_ref, v_ref, kc_ref, vc_ref, o_ref, b)


def _attn_block(sink_ref, q_ref, k_ref, v_ref, kc_ref, vc_ref, o_ref, b):
    kv = pl.program_id(1)
    i = pl.program_id(2) * (q_ref.shape[1] // Q_BLOCK) + b
    seq, nctx = k_ref.shape[1], kc_ref.shape[1]
    nwin = Q_BLOCK + 2 * WINDOW
    start = pl.multiple_of(jnp.clip(i * Q_BLOCK - WINDOW, 0, seq - nwin), Q_BLOCK)
    nkey = nctx + nwin
    kall = jnp.concatenate([kc_ref[0], k_ref[0, pl.ds(start, nwin), :]], axis=0)
    vall = jnp.concatenate([vc_ref[0], v_ref[0, pl.ds(start, nwin), :]], axis=0)
    low = lax.broadcasted_iota(I32, (nkey, LANES), 1) < HEAD_DIM
    zero = jnp.zeros((nkey, LANES), BF16)
    npair = Q_PER_KV // 2
    q = q_ref[0, Q_BLOCK * b:Q_BLOCK * (b + 1), :]
    qs = jnp.concatenate([q[:, LANES * j:LANES * (j + 1)] for j in range(npair)], axis=0)
    nrow = npair * Q_BLOCK
    rows = lax.broadcasted_iota(I32, (Q_BLOCK, nkey), 0)
    cols = lax.broadcasted_iota(I32, (Q_BLOCK, nkey), 1)
    dist = (i * Q_BLOCK - start + nctx) + rows - cols
    valid = (cols < nctx) | (jnp.abs(dist) <= WINDOW)
    contract_last = (((1,), (1,)), ((), ()))
    out = jnp.zeros((nrow, LANES), F32)
    for par in range(2):
        keep = low if par == 0 else jnp.logical_not(low)
        kh = jnp.where(keep, kall, zero)
        vh = jnp.where(keep, vall, zero)
        s = lax.dot_general(qs, kh, contract_last, preferred_element_type=F32)
        s = jnp.concatenate(
            [jnp.where(valid, s[Q_BLOCK * j:Q_BLOCK * (j + 1), :], NEG_INF) for j in range(npair)], axis=0)
        sink = jnp.concatenate(
            [jnp.full((Q_BLOCK, 1), sink_ref[kv * Q_PER_KV + 2 * j + par], F32) for j in range(npair)], axis=0)
        m = jnp.maximum(jnp.max(s, axis=1, keepdims=True), sink)
        p = jnp.exp(s - m)
        den = jnp.sum(p, axis=1, keepdims=True) + jnp.exp(sink - m)
        out = out + _bdot(p.astype(BF16), vh) / den
    for j in range(npair):
        o_ref[0, Q_BLOCK * b:Q_BLOCK * (b + 1), LANES * j:LANES * (j + 1)] = (
            out[Q_BLOCK * j:Q_BLOCK * (j + 1), :].astype(BF16))


def _attention(q, k2, v2, kc2, vc2, sink, batch, seq, nctx):
    qd = q.shape[1]
    gw = Q_PER_KV * HEAD_DIM
    q3 = q.reshape(batch, seq, qd)
    k3, v3 = k2.reshape(batch, seq, -1), v2.reshape(batch, seq, -1)
    kc3, vc3 = kc2.reshape(batch, nctx, -1), vc2.reshape(batch, nctx, -1)
    full = lambda b, h, i: (b, 0, h)
    blk = lambda b, h, i: (b, i, h)
    rows = next(r * Q_BLOCK for r in (8, 4, 2, 1) if seq % (r * Q_BLOCK) == 0)
    out = pl.pallas_call(
        _attn_kernel,
        out_shape=jax.ShapeDtypeStruct((batch, seq, qd), BF16),
        grid=(batch, N_KV_HEADS, seq // rows),
        in_specs=[pl.BlockSpec(memory_space=pltpu.SMEM),
                  pl.BlockSpec((1, rows, gw), blk),
                  pl.BlockSpec((1, seq, LANES), full), pl.BlockSpec((1, seq, LANES), full),
                  pl.BlockSpec((1, nctx, LANES), full), pl.BlockSpec((1, nctx, LANES), full)],
        out_specs=pl.BlockSpec((1, rows, gw), blk),
        compiler_params=_params("parallel", "parallel", "parallel"),
        name="window_gqa",
    )(sink, q3, k3, v3, kc3, vc3)
    return out.reshape(batch * seq, qd)


def _split_bf16(a):
    hi = a.astype(BF16)
    return hi, (a - hi.astype(F32)).astype(BF16)


def _post_mixer(x, y, gm, lng, lnb, scf, shf, wr_ref, rb_ref, run_ref,
                x1_ref, f_ref, ids_ref, wts_ref, rank_ref, cnt_ref, alpha):
    tm = x.shape[0]
    x1 = _layernorm(alpha * x + gm * y, lng, lnb)
    x1_ref[...] = x1
    f = x1 * (1.0 + scf) + shf
    _store_row_tiles(f_ref, _pack_pairs(f))

    fh, fl = _split_bf16(f)
    prod = _bdot(fh, wr_ref[...]) + _bdot(fl, wr_ref[...])
    logits = (prod[:, :LANES] + prod[:, LANES:]).T[:N_EXPERTS, :]
    scores = jax.nn.sigmoid(logits)
    biased = scores + rb_ref[...]

    eg = EXPERTS_PER_GROUP
    sub = lax.broadcasted_iota(I32, (eg, tm), 0)
    gscore = []
    for g in range(N_EXPERT_GROUPS):
        tg = biased[eg * g:eg * (g + 1), :]
        m1 = jnp.max(tg, axis=0, keepdims=True)
        i1 = jnp.min(jnp.where(tg == m1, sub, eg), axis=0, keepdims=True)
        m2 = jnp.max(jnp.where(sub == i1, -jnp.inf, tg), axis=0, keepdims=True)
        gscore.append(m1 + m2)
    gsel = [jnp.zeros((1, tm), jnp.bool_) for _ in range(N_EXPERT_GROUPS)]
    for _ in range(TOPK_GROUPS):
        best = functools.reduce(jnp.maximum, gscore)
        taken = jnp.zeros((1, tm), jnp.bool_)
        for g in range(N_EXPERT_GROUPS):
            hit = jnp.logical_and(gscore[g] == best, jnp.logical_not(taken))
            taken = jnp.logical_or(taken, hit)
            gsel[g] = jnp.logical_or(gsel[g], hit)
            gscore[g] = jnp.where(hit, -jnp.inf, gscore[g])
    cur = jnp.concatenate(
        [jnp.where(gsel[g], biased[eg * g:eg * (g + 1), :], -jnp.inf) for g in range(N_EXPERT_GROUPS)], axis=0)

    eidx = lax.broadcasted_iota(I32, (N_EXPERTS, tm), 0)
    picks, wts, hots = [], [], []
    for _ in range(TOP_K):
        m = jnp.max(cur, axis=0, keepdims=True)
        idx = jnp.min(jnp.where(cur == m, eidx, N_EXPERTS), axis=0, keepdims=True)
        hot = eidx == idx
        picks.append(idx)
        wts.append(jnp.sum(jnp.where(hot, scores, 0.0), axis=0, keepdims=True))
        hots.append(hot)
        cur = jnp.where(hot, -jnp.inf, cur)
    wsum = functools.reduce(jnp.add, wts)

    assigned = functools.reduce(jnp.add, [h.astype(F32) for h in hots])
    before = (lax.broadcasted_iota(I32, (tm, tm), 0) < lax.broadcasted_iota(I32, (tm, tm), 1)).astype(BF16)
    pos = _bdot(assigned.astype(BF16), before) + run_ref[:, 0:1]
    for k in range(TOP_K):
        ids_ref[k:k + 1, :] = picks[k]
        wts_ref[k:k + 1, :] = wts[k] / wsum * ROUTED_SCALE
        rank_ref[k:k + 1, :] = jnp.sum(jnp.where(hots[k], pos, 0.0), axis=0, keepdims=True).astype(I32)
    run_ref[...] = run_ref[...] + jnp.sum(assigned, axis=1, keepdims=True)
    cnt_ref[...] = run_ref[...]


def _route_out_shapes(t, d):
    return (jax.ShapeDtypeStruct((t, d), F32), jax.ShapeDtypeStruct((t * d // (2 * LANES), LANES), U32),
            jax.ShapeDtypeStruct((TOP_K, t), I32), jax.ShapeDtypeStruct((TOP_K, t), F32),
            jax.ShapeDtypeStruct((TOP_K, t), I32), jax.ShapeDtypeStruct((N_EXPERTS, LANES), F32))


def _route_out_specs(tm, d):
    row = lambda i: (i, 0)
    col = lambda i: (0, i)
    return (pl.BlockSpec((tm, d), row), pl.BlockSpec((tm * d // (2 * LANES), LANES), row),
            pl.BlockSpec((TOP_K, tm), col), pl.BlockSpec((TOP_K, tm), col), pl.BlockSpec((TOP_K, tm), col),
            pl.BlockSpec((N_EXPERTS, LANES), lambda i: (0, 0)))


def _route_in_specs(tm, d, per_b, layer):
    mod = lambda i: (i // per_b, 0, 0)
    lrow = lambda i: (layer, 0, 0)
    fix = lambda i: (0, 0)
    return [pl.BlockSpec((tm, d), lambda i: (i, 0)),
            pl.BlockSpec((1, 1, d), mod), pl.BlockSpec((1, 1, d), mod), pl.BlockSpec((1, 1, d), mod),
            pl.BlockSpec((1, 1, d), lrow), pl.BlockSpec((1, 1, d), lrow),
            pl.BlockSpec((d, 2 * LANES), fix),
            pl.BlockSpec((N_EXPERTS, 1), fix)]


def _oproj_kernel(o_ref, wo_ref, x_ref, gm_ref, scf_ref, shf_ref, lng_ref, lnb_ref, wr_ref, rb_ref,
                  x1_ref, f_ref, ids_ref, wts_ref, rank_ref, cnt_ref, run_ref, *, alpha):
    @pl.when(pl.program_id(0) == 0)
    def _():
        run_ref[...] = jnp.zeros_like(run_ref)

    y = _bdot(o_ref[...], wo_ref[...])
    _post_mixer(x_ref[...], y, gm_ref[0], lng_ref[0], lnb_ref[0], scf_ref[0], shf_ref[0],
                wr_ref, rb_ref, run_ref, x1_ref, f_ref, ids_ref, wts_ref, rank_ref, cnt_ref, alpha)


def _attn_out(o, wo, x2, gm, scf, shf, lng, lnb, wr, rb, seq, layer, alpha):
    t, d = x2.shape
    tm = min(512, seq)
    return pl.pallas_call(
        functools.partial(_oproj_kernel, alpha=alpha),
        out_shape=_route_out_shapes(t, d),
        grid=(t // tm,),
        in_specs=[pl.BlockSpec((tm, o.shape[1]), lambda i: (i, 0)), pl.BlockSpec(wo.shape, lambda i: (0, 0))]
        + _route_in_specs(tm, d, seq // tm, layer),
        out_specs=_route_out_specs(tm, d),
        scratch_shapes=[pltpu.VMEM((N_EXPERTS, LANES), F32)],
        compiler_params=_params("arbitrary"),
        name="attn_out_route",
    )(o, wo, x2, gm, scf, shf, lng, lnb, wr, rb)


def _gmlp_in_kernel(x_ref, sc_ref, sh_ref, w_ref, b_ref, g_ref, beta_ref, o_ref, *, cw, normalize):
    h = (x_ref[...] * (1.0 + sc_ref[0]) + sh_ref[0]).astype(BF16)
    n = o_ref.shape[1]
    inv_sqrt2 = 0.7071067811865476
    parts = []
    for c0 in range(0, n, cw):
        z = _bdot(h, w_ref[:, c0:c0 + cw]) + b_ref[:, c0:c0 + cw]
        z = 0.5 * z * (1.0 + lax.erf(z * inv_sqrt2))
        if normalize:
            parts.append(z)
        else:
            o_ref[:, c0:c0 + cw] = z.astype(BF16)
    if normalize:
        v = jnp.concatenate(parts, axis=1)
        o_ref[...] = _layernorm(v, g_ref[...], beta_ref[...]).astype(BF16)


def _gmlp_in(x2, sc, sh, w, b, g, beta, seq, normalize):
    t, d = x2.shape
    n = w.shape[1]
    tm = min(512, seq)
    per_b = seq // tm
    row = lambda i: (i, 0)
    mod = lambda i: (i // per_b, 0, 0)
    fix = lambda i: (0, 0)
    return pl.pallas_call(
        functools.partial(_gmlp_in_kernel, cw=min(512, n), normalize=normalize),
        out_shape=jax.ShapeDtypeStruct((t, n), BF16),
        grid=(t // tm,),
        in_specs=[pl.BlockSpec((tm, d), row), pl.BlockSpec((1, 1, d), mod), pl.BlockSpec((1, 1, d), mod),
                  pl.BlockSpec(w.shape, fix, pipeline_mode=pl.Buffered(1)),
                  pl.BlockSpec((1, n), fix), pl.BlockSpec((1, n), fix),
                  pl.BlockSpec((1, n), fix)],
        out_specs=pl.BlockSpec((tm, n), row),
        compiler_params=_params("parallel"),
        name="gmlp_in_v" if normalize else "gmlp_in_u",
    )(x2, sc, sh, w, b, g, beta)


def _gmlp_out_kernel(u_ref, v_ref, ws_ref, bs_ref, wo_ref, x_ref, gm_ref, scf_ref, shf_ref, lng_ref, lnb_ref,
                     wr_ref, rb_ref, x1_ref, f_ref, ids_ref, wts_ref, rank_ref, cnt_ref,
                     run_ref, gated_ref, *, alpha):
    @pl.when(pl.program_id(0) == 0)
    def _():
        run_ref[...] = jnp.zeros_like(run_ref)

    tm, width = u_ref.shape
    gd = width // N_GMLP_GROUPS
    for r0 in range(0, tm, CHUNK):
        for g in range(N_GMLP_GROUPS):
            c0 = g * gd
            mixed = _bdot(ws_ref[g], v_ref[r0:r0 + CHUNK, c0:c0 + gd]) + bs_ref[g]
            gated_ref[r0:r0 + CHUNK, c0:c0 + gd] = (
                u_ref[r0:r0 + CHUNK, c0:c0 + gd].astype(F32) * mixed).astype(BF16)
    y = _bdot(gated_ref[...], wo_ref[...])
    _post_mixer(x_ref[...], y, gm_ref[0], lng_ref[0], lnb_ref[0], scf_ref[0], shf_ref[0],
                wr_ref, rb_ref, run_ref, x1_ref, f_ref, ids_ref, wts_ref, rank_ref, cnt_ref, alpha)


def _gmlp_out(u, v, ws, bs, wo, x2, gm, scf, shf, lng, lnb, wr, rb, seq, layer, alpha):
    t, d = x2.shape
    width = u.shape[1]
    tm = min(256, seq)
    row = lambda i: (i, 0)
    return pl.pallas_call(
        functools.partial(_gmlp_out_kernel, alpha=alpha),
        out_shape=_route_out_shapes(t, d),
        grid=(t // tm,),
        in_specs=[pl.BlockSpec((tm, width), row), pl.BlockSpec((tm, width), row),
                  pl.BlockSpec(ws.shape, lambda i: (0, 0, 0)), pl.BlockSpec(bs.shape, lambda i: (0, 0, 0)),
                  pl.BlockSpec(wo.shape, lambda i: (0, 0), pipeline_mode=pl.Buffered(1))]
        + _route_in_specs(tm, d, seq // tm, layer),
        out_specs=_route_out_specs(tm, d),
        scratch_shapes=[pltpu.VMEM((N_EXPERTS, LANES), F32), pltpu.VMEM((tm, width), BF16)],
        compiler_params=_params("arbitrary"),
        name="gmlp_out_route",
    )(u, v, ws, bs, wo, x2, gm, scf, shf, lng, lnb, wr, rb)


SC_CORES, SC_SUBCORES, SC_LANES = 2, 16, 16
MXU_N = 256


def _shared_kernel(f_ref, ws1_ref, ws3_ref, ws2_ref, sh_ref):
    h = _load_rows_bf16(f_ref, sh_ref.shape[0])
    a = _silu(_bdot(h, ws1_ref[...])) * _bdot(h, ws3_ref[...])
    sh_ref[...] = _bdot(a.astype(BF16), ws2_ref[...]).astype(BF16)


def _shared_expert(f, ws1, ws3, ws2, t):
    sub = f.shape[0] // t
    d = ws1.shape[0]
    tm = min(512, t)
    fix = lambda i: (0, 0)
    return pl.pallas_call(
        _shared_kernel,
        out_shape=jax.ShapeDtypeStruct((t, d), BF16),
        grid=(t // tm,),
        in_specs=[pl.BlockSpec((tm * sub, LANES), lambda i: (i, 0)),
                  pl.BlockSpec(ws1.shape, fix), pl.BlockSpec(ws3.shape, fix), pl.BlockSpec(ws2.shape, fix)],
        out_specs=pl.BlockSpec((tm, d), lambda i: (i, 0)),
        compiler_params=_params("parallel"),
        name="moe_shared",
    )(f, ws1, ws3, ws2)


def _inverse_map(slots_flat, nslot):
    n = slots_flat.shape[0]
    workers = SC_CORES * SC_SUBCORES
    per_w = nslot // workers
    chunk = min(8192, n)
    assert nslot % (workers * SC_LANES) == 0 and n % chunk == 0 and chunk % SC_LANES == 0
    mesh = plsc.VectorSubcoreMesh(core_axis_name="c", subcore_axis_name="s")

    @functools.partial(
        pl.kernel, out_type=jax.ShapeDtypeStruct((nslot,), I32), mesh=mesh,
        scratch_types=[pltpu.VMEM((per_w,), I32), pltpu.VMEM((chunk,), I32)],
        compiler_params=pltpu.CompilerParams(needs_layout_passes=False))
    def inverse(slots_hbm, inv_hbm, local, buf):
        lo = (lax.axis_index("s") * SC_CORES + lax.axis_index("c")) * per_w
        unused = jnp.full((SC_LANES,), -1, I32)

        @pl.loop(0, per_w, step=SC_LANES)
        def _(j):
            local[pl.ds(j, SC_LANES)] = unused

        lane = lax.iota(I32, SC_LANES)

        @pl.loop(0, n, step=chunk)
        def _(c0):
            pltpu.sync_copy(slots_hbm.at[pl.ds(c0, chunk)], buf)

            @pl.loop(0, chunk, step=SC_LANES)
            def _(j):
                idx = buf[pl.ds(j, SC_LANES)] - lo
                mine = jnp.logical_and(idx >= 0, idx < per_w)
                plsc.store_scatter(local, [idx], c0 + j + lane, mask=mine)

        pltpu.sync_copy(local, inv_hbm.at[pl.ds(lo, per_w)])

    return inverse(slots_flat)


EXPERT_RING = 3


def _experts_kernel(texp_ref, nstep_ref, tok_ref, w1_ref, w3_ref, w2_ref, f_hbm, y_ref,
                    w1b, w3b, w2b, xa, xb, xc_, gsem, *, tme):
    i = pl.program_id(0)
    last = nstep_ref[0] - 1
    ring = (xa, xb, xc_)
    sub = xa.shape[0] // tme
    d, fdim = w1b.shape
    half = d // 2

    @pl.when(i == 0)
    def _():
        xb[...] = jnp.zeros_like(xb)
        xc_[...] = jnp.zeros_like(xc_)

    @pl.when(jnp.logical_and(i <= last,
                             jnp.logical_or(i == 0, texp_ref[i] != texp_ref[jnp.maximum(i - 1, 0)])))
    def _():
        w1b[...] = w1_ref[...].astype(BF16)
        w3b[...] = w3_ref[...].astype(BF16)
        w2b[...] = w2_ref[...].astype(BF16)

    def whole_tile(buf, sem):
        return pltpu.make_async_copy(f_hbm.at[pl.ds(0, tme * sub), :], buf, sem)

    def step(par):
        nxt, prv = (par + 1) % EXPERT_RING, (par + 2) % EXPERT_RING
        xg, xc = ring[par], ring[nxt]

        @pl.when(i >= EXPERT_RING - 1)
        def _():
            whole_tile(xc, gsem.at[nxt]).wait()

        pw1, pw2 = min(MXU_N, fdim), min(MXU_N, half)
        pieces = 2 * (fdim // pw1) + 2 * (half // pw2)
        per_piece = -(-tme // pieces)
        issued = [0]

        def issue_rows():
            for r in range(issued[0], min(issued[0] + per_piece, tme)):
                tok = pl.multiple_of(tok_ref[r] * sub, sub)
                pltpu.make_async_copy(f_hbm.at[pl.ds(tok, sub), :], xg.at[pl.ds(r * sub, sub), :],
                                      gsem.at[par]).start(priority=r % 2)
            issued[0] = min(issued[0] + per_piece, tme)

        h = _load_rows_bf16(xc, tme)
        gate, up = [], []
        for c0 in range(0, fdim, pw1):
            gate.append(_bdot(h, w1b[:, c0:c0 + pw1]))
            issue_rows()
        for c0 in range(0, fdim, pw1):
            up.append(_bdot(h, w3b[:, c0:c0 + pw1]))
            issue_rows()
        a = (_silu(jnp.concatenate(gate, axis=1)) * jnp.concatenate(up, axis=1)).astype(BF16)
        for c0 in range(0, half, pw2):
            lo = _bdot(a, w2b[:, c0:c0 + pw2])
            issue_rows()
            hi = _bdot(a, w2b[:, half + c0:half + c0 + pw2])
            issue_rows()
            packed = _pack_pairs(jnp.concatenate([lo, hi], axis=1))
            for b0 in range(0, pw2, LANES):
                y_ref[pl.ds((c0 + b0) // LANES, tme, stride=sub), :] = packed[:, b0:b0 + LANES]
        assert issued[0] == tme

        @pl.when(i == last)
        def _():
            whole_tile(ring[prv], gsem.at[prv]).wait()
            whole_tile(xg, gsem.at[par]).wait()

    for par in range(EXPERT_RING):
        pl.when(jnp.logical_and(i <= last, lax.rem(i, EXPERT_RING) == par))(functools.partial(step, par))


def _experts(texp, nstep, tok, w1, w3, w2, f, layer, tme):
    d, fdim = w1.shape[-2:]
    sub = d // (2 * LANES)
    lag = EXPERT_RING - 1
    ntile = texp.shape[0] - lag
    wsel = lambda i, te, ns: (layer, te[i], 0, 0)
    return pl.pallas_call(
        functools.partial(_experts_kernel, tme=tme),
        out_shape=jax.ShapeDtypeStruct((ntile * tme * sub, LANES), U32),
        grid_spec=pltpu.PrefetchScalarGridSpec(
            num_scalar_prefetch=2,
            grid=(ntile + lag,),
            in_specs=[pl.BlockSpec((tme,), lambda i, te, ns: (i,), memory_space=pltpu.SMEM),
                      pl.BlockSpec((None, None, d, fdim), wsel), pl.BlockSpec((None, None, d, fdim), wsel),
                      pl.BlockSpec((None, None, fdim, d), wsel),
                      pl.BlockSpec(memory_space=pl.ANY)],
            out_specs=pl.BlockSpec((tme * sub, LANES),
                                   lambda i, te, ns: (jnp.clip(i - lag, 0, ns[0] - lag - 1), 0)),
            scratch_shapes=[pltpu.VMEM((d, fdim), BF16), pltpu.VMEM((d, fdim), BF16), pltpu.VMEM((fdim, d), BF16)]
            + [pltpu.VMEM((tme * sub, LANES), U32)] * EXPERT_RING
            + [pltpu.SemaphoreType.DMA((EXPERT_RING,))]),
        compiler_params=_params("arbitrary"),
        name="moe_experts",
    )(texp, nstep, tok, w1, w3, w2, f)


def _combine_kernel(slot_ref, wt_ref, x1_ref, sh_ref, gf_ref, lng_ref, lnb_ref, ys_hbm, o_ref, ga, gb, gsem, *, alpha):
    i = pl.program_id(0)
    last = pl.num_programs(0) - 1
    tm = x1_ref.shape[0]
    sub = ga.shape[1] // tm

    @pl.when(i == 0)
    def _():
        gb[...] = jnp.zeros_like(gb)

    def whole_plane(buf, k, sem):
        return pltpu.make_async_copy(ys_hbm.at[pl.ds(0, tm * sub), :], buf.at[k], sem)

    def step(par):
        gg, gc = (ga, gb) if par == 0 else (gb, ga)

        @pl.when(i > 0)
        def _():
            for k in range(TOP_K):
                whole_plane(gc, k, gsem.at[1 - par]).wait()

        def issue(t0, t1):
            for tt in range(t0, t1):
                for k in range(TOP_K):
                    row = pl.multiple_of(slot_ref[tt * TOP_K + k] * sub, sub)
                    pltpu.make_async_copy(ys_hbm.at[pl.ds(row, sub), :], gg.at[k, pl.ds(tt * sub, sub), :],
                                          gsem.at[par]).start(priority=k % 2)

        lo = hi = None
        for k in range(TOP_K):
            issue(k * tm // TOP_K, (k + 1) * tm // TOP_K)
            w = wt_ref[:, k:k + 1]
            rl, rh = _load_row_tiles(gc.at[k], tm)
            lo = [w * b for b in rl] if lo is None else [a + w * b for a, b in zip(lo, rl)]
            hi = [w * b for b in rh] if hi is None else [a + w * b for a, b in zip(hi, rh)]
        f = jnp.concatenate(lo + hi, axis=1) + sh_ref[...].astype(F32)
        o_ref[...] = _layernorm(alpha * x1_ref[...] + gf_ref[0] * f, lng_ref[0], lnb_ref[0])

        @pl.when(i == last)
        def _():
            for k in range(TOP_K):
                whole_plane(gg, k, gsem.at[par]).wait()

    for par in range(2):
        pl.when(jnp.bitwise_and(i, 1) == par)(functools.partial(step, par))


def _combine(slot_tk, wt, x1, sh, gf, lng, lnb, ys, seq, tm, layer, alpha):
    t, d = x1.shape
    sub = d // (2 * LANES)
    per_b = seq // tm
    nstep = t // tm + 1
    prev = lambda i: jnp.maximum(i - 1, 0)
    row = lambda i: (prev(i), 0)
    lrow = lambda i: (layer, 0, 0)
    return pl.pallas_call(
        functools.partial(_combine_kernel, alpha=alpha),
        out_shape=jax.ShapeDtypeStruct((t, d), F32),
        grid=(nstep,),
        in_specs=[pl.BlockSpec((tm * TOP_K,), lambda i: (i,), memory_space=pltpu.SMEM),
                  pl.BlockSpec((tm, TOP_K), row), pl.BlockSpec((tm, d), row), pl.BlockSpec((tm, d), row),
                  pl.BlockSpec((1, 1, d), lambda i: (prev(i) // per_b, 0, 0)),
                  pl.BlockSpec((1, 1, d), lrow), pl.BlockSpec((1, 1, d), lrow),
                  pl.BlockSpec(memory_space=pl.ANY)],
        out_specs=pl.BlockSpec((tm, d), row),
        scratch_shapes=[pltpu.VMEM((TOP_K, tm * sub, LANES), U32), pltpu.VMEM((TOP_K, tm * sub, LANES), U32),
                        pltpu.SemaphoreType.DMA((2,))],
        compiler_params=_params("arbitrary"),
        name="moe_combine_ln",
    )(slot_tk, wt, x1, sh, gf, lng, lnb, ys)


def _moe(f, x1, ids, wts, rank, cnt, gf, lng, lnb, w1, w3, w2, ws1, ws3, ws2, seq, layer, alpha):
    t, d = x1.shape
    tme = min(512, t)
    counts = cnt[:, 0].astype(I32)
    tiles_e = (counts + tme - 1) // tme
    padded = tiles_e * tme
    base = jnp.cumsum(padded) - padded
    expert = jnp.arange(N_EXPERTS, dtype=I32)
    slots = jnp.sum(jnp.where(ids[:, :, None] == expert, base, 0), axis=-1) + rank
    ntile = (t * TOP_K) // tme + N_EXPERTS
    nslot = ntile * tme
    tile_ends = jnp.cumsum(tiles_e)
    texp = jnp.sum((jnp.arange(ntile, dtype=I32)[:, None] >= tile_ends[None, :]).astype(I32), axis=1)
    texp = jnp.minimum(texp, N_EXPERTS - 1)
    lag = EXPERT_RING - 1
    texp_step = jnp.concatenate([texp[:1]] * lag + [texp])
    nstep = tile_ends[-1:] + lag
    idle = jnp.arange(ntile + lag, dtype=I32) >= nstep
    texp_step = jnp.where(idle, jnp.take(texp_step, nstep - 1), texp_step)

    inv = _inverse_map(slots.reshape(-1), nslot)
    filler = jnp.arange(nslot + lag * tme, dtype=I32) % t
    tok = jnp.concatenate([jnp.where(inv >= 0, inv % t, filler[:nslot]), filler[nslot:]])

    shared = _shared_expert(f, ws1, ws3, ws2, t)
    ys = _experts(texp_step, nstep, tok, w1, w3, w2, f, layer, tme)
    tmc = min(256, seq)
    slot_tk = jnp.concatenate([slots.T.reshape(-1), jnp.zeros((tmc * TOP_K,), I32)])
    return _combine(slot_tk, wts.T, x1, shared, gf, lng, lnb, ys, seq, tmc, layer, alpha)


def kernel(x, c, ctx, c_ctx, w_ada, b_ada, ln_mix_g, ln_mix_b, ln_ffn_g, ln_ffn_b, attn_w_qkv, attn_w_o, attn_sink, gmlp_w_in, gmlp_b_in, gmlp_v_g, gmlp_v_b, gmlp_w_s, gmlp_b_s, gmlp_w_o, moe_w_router, moe_bias, moe_w1, moe_w3, moe_w2, moe_ws1, moe_ws3, moe_ws2):
    batch, seq, d = x.shape
    nctx = ctx.shape[1]
    depth = w_ada.shape[0]
    assert depth == 2, "layer 0 is the attention mixer, layer 1 the gMLP mixer"
    alpha = float((2 * depth) ** 0.25)
    t = batch * seq

    pad = (-(batch + 1)) % 8
    cc = jnp.concatenate([c, c_ctx[None, :], jnp.zeros((pad, d), F32)], axis=0)
    mods = _modulations(cc, w_ada, b_ada)

    def mod(layer, j, rows=slice(0, batch)):
        return mods[layer, rows, j * d:(j + 1) * d]

    def per_batch(layer, j):
        return mod(layer, j).reshape(batch, 1, d)

    def router(layer):
        w = jnp.pad(moe_w_router[layer], ((0, 0), (0, LANES - N_EXPERTS)))
        hi = w.astype(BF16)
        lo = (w - hi.astype(F32)).astype(BF16)
        return jnp.concatenate([hi, lo], axis=1), moe_bias[layer].reshape(N_EXPERTS, 1)

    def shared_w(layer):
        return moe_ws1[layer].astype(BF16), moe_ws3[layer].astype(BF16), moe_ws2[layer].astype(BF16)

    x2 = x.reshape(t, d)
    ln_mix_g, ln_mix_b, ln_ffn_g, ln_ffn_b = (
        p.reshape(depth, 1, d) for p in (ln_mix_g, ln_mix_b, ln_ffn_g, ln_ffn_b))

    qd = Q_PER_KV * N_KV_HEADS * HEAD_DIM
    kvd = N_KV_HEADS * HEAD_DIM
    wqkv = attn_w_qkv[0]
    dup = lambda w: jnp.concatenate([w.reshape(d, N_KV_HEADS, 1, HEAD_DIM)] * (LANES // HEAD_DIM), axis=2
                                    ).reshape(d, N_KV_HEADS * LANES)
    wk2, wv2 = dup(wqkv[:, qd:qd + kvd]), dup(wqkv[:, qd + kvd:])
    w_all = jnp.concatenate([wqkv[:, :qd] * (HEAD_DIM ** -0.5), wk2, wv2], axis=1).astype(BF16)
    q, k2, v2 = _qkv_proj(x2, per_batch(0, 1), per_batch(0, 0), w_all, _rope_tables(seq), seq)
    ctx_row = slice(batch, batch + 1)
    kc2, vc2 = _ctx_kv(ctx.reshape(batch * nctx, d), mod(0, 1, ctx_row), mod(0, 0, ctx_row),
                       jnp.concatenate([wk2, wv2], axis=1).astype(BF16))
    o = _attention(q, k2, v2, kc2, vc2, attn_sink[0], batch, seq, nctx)
    x1, f, ids, wts, rank, cnt = _attn_out(
        o, attn_w_o[0].astype(BF16), x2, per_batch(0, 2), per_batch(0, 4), per_batch(0, 3),
        ln_mix_g, ln_mix_b, *router(0), seq, 0, alpha)
    x2 = _moe(f, x1, ids, wts, rank, cnt, per_batch(0, 5), ln_ffn_g, ln_ffn_b,
              moe_w1, moe_w3, moe_w2, *shared_w(0), seq, 0, alpha)

    width = gmlp_w_in.shape[2] // 2
    w_in = gmlp_w_in[0].astype(BF16)
    b_in = gmlp_b_in[0].reshape(1, 2 * width)
    vg, vb = gmlp_v_g[0].reshape(1, width), gmlp_v_b[0].reshape(1, width)
    sc, sh = per_batch(1, 1), per_batch(1, 0)
    u = _gmlp_in(x2, sc, sh, w_in[:, :width], b_in[:, :width], vg, vb, seq, False)
    v = _gmlp_in(x2, sc, sh, w_in[:, width:], b_in[:, width:], vg, vb, seq, True)
    x1, f, ids, wts, rank, cnt = _gmlp_out(
        u, v, gmlp_w_s[0].astype(BF16), gmlp_b_s[0][:, :, None], gmlp_w_o[0].astype(BF16), x2,
        per_batch(1, 2), per_batch(1, 4), per_batch(1, 3), ln_mix_g, ln_mix_b, *router(1), seq, 1, alpha)
    x2 = _moe(f, x1, ids, wts, rank, cnt, per_batch(1, 5), ln_ffn_g, ln_ffn_b,
              moe_w1, moe_w3, moe_w2, *shared_w(1), seq, 1, alpha)
    return x2.reshape(batch, seq, d)
```
